```python
import jax, jax.numpy as jnp
from jax import lax
import numpy as np

D_MODEL = 2048
BATCH = 4
SEQ = 2048
DEPTH = 2

CTX_LEN = 256
GRID_W = 64
MLA_HEADS = 6
MLA_Q_RANK = 512
MLA_KV_RANK = 512
MLA_NOPE = 128
MLA_ROPE = 64
MLA_V = 128
MLA_QK = MLA_NOPE + MLA_ROPE
MLA_OUT = MLA_HEADS * MLA_V
CONV_DIM = 512
CONV_WIDTH = 3
NA_HEADS = 6
NA_HEAD_DIM = 128
NA_DIM = NA_HEADS * NA_HEAD_DIM
NA_WIN_R = 8
NA_WIN_C = 16
MIX_DIM = MLA_OUT + CONV_DIM + NA_DIM
OFF_MLA_Q = 0
OFF_MLA_KV = OFF_MLA_Q + MLA_Q_RANK
OFF_CONV_B = OFF_MLA_KV + MLA_KV_RANK + MLA_ROPE
OFF_CONV_C = OFF_CONV_B + CONV_DIM
OFF_CONV_H = OFF_CONV_C + CONV_DIM
OFF_NA_Q = OFF_CONV_H + CONV_DIM
OFF_NA_K = OFF_NA_Q + NA_DIM
OFF_NA_V = OFF_NA_K + NA_DIM
IN_DIM = OFF_NA_V + NA_DIM
D_FF = -(-8 * D_MODEL // (3 * 256)) * 256
ROPE_THETA = 10000.0
LN_EPS = 1e-6
RMS_EPS = 1e-6
Q_BLOCK = 128
ADA_SCALE = 0.5
DEEPNORM_ALPHA = (2 * DEPTH) ** 0.25
DEEPNORM_BETA = (8 * DEPTH) ** -0.25

kernel_name = 'hybrid_mla_conv_natten_prefix_block'


def layer_norm(x, g, b):
    xf = x.astype(jnp.float32)
    mu = jnp.mean(xf, axis=-1, keepdims=True)
    var = jnp.mean(jnp.square(xf - mu), axis=-1, keepdims=True)
    return ((xf - mu) * lax.rsqrt(var + LN_EPS) * g + b).astype(x.dtype)


def rms_norm(x, g):
    xf = x.astype(jnp.float32)
    return (xf * lax.rsqrt(jnp.mean(xf * xf, axis=-1, keepdims=True) + RMS_EPS) * g).astype(x.dtype)


def modulate(x, shift, scale):
    return x * (1 + scale) + shift


def heads(z, lo, n_heads, d):
    return z[..., lo:lo + n_heads * d].reshape(*z.shape[:-1], n_heads, d)


def axial_rope_tables(n_tokens, dtype):
    t = jnp.arange(n_tokens)
    row = (t // GRID_W).astype(jnp.float32)
    col = (t % GRID_W).astype(jnp.float32)
    n_freq = MLA_ROPE // 4
    inv = ROPE_THETA ** (-jnp.arange(n_freq, dtype=jnp.float32) / n_freq)
    ar = row[:, None] * inv
    ac = col[:, None] * inv
    ang = jnp.concatenate([ar, ar, ac, ac], axis=-1)
    return jnp.cos(ang).astype(dtype), jnp.sin(ang).astype(dtype)


def apply_axial_rope(x, cos, sin):
    a1, a2, b1, b2 = jnp.split(x, 4, axis=-1)
    rot = jnp.concatenate([-a2, a1, -b2, b1], axis=-1)
    return x * cos + rot * sin


def full_attention(q, k, v):
    scale = q.shape[-1] ** -0.5
    s = jnp.einsum('bqhd,bkhd->bhqk', q, k).astype(jnp.float32) * scale
    p = jax.nn.softmax(s, axis=-1).astype(v.dtype)
    return jnp.einsum('bhqk,bkhd->bqhd', p, v)


def blocked_attention(q, k, v):
    B, S, H, d = q.shape
    nb = S // Q_BLOCK
    qb = jnp.moveaxis(q.reshape(B, nb, Q_BLOCK, H, d), 1, 0)
    o = lax.map(lambda qi: full_attention(qi, k, v), qb)
    return jnp.moveaxis(o, 0, 1).reshape(B, S, H, v.shape[-1])


def mla_queries(z, g_q, w_qb):
    cq = rms_norm(z[..., OFF_MLA_Q:OFF_MLA_Q + MLA_Q_RANK], g_q)
    q = (cq @ w_qb).reshape(*z.shape[:-1], MLA_HEADS, MLA_QK)
    return q[..., :MLA_NOPE], q[..., MLA_NOPE:]


def mla_kv(z, g_kv, w_kvb):
    ckv = rms_norm(z[..., OFF_MLA_KV:OFF_MLA_KV + MLA_KV_RANK], g_kv)
    k_rope = z[..., OFF_MLA_KV + MLA_KV_RANK:OFF_CONV_B]
    kv = (ckv @ w_kvb).reshape(*z.shape[:-1], MLA_HEADS, MLA_NOPE + MLA_V)
    return kv[..., :MLA_NOPE], k_rope, kv[..., MLA_NOPE:]


def mla_keys(k_nope, k_rope):
    kr = jnp.broadcast_to(k_rope[..., None, :], k_nope.shape[:-1] + (MLA_ROPE,))
    return jnp.concatenate([k_nope, kr], axis=-1)


def gated_short_conv(z, w):
    gb = z[..., OFF_CONV_B:OFF_CONV_C]
    gc = z[..., OFF_CONV_C:OFF_CONV_H]
    h = z[..., OFF_CONV_H:OFF_NA_Q]
    u = jnp.pad(gc * h, ((0, 0), (1, 1), (0, 0)))
    y = u[:, :-2] * w[0] + u[:, 1:-1] * w[1] + u[:, 2:] * w[2]
    return gb * y


def neighbourhood_attention(q, k, v, k_ctx, v_ctx, rpb):
    B, S, H, d = q.shape
    rows = S // GRID_W
    kr = min(NA_WIN_R, rows)
    scale = d ** -0.5
    qg = q.reshape(B, rows, GRID_W, H, d)
    kg = k.reshape(B, rows, GRID_W, H, d)
    vg = v.reshape(B, rows, GRID_W, H, d)
    r = jnp.arange(rows)
    r0 = jnp.clip(r - kr // 2, 0, rows - kr)
    row_idx = r0[:, None] + jnp.arange(kr)[None, :]
    k_band = kg[:, row_idx]
    v_band = vg[:, row_idx]
    col = jnp.arange(GRID_W)
    c0 = jnp.clip(col - NA_WIN_C // 2, 0, GRID_W - NA_WIN_C)
    in_win = (col[None, :] >= c0[:, None]) & (col[None, :] < c0[:, None] + NA_WIN_C)
    dr = row_idx - r[:, None] + NA_WIN_R - 1
    dc = jnp.clip(col[None, :] - col[:, None], 1 - NA_WIN_C, NA_WIN_C - 1) + NA_WIN_C - 1
    bias = rpb[:, dr[:, None, :, None], dc[None, :, None, :]]
    s_loc = jnp.einsum('brqhd,brikhd->bhrqik', qg, k_band).astype(jnp.float32) * scale + bias.astype(jnp.float32)
    s_loc = jnp.where(in_win[:, None, :], s_loc, -jnp.inf)
    s_ctx = jnp.einsum('brqhd,bmhd->bhrqm', qg, k_ctx).astype(jnp.float32) * scale
    s = jnp.concatenate([s_loc.reshape(B, H, rows, GRID_W, kr * GRID_W), s_ctx], axis=-1)
    p = jax.nn.softmax(s, axis=-1).astype(v.dtype)
    p_loc = p[..., :kr * GRID_W].reshape(B, H, rows, GRID_W, kr, GRID_W)
    p_ctx = p[..., kr * GRID_W:]
    o = (jnp.einsum('bhrqik,brikhd->brqhd', p_loc, v_band)
         + jnp.einsum('bhrqm,bmhd->brqhd', p_ctx, v_ctx))
    return o.reshape(B, S, H, d)


def swiglu(x, w_gate, w_up, w_down):
    return (jax.nn.silu(x @ w_gate) * (x @ w_up)) @ w_down


def setup_inputs(seed: int = 0) -> dict:
    key = jax.random.key(seed)
    ks = jax.random.split(key, 24)
    L = DEPTH

    def nrm(k, shape, s):
        return jax.random.normal(k, shape, jnp.float32) * s

    return {
        'x': nrm(ks[0], (BATCH, SEQ, D_MODEL), 1.0),
        'c': nrm(ks[1], (BATCH, D_MODEL), 1.0),
        'ctx': nrm(ks[2], (BATCH, CTX_LEN, D_MODEL), 1.0),
        'c_ctx': nrm(ks[3], (D_MODEL,), 1.0),
        'ada_w': nrm(ks[4], (L, D_MODEL, 6 * D_MODEL), ADA_SCALE * D_MODEL ** -0.5),
        'ada_b': nrm(ks[5], (L, 6 * D_MODEL), 0.02),
        'w_in': nrm(ks[6], (L, D_MODEL, IN_DIM), D_MODEL ** -0.5),
        'mla_q_norm': 1.0 + nrm(ks[7], (L, MLA_Q_RANK), 0.02),
        'mla_wq_b': nrm(ks[8], (L, MLA_Q_RANK, MLA_HEADS * MLA_QK), MLA_Q_RANK ** -0.5),
        'mla_kv_norm': 1.0 + nrm(ks[9], (L, MLA_KV_RANK), 0.02),
        'mla_wkv_b': nrm(ks[10], (L, MLA_KV_RANK, MLA_HEADS * (MLA_NOPE + MLA_V)), MLA_KV_RANK ** -0.5),
        'conv_w': nrm(ks[11], (L, CONV_WIDTH, CONV_DIM), CONV_WIDTH ** -0.5),
        'na_rpb': nrm(ks[12], (L, NA_HEADS, 2 * NA_WIN_R - 1, 2 * NA_WIN_C - 1), 0.1),
        'w_out': nrm(ks[13], (L, MIX_DIM, D_MODEL), DEEPNORM_BETA * MIX_DIM ** -0.5),
        'ln1_g': 1.0 + nrm(ks[14], (L, D_MODEL), 0.02),
        'ln1_b': nrm(ks[15], (L, D_MODEL), 0.02),
        'ffn_w_gate': nrm(ks[16], (L, D_MODEL, D_FF), D_MODEL ** -0.5),
        'ffn_w_up': nrm(ks[17], (L, D_MODEL, D_FF), D_MODEL ** -0.5),
        'ffn_w_down': nrm(ks[18], (L, D_FF, D_MODEL), DEEPNORM_BETA * D_FF ** -0.5),
        'ln2_g': 1.0 + nrm(ks[19], (L, D_MODEL), 0.02),
        'ln2_b': nrm(ks[20], (L, D_MODEL), 0.02),
    }


def reference(x, c, ctx, c_ctx, ada_w, ada_b, w_in, mla_q_norm, mla_wq_b, mla_kv_norm, mla_wkv_b,
              conv_w, na_rpb, w_out, ln1_g, ln1_b, ffn_w_gate, ffn_w_up, ffn_w_down, ln2_g, ln2_b):
    B, S, _ = x.shape
    cos, sin = axial_rope_tables(S, x.dtype)
    h_ctx = ctx
    for l in range(DEPTH):
        last = l == DEPTH - 1
        sh1, sc1, g1, sh2, sc2, g2 = jnp.split(jax.nn.silu(c) @ ada_w[l] + ada_b[l], 6, axis=-1)
        csh1, csc1, cg1, csh2, csc2, cg2 = jnp.split(jax.nn.silu(c_ctx) @ ada_w[l] + ada_b[l], 6, axis=-1)

        z_lat = modulate(x, sh1[:, None], sc1[:, None]) @ w_in[l]
        z_ctx = modulate(h_ctx, csh1, csc1) @ w_in[l]

        qn, qr = mla_queries(z_lat, mla_q_norm[l], mla_wq_b[l])
        kn, kr, v = mla_kv(z_lat, mla_kv_norm[l], mla_wkv_b[l])
        q_lat = jnp.concatenate([qn, apply_axial_rope(qr, cos[:, None, :], sin[:, None, :])], axis=-1)
        k_lat = mla_keys(kn, apply_axial_rope(kr, cos, sin))
        ckn, ckr, cv = mla_kv(z_ctx, mla_kv_norm[l], mla_wkv_b[l])
        k_ctx = mla_keys(ckn, ckr)
        a_lat = blocked_attention(q_lat, jnp.concatenate([k_ctx, k_lat], axis=1),
                                  jnp.concatenate([cv, v], axis=1))

        conv_lat = gated_short_conv(z_lat, conv_w[l])

        nq = heads(z_lat, OFF_NA_Q, NA_HEADS, NA_HEAD_DIM)
        nk = heads(z_lat, OFF_NA_K, NA_HEADS, NA_HEAD_DIM)
        nv = heads(z_lat, OFF_NA_V, NA_HEADS, NA_HEAD_DIM)
        cnk = heads(z_ctx, OFF_NA_K, NA_HEADS, NA_HEAD_DIM)
        cnv = heads(z_ctx, OFF_NA_V, NA_HEADS, NA_HEAD_DIM)
        n_lat = neighbourhood_attention(nq, nk, nv, cnk, cnv, na_rpb[l])

        mix = jnp.concatenate([a_lat.reshape(B, S, MLA_OUT), conv_lat, n_lat.reshape(B, S, NA_DIM)], axis=-1)
        x_new = layer_norm(DEEPNORM_ALPHA * x + g1[:, None] * (mix @ w_out[l]), ln1_g[l], ln1_b[l])
        ffn = swiglu(modulate(x_new, sh2[:, None], sc2[:, None]), ffn_w_gate[l], ffn_w_up[l], ffn_w_down[l])
        x_new = layer_norm(DEEPNORM_ALPHA * x_new + g2[:, None] * ffn, ln2_g[l], ln2_b[l])

        if not last:
            cqn, cqr = mla_queries(z_ctx, mla_q_norm[l], mla_wq_b[l])
            a_ctx = full_attention(jnp.concatenate([cqn, cqr], axis=-1), k_ctx, cv)
            conv_ctx = gated_short_conv(z_ctx, conv_w[l])
            cnq = heads(z_ctx, OFF_NA_Q, NA_HEADS, NA_HEAD_DIM)
            n_ctx = full_attention(cnq, cnk, cnv)
            M = h_ctx.shape[1]
            mix_c = jnp.concatenate([a_ctx.reshape(B, M, MLA_OUT), conv_ctx, n_ctx.reshape(B, M, NA_DIM)], axis=-1)
            hc = layer_norm(DEEPNORM_ALPHA * h_ctx + cg1 * (mix_c @ w_out[l]), ln1_g[l], ln1_b[l])
            ffn_c = swiglu(modulate(hc, csh2, csc2), ffn_w_gate[l], ffn_w_up[l], ffn_w_down[l])
            h_ctx = layer_norm(DEEPNORM_ALPHA * hc + cg2 * ffn_c, ln2_g[l], ln2_b[l])
        x = x_new
    return x
```

```python
import functools

import numpy as np
import jax
import jax.numpy as jnp
from jax import lax
from jax.experimental import pallas as pl
from jax.experimental.pallas import tpu as pltpu

D_MODEL = 2048
BATCH = 4
SEQ = 2048
DEPTH = 2
CTX_LEN = 256
GRID_W = 64
GRID_ROWS = SEQ // GRID_W
MLA_HEADS = 6
MLA_Q_RANK = 512
MLA_KV_RANK = 512
MLA_NOPE = 128
MLA_ROPE = 64
MLA_V = 128
MLA_QK = MLA_NOPE + MLA_ROPE
CONV_DIM = 512
NA_HEADS = 6
NA_HEAD_DIM = 128
NA_DIM = NA_HEADS * NA_HEAD_DIM
NA_WIN_R = 8
NA_WIN_C = 16
OFF_MLA_Q = 0
OFF_MLA_KV = OFF_MLA_Q + MLA_Q_RANK
OFF_ROPE = OFF_MLA_KV + MLA_KV_RANK
OFF_CONV_B = OFF_ROPE + MLA_ROPE
OFF_CONV_C = OFF_CONV_B + CONV_DIM
OFF_CONV_H = OFF_CONV_C + CONV_DIM
OFF_NA_Q = OFF_CONV_H + CONV_DIM
OFF_NA_K = OFF_NA_Q + NA_DIM
OFF_NA_V = OFF_NA_K + NA_DIM
IN_DIM = OFF_NA_V + NA_DIM
D_FF = -(-8 * D_MODEL // (3 * 256)) * 256
ROPE_THETA = 10000.0
LN_EPS = 1e-6
RMS_EPS = 1e-6
DEEPNORM_ALPHA = (2 * DEPTH) ** 0.25
N_MOD = 6
MOD_ROWS = 8
CTX_MOD_ROW = BATCH

LANES = 128
V7X_VMEM_BYTES = 64 * 1024 * 1024
V7X_VMEM_BUDGET = 58 * 1024 * 1024

Z_Q = 0
Z_KV = Z_Q + MLA_Q_RANK
Z_CONV_B = Z_KV + MLA_KV_RANK
Z_CONV_C = Z_CONV_B + CONV_DIM
Z_CONV_H = Z_CONV_C + CONV_DIM
Z_NA_Q = Z_CONV_H + CONV_DIM
Z_NA_K = Z_NA_Q + NA_DIM
Z_NA_V = Z_NA_K + NA_DIM
Z_ROPE = Z_NA_V + NA_DIM
Z_DIM = Z_ROPE + 2 * MLA_ROPE

QK_PAD = 2 * LANES
WQ_HEAD = 3 * LANES

TM = 1024
TM_OUT = 512
TM_PROJ = 512
TN_IN = Z_DIM // 3
TN_ADA = 1024
TF = 512
TQ = 256
NA_RQ = 4
NA_RK = NA_RQ + NA_WIN_R - 1
NA_NQ = NA_RQ * GRID_W
NA_NK = NA_RK * GRID_W
NA_BLOCKS = GRID_ROWS // NA_RQ

F32 = jnp.float32
BF16 = jnp.bfloat16


def _params(semantics, vmem_bytes):
    assert vmem_bytes <= V7X_VMEM_BUDGET, vmem_bytes
    return pltpu.CompilerParams(dimension_semantics=semantics, vmem_limit_bytes=int(vmem_bytes))


def _nbytes(shape, dtype):
    return int(np.prod(shape)) * jnp.dtype(dtype).itemsize


def _dot(a, b):
    return jnp.dot(a, b, preferred_element_type=F32)


def _dot_nt(a, b):
    return lax.dot_general(a, b, (((1,), (1,)), ((), ())), preferred_element_type=F32)


def _mod_spec(layer, chunk, row_of_tile):
    base = layer * MOD_ROWS * N_MOD + chunk
    return pl.BlockSpec((None, 1, D_MODEL), lambda i, *_: (base + row_of_tile(i) * N_MOD, 0, 0))


def _layer_norm(r, g, b):
    mu = jnp.mean(r, axis=-1, keepdims=True)
    c = r - mu
    var = jnp.mean(c * c, axis=-1, keepdims=True)
    return c * lax.rsqrt(var + LN_EPS) * g + b


def _ada_kernel(cc_ref, w_ref, b_ref, o_ref):
    cc = cc_ref[...]
    s = (cc * jax.nn.sigmoid(cc)).astype(BF16)
    o_ref[...] = _dot(s, w_ref[...].astype(BF16)) + b_ref[...]


def _ada(cc, ada_w, ada_b):
    n = N_MOD * D_MODEL
    vmem = 2 * (_nbytes((D_MODEL, TN_ADA), F32) + _nbytes((MOD_ROWS, TN_ADA), F32) * 2
                + _nbytes((MOD_ROWS, D_MODEL), F32)) + _nbytes((D_MODEL, TN_ADA), BF16) * 2
    return pl.pallas_call(
        _ada_kernel,
        out_shape=jax.ShapeDtypeStruct((DEPTH, MOD_ROWS, n), F32),
        grid=(DEPTH, n // TN_ADA),
        in_specs=[
            pl.BlockSpec((MOD_ROWS, D_MODEL), lambda l, j: (0, 0)),
            pl.BlockSpec((None, D_MODEL, TN_ADA), lambda l, j: (l, 0, j)),
            pl.BlockSpec((None, 1, TN_ADA), lambda l, j: (l, 0, j)),
        ],
        out_specs=pl.BlockSpec((None, MOD_ROWS, TN_ADA), lambda l, j: (l, 0, j)),
        compiler_params=_params(("parallel", "parallel"), vmem),
        name="ada_mod",
    )(cc, ada_w, ada_b.reshape(DEPTH, 1, n))


def _in_proj_kernel(sh_ref, sc_ref, x_ref, w_ref, o_ref, xm_ref):
    @pl.when(pl.program_id(1) == 0)
    def _():
        xm_ref[...] = (x_ref[...] * (1.0 + sc_ref[...]) + sh_ref[...]).astype(BF16)

    o_ref[...] = _dot(xm_ref[...], w_ref[...]).astype(BF16)


def _in_proj(x, mods, w_in_p, layer, row_of_tile):
    rows = x.shape[0]
    vmem = (2 * (_nbytes((TM, D_MODEL), F32) + _nbytes((D_MODEL, TN_IN), BF16) + _nbytes((TM, TN_IN), BF16))
            + _nbytes((TM, D_MODEL), BF16) * 2 + _nbytes((TM, TN_IN), F32))
    return pl.pallas_call(
        _in_proj_kernel,
        out_shape=jax.ShapeDtypeStruct((rows, Z_DIM), BF16),
        grid=(rows // TM, Z_DIM // TN_IN),
        in_specs=[
            _mod_spec(layer, 0, row_of_tile),
            _mod_spec(layer, 1, row_of_tile),
            pl.BlockSpec((TM, D_MODEL), lambda i, j: (i, 0)),
            pl.BlockSpec((D_MODEL, TN_IN), lambda i, j: (0, j)),
        ],
        out_specs=pl.BlockSpec((TM, TN_IN), lambda i, j: (i, j)),
        scratch_shapes=[pltpu.VMEM((TM, D_MODEL), BF16)],
        compiler_params=_params(("parallel", "arbitrary"), vmem),
        name="in_proj",
    )(mods, mods, x, w_in_p)


def _rms_norm(x, g):
    return x * lax.rsqrt(jnp.mean(x * x, axis=-1, keepdims=True) + RMS_EPS) * g


def _mla_proj_kernel(z_ref, slot_ref, gq_ref, gkv_ref, wq_ref, wkv_ref, ta_ref, tb_ref, q_ref, k_ref, v_ref):
    cq = z_ref[:, Z_Q:Z_Q + MLA_Q_RANK].astype(F32)
    ckv = z_ref[:, Z_KV:Z_KV + MLA_KV_RANK].astype(F32)
    qa = _dot(_rms_norm(cq, gq_ref[...]).astype(BF16), wq_ref[...])
    kv = _dot(_rms_norm(ckv, gkv_ref[...]).astype(BF16), wkv_ref[...])
    ta = ta_ref[...]
    tb = tb_ref[...]
    y = slot_ref[...].astype(F32) * ta
    kro = y + pltpu.roll(y, MLA_ROPE, 1)
    lane = lax.broadcasted_iota(jnp.int32, kro.shape, 1)
    kro = jnp.where(lane < MLA_ROPE, kro, 0.0).astype(BF16)
    for h in range(MLA_HEADS):
        qb = h * WQ_HEAD
        ob = h * QK_PAD
        q_ref[:, ob:ob + LANES] = qa[:, qb:qb + LANES].astype(BF16)
        q_ref[:, ob + LANES:ob + QK_PAD] = (
            qa[:, qb + LANES:qb + 2 * LANES] * ta + qa[:, qb + 2 * LANES:qb + 3 * LANES] * tb).astype(BF16)
        k_ref[:, ob:ob + LANES] = kv[:, ob:ob + LANES].astype(BF16)
        k_ref[:, ob + LANES:ob + QK_PAD] = kro
        v_ref[:, h * MLA_V:(h + 1) * MLA_V] = kv[:, ob + LANES:ob + QK_PAD].astype(BF16)


def _mla_proj(z, g_q, g_kv, wq_all, wkv, ta, tb, table_tile):
    rows = z.shape[0]
    zw = Z_KV + MLA_KV_RANK
    hq = MLA_HEADS * QK_PAD
    hv = MLA_HEADS * MLA_V
    tm = TM_PROJ
    vmem = (2 * (_nbytes((tm, zw), BF16) + _nbytes((tm, LANES), BF16) + 2 * _nbytes((tm, LANES), F32)
                 + _nbytes(wq_all.shape, BF16) + _nbytes(wkv.shape, BF16)
                 + 2 * _nbytes((tm, hq), BF16) + _nbytes((tm, hv), BF16))
            + _nbytes((tm, MLA_HEADS * WQ_HEAD), F32) + _nbytes((tm, hq), F32) + 4 * _nbytes((tm, zw), F32))
    return pl.pallas_call(
        _mla_proj_kernel,
        out_shape=(jax.ShapeDtypeStruct((rows, hq), BF16), jax.ShapeDtypeStruct((rows, hq), BF16),
                   jax.ShapeDtypeStruct((rows, hv), BF16)),
        grid=(rows // tm,),
        in_specs=[
            pl.BlockSpec((tm, zw), lambda i: (i, 0)),
            pl.BlockSpec((tm, 2 * MLA_ROPE), lambda i: (i, Z_ROPE // (2 * MLA_ROPE))),
            pl.BlockSpec((1, MLA_Q_RANK), lambda i: (0, 0)),
            pl.BlockSpec((1, MLA_KV_RANK), lambda i: (0, 0)),
            pl.BlockSpec(wq_all.shape, lambda i: (0, 0)),
            pl.BlockSpec(wkv.shape, lambda i: (0, 0)),
            pl.BlockSpec((tm, LANES), lambda i: (table_tile(i), 0)),
            pl.BlockSpec((tm, LANES), lambda i: (table_tile(i), 0)),
        ],
        out_specs=(pl.BlockSpec((tm, hq), lambda i: (i, 0)), pl.BlockSpec((tm, hq), lambda i: (i, 0)),
                   pl.BlockSpec((tm, hv), lambda i: (i, 0))),
        compiler_params=_params(("parallel",), vmem),
        name="mla_proj",
    )(z, z, g_q, g_kv, wq_all, wkv, ta, tb)


def _softmax_pv(scores, values):
    m = functools.reduce(jnp.maximum, [jnp.max(s, axis=-1, keepdims=True) for s in scores])
    ps = [jnp.exp(s - m) for s in scores]
    denom = functools.reduce(jnp.add, [jnp.sum(p, axis=-1, keepdims=True) for p in ps])
    acc = functools.reduce(jnp.add, [_dot(p.astype(BF16), v) for p, v in zip(ps, values)])
    return acc / denom


def _ctx_self_attention(qc_ref, kc_ref, vc_ref, oc_ref, scale):
    s = _dot_nt(qc_ref[...], kc_ref[...]) * scale
    oc_ref[...] = _softmax_pv([s], [vc_ref[...]]).astype(BF16)


def _mla_attn_kernel(*refs, with_ctx):
    if with_ctx:
        q_ref, kl_ref, vl_ref, kc_ref, vc_ref, qc_ref, o_ref, oc_ref = refs
    else:
        q_ref, kl_ref, vl_ref, kc_ref, vc_ref, o_ref = refs
    scale = MLA_QK ** -0.5

    def q_tile(t, carry):
        r0 = pl.multiple_of(t * TQ, TQ)
        q = q_ref[pl.ds(r0, TQ), :]
        s_lat = _dot_nt(q, kl_ref[...]) * scale
        s_ctx = _dot_nt(q, kc_ref[...]) * scale
        o = _softmax_pv([s_lat, s_ctx], [vl_ref[...], vc_ref[...]])
        o_ref[pl.ds(r0, TQ), :] = o.astype(BF16)
        return carry

    lax.fori_loop(0, SEQ // TQ, q_tile, 0)
    if with_ctx:
        _ctx_self_attention(qc_ref, kc_ref, vc_ref, oc_ref, scale)


def _mla_attn(ql, kl, vl, qc, kc, vc, with_ctx):
    head = lambda b, h: (b, h)
    in_specs = [
        pl.BlockSpec((SEQ, QK_PAD), head), pl.BlockSpec((SEQ, QK_PAD), head), pl.BlockSpec((SEQ, MLA_V), head),
        pl.BlockSpec((CTX_LEN, QK_PAD), head), pl.BlockSpec((CTX_LEN, MLA_V), head),
    ]
    args = [ql, kl, vl, kc, vc]
    out_shape = [jax.ShapeDtypeStruct((BATCH * SEQ, MLA_HEADS * MLA_V), BF16)]
    out_specs = [pl.BlockSpec((SEQ, MLA_V), head)]
    if with_ctx:
        in_specs.append(pl.BlockSpec((CTX_LEN, QK_PAD), head))
        args.append(qc)
        out_shape.append(jax.ShapeDtypeStruct((BATCH * CTX_LEN, MLA_HEADS * MLA_V), BF16))
        out_specs.append(pl.BlockSpec((CTX_LEN, MLA_V), head))
    vmem = (2 * (3 * _nbytes((SEQ, QK_PAD), BF16) + 2 * _nbytes((SEQ, MLA_V), BF16))
            + 4 * _nbytes((TQ, SEQ + CTX_LEN), F32))
    return pl.pallas_call(
        functools.partial(_mla_attn_kernel, with_ctx=with_ctx),
        out_shape=tuple(out_shape), grid=(BATCH, MLA_HEADS), in_specs=in_specs, out_specs=tuple(out_specs),
        compiler_params=_params(("parallel", "parallel"), vmem),
        name="mla_attn_ctx" if with_ctx else "mla_attn",
    )(*args)


def _conv_kernel(gb_ref, gc_ref, h_ref, w_ref, o_ref):
    n = o_ref.shape[0]
    u = gc_ref[...].astype(F32) * h_ref[...].astype(F32)
    row = lax.broadcasted_iota(jnp.int32, u.shape, 0)
    u_prev = jnp.where(row == 0, 0.0, pltpu.roll(u, 1, 0))
    u_next = jnp.where(row == n - 1, 0.0, pltpu.roll(u, n - 1, 0))
    y = u_prev * w_ref[0:1, :] + u * w_ref[1:2, :] + u_next * w_ref[2:3, :]
    o_ref[...] = (gb_ref[...].astype(F32) * y).astype(BF16)


def _conv(z, conv_w, seq_len):
    rows = z.shape[0]
    blk = (seq_len, CONV_DIM)
    vmem = 2 * 4 * _nbytes(blk, BF16) + 8 * _nbytes(blk, F32)
    return pl.pallas_call(
        _conv_kernel,
        out_shape=jax.ShapeDtypeStruct((rows, CONV_DIM), BF16),
        grid=(rows // seq_len,),
        in_specs=[
            pl.BlockSpec(blk, lambda s: (s, Z_CONV_B // CONV_DIM)),
            pl.BlockSpec(blk, lambda s: (s, Z_CONV_C // CONV_DIM)),
            pl.BlockSpec(blk, lambda s: (s, Z_CONV_H // CONV_DIM)),
            pl.BlockSpec(conv_w.shape, lambda s: (0, 0)),
        ],
        out_specs=pl.BlockSpec(blk, lambda s: (s, 0)),
        compiler_params=_params(("parallel",), vmem),
        name="short_conv",
    )(z, z, z, conv_w)


def _na_band_start(block):
    return min(max(block * NA_RQ - NA_WIN_R // 2, 0), GRID_ROWS - NA_RK)


def _na_bias_kind(block):
    return 0 if block == 0 else (2 if block == NA_BLOCKS - 1 else 1)


def _na_bias_indices():
    dr = np.zeros((3, NA_NQ, NA_NK), np.int32)
    dc = np.zeros((3, NA_NQ, NA_NK), np.int32)
    ok = np.zeros((3, NA_NQ, NA_NK), bool)
    qi = np.arange(NA_NQ)
    kj = np.arange(NA_NK)
    for kind, block in ((0, 0), (1, 2), (2, NA_BLOCKS - 1)):
        r = block * NA_RQ + qi // GRID_W
        c = qi % GRID_W
        kr = _na_band_start(block) + kj // GRID_W
        kc = kj % GRID_W
        r0 = np.clip(r - NA_WIN_R // 2, 0, GRID_ROWS - NA_WIN_R)
        c0 = np.clip(c - NA_WIN_C // 2, 0, GRID_W - NA_WIN_C)
        row_ok = (kr[None, :] >= r0[:, None]) & (kr[None, :] < r0[:, None] + NA_WIN_R)
        col_ok = (kc[None, :] >= c0[:, None]) & (kc[None, :] < c0[:, None] + NA_WIN_C)
        ok[kind] = row_ok & col_ok
        dr[kind] = np.clip(kr[None, :] - r[:, None] + NA_WIN_R - 1, 0, 2 * NA_WIN_R - 2)
        dc[kind] = np.clip(kc[None, :] - c[:, None], 1 - NA_WIN_C, NA_WIN_C - 1) + NA_WIN_C - 1
    return dr, dc, ok


def _na_bias(rpb):
    dr, dc, ok = _na_bias_indices()
    return jnp.where(ok[None], rpb[:, dr, dc], -jnp.inf).astype(F32)


def _na_kernel(*refs, with_ctx):
    if with_ctx:
        q_ref, k_ref, v_ref, kc_ref, vc_ref, bias_ref, qc_ref, o_ref, oc_ref = refs
    else:
        q_ref, k_ref, v_ref, kc_ref, vc_ref, bias_ref, o_ref = refs
    scale = NA_HEAD_DIM ** -0.5
    for block in range(NA_BLOCKS):
        q = q_ref[block * NA_NQ:(block + 1) * NA_NQ, :]
        k0 = _na_band_start(block) * GRID_W
        s_loc = _dot_nt(q, k_ref[k0:k0 + NA_NK, :]) * scale + bias_ref[_na_bias_kind(block)]
        s_ctx = _dot_nt(q, kc_ref[...]) * scale
        o = _softmax_pv([s_loc, s_ctx], [v_ref[k0:k0 + NA_NK, :], vc_ref[...]])
        o_ref[block * NA_NQ:(block + 1) * NA_NQ, :] = o.astype(BF16)
    if with_ctx:
        _ctx_self_attention(qc_ref, kc_ref, vc_ref, oc_ref, scale)


def _na_attn(zl, zc, bias, with_ctx):
    d = NA_HEAD_DIM
    col = lambda off: (lambda b, h: (b, off // d + h))
    in_specs = [
        pl.BlockSpec((SEQ, d), col(Z_NA_Q)), pl.BlockSpec((SEQ, d), col(Z_NA_K)), pl.BlockSpec((SEQ, d), col(Z_NA_V)),
        pl.BlockSpec((CTX_LEN, d), col(Z_NA_K)), pl.BlockSpec((CTX_LEN, d), col(Z_NA_V)),
        pl.BlockSpec((None, 3, NA_NQ, NA_NK), lambda b, h: (h, 0, 0, 0)),
    ]
    args = [zl, zl, zl, zc, zc, bias]
    out_shape = [jax.ShapeDtypeStruct((BATCH * SEQ, NA_DIM), BF16)]
    out_specs = [pl.BlockSpec((SEQ, d), lambda b, h: (b, h))]
    if with_ctx:
        in_specs.append(pl.BlockSpec((CTX_LEN, d), col(Z_NA_Q)))
        args.append(zc)
        out_shape.append(jax.ShapeDtypeStruct((BATCH * CTX_LEN, NA_DIM), BF16))
        out_specs.append(pl.BlockSpec((CTX_LEN, d), lambda b, h: (b, h)))
    vmem = (2 * (5 * _nbytes((SEQ, d), BF16) + _nbytes((3, NA_NQ, NA_NK), F32))
            + 8 * _nbytes((NA_NQ, NA_NK + CTX_LEN), F32))
    return pl.pallas_call(
        functools.partial(_na_kernel, with_ctx=with_ctx),
        out_shape=tuple(out_shape), grid=(BATCH, NA_HEADS), in_specs=in_specs, out_specs=tuple(out_specs),
        compiler_params=_params(("parallel", "parallel"), vmem),
        name="na_attn_ctx" if with_ctx else "na_attn",
    )(*args)


def _out_proj_kernel(g_ref, a_ref, c_ref, n_ref, x_ref, w_ref, lng_ref, lnb_ref, o_ref):
    a_w = MLA_HEADS * MLA_V
    y = (_dot(a_ref[...], w_ref[0:a_w, :]) + _dot(c_ref[...], w_ref[a_w:a_w + CONV_DIM, :])
         + _dot(n_ref[...], w_ref[a_w + CONV_DIM:, :]))
    r = DEEPNORM_ALPHA * x_ref[...] + g_ref[...] * y
    o_ref[...] = _layer_norm(r, lng_ref[...], lnb_ref[...])


def _out_proj(a, cv, n, x, mods, w_out, ln_g, ln_b, layer, row_of_tile):
    rows = x.shape[0]
    tm = TM_OUT
    row = lambda i: (i, 0)
    const = lambda i: (0, 0)
    vmem = (2 * (2 * _nbytes((tm, D_MODEL), F32) + _nbytes((tm, D_MODEL), BF16) + _nbytes(w_out.shape, BF16))
            + 3 * _nbytes((tm, D_MODEL), F32))
    return pl.pallas_call(
        _out_proj_kernel,
        out_shape=jax.ShapeDtypeStruct((rows, D_MODEL), F32),
        grid=(rows // tm,),
        in_specs=[
            _mod_spec(layer, 2, row_of_tile),
            pl.BlockSpec((tm, a.shape[1]), row), pl.BlockSpec((tm, cv.shape[1]), row),
            pl.BlockSpec((tm, n.shape[1]), row), pl.BlockSpec((tm, D_MODEL), row),
            pl.BlockSpec(w_out.shape, const),
            pl.BlockSpec((1, D_MODEL), const), pl.BlockSpec((1, D_MODEL), const),
        ],
        out_specs=pl.BlockSpec((tm, D_MODEL), row),
        compiler_params=_params(("parallel",), vmem),
        name="out_proj",
    )(mods, a, cv, n, x, w_out, ln_g, ln_b)


def _ffn_kernel(sh_ref, sc_ref, g_ref, x_ref, wg_ref, wu_ref, wd_ref, lng_ref, lnb_ref, o_ref, xm_ref):
    f = pl.program_id(1)

    @pl.when(f == 0)
    def _():
        x = x_ref[...]
        xm_ref[...] = (x * (1.0 + sc_ref[...]) + sh_ref[...]).astype(BF16)
        o_ref[...] = DEEPNORM_ALPHA * x

    xm = xm_ref[...]
    gate = _dot(xm, wg_ref[...])
    up = _dot(xm, wu_ref[...])
    hidden = (gate * jax.nn.sigmoid(gate) * up).astype(BF16)
    o_ref[...] += g_ref[...] * _dot(hidden, wd_ref[...])

    @pl.when(f == pl.num_programs(1) - 1)
    def _():
        o_ref[...] = _layer_norm(o_ref[...], lng_ref[...], lnb_ref[...])


def _ffn(x, mods, w_gate, w_up, w_down, ln_g, ln_b, layer, row_of_tile):
    rows = x.shape[0]
    const = lambda i, f: (0, 0)
    vmem = (3 * _nbytes((TM, D_MODEL), F32) + 2 * 3 * _nbytes((D_MODEL, TF), BF16)
            + _nbytes((TM, D_MODEL), BF16) + 3 * _nbytes((TM, TF), F32) + _nbytes((TM, D_MODEL), F32))
    return pl.pallas_call(
        _ffn_kernel,
        out_shape=jax.ShapeDtypeStruct((rows, D_MODEL), F32),
        grid=(rows // TM, D_FF // TF),
        in_specs=[
            _mod_spec(layer, 3, row_of_tile), _mod_spec(layer, 4, row_of_tile), _mod_spec(layer, 5, row_of_tile),
            pl.BlockSpec((TM, D_MODEL), lambda i, f: (i, 0), pipeline_mode=pl.Buffered(1)),
            pl.BlockSpec((D_MODEL, TF), lambda i, f: (0, f)),
            pl.BlockSpec((D_MODEL, TF), lambda i, f: (0, f)),
            pl.BlockSpec((TF, D_MODEL), lambda i, f: (f, 0)),
            pl.BlockSpec((1, D_MODEL), const), pl.BlockSpec((1, D_MODEL), const),
        ],
        out_specs=pl.BlockSpec((TM, D_MODEL), lambda i, f: (i, 0)),
        scratch_shapes=[pltpu.VMEM((TM, D_MODEL), BF16)],
        compiler_params=_params(("parallel", "arbitrary"), vmem),
        name="ffn",
    )(mods, mods, mods, x, w_gate, w_up, w_down, ln_g, ln_b)


def _rotate_half_cols(w):
    a1, a2, b1, b2 = jnp.split(w, 4, axis=-1)
    return jnp.concatenate([-a2, a1, -b2, b1], axis=-1)


def _relayout_w_in(w):
    rope = w[:, OFF_ROPE:OFF_ROPE + MLA_ROPE]
    parts = [w[:, OFF_MLA_Q:OFF_ROPE], w[:, OFF_CONV_B:IN_DIM], rope, _rotate_half_cols(rope)]
    return jnp.concatenate(parts, axis=1).astype(BF16)


def _relayout_wq(w):
    w = w.reshape(MLA_Q_RANK, MLA_HEADS, MLA_QK)
    nope, rope = w[..., :MLA_NOPE], w[..., MLA_NOPE:]
    zeros = jnp.zeros_like(rope)
    cols = jnp.concatenate([nope, rope, zeros, _rotate_half_cols(rope), zeros], axis=-1)
    return cols.reshape(MLA_Q_RANK, MLA_HEADS * WQ_HEAD).astype(BF16)


def _rope_tables():
    t = jnp.arange(SEQ)
    row = (t // GRID_W).astype(F32)
    col = (t % GRID_W).astype(F32)
    n_freq = MLA_ROPE // 4
    inv = ROPE_THETA ** (-jnp.arange(n_freq, dtype=F32) / n_freq)
    ar = row[:, None] * inv
    ac = col[:, None] * inv
    ang = jnp.concatenate([ar, ar, ac, ac], axis=-1)
    cos, sin = jnp.cos(ang), jnp.sin(ang)
    ones, zeros = jnp.ones((TM_PROJ, MLA_ROPE), F32), jnp.zeros((TM_PROJ, MLA_ROPE), F32)
    lat = (jnp.concatenate([cos, sin], axis=1), jnp.concatenate([sin, cos], axis=1))
    ctx = (jnp.concatenate([ones, zeros], axis=1), jnp.concatenate([zeros, ones], axis=1))
    return lat, ctx


def kernel(x, c, ctx, c_ctx, ada_w, ada_b, w_in, mla_q_norm, mla_wq_b, mla_kv_norm, mla_wkv_b, conv_w, na_rpb,
           w_out, ln1_g, ln1_b, ffn_w_gate, ffn_w_up, ffn_w_down, ln2_g, ln2_b):
    assert x.shape == (BATCH, SEQ, D_MODEL) and ctx.shape == (BATCH, CTX_LEN, D_MODEL)
    cc = jnp.concatenate([c, c_ctx[None, :], jnp.zeros((MOD_ROWS - BATCH - 1, D_MODEL), F32)], axis=0)
    mods = _ada(cc, ada_w, ada_b).reshape(DEPTH * MOD_ROWS * N_MOD, 1, D_MODEL)
    (ta_lat, tb_lat), (ta_ctx, tb_ctx) = _rope_tables()
    row2d = lambda v: v.reshape(1, -1)

    xl = x.reshape(BATCH * SEQ, D_MODEL)
    xc = ctx.reshape(BATCH * CTX_LEN, D_MODEL)
    lat_row = lambda tm: (lambda i: i // (SEQ // tm))
    ctx_row = lambda i: CTX_MOD_ROW
    for l in range(DEPTH):
        last = l == DEPTH - 1
        w_in_p = _relayout_w_in(w_in[l])
        wq_all = _relayout_wq(mla_wq_b[l])
        wkv = mla_wkv_b[l].astype(BF16)
        g_q, g_kv = row2d(mla_q_norm[l]), row2d(mla_kv_norm[l])
        bias = _na_bias(na_rpb[l])

        zl = _in_proj(xl, mods, w_in_p, l, lat_row(TM))
        zc = _in_proj(xc, mods, w_in_p, l, ctx_row)
        ql, kl, vl = _mla_proj(zl, g_q, g_kv, wq_all, wkv, ta_lat, tb_lat, lambda i: i % (SEQ // TM_PROJ))
        qc, kc, vc = _mla_proj(zc, g_q, g_kv, wq_all, wkv, ta_ctx, tb_ctx, lambda i: 0)
        a = _mla_attn(ql, kl, vl, qc, kc, vc, with_ctx=not last)
        n = _na_attn(zl, zc, bias, with_ctx=not last)
        cv_l = _conv(zl, conv_w[l], SEQ)

        w_o = w_out[l].astype(BF16)
        w_g, w_u, w_d = ffn_w_gate[l].astype(BF16), ffn_w_up[l].astype(BF16), ffn_w_down[l].astype(BF16)
        ln1 = (row2d(ln1_g[l]), row2d(ln1_b[l]))
        ln2 = (row2d(ln2_g[l]), row2d(ln2_b[l]))
        x1 = _out_proj(a[0], cv_l, n[0], xl, mods, w_o, *ln1, l, lat_row(TM_OUT))
        xl = _ffn(x1, mods, w_g, w_u, w_d, *ln2, l, lat_row(TM))
        if not last:
            cv_c = _conv(zc, conv_w[l], CTX_LEN)
            x1c = _out_proj(a[1], cv_c, n[1], xc, mods, w_o, *ln1, l, ctx_row)
            xc = _ffn(x1c, mods, w_g, w_u, w_d, *ln2, l, ctx_row)
    return xl.reshape(BATCH, SEQ, D_MODEL)
```

```python
import functools

import numpy as np
import jax
import jax.numpy as jnp
from jax import lax
from jax.experimental import pallas as pl
from jax.experimental.pallas import tpu as pltpu

D_MODEL = 2048
BATCH = 4
SEQ = 2048
DEPTH = 2
CTX_LEN = 256
GRID_W = 64
GRID_ROWS = SEQ // GRID_W
MLA_HEADS = 6
MLA_Q_RANK = 512
MLA_KV_RANK = 512
MLA_NOPE = 128
MLA_ROPE = 64
MLA_V = 128
MLA_QK = MLA_NOPE + MLA_ROPE
CONV_DIM = 512
NA_HEADS = 6
NA_HEAD_DIM = 128
NA_DIM = NA_HEADS * NA_HEAD_DIM
NA_WIN_R = 8
NA_WIN_C = 16
OFF_MLA_Q = 0
OFF_MLA_KV = OFF_MLA_Q + MLA_Q_RANK
OFF_ROPE = OFF_MLA_KV + MLA_KV_RANK
OFF_CONV_B = OFF_ROPE + MLA_ROPE
OFF_CONV_C = OFF_CONV_B + CONV_DIM
OFF_CONV_H = OFF_CONV_C + CONV_DIM
OFF_NA_Q = OFF_CONV_H + CONV_DIM
OFF_NA_K = OFF_NA_Q + NA_DIM
OFF_NA_V = OFF_NA_K + NA_DIM
IN_DIM = OFF_NA_V + NA_DIM
D_FF = -(-8 * D_MODEL // (3 * 256)) * 256
ROPE_THETA = 10000.0
LN_EPS = 1e-6
RMS_EPS = 1e-6
DEEPNORM_ALPHA = (2 * DEPTH) ** 0.25
N_MOD = 6
MOD_ROWS = 8
CTX_MOD_ROW = BATCH

LANES = 128
V7X_VMEM_BYTES = 64 * 1024 * 1024
V7X_VMEM_BUDGET = 58 * 1024 * 1024

Z_Q = 0
Z_KV = Z_Q + MLA_Q_RANK
Z_CONV_B = Z_KV + MLA_KV_RANK
Z_CONV_C = Z_CONV_B + CONV_DIM
Z_CONV_H = Z_CONV_C + CONV_DIM
Z_NA_Q = Z_CONV_H + CONV_DIM
Z_NA_K = Z_NA_Q + NA_DIM
Z_NA_V = Z_NA_K + NA_DIM
Z_ROPE = Z_NA_V + NA_DIM
Z_DIM = Z_ROPE + 2 * MLA_ROPE

QK_PAD = 2 * LANES
WQ_HEAD = 3 * LANES

TM = 1024
TM_OUT = 512
TM_PROJ = 512
TN_IN = Z_DIM // 3
TN_ADA = 1024
TF = 512
TQ = 256
NA_RQ = 4
NA_RK = NA_RQ + NA_WIN_R - 1
NA_NQ = NA_RQ * GRID_W
NA_NK = NA_RK * GRID_W
NA_BLOCKS = GRID_ROWS // NA_RQ

F32 = jnp.float32
BF16 = jnp.bfloat16


def _params(semantics, vmem_bytes):
    assert vmem_bytes <= V7X_VMEM_BUDGET, vmem_bytes
    return pltpu.CompilerParams(dimension_semantics=semantics, vmem_limit_bytes=int(vmem_bytes))


def _nbytes(shape, dtype):
    return int(np.prod(shape)) * jnp.dtype(dtype).itemsize


def _dot(a, b):
    return jnp.dot(a, b, preferred_element_type=F32)


def _dot_nt(a, b):
    return lax.dot_general(a, b, (((1,), (1,)), ((), ())), preferred_element_type=F32)


def _mod_spec(layer, chunk, row_of_tile):
    base = layer * MOD_ROWS * N_MOD + chunk
    return pl.BlockSpec((None, 1, D_MODEL), lambda i, *_: (base + row_of_tile(i) * N_MOD, 0, 0))


def _layer_norm(r, g, b):
    mu = jnp.mean(r, axis=-1, keepdims=True)
    c = r - mu
    var = jnp.mean(c * c, axis=-1, keepdims=True)
    return c * lax.rsqrt(var + LN_EPS) * g + b


def _ada_kernel(cc_ref, w_ref, b_ref, o_ref):
    cc = cc_ref[...]
    s = (cc * jax.nn.sigmoid(cc)).astype(BF16)
    o_ref[...] = _dot(s, w_ref[...].astype(BF16)) + b_ref[...]


def _ada(cc, ada_w, ada_b):
    n = N_MOD * D_MODEL
    vmem = 2 * (_nbytes((D_MODEL, TN_ADA), F32) + _nbytes((MOD_ROWS, TN_ADA), F32) * 2
                + _nbytes((MOD_ROWS, D_MODEL), F32)) + _nbytes((D_MODEL, TN_ADA), BF16) * 2
    return pl.pallas_call(
        _ada_kernel,
        out_shape=jax.ShapeDtypeStruct((DEPTH, MOD_ROWS, n), F32),
        grid=(DEPTH, n // TN_ADA),
        in_specs=[
            pl.BlockSpec((MOD_ROWS, D_MODEL), lambda l, j: (0, 0)),
            pl.BlockSpec((None, D_MODEL, TN_ADA), lambda l, j: (l, 0, j)),
            pl.BlockSpec((None, 1, TN_ADA), lambda l, j: (l, 0, j)),
        ],
        out_specs=pl.BlockSpec((None, MOD_ROWS, TN_ADA), lambda l, j: (l, 0, j)),
        compiler_params=_params(("parallel", "parallel"), vmem),
        name="ada_mod",
    )(cc, ada_w, ada_b.reshape(DEPTH, 1, n))


def _in_proj_kernel(sh_ref, sc_ref, x_ref, w_ref, o_ref, xm_ref):
    @pl.when(pl.program_id(1) == 0)
    def _():
        xm_ref[...] = (x_ref[...] * (1.0 + sc_ref[...]) + sh_ref[...]).astype(BF16)

    o_ref[...] = _dot(xm_ref[...], w_ref[...]).astype(BF16)


def _in_proj(x, mods, w_in_p, layer, row_of_tile):
    rows = x.shape[0]
    vmem = (2 * (_nbytes((TM, D_MODEL), F32) + _nbytes((D_MODEL, TN_IN), BF16) + _nbytes((TM, TN_IN), BF16))
            + _nbytes((TM, D_MODEL), BF16) * 2 + _nbytes((TM, TN_IN), F32))
    return pl.pallas_call(
        _in_proj_kernel,
        out_shape=jax.ShapeDtypeStruct((rows, Z_DIM), BF16),
        grid=(rows // TM, Z_DIM // TN_IN),
        in_specs=[
            _mod_spec(layer, 0, row_of_tile),
            _mod_spec(layer, 1, row_of_tile),
            pl.BlockSpec((TM, D_MODEL), lambda i, j: (i, 0)),
            pl.BlockSpec((D_MODEL, TN_IN), lambda i, j: (0, j)),
        ],
        out_specs=pl.BlockSpec((TM, TN_IN), lambda i, j: (i, j)),
        scratch_shapes=[pltpu.VMEM((TM, D_MODEL), BF16)],
        compiler_params=_params(("parallel", "arbitrary"), vmem),
        name="in_proj",
    )(mods, mods, x, w_in_p)


def _rms_norm(x, g):
    return x * lax.rsqrt(jnp.mean(x * x, axis=-1, keepdims=True) + RMS_EPS) * g


def _mla_proj_kernel(z_ref, slot_ref, gq_ref, gkv_ref, wq_ref, wkv_ref, ta_ref, tb_ref, q_ref, k_ref, v_ref):
    cq = z_ref[:, Z_Q:Z_Q + MLA_Q_RANK].astype(F32)
    ckv = z_ref[:, Z_KV:Z_KV + MLA_KV_RANK].astype(F32)
    qa = _dot(_rms_norm(cq, gq_ref[...]).astype(BF16), wq_ref[...])
    kv = _dot(_rms_norm(ckv, gkv_ref[...]).astype(BF16), wkv_ref[...])
    ta = ta_ref[...]
    tb = tb_ref[...]
    y = slot_ref[...].astype(F32) * ta
    kro = y + pltpu.roll(y, MLA_ROPE, 1)
    lane = lax.broadcasted_iota(jnp.int32, kro.shape, 1)
    kro = jnp.where(lane < MLA_ROPE, kro, 0.0).astype(BF16)
    for h in range(MLA_HEADS):
        qb = h * WQ_HEAD
        ob = h * QK_PAD
        q_ref[:, ob:ob + LANES] = qa[:, qb:qb + LANES].astype(BF16)
        q_ref[:, ob + LANES:ob + QK_PAD] = (
            qa[:, qb + LANES:qb + 2 * LANES] * ta + qa[:, qb + 2 * LANES:qb + 3 * LANES] * tb).astype(BF16)
        k_ref[:, ob:ob + LANES] = kv[:, ob:ob + LANES].astype(BF16)
        k_ref[:, ob + LANES:ob + QK_PAD] = kro
        v_ref[:, h * MLA_V:(h + 1) * MLA_V] = kv[:, ob + LANES:ob + QK_PAD].astype(BF16)


def _mla_proj(z, g_q, g_kv, wq_all, wkv, ta, tb, table_tile):
    rows = z.shape[0]
    zw = Z_KV + MLA_KV_RANK
    hq = MLA_HEADS * QK_PAD
    hv = MLA_HEADS * MLA_V
    tm = TM_PROJ
    vmem = (2 * (_nbytes((tm, zw), BF16) + _nbytes((tm, LANES), BF16) + 2 * _nbytes((tm, LANES), F32)
                 + _nbytes(wq_all.shape, BF16) + _nbytes(wkv.shape, BF16)
                 + 2 * _nbytes((tm, hq), BF16) + _nbytes((tm, hv), BF16))
            + _nbytes((tm, MLA_HEADS * WQ_HEAD), F32) + _nbytes((tm, hq), F32) + 4 * _nbytes((tm, zw), F32))
    return pl.pallas_call(
        _mla_proj_kernel,
        out_shape=(jax.ShapeDtypeStruct((rows, hq), BF16), jax.ShapeDtypeStruct((rows, hq), BF16),
                   jax.ShapeDtypeStruct((rows, hv), BF16)),
        grid=(rows // tm,),
        in_specs=[
            pl.BlockSpec((tm, zw), lambda i: (i, 0)),
            pl.BlockSpec((tm, 2 * MLA_ROPE), lambda i: (i, Z_ROPE // (2 * MLA_ROPE))),
            pl.BlockSpec((1, MLA_Q_RANK), lambda i: (0, 0)),
            pl.BlockSpec((1, MLA_KV_RANK), lambda i: (0, 0)),
            pl.BlockSpec(wq_all.shape, lambda i: (0, 0)),
            pl.BlockSpec(wkv.shape, lambda i: (0, 0)),
            pl.BlockSpec((tm, LANES), lambda i: (table_tile(i), 0)),
            pl.BlockSpec((tm, LANES), lambda i: (table_tile(i), 0)),
        ],
        out_specs=(pl.BlockSpec((tm, hq), lambda i: (i, 0)), pl.BlockSpec((tm, hq), lambda i: (i, 0)),
                   pl.BlockSpec((tm, hv), lambda i: (i, 0))),
        compiler_params=_params(("parallel",), vmem),
        name="mla_proj",
    )(z, z, g_q, g_kv, wq_all, wkv, ta, tb)


def _softmax_pv(scores, values):
    m = functools.reduce(jnp.maximum, [jnp.max(s, axis=-1, keepdims=True) for s in scores])
    ps = [jnp.exp(s - m) for s in scores]
    denom = functools.reduce(jnp.add, [jnp.sum(p, axis=-1, keepdims=True) for p in ps])
    acc = functools.reduce(jnp.add, [_dot(p.astype(BF16), v) for p, v in zip(ps, values)])
    return acc / denom


def _ctx_self_attention(qc_ref, kc_ref, vc_ref, oc_ref, scale):
    s = _dot_nt(qc_ref[...], kc_ref[...]) * scale
    oc_ref[...] = _softmax_pv([s], [vc_ref[...]]).astype(BF16)


def _mla_attn_kernel(*refs, with_ctx):
    if with_ctx:
        q_ref, kl_ref, vl_ref, kc_ref, vc_ref, qc_ref, o_ref, oc_ref = refs
    else:
        q_ref, kl_ref, vl_ref, kc_ref, vc_ref, o_ref = refs
    scale = MLA_QK ** -0.5

    def q_tile(t, carry):
        r0 = pl.multiple_of(t * TQ, TQ)
        q = q_ref[pl.ds(r0, TQ), :]
        s_lat = _dot_nt(q, kl_ref[...]) * scale
        s_ctx = _dot_nt(q, kc_ref[...]) * scale
        o = _softmax_pv([s_lat, s_ctx], [vl_ref[...], vc_ref[...]])
        o_ref[pl.ds(r0, TQ), :] = o.astype(BF16)
        return carry

    lax.fori_loop(0, SEQ // TQ, q_tile, 0)
    if with_ctx:
        _ctx_self_attention(qc_ref, kc_ref, vc_ref, oc_ref, scale)


def _mla_attn(ql, kl, vl, qc, kc, vc, with_ctx):
    head = lambda b, h: (b, h)
    in_specs = [
        pl.BlockSpec((SEQ, QK_PAD), head), pl.BlockSpec((SEQ, QK_PAD), head), pl.BlockSpec((SEQ, MLA_V), head),
        pl.BlockSpec((CTX_LEN, QK_PAD), head), pl.BlockSpec((CTX_LEN, MLA_V), head),
    ]
    args = [ql, kl, vl, kc, vc]
    out_shape = [jax.ShapeDtypeStruct((BATCH * SEQ, MLA_HEADS * MLA_V), BF16)]
    out_specs = [pl.BlockSpec((SEQ, MLA_V), head)]
    if with_ctx:
        in_specs.append(pl.BlockSpec((CTX_LEN, QK_PAD), head))
        args.append(qc)
        out_shape.append(jax.ShapeDtypeStruct((BATCH * CTX_LEN, MLA_HEADS * MLA_V), BF16))
        out_specs.append(pl.BlockSpec((CTX_LEN, MLA_V), head))
    vmem = (2 * (3 * _nbytes((SEQ, QK_PAD), BF16) + 2 * _nbytes((SEQ, MLA_V), BF16))
            + 4 * _nbytes((TQ, SEQ + CTX_LEN), F32))
    return pl.pallas_call(
        functools.partial(_mla_attn_kernel, with_ctx=with_ctx),
        out_shape=tuple(out_shape), grid=(BATCH, MLA_HEADS), in_specs=in_specs, out_specs=tuple(out_specs),
        compiler_params=_params(("parallel", "parallel"), vmem),
        name="mla_attn_ctx" if with_ctx else "mla_attn",
    )(*args)


def _conv_kernel(gb_ref, gc_ref, h_ref, w_ref, o_ref):
    n = o_ref.shape[0]
    u = gc_ref[...].astype(F32) * h_ref[...].astype(F32)
    row = lax.broadcasted_iota(jnp.int32, u.shape, 0)
    u_prev = jnp.where(row == 0, 0.0, pltpu.roll(u, 1, 0))
    u_next = jnp.where(row == n - 1, 0.0, pltpu.roll(u, n - 1, 0))
    y = u_prev * w_ref[0:1, :] + u * w_ref[1:2, :] + u_next * w_ref[2:3, :]
    o_ref[...] = (gb_ref[...].astype(F32) * y).astype(BF16)


def _conv(z, conv_w, seq_len):
    rows = z.shape[0]
    blk = (seq_len, CONV_DIM)
    vmem = 2 * 4 * _nbytes(blk, BF16) + 8 * _nbytes(blk, F32)
    return pl.pallas_call(
        _conv_kernel,
        out_shape=jax.ShapeDtypeStruct((rows, CONV_DIM), BF16),
        grid=(rows // seq_len,),
        in_specs=[
            pl.BlockSpec(blk, lambda s: (s, Z_CONV_B // CONV_DIM)),
            pl.BlockSpec(blk, lambda s: (s, Z_CONV_C // CONV_DIM)),
            pl.BlockSpec(blk, lambda s: (s, Z_CONV_H // CONV_DIM)),
            pl.BlockSpec(conv_w.shape, lambda s: (0, 0)),
        ],
        out_specs=pl.BlockSpec(blk, lambda s: (s, 0)),
        compiler_params=_params(("parallel",), vmem),
        name="short_conv",
    )(z, z, z, conv_w)


def _na_band_start(block):
    return min(max(block * NA_RQ - NA_WIN_R // 2, 0), GRID_ROWS - NA_RK)


def _na_bias_kind(block):
    return 0 if block == 0 else (2 if block == NA_BLOCKS - 1 else 1)


def _na_row_offsets():
    n_dr = 2 * NA_WIN_R - 1
    idx = np.full((3, NA_RQ, NA_RK), n_dr, np.int32)
    for kind, block in ((0, 0), (1, 2), (2, NA_BLOCKS - 1)):
        for qr in range(NA_RQ):
            r = block * NA_RQ + qr
            r0 = min(max(r - NA_WIN_R // 2, 0), GRID_ROWS - NA_WIN_R)
            for ki in range(NA_RK):
                kr = _na_band_start(block) + ki
                if r0 <= kr < r0 + NA_WIN_R:
                    idx[kind, qr, ki] = kr - r + NA_WIN_R - 1
    return idx


def _na_bias(rpb):
    n_dr = 2 * NA_WIN_R - 1
    neg = -jnp.inf
    side = GRID_W - 1 - (NA_WIN_C - 1)
    w = jnp.pad(rpb, ((0, 0), (0, 0), (side, side + 1)), constant_values=neg)
    flat = jnp.tile(w, (1, 1, GRID_W))
    lo = GRID_W - 1
    t = flat[:, :, lo:lo + GRID_W * (2 * GRID_W - 1)].reshape(NA_HEADS, n_dr, GRID_W, 2 * GRID_W - 1)[..., :GRID_W]
    col = np.arange(GRID_W)
    c0 = np.clip(col - NA_WIN_C // 2, 0, GRID_W - NA_WIN_C)
    col_ok = (col[None, :] >= c0[:, None]) & (col[None, :] < c0[:, None] + NA_WIN_C)
    t = jnp.where(col_ok[None, None], t, neg)
    t = jnp.concatenate([t, jnp.full((NA_HEADS, 1, GRID_W, GRID_W), neg, F32)], axis=1)
    blocks = jnp.stack([t[:, d] for d in _na_row_offsets().reshape(-1)], axis=1)
    blocks = blocks.reshape(NA_HEADS, 3, NA_RQ, NA_RK, GRID_W, GRID_W)
    return blocks.transpose(0, 1, 2, 4, 3, 5).reshape(NA_HEADS, 3, NA_NQ, NA_NK)


def _na_kernel(*refs, with_ctx):
    if with_ctx:
        q_ref, k_ref, v_ref, kc_ref, vc_ref, bias_ref, qc_ref, o_ref, oc_ref = refs
    else:
        q_ref, k_ref, v_ref, kc_ref, vc_ref, bias_ref, o_ref = refs
    scale = NA_HEAD_DIM ** -0.5
    for block in range(NA_BLOCKS):
        q = q_ref[block * NA_NQ:(block + 1) * NA_NQ, :]
        k0 = _na_band_start(block) * GRID_W
        s_loc = _dot_nt(q, k_ref[k0:k0 + NA_NK, :]) * scale + bias_ref[_na_bias_kind(block)]
        s_ctx = _dot_nt(q, kc_ref[...]) * scale
        o = _softmax_pv([s_loc, s_ctx], [v_ref[k0:k0 + NA_NK, :], vc_ref[...]])
        o_ref[block * NA_NQ:(block + 1) * NA_NQ, :] = o.astype(BF16)
    if with_ctx:
        _ctx_self_attention(qc_ref, kc_ref, vc_ref, oc_ref, scale)


def _na_attn(zl, zc, bias, with_ctx):
    d = NA_HEAD_DIM
    col = lambda off: (lambda b, h: (b, off // d + h))
    in_specs = [
        pl.BlockSpec((SEQ, d), col(Z_NA_Q)), pl.BlockSpec((SEQ, d), col(Z_NA_K)), pl.BlockSpec((SEQ, d), col(Z_NA_V)),
        pl.BlockSpec((CTX_LEN, d), col(Z_NA_K)), pl.BlockSpec((CTX_LEN, d), col(Z_NA_V)),
        pl.BlockSpec((None, 3, NA_NQ, NA_NK), lambda b, h: (h, 0, 0, 0)),
    ]
    args = [zl, zl, zl, zc, zc, bias]
    out_shape = [jax.ShapeDtypeStruct((BATCH * SEQ, NA_DIM), BF16)]
    out_specs = [pl.BlockSpec((SEQ, d), lambda b, h: (b, h))]
    if with_ctx:
        in_specs.append(pl.BlockSpec((CTX_LEN, d), col(Z_NA_Q)))
        args.append(zc)
        out_shape.append(jax.ShapeDtypeStruct((BATCH * CTX_LEN, NA_DIM), BF16))
        out_specs.append(pl.BlockSpec((CTX_LEN, d), lambda b, h: (b, h)))
    vmem = (2 * (5 * _nbytes((SEQ, d), BF16) + _nbytes((3, NA_NQ, NA_NK), F32))
            + 8 * _nbytes((NA_NQ, NA_NK + CTX_LEN), F32))
    return pl.pallas_call(
        functools.partial(_na_kernel, with_ctx=with_ctx),
        out_shape=tuple(out_shape), grid=(BATCH, NA_HEADS), in_specs=in_specs, out_specs=tuple(out_specs),
        compiler_params=_params(("parallel", "parallel"), vmem),
        name="na_attn_ctx" if with_ctx else "na_attn",
    )(*args)


def _out_proj_kernel(g_ref, a_ref, c_ref, n_ref, x_ref, w_ref, lng_ref, lnb_ref, o_ref):
    a_w = MLA_HEADS * MLA_V
    y = (_dot(a_ref[...], w_ref[0:a_w, :]) + _dot(c_ref[...], w_ref[a_w:a_w + CONV_DIM, :])
         + _dot(n_ref[...], w_ref[a_w + CONV_DIM:, :]))
    r = DEEPNORM_ALPHA * x_ref[...] + g_ref[...] * y
    o_ref[...] = _layer_norm(r, lng_ref[...], lnb_ref[...])


def _out_proj(a, cv, n, x, mods, w_out, ln_g, ln_b, layer, row_of_tile):
    rows = x.shape[0]
    tm = TM_OUT
    row = lambda i: (i, 0)
    const = lambda i: (0, 0)
    vmem = (2 * (2 * _nbytes((tm, D_MODEL), F32) + _nbytes((tm, D_MODEL), BF16) + _nbytes(w_out.shape, BF16))
            + 3 * _nbytes((tm, D_MODEL), F32))
    return pl.pallas_call(
        _out_proj_kernel,
        out_shape=jax.ShapeDtypeStruct((rows, D_MODEL), F32),
        grid=(rows // tm,),
        in_specs=[
            _mod_spec(layer, 2, row_of_tile),
            pl.BlockSpec((tm, a.shape[1]), row), pl.BlockSpec((tm, cv.shape[1]), row),
            pl.BlockSpec((tm, n.shape[1]), row), pl.BlockSpec((tm, D_MODEL), row),
            pl.BlockSpec(w_out.shape, const),
            pl.BlockSpec((1, D_MODEL), const), pl.BlockSpec((1, D_MODEL), const),
        ],
        out_specs=pl.BlockSpec((tm, D_MODEL), row),
        compiler_params=_params(("parallel",), vmem),
        name="out_proj",
    )(mods, a, cv, n, x, w_out, ln_g, ln_b)


def _ffn_kernel(sh_ref, sc_ref, g_ref, x_ref, wg_ref, wu_ref, wd_ref, lng_ref, lnb_ref, o_ref, xm_ref):
    f = pl.program_id(1)

    @pl.when(f == 0)
    def _():
        x = x_ref[...]
        xm_ref[...] = (x * (1.0 + sc_ref[...]) + sh_ref[...]).astype(BF16)
        o_ref[...] = DEEPNORM_ALPHA * x

    xm = xm_ref[...]
    gate = _dot(xm, wg_ref[...])
    up = _dot(xm, wu_ref[...])
    hidden = (gate * jax.nn.sigmoid(gate) * up).astype(BF16)
    o_ref[...] += g_ref[...] * _dot(hidden, wd_ref[...])

    @pl.when(f == pl.num_programs(1) - 1)
    def _():
        o_ref[...] = _layer_norm(o_ref[...], lng_ref[...], lnb_ref[...])


def _ffn(x, mods, w_gate, w_up, w_down, ln_g, ln_b, layer, row_of_tile):
    rows = x.shape[0]
    const = lambda i, f: (0, 0)
    vmem = (3 * _nbytes((TM, D_MODEL), F32) + 2 * 3 * _nbytes((D_MODEL, TF), BF16)
            + _nbytes((TM, D_MODEL), BF16) + 3 * _nbytes((TM, TF), F32) + _nbytes((TM, D_MODEL), F32))
    return pl.pallas_call(
        _ffn_kernel,
        out_shape=jax.ShapeDtypeStruct((rows, D_MODEL), F32),
        grid=(rows // TM, D_FF // TF),
        in_specs=[
            _mod_spec(layer, 3, row_of_tile), _mod_spec(layer, 4, row_of_tile), _mod_spec(layer, 5, row_of_tile),
            pl.BlockSpec((TM, D_MODEL), lambda i, f: (i, 0), pipeline_mode=pl.Buffered(1)),
            pl.BlockSpec((D_MODEL, TF), lambda i, f: (0, f)),
            pl.BlockSpec((D_MODEL, TF), lambda i, f: (0, f)),
            pl.BlockSpec((TF, D_MODEL), lambda i, f: (f, 0)),
            pl.BlockSpec((1, D_MODEL), const), pl.BlockSpec((1, D_MODEL), const),
        ],
        out_specs=pl.BlockSpec((TM, D_MODEL), lambda i, f: (i, 0)),
        scratch_shapes=[pltpu.VMEM((TM, D_MODEL), BF16)],
        compiler_params=_params(("parallel", "arbitrary"), vmem),
        name="ffn",
    )(mods, mods, mods, x, w_gate, w_up, w_down, ln_g, ln_b)


def _rotate_half_cols(w):
    a1, a2, b1, b2 = jnp.split(w, 4, axis=-1)
    return jnp.concatenate([-a2, a1, -b2, b1], axis=-1)


def _relayout_w_in(w):
    rope = w[:, OFF_ROPE:OFF_ROPE + MLA_ROPE]
    parts = [w[:, OFF_MLA_Q:OFF_ROPE], w[:, OFF_CONV_B:IN_DIM], rope, _rotate_half_cols(rope)]
    return jnp.concatenate(parts, axis=1).astype(BF16)


def _relayout_wq(w):
    w = w.reshape(MLA_Q_RANK, MLA_HEADS, MLA_QK)
    nope, rope = w[..., :MLA_NOPE], w[..., MLA_NOPE:]
    zeros = jnp.zeros_like(rope)
    cols = jnp.concatenate([nope, rope, zeros, _rotate_half_cols(rope), zeros], axis=-1)
    return cols.reshape(MLA_Q_RANK, MLA_HEADS * WQ_HEAD).astype(BF16)


def _rope_tables():
    t = jnp.arange(SEQ)
    row = (t // GRID_W).astype(F32)
    col = (t % GRID_W).astype(F32)
    n_freq = MLA_ROPE // 4
    inv = ROPE_THETA ** (-jnp.arange(n_freq, dtype=F32) / n_freq)
    ar = row[:, None] * inv
    ac = col[:, None] * inv
    ang = jnp.concatenate([ar, ar, ac, ac], axis=-1)
    cos, sin = jnp.cos(ang), jnp.sin(ang)
    ones, zeros = jnp.ones((TM_PROJ, MLA_ROPE), F32), jnp.zeros((TM_PROJ, MLA_ROPE), F32)
    lat = (jnp.concatenate([cos, sin], axis=1), jnp.concatenate([sin, cos], axis=1))
    ctx = (jnp.concatenate([ones, zeros], axis=1), jnp.concatenate([zeros, ones], axis=1))
    return lat, ctx


def kernel(x, c, ctx, c_ctx, ada_w, ada_b, w_in, mla_q_norm, mla_wq_b, mla_kv_norm, mla_wkv_b, conv_w, na_rpb,
           w_out, ln1_g, ln1_b, ffn_w_gate, ffn_w_up, ffn_w_down, ln2_g, ln2_b):
    assert x.shape == (BATCH, SEQ, D_MODEL) and ctx.shape == (BATCH, CTX_LEN, D_MODEL)
    cc = jnp.concatenate([c, c_ctx[None, :], jnp.zeros((MOD_ROWS - BATCH - 1, D_MODEL), F32)], axis=0)
    mods = _ada(cc, ada_w, ada_b).reshape(DEPTH * MOD_ROWS * N_MOD, 1, D_MODEL)
    (ta_lat, tb_lat), (ta_ctx, tb_ctx) = _rope_tables()
    row2d = lambda v: v.reshape(1, -1)

    xl = x.reshape(BATCH * SEQ, D_MODEL)
    xc = ctx.reshape(BATCH * CTX_LEN, D_MODEL)
    lat_row = lambda tm: (lambda i: i // (SEQ // tm))
    ctx_row = lambda i: CTX_MOD_ROW
    for l in range(DEPTH):
        last = l == DEPTH - 1
        w_in_p = _relayout_w_in(w_in[l])
        wq_all = _relayout_wq(mla_wq_b[l])
        wkv = mla_wkv_b[l].astype(BF16)
        g_q, g_kv = row2d(mla_q_norm[l]), row2d(mla_kv_norm[l])
        bias = _na_bias(na_rpb[l])

        zl = _in_proj(xl, mods, w_in_p, l, lat_row(TM))
        zc = _in_proj(xc, mods, w_in_p, l, ctx_row)
        ql, kl, vl = _mla_proj(zl, g_q, g_kv, wq_all, wkv, ta_lat, tb_lat, lambda i: i % (SEQ // TM_PROJ))
        qc, kc, vc = _mla_proj(zc, g_q, g_kv, wq_all, wkv, ta_ctx, tb_ctx, lambda i: 0)
        a = _mla_attn(ql, kl, vl, qc, kc, vc, with_ctx=not last)
        n = _na_attn(zl, zc, bias, with_ctx=not last)
        cv_l = _conv(zl, conv_w[l], SEQ)

        w_o = w_out[l].astype(BF16)
        w_g, w_u, w_d = ffn_w_gate[l].astype(BF16), ffn_w_up[l].astype(BF16), ffn_w_down[l].astype(BF16)
        ln1 = (row2d(ln1_g[l]), row2d(ln1_b[l]))
        ln2 = (row2d(ln2_g[l]), row2d(ln2_b[l]))
        x1 = _out_proj(a[0], cv_l, n[0], xl, mods, w_o, *ln1, l, lat_row(TM_OUT))
        xl = _ffn(x1, mods, w_g, w_u, w_d, *ln2, l, lat_row(TM))
        if not last:
            cv_c = _conv(zc, conv_w[l], CTX_LEN)
            x1c = _out_proj(a[1], cv_c, n[1], xc, mods, w_o, *ln1, l, ctx_row)
            xc = _ffn(x1c, mods, w_g, w_u, w_d, *ln2, l, ctx_row)
    return xl.reshape(BATCH, SEQ, D_MODEL)
```

```python
import functools

import numpy as np
import jax
import jax.numpy as jnp
from jax import lax
from jax.experimental import pallas as pl
from jax.experimental.pallas import tpu as pltpu

D_MODEL = 2048
BATCH = 4
SEQ = 2048
DEPTH = 2
CTX_LEN = 256
GRID_W = 64
GRID_ROWS = SEQ // GRID_W
MLA_HEADS = 6
MLA_Q_RANK = 512
MLA_KV_RANK = 512
MLA_NOPE = 128
MLA_ROPE = 64
MLA_V = 128
MLA_QK = MLA_NOPE + MLA_ROPE
CONV_DIM = 512
NA_HEADS = 6
NA_HEAD_DIM = 128
NA_DIM = NA_HEADS * NA_HEAD_DIM
NA_WIN_R = 8
NA_WIN_C = 16
OFF_MLA_Q = 0
OFF_MLA_KV = OFF_MLA_Q + MLA_Q_RANK
OFF_ROPE = OFF_MLA_KV + MLA_KV_RANK
OFF_CONV_B = OFF_ROPE + MLA_ROPE
OFF_CONV_C = OFF_CONV_B + CONV_DIM
OFF_CONV_H = OFF_CONV_C + CONV_DIM
OFF_NA_Q = OFF_CONV_H + CONV_DIM
OFF_NA_K = OFF_NA_Q + NA_DIM
OFF_NA_V = OFF_NA_K + NA_DIM
IN_DIM = OFF_NA_V + NA_DIM
D_FF = -(-8 * D_MODEL // (3 * 256)) * 256
ROPE_THETA = 10000.0
LN_EPS = 1e-6
RMS_EPS = 1e-6
DEEPNORM_ALPHA = (2 * DEPTH) ** 0.25
N_MOD = 6
MOD_ROWS = 8
CTX_MOD_ROW = BATCH

LANES = 128
V7X_VMEM_BYTES = 64 * 1024 * 1024
V7X_VMEM_BUDGET = 58 * 1024 * 1024

Z_Q = 0
Z_KV = Z_Q + MLA_Q_RANK
Z_CONV_B = Z_KV + MLA_KV_RANK
Z_CONV_C = Z_CONV_B + CONV_DIM
Z_CONV_H = Z_CONV_C + CONV_DIM
Z_NA_Q = Z_CONV_H + CONV_DIM
Z_NA_K = Z_NA_Q + NA_DIM
Z_NA_V = Z_NA_K + NA_DIM
Z_ROPE = Z_NA_V + NA_DIM
Z_DIM = Z_ROPE + 2 * MLA_ROPE

QK_PAD = 2 * LANES
WQ_HEAD = 3 * LANES

TM = 1024
TM_OUT = 512
TM_PROJ = 512
TR_PREP = 256
TN_IN = Z_DIM // 3
TN_ADA = 1024
TF = 512
TQ = 256
Q_TILE_UNROLL = 2
NA_RQ = 4
NA_RK = NA_RQ + NA_WIN_R - 1
NA_NQ = NA_RQ * GRID_W
NA_NK = NA_RK * GRID_W
NA_BLOCKS = GRID_ROWS // NA_RQ

F32 = jnp.float32
BF16 = jnp.bfloat16


def _params(semantics, vmem_bytes):
    assert vmem_bytes <= V7X_VMEM_BUDGET, vmem_bytes
    return pltpu.CompilerParams(dimension_semantics=semantics, vmem_limit_bytes=int(vmem_bytes))


def _nbytes(shape, dtype):
    return int(np.prod(shape)) * jnp.dtype(dtype).itemsize


def _dot(a, b):
    return jnp.dot(a, b, preferred_element_type=F32)


def _dot_nt(a, b):
    return lax.dot_general(a, b, (((1,), (1,)), ((), ())), preferred_element_type=F32)


def _mod_spec(layer, chunk, row_of_tile):
    base = layer * MOD_ROWS * N_MOD + chunk
    return pl.BlockSpec((None, 1, D_MODEL), lambda i, *_: (base + row_of_tile(i) * N_MOD, 0, 0))


def _layer_norm(r, g, b):
    mu = jnp.mean(r, axis=-1, keepdims=True)
    c = r - mu
    var = jnp.mean(c * c, axis=-1, keepdims=True)
    return c * lax.rsqrt(var + LN_EPS) * g + b


def _ada_kernel(cc_ref, w_ref, b_ref, o_ref):
    cc = cc_ref[...]
    s = (cc * jax.nn.sigmoid(cc)).astype(BF16)
    o_ref[...] = _dot(s, w_ref[...].astype(BF16)) + b_ref[...]


def _ada(cc, ada_w, ada_b):
    n = N_MOD * D_MODEL
    vmem = 2 * (_nbytes((D_MODEL, TN_ADA), F32) + _nbytes((MOD_ROWS, TN_ADA), F32) * 2
                + _nbytes((MOD_ROWS, D_MODEL), F32)) + _nbytes((D_MODEL, TN_ADA), BF16) * 2
    return pl.pallas_call(
        _ada_kernel,
        out_shape=jax.ShapeDtypeStruct((DEPTH, MOD_ROWS, n), F32),
        grid=(DEPTH, n // TN_ADA),
        in_specs=[
            pl.BlockSpec((MOD_ROWS, D_MODEL), lambda l, j: (0, 0)),
            pl.BlockSpec((None, D_MODEL, TN_ADA), lambda l, j: (l, 0, j)),
            pl.BlockSpec((None, 1, TN_ADA), lambda l, j: (l, 0, j)),
        ],
        out_specs=pl.BlockSpec((None, MOD_ROWS, TN_ADA), lambda l, j: (l, 0, j)),
        compiler_params=_params(("parallel", "parallel"), vmem),
        name="ada_mod",
    )(cc, ada_w, ada_b.reshape(DEPTH, 1, n))


def _in_proj_kernel(sh_ref, sc_ref, x_ref, w_ref, o_ref, xm_ref):
    @pl.when(pl.program_id(1) == 0)
    def _():
        xm_ref[...] = (x_ref[...] * (1.0 + sc_ref[...]) + sh_ref[...]).astype(BF16)

    o_ref[...] = _dot(xm_ref[...], w_ref[...]).astype(BF16)


def _in_proj(x, mods, w_in_p, layer, row_of_tile):
    rows = x.shape[0]
    vmem = (2 * (_nbytes((TM, D_MODEL), F32) + _nbytes((D_MODEL, TN_IN), BF16) + _nbytes((TM, TN_IN), BF16))
            + _nbytes((TM, D_MODEL), BF16) * 2 + _nbytes((TM, TN_IN), F32))
    return pl.pallas_call(
        _in_proj_kernel,
        out_shape=jax.ShapeDtypeStruct((rows, Z_DIM), BF16),
        grid=(rows // TM, Z_DIM // TN_IN),
        in_specs=[
            _mod_spec(layer, 0, row_of_tile),
            _mod_spec(layer, 1, row_of_tile),
            pl.BlockSpec((TM, D_MODEL), lambda i, j: (i, 0)),
            pl.BlockSpec((None, D_MODEL, TN_IN), lambda i, j: (layer, 0, j)),
        ],
        out_specs=pl.BlockSpec((TM, TN_IN), lambda i, j: (i, j)),
        scratch_shapes=[pltpu.VMEM((TM, D_MODEL), BF16)],
        compiler_params=_params(("parallel", "arbitrary"), vmem),
        name="in_proj",
    )(mods, mods, x, w_in_p)


def _rms_norm(x, g):
    return x * lax.rsqrt(jnp.mean(x * x, axis=-1, keepdims=True) + RMS_EPS) * g


def _mla_proj_kernel(z_ref, slot_ref, gq_ref, gkv_ref, wq_ref, wkv_ref, ta_ref, tb_ref, q_ref, k_ref, v_ref):
    cq = z_ref[:, Z_Q:Z_Q + MLA_Q_RANK].astype(F32)
    ckv = z_ref[:, Z_KV:Z_KV + MLA_KV_RANK].astype(F32)
    qa = _dot(_rms_norm(cq, gq_ref[...]).astype(BF16), wq_ref[...])
    kv = _dot(_rms_norm(ckv, gkv_ref[...]).astype(BF16), wkv_ref[...])
    ta = ta_ref[...]
    tb = tb_ref[...]
    y = slot_ref[...].astype(F32) * ta
    kro = y + pltpu.roll(y, MLA_ROPE, 1)
    lane = lax.broadcasted_iota(jnp.int32, kro.shape, 1)
    kro = jnp.where(lane < MLA_ROPE, kro, 0.0).astype(BF16)
    for h in range(MLA_HEADS):
        qb = h * WQ_HEAD
        ob = h * QK_PAD
        q_ref[:, ob:ob + LANES] = qa[:, qb:qb + LANES].astype(BF16)
        q_ref[:, ob + LANES:ob + QK_PAD] = (
            qa[:, qb + LANES:qb + 2 * LANES] * ta + qa[:, qb + 2 * LANES:qb + 3 * LANES] * tb).astype(BF16)
        k_ref[:, ob:ob + LANES] = kv[:, ob:ob + LANES].astype(BF16)
        k_ref[:, ob + LANES:ob + QK_PAD] = kro
        v_ref[:, h * MLA_V:(h + 1) * MLA_V] = kv[:, ob + LANES:ob + QK_PAD].astype(BF16)


def _mla_proj(z, g_q, g_kv, wq_all, wkv, ta, tb, layer, table_tile):
    rows = z.shape[0]
    per_layer = lambda i: (layer, 0, 0)
    zw = Z_KV + MLA_KV_RANK
    hq = MLA_HEADS * QK_PAD
    hv = MLA_HEADS * MLA_V
    tm = TM_PROJ
    vmem = (2 * (_nbytes((tm, zw), BF16) + _nbytes((tm, LANES), BF16) + 2 * _nbytes((tm, LANES), F32)
                 + _nbytes(wq_all.shape[1:], BF16) + _nbytes(wkv.shape[1:], BF16)
                 + 2 * _nbytes((tm, hq), BF16) + _nbytes((tm, hv), BF16))
            + _nbytes((tm, MLA_HEADS * WQ_HEAD), F32) + _nbytes((tm, hq), F32) + 4 * _nbytes((tm, zw), F32))
    return pl.pallas_call(
        _mla_proj_kernel,
        out_shape=(jax.ShapeDtypeStruct((rows, hq), BF16), jax.ShapeDtypeStruct((rows, hq), BF16),
                   jax.ShapeDtypeStruct((rows, hv), BF16)),
        grid=(rows // tm,),
        in_specs=[
            pl.BlockSpec((tm, zw), lambda i: (i, 0)),
            pl.BlockSpec((tm, 2 * MLA_ROPE), lambda i: (i, Z_ROPE // (2 * MLA_ROPE))),
            pl.BlockSpec((None, 1, MLA_Q_RANK), per_layer),
            pl.BlockSpec((None, 1, MLA_KV_RANK), per_layer),
            pl.BlockSpec((None,) + wq_all.shape[1:], per_layer),
            pl.BlockSpec((None,) + wkv.shape[1:], per_layer),
            pl.BlockSpec((tm, LANES), lambda i: (table_tile(i), 0)),
            pl.BlockSpec((tm, LANES), lambda i: (table_tile(i), 0)),
        ],
        out_specs=(pl.BlockSpec((tm, hq), lambda i: (i, 0)), pl.BlockSpec((tm, hq), lambda i: (i, 0)),
                   pl.BlockSpec((tm, hv), lambda i: (i, 0))),
        compiler_params=_params(("parallel",), vmem),
        name="mla_proj",
    )(z, z, g_q, g_kv, wq_all, wkv, ta, tb)


def _softmax_pv(scores, values):
    m = functools.reduce(jnp.maximum, [jnp.max(s, axis=-1, keepdims=True) for s in scores])
    ps = [jnp.exp(s - m) for s in scores]
    denom = functools.reduce(jnp.add, [jnp.sum(p, axis=-1, keepdims=True) for p in ps])
    acc = functools.reduce(jnp.add, [_dot(p.astype(BF16), v) for p, v in zip(ps, values)])
    return acc / denom


def _ctx_self_attention(qc_ref, kc_ref, vc_ref, oc_ref, scale):
    s = _dot_nt(qc_ref[...], kc_ref[...]) * scale
    oc_ref[...] = _softmax_pv([s], [vc_ref[...]]).astype(BF16)


def _mla_attn_kernel(*refs, with_ctx):
    if with_ctx:
        q_ref, kl_ref, vl_ref, kc_ref, vc_ref, qc_ref, o_ref, oc_ref = refs
    else:
        q_ref, kl_ref, vl_ref, kc_ref, vc_ref, o_ref = refs
    scale = MLA_QK ** -0.5

    def q_tile(t, carry):
        r0 = pl.multiple_of(t * TQ, TQ)
        q = q_ref[pl.ds(r0, TQ), :]
        s_lat = _dot_nt(q, kl_ref[...]) * scale
        s_ctx = _dot_nt(q, kc_ref[...]) * scale
        o = _softmax_pv([s_lat, s_ctx], [vl_ref[...], vc_ref[...]])
        o_ref[pl.ds(r0, TQ), :] = o.astype(BF16)
        return carry

    lax.fori_loop(0, SEQ // TQ, q_tile, 0, unroll=Q_TILE_UNROLL)
    if with_ctx:
        _ctx_self_attention(qc_ref, kc_ref, vc_ref, oc_ref, scale)


def _mla_attn(ql, kl, vl, qc, kc, vc, with_ctx):
    head = lambda b, h: (b, h)
    in_specs = [
        pl.BlockSpec((SEQ, QK_PAD), head), pl.BlockSpec((SEQ, QK_PAD), head), pl.BlockSpec((SEQ, MLA_V), head),
        pl.BlockSpec((CTX_LEN, QK_PAD), head), pl.BlockSpec((CTX_LEN, MLA_V), head),
    ]
    args = [ql, kl, vl, kc, vc]
    out_shape = [jax.ShapeDtypeStruct((BATCH * SEQ, MLA_HEADS * MLA_V), BF16)]
    out_specs = [pl.BlockSpec((SEQ, MLA_V), head)]
    if with_ctx:
        in_specs.append(pl.BlockSpec((CTX_LEN, QK_PAD), head))
        args.append(qc)
        out_shape.append(jax.ShapeDtypeStruct((BATCH * CTX_LEN, MLA_HEADS * MLA_V), BF16))
        out_specs.append(pl.BlockSpec((CTX_LEN, MLA_V), head))
    vmem = (2 * (3 * _nbytes((SEQ, QK_PAD), BF16) + 2 * _nbytes((SEQ, MLA_V), BF16))
            + 4 * Q_TILE_UNROLL * _nbytes((TQ, SEQ + CTX_LEN), F32))
    return pl.pallas_call(
        functools.partial(_mla_attn_kernel, with_ctx=with_ctx),
        out_shape=tuple(out_shape), grid=(BATCH, MLA_HEADS), in_specs=in_specs, out_specs=tuple(out_specs),
        compiler_params=_params(("parallel", "parallel"), vmem),
        name="mla_attn_ctx" if with_ctx else "mla_attn",
    )(*args)


def _conv_kernel(gb_ref, gc_ref, h_ref, w_ref, o_ref):
    n = o_ref.shape[0]
    u = gc_ref[...].astype(F32) * h_ref[...].astype(F32)
    row = lax.broadcasted_iota(jnp.int32, u.shape, 0)
    u_prev = jnp.where(row == 0, 0.0, pltpu.roll(u, 1, 0))
    u_next = jnp.where(row == n - 1, 0.0, pltpu.roll(u, n - 1, 0))
    y = u_prev * w_ref[0:1, :] + u * w_ref[1:2, :] + u_next * w_ref[2:3, :]
    o_ref[...] = (gb_ref[...].astype(F32) * y).astype(BF16)


def _conv(z, conv_w, layer, seq_len):
    rows = z.shape[0]
    blk = (seq_len, CONV_DIM)
    vmem = 2 * 4 * _nbytes(blk, BF16) + 8 * _nbytes(blk, F32)
    return pl.pallas_call(
        _conv_kernel,
        out_shape=jax.ShapeDtypeStruct((rows, CONV_DIM), BF16),
        grid=(rows // seq_len,),
        in_specs=[
            pl.BlockSpec(blk, lambda s: (s, Z_CONV_B // CONV_DIM)),
            pl.BlockSpec(blk, lambda s: (s, Z_CONV_C // CONV_DIM)),
            pl.BlockSpec(blk, lambda s: (s, Z_CONV_H // CONV_DIM)),
            pl.BlockSpec((None,) + conv_w.shape[1:], lambda s: (layer, 0, 0)),
        ],
        out_specs=pl.BlockSpec(blk, lambda s: (s, 0)),
        compiler_params=_params(("parallel",), vmem),
        name="short_conv",
    )(z, z, z, conv_w)


def _na_band_start(block):
    return min(max(block * NA_RQ - NA_WIN_R // 2, 0), GRID_ROWS - NA_RK)


def _na_bias_kind(block):
    return 0 if block == 0 else (2 if block == NA_BLOCKS - 1 else 1)


def _na_row_offsets():
    n_dr = 2 * NA_WIN_R - 1
    idx = np.full((3, NA_RQ, NA_RK), n_dr, np.int32)
    for kind, block in ((0, 0), (1, 2), (2, NA_BLOCKS - 1)):
        for qr in range(NA_RQ):
            r = block * NA_RQ + qr
            r0 = min(max(r - NA_WIN_R // 2, 0), GRID_ROWS - NA_WIN_R)
            for ki in range(NA_RK):
                kr = _na_band_start(block) + ki
                if r0 <= kr < r0 + NA_WIN_R:
                    idx[kind, qr, ki] = kr - r + NA_WIN_R - 1
    return idx


def _na_toeplitz(rpb):
    n_dr = 2 * NA_WIN_R - 1
    neg = -jnp.inf
    rpb = rpb.reshape(DEPTH * NA_HEADS, n_dr, 2 * NA_WIN_C - 1)
    side = GRID_W - 1 - (NA_WIN_C - 1)
    w = jnp.pad(rpb, ((0, 0), (0, 0), (side, side + 1)), constant_values=neg)
    flat = jnp.tile(w, (1, 1, GRID_W))
    lo = GRID_W - 1
    t = flat[:, :, lo:lo + GRID_W * (2 * GRID_W - 1)].reshape(-1, n_dr, GRID_W, 2 * GRID_W - 1)[..., :GRID_W]
    col = np.arange(GRID_W)
    c0 = np.clip(col - NA_WIN_C // 2, 0, GRID_W - NA_WIN_C)
    col_ok = (col[None, :] >= c0[:, None]) & (col[None, :] < c0[:, None] + NA_WIN_C)
    t = jnp.where(col_ok[None, None], t, neg)
    return jnp.pad(t, ((0, 0), (0, 1), (0, 0), (0, 0)), constant_values=neg)


def _na_bias_kernel(t_ref, o_ref):
    offsets = _na_row_offsets()
    for kind in range(3):
        for qr in range(NA_RQ):
            for ki in range(NA_RK):
                o_ref[kind, qr * GRID_W:(qr + 1) * GRID_W, ki * GRID_W:(ki + 1) * GRID_W] = (
                    t_ref[int(offsets[kind, qr, ki])])


def _na_bias(rpb):
    t = _na_toeplitz(rpb)
    vmem = 2 * (_nbytes((t.shape[1], GRID_W, LANES), F32) + _nbytes((3, NA_NQ, NA_NK + GRID_W), F32))
    return pl.pallas_call(
        _na_bias_kernel,
        out_shape=jax.ShapeDtypeStruct((t.shape[0], 3, NA_NQ, NA_NK), F32),
        grid=(t.shape[0],),
        in_specs=[pl.BlockSpec((None,) + t.shape[1:], lambda g: (g, 0, 0, 0))],
        out_specs=pl.BlockSpec((None, 3, NA_NQ, NA_NK), lambda g: (g, 0, 0, 0)),
        compiler_params=_params(("parallel",), vmem),
        name="na_bias",
    )(t)


def _na_kernel(*refs, with_ctx):
    if with_ctx:
        q_ref, k_ref, v_ref, kc_ref, vc_ref, bias_ref, qc_ref, o_ref, oc_ref = refs
    else:
        q_ref, k_ref, v_ref, kc_ref, vc_ref, bias_ref, o_ref = refs
    scale = NA_HEAD_DIM ** -0.5
    for block in range(NA_BLOCKS):
        q = q_ref[block * NA_NQ:(block + 1) * NA_NQ, :]
        k0 = _na_band_start(block) * GRID_W
        s_loc = _dot_nt(q, k_ref[k0:k0 + NA_NK, :]) * scale + bias_ref[_na_bias_kind(block)]
        s_ctx = _dot_nt(q, kc_ref[...]) * scale
        o = _softmax_pv([s_loc, s_ctx], [v_ref[k0:k0 + NA_NK, :], vc_ref[...]])
        o_ref[block * NA_NQ:(block + 1) * NA_NQ, :] = o.astype(BF16)
    if with_ctx:
        _ctx_self_attention(qc_ref, kc_ref, vc_ref, oc_ref, scale)


def _na_attn(zl, zc, bias, layer, with_ctx):
    d = NA_HEAD_DIM
    col = lambda off: (lambda b, h: (b, off // d + h))
    in_specs = [
        pl.BlockSpec((SEQ, d), col(Z_NA_Q)), pl.BlockSpec((SEQ, d), col(Z_NA_K)), pl.BlockSpec((SEQ, d), col(Z_NA_V)),
        pl.BlockSpec((CTX_LEN, d), col(Z_NA_K)), pl.BlockSpec((CTX_LEN, d), col(Z_NA_V)),
        pl.BlockSpec((None, 3, NA_NQ, NA_NK), lambda b, h: (layer * NA_HEADS + h, 0, 0, 0)),
    ]
    args = [zl, zl, zl, zc, zc, bias]
    out_shape = [jax.ShapeDtypeStruct((BATCH * SEQ, NA_DIM), BF16)]
    out_specs = [pl.BlockSpec((SEQ, d), lambda b, h: (b, h))]
    if with_ctx:
        in_specs.append(pl.BlockSpec((CTX_LEN, d), col(Z_NA_Q)))
        args.append(zc)
        out_shape.append(jax.ShapeDtypeStruct((BATCH * CTX_LEN, NA_DIM), BF16))
        out_specs.append(pl.BlockSpec((CTX_LEN, d), lambda b, h: (b, h)))
    vmem = (2 * (5 * _nbytes((SEQ, d), BF16) + _nbytes((3, NA_NQ, NA_NK), F32))
            + 8 * _nbytes((NA_NQ, NA_NK + CTX_LEN), F32))
    return pl.pallas_call(
        functools.partial(_na_kernel, with_ctx=with_ctx),
        out_shape=tuple(out_shape), grid=(BATCH, NA_HEADS), in_specs=in_specs, out_specs=tuple(out_specs),
        compiler_params=_params(("parallel", "parallel"), vmem),
        name="na_attn_ctx" if with_ctx else "na_attn",
    )(*args)


def _out_proj_kernel(g_ref, a_ref, c_ref, n_ref, x_ref, w_ref, lng_ref, lnb_ref, o_ref):
    a_w = MLA_HEADS * MLA_V
    y = (_dot(a_ref[...], w_ref[0:a_w, :]) + _dot(c_ref[...], w_ref[a_w:a_w + CONV_DIM, :])
         + _dot(n_ref[...], w_ref[a_w + CONV_DIM:, :]))
    r = DEEPNORM_ALPHA * x_ref[...] + g_ref[...] * y
    o_ref[...] = _layer_norm(r, lng_ref[...], lnb_ref[...])


def _out_proj(a, cv, n, x, mods, w_out, ln_g, ln_b, layer, row_of_tile):
    rows = x.shape[0]
    tm = TM_OUT
    row = lambda i: (i, 0)
    per_layer = lambda i: (layer, 0, 0)
    vmem = (2 * (2 * _nbytes((tm, D_MODEL), F32) + _nbytes((tm, D_MODEL), BF16) + _nbytes(w_out.shape[1:], BF16))
            + 3 * _nbytes((tm, D_MODEL), F32))
    return pl.pallas_call(
        _out_proj_kernel,
        out_shape=jax.ShapeDtypeStruct((rows, D_MODEL), F32),
        grid=(rows // tm,),
        in_specs=[
            _mod_spec(layer, 2, row_of_tile),
            pl.BlockSpec((tm, a.shape[1]), row), pl.BlockSpec((tm, cv.shape[1]), row),
            pl.BlockSpec((tm, n.shape[1]), row), pl.BlockSpec((tm, D_MODEL), row),
            pl.BlockSpec((None,) + w_out.shape[1:], per_layer),
            pl.BlockSpec((None, 1, D_MODEL), per_layer), pl.BlockSpec((None, 1, D_MODEL), per_layer),
        ],
        out_specs=pl.BlockSpec((tm, D_MODEL), row),
        compiler_params=_params(("parallel",), vmem),
        name="out_proj",
    )(mods, a, cv, n, x, w_out, ln_g, ln_b)


def _ffn_kernel(sh_ref, sc_ref, g_ref, x_ref, wg_ref, wu_ref, wd_ref, lng_ref, lnb_ref, o_ref, xm_ref):
    f = pl.program_id(1)

    @pl.when(f == 0)
    def _():
        x = x_ref[...]
        xm_ref[...] = (x * (1.0 + sc_ref[...]) + sh_ref[...]).astype(BF16)
        o_ref[...] = DEEPNORM_ALPHA * x

    xm = xm_ref[...]
    gate = _dot(xm, wg_ref[...])
    up = _dot(xm, wu_ref[...])
    hidden = (gate * jax.nn.sigmoid(gate) * up).astype(BF16)
    o_ref[...] += g_ref[...] * _dot(hidden, wd_ref[...])

    @pl.when(f == pl.num_programs(1) - 1)
    def _():
        o_ref[...] = _layer_norm(o_ref[...], lng_ref[...], lnb_ref[...])


def _ffn(x, mods, w_gate, w_up, w_down, ln_g, ln_b, layer, row_of_tile):
    rows = x.shape[0]
    per_layer = lambda i, f: (layer, 0, 0)
    vmem = (3 * _nbytes((TM, D_MODEL), F32) + 2 * 3 * _nbytes((D_MODEL, TF), BF16)
            + _nbytes((TM, D_MODEL), BF16) + 3 * _nbytes((TM, TF), F32) + _nbytes((TM, D_MODEL), F32))
    return pl.pallas_call(
        _ffn_kernel,
        out_shape=jax.ShapeDtypeStruct((rows, D_MODEL), F32),
        grid=(rows // TM, D_FF // TF),
        in_specs=[
            _mod_spec(layer, 3, row_of_tile), _mod_spec(layer, 4, row_of_tile), _mod_spec(layer, 5, row_of_tile),
            pl.BlockSpec((TM, D_MODEL), lambda i, f: (i, 0), pipeline_mode=pl.Buffered(1)),
            pl.BlockSpec((None, D_MODEL, TF), lambda i, f: (layer, 0, f)),
            pl.BlockSpec((None, D_MODEL, TF), lambda i, f: (layer, 0, f)),
            pl.BlockSpec((None, TF, D_MODEL), lambda i, f: (layer, f, 0)),
            pl.BlockSpec((None, 1, D_MODEL), per_layer), pl.BlockSpec((None, 1, D_MODEL), per_layer),
        ],
        out_specs=pl.BlockSpec((TM, D_MODEL), lambda i, f: (i, 0)),
        scratch_shapes=[pltpu.VMEM((TM, D_MODEL), BF16)],
        compiler_params=_params(("parallel", "arbitrary"), vmem),
        name="ffn",
    )(mods, mods, mods, x, w_gate, w_up, w_down, ln_g, ln_b)


def _rotate_half_cols(w):
    a1, a2, b1, b2 = jnp.split(w, 4, axis=-1)
    return jnp.concatenate([-a2, a1, -b2, b1], axis=-1)


def _w_in_prep_kernel(w_ref, o_ref):
    o_ref[:, Z_Q:Z_CONV_B] = w_ref[:, OFF_MLA_Q:OFF_ROPE].astype(BF16)
    o_ref[:, Z_CONV_B:Z_ROPE] = w_ref[:, OFF_CONV_B:IN_DIM].astype(BF16)
    x = w_ref[:, OFF_ROPE:OFF_ROPE + LANES]
    lane = lax.broadcasted_iota(jnp.int32, x.shape, 1)
    chunk = MLA_ROPE // 4
    rot = jnp.where((lane // chunk) % 2 == 0, -pltpu.roll(x, LANES - chunk, 1), pltpu.roll(x, chunk, 1))
    o_ref[:, Z_ROPE:Z_DIM] = jnp.where(lane < MLA_ROPE, x, pltpu.roll(rot, MLA_ROPE, 1)).astype(BF16)


def _w_in_prep(w_in):
    tr = TR_PREP
    vmem = 2 * (_nbytes((tr, IN_DIM + LANES), F32) + _nbytes((tr, Z_DIM), BF16)) + 2 * _nbytes((tr, Z_DIM), F32)
    return pl.pallas_call(
        _w_in_prep_kernel,
        out_shape=jax.ShapeDtypeStruct((DEPTH, D_MODEL, Z_DIM), BF16),
        grid=(DEPTH, D_MODEL // tr),
        in_specs=[pl.BlockSpec((None, tr, IN_DIM), lambda l, r: (l, r, 0))],
        out_specs=pl.BlockSpec((None, tr, Z_DIM), lambda l, r: (l, r, 0)),
        compiler_params=_params(("parallel", "parallel"), vmem),
        name="w_in_prep",
    )(w_in)


def _relayout_wq(w):
    w = w.reshape(DEPTH, MLA_Q_RANK, MLA_HEADS, MLA_QK)
    nope, rope = w[..., :MLA_NOPE], w[..., MLA_NOPE:]
    zeros = jnp.zeros_like(rope)
    cols = jnp.concatenate([nope, rope, zeros, _rotate_half_cols(rope), zeros], axis=-1)
    return cols.reshape(DEPTH, MLA_Q_RANK, MLA_HEADS * WQ_HEAD).astype(BF16)


def _rope_tables():
    t = jnp.arange(SEQ)
    row = (t // GRID_W).astype(F32)
    col = (t % GRID_W).astype(F32)
    n_freq = MLA_ROPE // 4
    inv = ROPE_THETA ** (-jnp.arange(n_freq, dtype=F32) / n_freq)
    ar = row[:, None] * inv
    ac = col[:, None] * inv
    ang = jnp.concatenate([ar, ar, ac, ac], axis=-1)
    cos, sin = jnp.cos(ang), jnp.sin(ang)
    ones, zeros = jnp.ones((TM_PROJ, MLA_ROPE), F32), jnp.zeros((TM_PROJ, MLA_ROPE), F32)
    lat = (jnp.concatenate([cos, sin], axis=1), jnp.concatenate([sin, cos], axis=1))
    ctx = (jnp.concatenate([ones, zeros], axis=1), jnp.concatenate([zeros, ones], axis=1))
    return lat, ctx


def kernel(x, c, ctx, c_ctx, ada_w, ada_b, w_in, mla_q_norm, mla_wq_b, mla_kv_norm, mla_wkv_b, conv_w, na_rpb,
           w_out, ln1_g, ln1_b, ffn_w_gate, ffn_w_up, ffn_w_down, ln2_g, ln2_b):
    assert x.shape == (BATCH, SEQ, D_MODEL) and ctx.shape == (BATCH, CTX_LEN, D_MODEL)
    cc = jnp.concatenate([c, c_ctx[None, :], jnp.zeros((MOD_ROWS - BATCH - 1, D_MODEL), F32)], axis=0)
    mods = _ada(cc, ada_w, ada_b).reshape(DEPTH * MOD_ROWS * N_MOD, 1, D_MODEL)
    (ta_lat, tb_lat), (ta_ctx, tb_ctx) = _rope_tables()
    rows3d = lambda v: v.reshape(DEPTH, 1, -1)

    w_in_p = _w_in_prep(w_in)
    wq_all = _relayout_wq(mla_wq_b)
    wkv = mla_wkv_b.astype(BF16)
    g_q, g_kv = rows3d(mla_q_norm), rows3d(mla_kv_norm)
    bias = _na_bias(na_rpb)
    w_o = w_out.astype(BF16)
    w_g, w_u, w_d = ffn_w_gate.astype(BF16), ffn_w_up.astype(BF16), ffn_w_down.astype(BF16)
    ln1 = (rows3d(ln1_g), rows3d(ln1_b))
    ln2 = (rows3d(ln2_g), rows3d(ln2_b))

    xl = x.reshape(BATCH * SEQ, D_MODEL)
    xc = ctx.reshape(BATCH * CTX_LEN, D_MODEL)
    lat_row = lambda tm: (lambda i: i // (SEQ // tm))
    ctx_row = lambda i: CTX_MOD_ROW
    for l in range(DEPTH):
        last = l == DEPTH - 1
        zl = _in_proj(xl, mods, w_in_p, l, lat_row(TM))
        zc = _in_proj(xc, mods, w_in_p, l, ctx_row)
        ql, kl, vl = _mla_proj(zl, g_q, g_kv, wq_all, wkv, ta_lat, tb_lat, l, lambda i: i % (SEQ // TM_PROJ))
        qc, kc, vc = _mla_proj(zc, g_q, g_kv, wq_all, wkv, ta_ctx, tb_ctx, l, lambda i: 0)
        a = _mla_attn(ql, kl, vl, qc, kc, vc, with_ctx=not last)
        n = _na_attn(zl, zc, bias, l, with_ctx=not last)
        cv_l = _conv(zl, conv_w, l, SEQ)
        x1 = _out_proj(a[0], cv_l, n[0], xl, mods, w_o, *ln1, l, lat_row(TM_OUT))
        xl = _ffn(x1, mods, w_g, w_u, w_d, *ln2, l, lat_row(TM))
        if not last:
            cv_c = _conv(zc, conv_w, l, CTX_LEN)
            x1c = _out_proj(a[1], cv_c, n[1], xc, mods, w_o, *ln1, l, ctx_row)
            xc = _ffn(x1c, mods, w_g, w_u, w_d, *ln2, l, ctx_row)
    return xl.reshape(BATCH, SEQ, D_MODEL)
```

```python
import functools

import numpy as np
import jax
import jax.numpy as jnp
from jax import lax
from jax.experimental import pallas as pl
from jax.experimental.pallas import tpu as pltpu

D_MODEL = 2048
BATCH = 4
SEQ = 2048
DEPTH = 2
CTX_LEN = 256
GRID_W = 64
GRID_ROWS = SEQ // GRID_W
MLA_HEADS = 6
MLA_Q_RANK = 512
MLA_KV_RANK = 512
MLA_NOPE = 128
MLA_ROPE = 64
MLA_V = 128
MLA_QK = MLA_NOPE + MLA_ROPE
CONV_DIM = 512
NA_HEADS = 6
NA_HEAD_DIM = 128
NA_DIM = NA_HEADS * NA_HEAD_DIM
NA_WIN_R = 8
NA_WIN_C = 16
OFF_MLA_Q = 0
OFF_MLA_KV = OFF_MLA_Q + MLA_Q_RANK
OFF_ROPE = OFF_MLA_KV + MLA_KV_RANK
OFF_CONV_B = OFF_ROPE + MLA_ROPE
OFF_CONV_C = OFF_CONV_B + CONV_DIM
OFF_CONV_H = OFF_CONV_C + CONV_DIM
OFF_NA_Q = OFF_CONV_H + CONV_DIM
OFF_NA_K = OFF_NA_Q + NA_DIM
OFF_NA_V = OFF_NA_K + NA_DIM
IN_DIM = OFF_NA_V + NA_DIM
D_FF = -(-8 * D_MODEL // (3 * 256)) * 256
ROPE_THETA = 10000.0
LN_EPS = 1e-6
RMS_EPS = 1e-6
DEEPNORM_ALPHA = (2 * DEPTH) ** 0.25
LOG2E = 1.4426950408889634
N_MOD = 6
MOD_ROWS = 8
CTX_MOD_ROW = BATCH

LANES = 128
V7X_VMEM_BYTES = 64 * 1024 * 1024
V7X_VMEM_BUDGET = 58 * 1024 * 1024

Z_Q = 0
Z_KV = Z_Q + MLA_Q_RANK
Z_CONV_B = Z_KV + MLA_KV_RANK
Z_CONV_C = Z_CONV_B + CONV_DIM
Z_CONV_H = Z_CONV_C + CONV_DIM
Z_NA_Q = Z_CONV_H + CONV_DIM
Z_NA_K = Z_NA_Q + NA_DIM
Z_NA_V = Z_NA_K + NA_DIM
Z_ROPE = Z_NA_V + NA_DIM
Z_DIM = Z_ROPE + 2 * MLA_ROPE

QK_PAD = 2 * LANES
WQ_HEAD = 3 * LANES

TM = 1024
TM_OUT = 1024
TM_OUT_CHUNK = 256
TM_PROJ = 512
PREP_PIECE = MLA_ROPE
PREP_COLS = 3 * LANES
TN_IN = Z_DIM // 3
TN_ADA = 1024
TF = 512
TQ = 256
Q_TILE_UNROLL = 4
NA_RQ = 4
NA_RK = NA_RQ + NA_WIN_R - 1
NA_NQ = NA_RQ * GRID_W
NA_NK = NA_RK * GRID_W
NA_BLOCKS = GRID_ROWS // NA_RQ

F32 = jnp.float32
BF16 = jnp.bfloat16


def _params(semantics, vmem_bytes):
    assert vmem_bytes <= V7X_VMEM_BUDGET, vmem_bytes
    return pltpu.CompilerParams(dimension_semantics=semantics, vmem_limit_bytes=int(vmem_bytes))


def _nbytes(shape, dtype):
    return int(np.prod(shape)) * jnp.dtype(dtype).itemsize


def _dot(a, b):
    return jnp.dot(a, b, preferred_element_type=F32)


def _dot_nt(a, b):
    return lax.dot_general(a, b, (((1,), (1,)), ((), ())), preferred_element_type=F32)


def _mod_spec(layer, chunk, row_of_tile):
    base = layer * MOD_ROWS * N_MOD + chunk
    return pl.BlockSpec((None, 1, D_MODEL), lambda i, *_: (base + row_of_tile(i) * N_MOD, 0, 0))


def _layer_norm(r, g, b):
    mu = jnp.mean(r, axis=-1, keepdims=True)
    c = r - mu
    var = jnp.mean(c * c, axis=-1, keepdims=True)
    return c * lax.rsqrt(var + LN_EPS) * g + b


def _ada_kernel(cc_ref, w_ref, b_ref, o_ref):
    cc = cc_ref[...]
    s = (cc * jax.nn.sigmoid(cc)).astype(BF16)
    o_ref[...] = _dot(s, w_ref[...].astype(BF16)) + b_ref[...]


def _ada(cc, ada_w, ada_b):
    n = N_MOD * D_MODEL
    vmem = 2 * (_nbytes((D_MODEL, TN_ADA), F32) + _nbytes((MOD_ROWS, TN_ADA), F32) * 2
                + _nbytes((MOD_ROWS, D_MODEL), F32)) + _nbytes((D_MODEL, TN_ADA), BF16) * 2
    return pl.pallas_call(
        _ada_kernel,
        out_shape=jax.ShapeDtypeStruct((DEPTH, MOD_ROWS, n), F32),
        grid=(DEPTH, n // TN_ADA),
        in_specs=[
            pl.BlockSpec((MOD_ROWS, D_MODEL), lambda l, j: (0, 0)),
            pl.BlockSpec((None, D_MODEL, TN_ADA), lambda l, j: (l, 0, j)),
            pl.BlockSpec((None, 1, TN_ADA), lambda l, j: (l, 0, j)),
        ],
        out_specs=pl.BlockSpec((None, MOD_ROWS, TN_ADA), lambda l, j: (l, 0, j)),
        compiler_params=_params(("parallel", "parallel"), vmem),
        name="ada_mod",
    )(cc, ada_w, ada_b.reshape(DEPTH, 1, n))


def _in_proj_kernel(sh_ref, sc_ref, x_ref, w_ref, o_ref, xm_ref):
    @pl.when(pl.program_id(1) == 0)
    def _():
        xm_ref[...] = (x_ref[...] * (1.0 + sc_ref[...]) + sh_ref[...]).astype(BF16)

    o_ref[...] = _dot(xm_ref[...], w_ref[...]).astype(BF16)


def _in_proj(x, mods, w_in_p, layer, row_of_tile):
    rows = x.shape[0]
    vmem = (2 * (_nbytes((TM, D_MODEL), F32) + _nbytes((D_MODEL, TN_IN), BF16) + _nbytes((TM, TN_IN), BF16))
            + _nbytes((TM, D_MODEL), BF16) * 2 + _nbytes((TM, TN_IN), F32))
    return pl.pallas_call(
        _in_proj_kernel,
        out_shape=jax.ShapeDtypeStruct((rows, Z_DIM), BF16),
        grid=(rows // TM, Z_DIM // TN_IN),
        in_specs=[
            _mod_spec(layer, 0, row_of_tile),
            _mod_spec(layer, 1, row_of_tile),
            pl.BlockSpec((TM, D_MODEL), lambda i, j: (i, 0)),
            pl.BlockSpec((None, D_MODEL, TN_IN), lambda i, j: (layer, 0, j)),
        ],
        out_specs=pl.BlockSpec((TM, TN_IN), lambda i, j: (i, j)),
        scratch_shapes=[pltpu.VMEM((TM, D_MODEL), BF16)],
        compiler_params=_params(("parallel", "arbitrary"), vmem),
        name="in_proj",
    )(mods, mods, x, w_in_p)


def _rms_norm(x, g):
    return x * lax.rsqrt(jnp.mean(x * x, axis=-1, keepdims=True) + RMS_EPS) * g


def _mla_proj_kernel(z_ref, slot_ref, gq_ref, gkv_ref, wq_ref, wkv_ref, ta_ref, tb_ref, q_ref, k_ref, v_ref):
    cq = z_ref[:, Z_Q:Z_Q + MLA_Q_RANK].astype(F32)
    ckv = z_ref[:, Z_KV:Z_KV + MLA_KV_RANK].astype(F32)
    qa = _dot(_rms_norm(cq, gq_ref[...]).astype(BF16), wq_ref[...]) * (MLA_QK ** -0.5 * LOG2E)
    kv = _dot(_rms_norm(ckv, gkv_ref[...]).astype(BF16), wkv_ref[...])
    ta = ta_ref[...]
    tb = tb_ref[...]
    y = slot_ref[...].astype(F32) * ta
    kro = y + pltpu.roll(y, MLA_ROPE, 1)
    lane = lax.broadcasted_iota(jnp.int32, kro.shape, 1)
    kro = jnp.where(lane < MLA_ROPE, kro, 0.0).astype(BF16)
    for h in range(MLA_HEADS):
        qb = h * WQ_HEAD
        ob = h * QK_PAD
        q_ref[:, ob:ob + LANES] = qa[:, qb:qb + LANES].astype(BF16)
        q_ref[:, ob + LANES:ob + QK_PAD] = (
            qa[:, qb + LANES:qb + 2 * LANES] * ta + qa[:, qb + 2 * LANES:qb + 3 * LANES] * tb).astype(BF16)
        k_ref[:, ob:ob + LANES] = kv[:, ob:ob + LANES].astype(BF16)
        k_ref[:, ob + LANES:ob + QK_PAD] = kro
        v_ref[:, h * MLA_V:(h + 1) * MLA_V] = kv[:, ob + LANES:ob + QK_PAD].astype(BF16)


def _mla_proj(z, g_q, g_kv, wq_all, wkv, ta, tb, layer, table_tile):
    rows = z.shape[0]
    per_layer = lambda i: (layer, 0, 0)
    zw = Z_KV + MLA_KV_RANK
    hq = MLA_HEADS * QK_PAD
    hv = MLA_HEADS * MLA_V
    tm = TM_PROJ
    vmem = (2 * (_nbytes((tm, zw), BF16) + _nbytes((tm, LANES), BF16) + 2 * _nbytes((tm, LANES), F32)
                 + _nbytes(wq_all.shape[1:], BF16) + _nbytes(wkv.shape[1:], BF16)
                 + 2 * _nbytes((tm, hq), BF16) + _nbytes((tm, hv), BF16))
            + _nbytes((tm, MLA_HEADS * WQ_HEAD), F32) + _nbytes((tm, hq), F32) + 4 * _nbytes((tm, zw), F32))
    return pl.pallas_call(
        _mla_proj_kernel,
        out_shape=(jax.ShapeDtypeStruct((rows, hq), BF16), jax.ShapeDtypeStruct((rows, hq), BF16),
                   jax.ShapeDtypeStruct((rows, hv), BF16)),
        grid=(rows // tm,),
        in_specs=[
            pl.BlockSpec((tm, zw), lambda i: (i, 0)),
            pl.BlockSpec((tm, 2 * MLA_ROPE), lambda i: (i, Z_ROPE // (2 * MLA_ROPE))),
            pl.BlockSpec((None, 1, MLA_Q_RANK), per_layer),
            pl.BlockSpec((None, 1, MLA_KV_RANK), per_layer),
            pl.BlockSpec((None,) + wq_all.shape[1:], per_layer),
            pl.BlockSpec((None,) + wkv.shape[1:], per_layer),
            pl.BlockSpec((tm, LANES), lambda i: (table_tile(i), 0)),
            pl.BlockSpec((tm, LANES), lambda i: (table_tile(i), 0)),
        ],
        out_specs=(pl.BlockSpec((tm, hq), lambda i: (i, 0)), pl.BlockSpec((tm, hq), lambda i: (i, 0)),
                   pl.BlockSpec((tm, hv), lambda i: (i, 0))),
        compiler_params=_params(("parallel",), vmem),
        name="mla_proj",
    )(z, z, g_q, g_kv, wq_all, wkv, ta, tb)


def _softmax_pv(scores, values):
    m = functools.reduce(jnp.maximum, [jnp.max(s, axis=-1, keepdims=True) for s in scores])
    ps = [jnp.exp2(s - m) for s in scores]
    denom = functools.reduce(jnp.add, [jnp.sum(p, axis=-1, keepdims=True) for p in ps])
    acc = functools.reduce(jnp.add, [_dot(p.astype(BF16), v) for p, v in zip(ps, values)])
    return acc / denom


def _scaled_q(q, head_dim):
    return (q.astype(F32) * (head_dim ** -0.5 * LOG2E)).astype(BF16)


def _mla_attn_kernel(*refs, with_ctx):
    if with_ctx:
        q_ref, kl_ref, vl_ref, kc_ref, vc_ref, qc_ref, o_ref, oc_ref = refs
    else:
        q_ref, kl_ref, vl_ref, kc_ref, vc_ref, o_ref = refs

    def q_tile(t, carry):
        r0 = pl.multiple_of(t * TQ, TQ)
        q = q_ref[pl.ds(r0, TQ), :]
        s_lat = _dot_nt(q, kl_ref[...])
        s_ctx = _dot_nt(q, kc_ref[...])
        o = _softmax_pv([s_lat, s_ctx], [vl_ref[...], vc_ref[...]])
        o_ref[pl.ds(r0, TQ), :] = o.astype(BF16)
        return carry

    lax.fori_loop(0, SEQ // TQ, q_tile, 0, unroll=Q_TILE_UNROLL)
    if with_ctx:
        s = _dot_nt(qc_ref[...], kc_ref[...])
        oc_ref[...] = _softmax_pv([s], [vc_ref[...]]).astype(BF16)


def _mla_attn(ql, kl, vl, qc, kc, vc, with_ctx):
    head = lambda b, h: (b, h)
    in_specs = [
        pl.BlockSpec((SEQ, QK_PAD), head), pl.BlockSpec((SEQ, QK_PAD), head), pl.BlockSpec((SEQ, MLA_V), head),
        pl.BlockSpec((CTX_LEN, QK_PAD), head), pl.BlockSpec((CTX_LEN, MLA_V), head),
    ]
    args = [ql, kl, vl, kc, vc]
    out_shape = [jax.ShapeDtypeStruct((BATCH * SEQ, MLA_HEADS * MLA_V), BF16)]
    out_specs = [pl.BlockSpec((SEQ, MLA_V), head)]
    if with_ctx:
        in_specs.append(pl.BlockSpec((CTX_LEN, QK_PAD), head))
        args.append(qc)
        out_shape.append(jax.ShapeDtypeStruct((BATCH * CTX_LEN, MLA_HEADS * MLA_V), BF16))
        out_specs.append(pl.BlockSpec((CTX_LEN, MLA_V), head))
    vmem = (2 * (3 * _nbytes((SEQ, QK_PAD), BF16) + 2 * _nbytes((SEQ, MLA_V), BF16))
            + 4 * Q_TILE_UNROLL * _nbytes((TQ, SEQ + CTX_LEN), F32))
    return pl.pallas_call(
        functools.partial(_mla_attn_kernel, with_ctx=with_ctx),
        out_shape=tuple(out_shape), grid=(BATCH, MLA_HEADS), in_specs=in_specs, out_specs=tuple(out_specs),
        compiler_params=_params(("parallel", "parallel"), vmem),
        name="mla_attn_ctx" if with_ctx else "mla_attn",
    )(*args)


def _conv_kernel(gb_ref, gc_ref, h_ref, w_ref, o_ref):
    n = o_ref.shape[0]
    u = gc_ref[...].astype(F32) * h_ref[...].astype(F32)
    row = lax.broadcasted_iota(jnp.int32, u.shape, 0)
    u_prev = jnp.where(row == 0, 0.0, pltpu.roll(u, 1, 0))
    u_next = jnp.where(row == n - 1, 0.0, pltpu.roll(u, n - 1, 0))
    y = u_prev * w_ref[0:1, :] + u * w_ref[1:2, :] + u_next * w_ref[2:3, :]
    o_ref[...] = (gb_ref[...].astype(F32) * y).astype(BF16)


def _conv(z, conv_w, layer, seq_len):
    rows = z.shape[0]
    blk = (seq_len, CONV_DIM)
    vmem = 2 * 4 * _nbytes(blk, BF16) + 8 * _nbytes(blk, F32)
    return pl.pallas_call(
        _conv_kernel,
        out_shape=jax.ShapeDtypeStruct((rows, CONV_DIM), BF16),
        grid=(rows // seq_len,),
        in_specs=[
            pl.BlockSpec(blk, lambda s: (s, Z_CONV_B // CONV_DIM)),
            pl.BlockSpec(blk, lambda s: (s, Z_CONV_C // CONV_DIM)),
            pl.BlockSpec(blk, lambda s: (s, Z_CONV_H // CONV_DIM)),
            pl.BlockSpec((None,) + conv_w.shape[1:], lambda s: (layer, 0, 0)),
        ],
        out_specs=pl.BlockSpec(blk, lambda s: (s, 0)),
        compiler_params=_params(("parallel",), vmem),
        name="short_conv",
    )(z, z, z, conv_w)


def _na_band_start(block):
    return min(max(block * NA_RQ - NA_WIN_R // 2, 0), GRID_ROWS - NA_RK)


def _na_bias_kind(block):
    return 0 if block == 0 else (2 if block == NA_BLOCKS - 1 else 1)


def _na_row_offsets():
    n_dr = 2 * NA_WIN_R - 1
    idx = np.full((3, NA_RQ, NA_RK), n_dr, np.int32)
    for kind, block in ((0, 0), (1, 2), (2, NA_BLOCKS - 1)):
        for qr in range(NA_RQ):
            r = block * NA_RQ + qr
            r0 = min(max(r - NA_WIN_R // 2, 0), GRID_ROWS - NA_WIN_R)
            for ki in range(NA_RK):
                kr = _na_band_start(block) + ki
                if r0 <= kr < r0 + NA_WIN_R:
                    idx[kind, qr, ki] = kr - r + NA_WIN_R - 1
    return idx


def _na_toeplitz(rpb):
    n_dr = 2 * NA_WIN_R - 1
    neg = -jnp.inf
    rpb = rpb.reshape(DEPTH * NA_HEADS, n_dr, 2 * NA_WIN_C - 1) * LOG2E
    side = GRID_W - 1 - (NA_WIN_C - 1)
    w = jnp.pad(rpb, ((0, 0), (0, 0), (side, side + 1)), constant_values=neg)
    flat = jnp.tile(w, (1, 1, GRID_W))
    lo = GRID_W - 1
    t = flat[:, :, lo:lo + GRID_W * (2 * GRID_W - 1)].reshape(-1, n_dr, GRID_W, 2 * GRID_W - 1)[..., :GRID_W]
    col = np.arange(GRID_W)
    c0 = np.clip(col - NA_WIN_C // 2, 0, GRID_W - NA_WIN_C)
    col_ok = (col[None, :] >= c0[:, None]) & (col[None, :] < c0[:, None] + NA_WIN_C)
    t = jnp.where(col_ok[None, None], t, neg)
    return jnp.pad(t, ((0, 0), (0, 1), (0, 0), (0, 0)), constant_values=neg)


def _na_bias_kernel(t_ref, o_ref):
    offsets = _na_row_offsets()
    for kind in range(3):
        for qr in range(NA_RQ):
            for ki in range(NA_RK):
                o_ref[kind, qr * GRID_W:(qr + 1) * GRID_W, ki * GRID_W:(ki + 1) * GRID_W] = (
                    t_ref[int(offsets[kind, qr, ki])])


def _na_bias(rpb):
    t = _na_toeplitz(rpb)
    vmem = 2 * (_nbytes((t.shape[1], GRID_W, LANES), F32) + _nbytes((3, NA_NQ, NA_NK + GRID_W), F32))
    return pl.pallas_call(
        _na_bias_kernel,
        out_shape=jax.ShapeDtypeStruct((t.shape[0], 3, NA_NQ, NA_NK), F32),
        grid=(t.shape[0],),
        in_specs=[pl.BlockSpec((None,) + t.shape[1:], lambda g: (g, 0, 0, 0))],
        out_specs=pl.BlockSpec((None, 3, NA_NQ, NA_NK), lambda g: (g, 0, 0, 0)),
        compiler_params=_params(("parallel",), vmem),
        name="na_bias",
    )(t)


def _na_kernel(*refs, with_ctx):
    if with_ctx:
        q_ref, k_ref, v_ref, kc_ref, vc_ref, bias_ref, qc_ref, o_ref, oc_ref = refs
    else:
        q_ref, k_ref, v_ref, kc_ref, vc_ref, bias_ref, o_ref = refs
    for block in range(NA_BLOCKS):
        q = _scaled_q(q_ref[block * NA_NQ:(block + 1) * NA_NQ, :], NA_HEAD_DIM)
        k0 = _na_band_start(block) * GRID_W
        s_loc = _dot_nt(q, k_ref[k0:k0 + NA_NK, :]) + bias_ref[_na_bias_kind(block)]
        s_ctx = _dot_nt(q, kc_ref[...])
        o = _softmax_pv([s_loc, s_ctx], [v_ref[k0:k0 + NA_NK, :], vc_ref[...]])
        o_ref[block * NA_NQ:(block + 1) * NA_NQ, :] = o.astype(BF16)
    if with_ctx:
        s = _dot_nt(_scaled_q(qc_ref[...], NA_HEAD_DIM), kc_ref[...])
        oc_ref[...] = _softmax_pv([s], [vc_ref[...]]).astype(BF16)


def _na_attn(zl, zc, bias, layer, with_ctx):
    d = NA_HEAD_DIM
    col = lambda off: (lambda b, h: (b, off // d + h))
    in_specs = [
        pl.BlockSpec((SEQ, d), col(Z_NA_Q)), pl.BlockSpec((SEQ, d), col(Z_NA_K)), pl.BlockSpec((SEQ, d), col(Z_NA_V)),
        pl.BlockSpec((CTX_LEN, d), col(Z_NA_K)), pl.BlockSpec((CTX_LEN, d), col(Z_NA_V)),
        pl.BlockSpec((None, 3, NA_NQ, NA_NK), lambda b, h: (layer * NA_HEADS + h, 0, 0, 0)),
    ]
    args = [zl, zl, zl, zc, zc, bias]
    out_shape = [jax.ShapeDtypeStruct((BATCH * SEQ, NA_DIM), BF16)]
    out_specs = [pl.BlockSpec((SEQ, d), lambda b, h: (b, h))]
    if with_ctx:
        in_specs.append(pl.BlockSpec((CTX_LEN, d), col(Z_NA_Q)))
        args.append(zc)
        out_shape.append(jax.ShapeDtypeStruct((BATCH * CTX_LEN, NA_DIM), BF16))
        out_specs.append(pl.BlockSpec((CTX_LEN, d), lambda b, h: (b, h)))
    vmem = (2 * (5 * _nbytes((SEQ, d), BF16) + _nbytes((3, NA_NQ, NA_NK), F32))
            + 8 * _nbytes((NA_NQ, NA_NK + CTX_LEN), F32))
    return pl.pallas_call(
        functools.partial(_na_kernel, with_ctx=with_ctx),
        out_shape=tuple(out_shape), grid=(BATCH, NA_HEADS), in_specs=in_specs, out_specs=tuple(out_specs),
        compiler_params=_params(("parallel", "parallel"), vmem),
        name="na_attn_ctx" if with_ctx else "na_attn",
    )(*args)


def _out_proj_kernel(g_ref, a_ref, c_ref, n_ref, x_ref, w_ref, lng_ref, lnb_ref, o_ref):
    a_w = MLA_HEADS * MLA_V
    for r0 in range(0, o_ref.shape[0], TM_OUT_CHUNK):
        rows = slice(r0, r0 + TM_OUT_CHUNK)
        y = (_dot(a_ref[rows, :], w_ref[0:a_w, :]) + _dot(c_ref[rows, :], w_ref[a_w:a_w + CONV_DIM, :])
             + _dot(n_ref[rows, :], w_ref[a_w + CONV_DIM:, :]))
        r = DEEPNORM_ALPHA * x_ref[rows, :] + g_ref[...] * y
        o_ref[rows, :] = _layer_norm(r, lng_ref[...], lnb_ref[...])


def _out_proj(a, cv, n, x, mods, w_out, ln_g, ln_b, layer, row_of_tile):
    rows = x.shape[0]
    tm = TM_OUT
    row = lambda i: (i, 0)
    per_layer = lambda i: (layer, 0, 0)
    vmem = (2 * (2 * _nbytes((tm, D_MODEL), F32) + _nbytes((tm, D_MODEL), BF16)) + _nbytes(w_out.shape[1:], BF16)
            + 3 * _nbytes((TM_OUT_CHUNK, D_MODEL), F32))
    return pl.pallas_call(
        _out_proj_kernel,
        out_shape=jax.ShapeDtypeStruct((rows, D_MODEL), F32),
        grid=(rows // tm,),
        in_specs=[
            _mod_spec(layer, 2, row_of_tile),
            pl.BlockSpec((tm, a.shape[1]), row), pl.BlockSpec((tm, cv.shape[1]), row),
            pl.BlockSpec((tm, n.shape[1]), row), pl.BlockSpec((tm, D_MODEL), row),
            pl.BlockSpec((None,) + w_out.shape[1:], per_layer, pipeline_mode=pl.Buffered(1)),
            pl.BlockSpec((None, 1, D_MODEL), per_layer), pl.BlockSpec((None, 1, D_MODEL), per_layer),
        ],
        out_specs=pl.BlockSpec((tm, D_MODEL), row),
        compiler_params=_params(("parallel",), vmem),
        name="out_proj",
    )(mods, a, cv, n, x, w_out, ln_g, ln_b)


def _ffn_kernel(sh_ref, sc_ref, g_ref, x_ref, wg_ref, wu_ref, wd_ref, lng_ref, lnb_ref, o_ref, xm_ref):
    f = pl.program_id(1)

    @pl.when(f == 0)
    def _():
        x = x_ref[...]
        xm_ref[...] = (x * (1.0 + sc_ref[...]) + sh_ref[...]).astype(BF16)
        o_ref[...] = DEEPNORM_ALPHA * x

    xm = xm_ref[...]
    gate = _dot(xm, wg_ref[...])
    up = _dot(xm, wu_ref[...])
    hidden = (gate * jax.nn.sigmoid(gate) * up).astype(BF16)
    o_ref[...] += g_ref[...] * _dot(hidden, wd_ref[...])

    @pl.when(f == pl.num_programs(1) - 1)
    def _():
        o_ref[...] = _layer_norm(o_ref[...], lng_ref[...], lnb_ref[...])


def _ffn(x, mods, w_gate, w_up, w_down, ln_g, ln_b, layer, row_of_tile):
    rows = x.shape[0]
    per_layer = lambda i, f: (layer, 0, 0)
    vmem = (3 * _nbytes((TM, D_MODEL), F32) + 2 * 3 * _nbytes((D_MODEL, TF), BF16)
            + _nbytes((TM, D_MODEL), BF16) + 3 * _nbytes((TM, TF), F32) + _nbytes((TM, D_MODEL), F32))
    return pl.pallas_call(
        _ffn_kernel,
        out_shape=jax.ShapeDtypeStruct((rows, D_MODEL), F32),
        grid=(rows // TM, D_FF // TF),
        in_specs=[
            _mod_spec(layer, 3, row_of_tile), _mod_spec(layer, 4, row_of_tile), _mod_spec(layer, 5, row_of_tile),
            pl.BlockSpec((TM, D_MODEL), lambda i, f: (i, 0), pipeline_mode=pl.Buffered(1)),
            pl.BlockSpec((None, D_MODEL, TF), lambda i, f: (layer, 0, f)),
            pl.BlockSpec((None, D_MODEL, TF), lambda i, f: (layer, 0, f)),
            pl.BlockSpec((None, TF, D_MODEL), lambda i, f: (layer, f, 0)),
            pl.BlockSpec((None, 1, D_MODEL), per_layer), pl.BlockSpec((None, 1, D_MODEL), per_layer),
        ],
        out_specs=pl.BlockSpec((TM, D_MODEL), lambda i, f: (i, 0)),
        scratch_shapes=[pltpu.VMEM((TM, D_MODEL), BF16)],
        compiler_params=_params(("parallel", "arbitrary"), vmem),
        name="ffn",
    )(mods, mods, mods, x, w_gate, w_up, w_down, ln_g, ln_b)


def _rotate_half_cols(w):
    a1, a2, b1, b2 = jnp.split(w, 4, axis=-1)
    return jnp.concatenate([-a2, a1, -b2, b1], axis=-1)


def _w_in_prep_src(piece):
    rope = OFF_ROPE // PREP_PIECE
    tail = Z_ROPE // PREP_PIECE
    return jnp.where(piece < rope, piece, jnp.where(piece < tail, piece + 1, rope))


def _w_in_prep_kernel(*refs):
    *piece_refs, o_ref = refs
    pieces = [r[...] for r in piece_refs]
    a1, a2, b1, b2 = jnp.split(pieces[-1], 4, axis=0)
    rot = jnp.concatenate([-a2, a1, -b2, b1], axis=0)
    is_last = pl.program_id(1) == pl.num_programs(1) - 1
    pieces[-1] = jnp.where(is_last, rot, pieces[-1])
    o_ref[...] = jnp.concatenate(pieces, axis=0).T.astype(BF16)


def _w_in_prep(w_in):
    w_t = jnp.swapaxes(w_in, 1, 2)
    n = PREP_COLS // PREP_PIECE
    piece_spec = lambda p: pl.BlockSpec((None, PREP_PIECE, D_MODEL), lambda l, g: (l, _w_in_prep_src(g * n + p), 0))
    vmem = 2 * (_nbytes((PREP_COLS, D_MODEL), F32) + _nbytes((D_MODEL, PREP_COLS), BF16)) + 3 * _nbytes(
        (PREP_COLS, D_MODEL), F32)
    return pl.pallas_call(
        _w_in_prep_kernel,
        out_shape=jax.ShapeDtypeStruct((DEPTH, D_MODEL, Z_DIM), BF16),
        grid=(DEPTH, Z_DIM // PREP_COLS),
        in_specs=[piece_spec(p) for p in range(n)],
        out_specs=pl.BlockSpec((None, D_MODEL, PREP_COLS), lambda l, g: (l, 0, g)),
        compiler_params=_params(("parallel", "parallel"), vmem),
        name="w_in_prep",
    )(*([w_t] * n))


def _relayout_wq(w):
    w = w.reshape(DEPTH, MLA_Q_RANK, MLA_HEADS, MLA_QK)
    nope, rope = w[..., :MLA_NOPE], w[..., MLA_NOPE:]
    zeros = jnp.zeros_like(rope)
    cols = jnp.concatenate([nope, rope, zeros, _rotate_half_cols(rope), zeros], axis=-1)
    return cols.reshape(DEPTH, MLA_Q_RANK, MLA_HEADS * WQ_HEAD).astype(BF16)


def _rope_tables():
    t = jnp.arange(SEQ)
    row = (t // GRID_W).astype(F32)
    col = (t % GRID_W).astype(F32)
    n_freq = MLA_ROPE // 4
    inv = ROPE_THETA ** (-jnp.arange(n_freq, dtype=F32) / n_freq)
    ar = row[:, None] * inv
    ac = col[:, None] * inv
    ang = jnp.concatenate([ar, ar, ac, ac], axis=-1)
    cos, sin = jnp.cos(ang), jnp.sin(ang)
    ones, zeros = jnp.ones((TM_PROJ, MLA_ROPE), F32), jnp.zeros((TM_PROJ, MLA_ROPE), F32)
    lat = (jnp.concatenate([cos, sin], axis=1), jnp.concatenate([sin, cos], axis=1))
    ctx = (jnp.concatenate([ones, zeros], axis=1), jnp.concatenate([zeros, ones], axis=1))
    return lat, ctx


def kernel(x, c, ctx, c_ctx, ada_w, ada_b, w_in, mla_q_norm, mla_wq_b, mla_kv_norm, mla_wkv_b, conv_w, na_rpb,
           w_out, ln1_g, ln1_b, ffn_w_gate, ffn_w_up, ffn_w_down, ln2_g, ln2_b):
    assert x.shape == (BATCH, SEQ, D_MODEL) and ctx.shape == (BATCH, CTX_LEN, D_MODEL)
    cc = jnp.concatenate([c, c_ctx[None, :], jnp.zeros((MOD_ROWS - BATCH - 1, D_MODEL), F32)], axis=0)
    mods = _ada(cc, ada_w, ada_b).reshape(DEPTH * MOD_ROWS * N_MOD, 1, D_MODEL)
    (ta_lat, tb_lat), (ta_ctx, tb_ctx) = _rope_tables()
    rows3d = lambda v: v.reshape(DEPTH, 1, -1)

    w_in_p = _w_in_prep(w_in)
    wq_all = _relayout_wq(mla_wq_b)
    wkv = mla_wkv_b.astype(BF16)
    g_q, g_kv = rows3d(mla_q_norm), rows3d(mla_kv_norm)
    bias = _na_bias(na_rpb)
    w_o = w_out.astype(BF16)
    w_g, w_u, w_d = ffn_w_gate.astype(BF16), ffn_w_up.astype(BF16), ffn_w_down.astype(BF16)
    ln1 = (rows3d(ln1_g), rows3d(ln1_b))
    ln2 = (rows3d(ln2_g), rows3d(ln2_b))

    xl = x.reshape(BATCH * SEQ, D_MODEL)
    xc = ctx.reshape(BATCH * CTX_LEN, D_MODEL)
    lat_row = lambda tm: (lambda i: i // (SEQ // tm))
    ctx_row = lambda i: CTX_MOD_ROW
    for l in range(DEPTH):
        last = l == DEPTH - 1
        zl = _in_proj(xl, mods, w_in_p, l, lat_row(TM))
        zc = _in_proj(xc, mods, w_in_p, l, ctx_row)
        ql, kl, vl = _mla_proj(zl, g_q, g_kv, wq_all, wkv, ta_lat, tb_lat, l, lambda i: i % (SEQ // TM_PROJ))
        qc, kc, vc = _mla_proj(zc, g_q, g_kv, wq_all, wkv, ta_ctx, tb_ctx, l, lambda i: 0)
        a = _mla_attn(ql, kl, vl, qc, kc, vc, with_ctx=not last)
        n = _na_attn(zl, zc, bias, l, with_ctx=not last)
        cv_l = _conv(zl, conv_w, l, SEQ)
        x1 = _out_proj(a[0], cv_l, n[0], xl, mods, w_o, *ln1, l, lat_row(TM_OUT))
        xl = _ffn(x1, mods, w_g, w_u, w_d, *ln2, l, lat_row(TM))
        if not last:
            cv_c = _conv(zc, conv_w, l, CTX_LEN)
            x1c = _out_proj(a[1], cv_c, n[1], xc, mods, w_o, *ln1, l, ctx_row)
            xc = _ffn(x1c, mods, w_g, w_u, w_d, *ln2, l, ctx_row)
    return xl.reshape(BATCH, SEQ, D_MODEL)
```

```python
import functools

import numpy as np
import jax
import jax.numpy as jnp
from jax import lax
from jax.experimental import pallas as pl
from jax.experimental.pallas import tpu as pltpu

D_MODEL = 2048
BATCH = 4
SEQ = 2048
DEPTH = 2
CTX_LEN = 256
GRID_W = 64
GRID_ROWS = SEQ // GRID_W
MLA_HEADS = 6
MLA_Q_RANK = 512
MLA_KV_RANK = 512
MLA_NOPE = 128
MLA_ROPE = 64
MLA_V = 128
MLA_QK = MLA_NOPE + MLA_ROPE
CONV_DIM = 512
NA_HEADS = 6
NA_HEAD_DIM = 128
NA_DIM = NA_HEADS * NA_HEAD_DIM
NA_WIN_R = 8
NA_WIN_C = 16
OFF_MLA_Q = 0
OFF_MLA_KV = OFF_MLA_Q + MLA_Q_RANK
OFF_ROPE = OFF_MLA_KV + MLA_KV_RANK
OFF_CONV_B = OFF_ROPE + MLA_ROPE
OFF_CONV_C = OFF_CONV_B + CONV_DIM
OFF_CONV_H = OFF_CONV_C + CONV_DIM
OFF_NA_Q = OFF_CONV_H + CONV_DIM
OFF_NA_K = OFF_NA_Q + NA_DIM
OFF_NA_V = OFF_NA_K + NA_DIM
IN_DIM = OFF_NA_V + NA_DIM
D_FF = -(-8 * D_MODEL // (3 * 256)) * 256
ROPE_THETA = 10000.0
LN_EPS = 1e-6
RMS_EPS = 1e-6
DEEPNORM_ALPHA = (2 * DEPTH) ** 0.25
LOG2E = 1.4426950408889634
N_MOD = 6
MOD_ROWS = 8
CTX_MOD_ROW = BATCH

LANES = 128
V7X_VMEM_BYTES = 64 * 1024 * 1024
V7X_VMEM_BUDGET = 58 * 1024 * 1024

Z_Q = 0
Z_KV = Z_Q + MLA_Q_RANK
Z_CONV_B = Z_KV + MLA_KV_RANK
Z_CONV_C = Z_CONV_B + CONV_DIM
Z_CONV_H = Z_CONV_C + CONV_DIM
Z_NA_Q = Z_CONV_H + CONV_DIM
Z_NA_K = Z_NA_Q + NA_DIM
Z_NA_V = Z_NA_K + NA_DIM
Z_ROPE = Z_NA_V + NA_DIM
Z_DIM = Z_ROPE + 2 * MLA_ROPE
V7X_MXU_COLS = 256
Z_PAD = -(-Z_DIM // (5 * V7X_MXU_COLS)) * (5 * V7X_MXU_COLS)

QK_PAD = 2 * LANES
WQ_HEAD = 3 * LANES

TM = 1024
TM_OUT = 1024
TM_OUT_CHUNK = 256
TM_PROJ = 512
PREP_PIECE = MLA_ROPE
PREP_COLS = 5 * LANES
TN_IN = 5 * V7X_MXU_COLS
TN_ADA = 1024
TF = 512
TQ = 256
Q_TILE_UNROLL = 8
NA_RQ = 4
NA_RK = NA_RQ + NA_WIN_R - 1
NA_NQ = NA_RQ * GRID_W
NA_NK = NA_RK * GRID_W
NA_BLOCKS = GRID_ROWS // NA_RQ

F32 = jnp.float32
BF16 = jnp.bfloat16


def _params(semantics, vmem_bytes):
    assert vmem_bytes <= V7X_VMEM_BUDGET, vmem_bytes
    return pltpu.CompilerParams(dimension_semantics=semantics, vmem_limit_bytes=int(vmem_bytes))


def _nbytes(shape, dtype):
    return int(np.prod(shape)) * jnp.dtype(dtype).itemsize


def _dot(a, b):
    return jnp.dot(a, b, preferred_element_type=F32)


def _dot_nt(a, b):
    return lax.dot_general(a, b, (((1,), (1,)), ((), ())), preferred_element_type=F32)


def _mod_spec(layer, chunk, row_of_tile):
    base = layer * MOD_ROWS * N_MOD + chunk
    return pl.BlockSpec((None, 1, D_MODEL), lambda i, *_: (base + row_of_tile(i) * N_MOD, 0, 0))


def _layer_norm(r, g, b):
    mu = jnp.mean(r, axis=-1, keepdims=True)
    c = r - mu
    var = jnp.mean(c * c, axis=-1, keepdims=True)
    return c * lax.rsqrt(var + LN_EPS) * g + b


def _ada_kernel(cc_ref, w_ref, b_ref, o_ref):
    cc = cc_ref[...]
    s = (cc * jax.nn.sigmoid(cc)).astype(BF16)
    o_ref[...] = _dot(s, w_ref[...].astype(BF16)) + b_ref[...]


def _ada(cc, ada_w, ada_b):
    n = N_MOD * D_MODEL
    vmem = 2 * (_nbytes((D_MODEL, TN_ADA), F32) + _nbytes((MOD_ROWS, TN_ADA), F32) * 2
                + _nbytes((MOD_ROWS, D_MODEL), F32)) + _nbytes((D_MODEL, TN_ADA), BF16) * 2
    return pl.pallas_call(
        _ada_kernel,
        out_shape=jax.ShapeDtypeStruct((DEPTH, MOD_ROWS, n), F32),
        grid=(DEPTH, n // TN_ADA),
        in_specs=[
            pl.BlockSpec((MOD_ROWS, D_MODEL), lambda l, j: (0, 0)),
            pl.BlockSpec((None, D_MODEL, TN_ADA), lambda l, j: (l, 0, j)),
            pl.BlockSpec((None, 1, TN_ADA), lambda l, j: (l, 0, j)),
        ],
        out_specs=pl.BlockSpec((None, MOD_ROWS, TN_ADA), lambda l, j: (l, 0, j)),
        compiler_params=_params(("parallel", "parallel"), vmem),
        name="ada_mod",
    )(cc, ada_w, ada_b.reshape(DEPTH, 1, n))


def _in_proj_kernel(sh_ref, sc_ref, x_ref, w_ref, o_ref, xm_ref):
    @pl.when(pl.program_id(1) == 0)
    def _():
        xm_ref[...] = (x_ref[...] * (1.0 + sc_ref[...]) + sh_ref[...]).astype(BF16)

    o_ref[...] = _dot(xm_ref[...], w_ref[...]).astype(BF16)


def _in_proj(x, mods, w_in_p, layer, row_of_tile):
    rows = x.shape[0]
    vmem = (2 * (_nbytes((TM, D_MODEL), F32) + _nbytes((D_MODEL, TN_IN), BF16) + _nbytes((TM, TN_IN), BF16))
            + _nbytes((TM, D_MODEL), BF16) * 2 + _nbytes((TM, TN_IN), F32))
    return pl.pallas_call(
        _in_proj_kernel,
        out_shape=jax.ShapeDtypeStruct((rows, Z_PAD), BF16),
        grid=(rows // TM, Z_PAD // TN_IN),
        in_specs=[
            _mod_spec(layer, 0, row_of_tile),
            _mod_spec(layer, 1, row_of_tile),
            pl.BlockSpec((TM, D_MODEL), lambda i, j: (i, 0)),
            pl.BlockSpec((None, D_MODEL, TN_IN), lambda i, j: (layer, 0, j)),
        ],
        out_specs=pl.BlockSpec((TM, TN_IN), lambda i, j: (i, j)),
        scratch_shapes=[pltpu.VMEM((TM, D_MODEL), BF16)],
        compiler_params=_params(("parallel", "arbitrary"), vmem),
        name="in_proj",
    )(mods, mods, x, w_in_p)


def _rms_norm(x, g):
    return x * lax.rsqrt(jnp.mean(x * x, axis=-1, keepdims=True) + RMS_EPS) * g


def _mla_proj_kernel(z_ref, slot_ref, gq_ref, gkv_ref, wq_ref, wkv_ref, ta_ref, tb_ref, q_ref, k_ref, v_ref):
    cq = z_ref[:, Z_Q:Z_Q + MLA_Q_RANK].astype(F32)
    ckv = z_ref[:, Z_KV:Z_KV + MLA_KV_RANK].astype(F32)
    qa = _dot(_rms_norm(cq, gq_ref[...]).astype(BF16), wq_ref[...]) * (MLA_QK ** -0.5 * LOG2E)
    kv = _dot(_rms_norm(ckv, gkv_ref[...]).astype(BF16), wkv_ref[...])
    ta = ta_ref[...]
    tb = tb_ref[...]
    y = slot_ref[...].astype(F32) * ta
    kro = y + pltpu.roll(y, MLA_ROPE, 1)
    lane = lax.broadcasted_iota(jnp.int32, kro.shape, 1)
    kro = jnp.where(lane < MLA_ROPE, kro, 0.0).astype(BF16)
    for h in range(MLA_HEADS):
        qb = h * WQ_HEAD
        ob = h * QK_PAD
        q_ref[:, ob:ob + LANES] = qa[:, qb:qb + LANES].astype(BF16)
        q_ref[:, ob + LANES:ob + QK_PAD] = (
            qa[:, qb + LANES:qb + 2 * LANES] * ta + qa[:, qb + 2 * LANES:qb + 3 * LANES] * tb).astype(BF16)
        k_ref[:, ob:ob + LANES] = kv[:, ob:ob + LANES].astype(BF16)
        k_ref[:, ob + LANES:ob + QK_PAD] = kro
        v_ref[:, h * MLA_V:(h + 1) * MLA_V] = kv[:, ob + LANES:ob + QK_PAD].astype(BF16)


def _mla_proj(z, g_q, g_kv, wq_all, wkv, ta, tb, layer, table_tile):
    rows = z.shape[0]
    per_layer = lambda i: (layer, 0, 0)
    zw = Z_KV + MLA_KV_RANK
    hq = MLA_HEADS * QK_PAD
    hv = MLA_HEADS * MLA_V
    tm = TM_PROJ
    vmem = (2 * (_nbytes((tm, zw), BF16) + _nbytes((tm, LANES), BF16) + 2 * _nbytes((tm, LANES), F32)
                 + _nbytes(wq_all.shape[1:], BF16) + _nbytes(wkv.shape[1:], BF16)
                 + 2 * _nbytes((tm, hq), BF16) + _nbytes((tm, hv), BF16))
            + _nbytes((tm, MLA_HEADS * WQ_HEAD), F32) + _nbytes((tm, hq), F32) + 4 * _nbytes((tm, zw), F32))
    return pl.pallas_call(
        _mla_proj_kernel,
        out_shape=(jax.ShapeDtypeStruct((rows, hq), BF16), jax.ShapeDtypeStruct((rows, hq), BF16),
                   jax.ShapeDtypeStruct((rows, hv), BF16)),
        grid=(rows // tm,),
        in_specs=[
            pl.BlockSpec((tm, zw), lambda i: (i, 0)),
            pl.BlockSpec((tm, 2 * MLA_ROPE), lambda i: (i, Z_ROPE // (2 * MLA_ROPE))),
            pl.BlockSpec((None, 1, MLA_Q_RANK), per_layer),
            pl.BlockSpec((None, 1, MLA_KV_RANK), per_layer),
            pl.BlockSpec((None,) + wq_all.shape[1:], per_layer),
            pl.BlockSpec((None,) + wkv.shape[1:], per_layer),
            pl.BlockSpec((tm, LANES), lambda i: (table_tile(i), 0)),
            pl.BlockSpec((tm, LANES), lambda i: (table_tile(i), 0)),
        ],
        out_specs=(pl.BlockSpec((tm, hq), lambda i: (i, 0)), pl.BlockSpec((tm, hq), lambda i: (i, 0)),
                   pl.BlockSpec((tm, hv), lambda i: (i, 0))),
        compiler_params=_params(("parallel",), vmem),
        name="mla_proj",
    )(z, z, g_q, g_kv, wq_all, wkv, ta, tb)


def _softmax_pv(scores, values_ones):
    m = functools.reduce(jnp.maximum, [jnp.max(s, axis=-1, keepdims=True) for s in scores])
    acc = functools.reduce(jnp.add, [_dot(jnp.exp2(s - m).astype(BF16), v) for s, v in zip(scores, values_ones)])
    return acc[:, :LANES] / acc[:, LANES:]


def _fill_values_ones(v_ref, vx_ref):
    vx_ref[:, :LANES] = v_ref[...]
    vx_ref[:, LANES:] = jnp.ones((vx_ref.shape[0], LANES), BF16)


def _scaled_q(q, head_dim):
    return (q.astype(F32) * (head_dim ** -0.5 * LOG2E)).astype(BF16)


def _mla_attn_kernel(*refs, with_ctx):
    if with_ctx:
        q_ref, kl_ref, vl_ref, kc_ref, vc_ref, qc_ref, o_ref, oc_ref, vxl_ref, vxc_ref = refs
    else:
        q_ref, kl_ref, vl_ref, kc_ref, vc_ref, o_ref, vxl_ref, vxc_ref = refs
    _fill_values_ones(vl_ref, vxl_ref)
    _fill_values_ones(vc_ref, vxc_ref)

    def q_tile(t, carry):
        r0 = pl.multiple_of(t * TQ, TQ)
        q = q_ref[pl.ds(r0, TQ), :]
        s_lat = _dot_nt(q, kl_ref[...])
        s_ctx = _dot_nt(q, kc_ref[...])
        o = _softmax_pv([s_lat, s_ctx], [vxl_ref[...], vxc_ref[...]])
        o_ref[pl.ds(r0, TQ), :] = o.astype(BF16)
        return carry

    lax.fori_loop(0, SEQ // TQ, q_tile, 0, unroll=Q_TILE_UNROLL)
    if with_ctx:
        s = _dot_nt(qc_ref[...], kc_ref[...])
        oc_ref[...] = _softmax_pv([s], [vxc_ref[...]]).astype(BF16)


def _mla_attn(ql, kl, vl, qc, kc, vc, with_ctx):
    head = lambda b, h: (b, h)
    in_specs = [
        pl.BlockSpec((SEQ, QK_PAD), head), pl.BlockSpec((SEQ, QK_PAD), head), pl.BlockSpec((SEQ, MLA_V), head),
        pl.BlockSpec((CTX_LEN, QK_PAD), head), pl.BlockSpec((CTX_LEN, MLA_V), head),
    ]
    args = [ql, kl, vl, kc, vc]
    out_shape = [jax.ShapeDtypeStruct((BATCH * SEQ, MLA_HEADS * MLA_V), BF16)]
    out_specs = [pl.BlockSpec((SEQ, MLA_V), head)]
    if with_ctx:
        in_specs.append(pl.BlockSpec((CTX_LEN, QK_PAD), head))
        args.append(qc)
        out_shape.append(jax.ShapeDtypeStruct((BATCH * CTX_LEN, MLA_HEADS * MLA_V), BF16))
        out_specs.append(pl.BlockSpec((CTX_LEN, MLA_V), head))
    vmem = (2 * (3 * _nbytes((SEQ, QK_PAD), BF16) + 2 * _nbytes((SEQ, MLA_V), BF16)) + 2 * _nbytes((SEQ, 2 * LANES), BF16)
            + 2 * Q_TILE_UNROLL * _nbytes((TQ, SEQ + CTX_LEN), F32))
    return pl.pallas_call(
        functools.partial(_mla_attn_kernel, with_ctx=with_ctx),
        out_shape=tuple(out_shape), grid=(BATCH, MLA_HEADS), in_specs=in_specs, out_specs=tuple(out_specs),
        scratch_shapes=[pltpu.VMEM((SEQ, 2 * LANES), BF16), pltpu.VMEM((CTX_LEN, 2 * LANES), BF16)],
        compiler_params=_params(("parallel", "parallel"), vmem),
        name="mla_attn_ctx" if with_ctx else "mla_attn",
    )(*args)


def _conv_kernel(gb_ref, gc_ref, h_ref, w_ref, o_ref):
    n = o_ref.shape[0]
    u = gc_ref[...].astype(F32) * h_ref[...].astype(F32)
    row = lax.broadcasted_iota(jnp.int32, u.shape, 0)
    u_prev = jnp.where(row == 0, 0.0, pltpu.roll(u, 1, 0))
    u_next = jnp.where(row == n - 1, 0.0, pltpu.roll(u, n - 1, 0))
    y = u_prev * w_ref[0:1, :] + u * w_ref[1:2, :] + u_next * w_ref[2:3, :]
    o_ref[...] = (gb_ref[...].astype(F32) * y).astype(BF16)


def _conv(z, conv_w, layer, seq_len):
    rows = z.shape[0]
    blk = (seq_len, CONV_DIM)
    vmem = 2 * 4 * _nbytes(blk, BF16) + 8 * _nbytes(blk, F32)
    return pl.pallas_call(
        _conv_kernel,
        out_shape=jax.ShapeDtypeStruct((rows, CONV_DIM), BF16),
        grid=(rows // seq_len,),
        in_specs=[
            pl.BlockSpec(blk, lambda s: (s, Z_CONV_B // CONV_DIM)),
            pl.BlockSpec(blk, lambda s: (s, Z_CONV_C // CONV_DIM)),
            pl.BlockSpec(blk, lambda s: (s, Z_CONV_H // CONV_DIM)),
            pl.BlockSpec((None,) + conv_w.shape[1:], lambda s: (layer, 0, 0)),
        ],
        out_specs=pl.BlockSpec(blk, lambda s: (s, 0)),
        compiler_params=_params(("parallel",), vmem),
        name="short_conv",
    )(z, z, z, conv_w)


def _na_band_start(block):
    return min(max(block * NA_RQ - NA_WIN_R // 2, 0), GRID_ROWS - NA_RK)


def _na_bias_kind(block):
    return 0 if block == 0 else (2 if block == NA_BLOCKS - 1 else 1)


def _na_row_offsets():
    n_dr = 2 * NA_WIN_R - 1
    idx = np.full((3, NA_RQ, NA_RK), n_dr, np.int32)
    for kind, block in ((0, 0), (1, 2), (2, NA_BLOCKS - 1)):
        for qr in range(NA_RQ):
            r = block * NA_RQ + qr
            r0 = min(max(r - NA_WIN_R // 2, 0), GRID_ROWS - NA_WIN_R)
            for ki in range(NA_RK):
                kr = _na_band_start(block) + ki
                if r0 <= kr < r0 + NA_WIN_R:
                    idx[kind, qr, ki] = kr - r + NA_WIN_R - 1
    return idx


def _na_toeplitz(rpb):
    n_dr = 2 * NA_WIN_R - 1
    neg = -jnp.inf
    rpb = rpb.reshape(DEPTH * NA_HEADS, n_dr, 2 * NA_WIN_C - 1) * LOG2E
    side = GRID_W - 1 - (NA_WIN_C - 1)
    w = jnp.pad(rpb, ((0, 0), (0, 0), (side, side + 1)), constant_values=neg)
    flat = jnp.tile(w, (1, 1, GRID_W))
    lo = GRID_W - 1
    t = flat[:, :, lo:lo + GRID_W * (2 * GRID_W - 1)].reshape(-1, n_dr, GRID_W, 2 * GRID_W - 1)[..., :GRID_W]
    col = np.arange(GRID_W)
    c0 = np.clip(col - NA_WIN_C // 2, 0, GRID_W - NA_WIN_C)
    col_ok = (col[None, :] >= c0[:, None]) & (col[None, :] < c0[:, None] + NA_WIN_C)
    t = jnp.where(col_ok[None, None], t, neg)
    return jnp.pad(t, ((0, 0), (0, 1), (0, 0), (0, 0)), constant_values=neg)


def _na_bias_kernel(t_ref, o_ref):
    offsets = _na_row_offsets()
    for kind in range(3):
        for qr in range(NA_RQ):
            for ki in range(NA_RK):
                o_ref[kind, qr * GRID_W:(qr + 1) * GRID_W, ki * GRID_W:(ki + 1) * GRID_W] = (
                    t_ref[int(offsets[kind, qr, ki])])


def _na_bias(rpb):
    t = _na_toeplitz(rpb)
    vmem = 2 * (_nbytes((t.shape[1], GRID_W, LANES), F32) + _nbytes((3, NA_NQ, NA_NK + GRID_W), F32))
    return pl.pallas_call(
        _na_bias_kernel,
        out_shape=jax.ShapeDtypeStruct((t.shape[0], 3, NA_NQ, NA_NK), F32),
        grid=(t.shape[0],),
        in_specs=[pl.BlockSpec((None,) + t.shape[1:], lambda g: (g, 0, 0, 0))],
        out_specs=pl.BlockSpec((None, 3, NA_NQ, NA_NK), lambda g: (g, 0, 0, 0)),
        compiler_params=_params(("parallel",), vmem),
        name="na_bias",
    )(t)


def _na_kernel(*refs, with_ctx):
    if with_ctx:
        q_ref, k_ref, v_ref, kc_ref, vc_ref, bias_ref, qc_ref, o_ref, oc_ref, vx_ref, vxc_ref = refs
    else:
        q_ref, k_ref, v_ref, kc_ref, vc_ref, bias_ref, o_ref, vx_ref, vxc_ref = refs
    _fill_values_ones(v_ref, vx_ref)
    _fill_values_ones(vc_ref, vxc_ref)
    for block in range(NA_BLOCKS):
        q = _scaled_q(q_ref[block * NA_NQ:(block + 1) * NA_NQ, :], NA_HEAD_DIM)
        k0 = _na_band_start(block) * GRID_W
        s_loc = _dot_nt(q, k_ref[k0:k0 + NA_NK, :]) + bias_ref[_na_bias_kind(block)]
        s_ctx = _dot_nt(q, kc_ref[...])
        o = _softmax_pv([s_loc, s_ctx], [vx_ref[k0:k0 + NA_NK, :], vxc_ref[...]])
        o_ref[block * NA_NQ:(block + 1) * NA_NQ, :] = o.astype(BF16)
    if with_ctx:
        s = _dot_nt(_scaled_q(qc_ref[...], NA_HEAD_DIM), kc_ref[...])
        oc_ref[...] = _softmax_pv([s], [vxc_ref[...]]).astype(BF16)


def _na_attn(zl, zc, bias, layer, with_ctx):
    d = NA_HEAD_DIM
    col = lambda off: (lambda b, h: (b, off // d + h))
    in_specs = [
        pl.BlockSpec((SEQ, d), col(Z_NA_Q)), pl.BlockSpec((SEQ, d), col(Z_NA_K)), pl.BlockSpec((SEQ, d), col(Z_NA_V)),
        pl.BlockSpec((CTX_LEN, d), col(Z_NA_K)), pl.BlockSpec((CTX_LEN, d), col(Z_NA_V)),
        pl.BlockSpec((None, 3, NA_NQ, NA_NK), lambda b, h: (layer * NA_HEADS + h, 0, 0, 0)),
    ]
    args = [zl, zl, zl, zc, zc, bias]
    out_shape = [jax.ShapeDtypeStruct((BATCH * SEQ, NA_DIM), BF16)]
    out_specs = [pl.BlockSpec((SEQ, d), lambda b, h: (b, h))]
    if with_ctx:
        in_specs.append(pl.BlockSpec((CTX_LEN, d), col(Z_NA_Q)))
        args.append(zc)
        out_shape.append(jax.ShapeDtypeStruct((BATCH * CTX_LEN, NA_DIM), BF16))
        out_specs.append(pl.BlockSpec((CTX_LEN, d), lambda b, h: (b, h)))
    vmem = (2 * (5 * _nbytes((SEQ, d), BF16) + _nbytes((3, NA_NQ, NA_NK), F32)) + 2 * _nbytes((SEQ, 2 * LANES), BF16)
            + 8 * _nbytes((NA_NQ, NA_NK + CTX_LEN), F32))
    return pl.pallas_call(
        functools.partial(_na_kernel, with_ctx=with_ctx),
        out_shape=tuple(out_shape), grid=(BATCH, NA_HEADS), in_specs=in_specs, out_specs=tuple(out_specs),
        scratch_shapes=[pltpu.VMEM((SEQ, 2 * LANES), BF16), pltpu.VMEM((CTX_LEN, 2 * LANES), BF16)],
        compiler_params=_params(("parallel", "parallel"), vmem),
        name="na_attn_ctx" if with_ctx else "na_attn",
    )(*args)


def _out_proj_kernel(g_ref, a_ref, c_ref, n_ref, x_ref, w_ref, lng_ref, lnb_ref, o_ref):
    a_w = MLA_HEADS * MLA_V
    for r0 in range(0, o_ref.shape[0], TM_OUT_CHUNK):
        rows = slice(r0, r0 + TM_OUT_CHUNK)
        y = (_dot(a_ref[rows, :], w_ref[0:a_w, :]) + _dot(c_ref[rows, :], w_ref[a_w:a_w + CONV_DIM, :])
             + _dot(n_ref[rows, :], w_ref[a_w + CONV_DIM:, :]))
        r = DEEPNORM_ALPHA * x_ref[rows, :] + g_ref[...] * y
        o_ref[rows, :] = _layer_norm(r, lng_ref[...], lnb_ref[...])


def _out_proj(a, cv, n, x, mods, w_out, ln_g, ln_b, layer, row_of_tile):
    rows = x.shape[0]
    tm = TM_OUT
    row = lambda i: (i, 0)
    per_layer = lambda i: (layer, 0, 0)
    vmem = (2 * (2 * _nbytes((tm, D_MODEL), F32) + _nbytes((tm, D_MODEL), BF16)) + _nbytes(w_out.shape[1:], BF16)
            + 3 * _nbytes((TM_OUT_CHUNK, D_MODEL), F32))
    return pl.pallas_call(
        _out_proj_kernel,
        out_shape=jax.ShapeDtypeStruct((rows, D_MODEL), F32),
        grid=(rows // tm,),
        in_specs=[
            _mod_spec(layer, 2, row_of_tile),
            pl.BlockSpec((tm, a.shape[1]), row), pl.BlockSpec((tm, cv.shape[1]), row),
            pl.BlockSpec((tm, n.shape[1]), row), pl.BlockSpec((tm, D_MODEL), row),
            pl.BlockSpec((None,) + w_out.shape[1:], per_layer, pipeline_mode=pl.Buffered(1)),
            pl.BlockSpec((None, 1, D_MODEL), per_layer), pl.BlockSpec((None, 1, D_MODEL), per_layer),
        ],
        out_specs=pl.BlockSpec((tm, D_MODEL), row),
        compiler_params=_params(("parallel",), vmem),
        name="out_proj",
    )(mods, a, cv, n, x, w_out, ln_g, ln_b)


def _ffn_kernel(sh_ref, sc_ref, g_ref, x_ref, wg_ref, wu_ref, wd_ref, lng_ref, lnb_ref, o_ref, xm_ref):
    f = pl.program_id(1)

    @pl.when(f == 0)
    def _():
        x = x_ref[...]
        xm_ref[...] = (x * (1.0 + sc_ref[...]) + sh_ref[...]).astype(BF16)
        o_ref[...] = DEEPNORM_ALPHA * x

    xm = xm_ref[...]
    gate = _dot(xm, wg_ref[...])
    up = _dot(xm, wu_ref[...])
    hidden = (gate * jax.nn.sigmoid(gate) * up).astype(BF16)
    o_ref[...] += g_ref[...] * _dot(hidden, wd_ref[...])

    @pl.when(f == pl.num_programs(1) - 1)
    def _():
        o_ref[...] = _layer_norm(o_ref[...], lng_ref[...], lnb_ref[...])


def _ffn(x, mods, w_gate, w_up, w_down, ln_g, ln_b, layer, row_of_tile):
    rows = x.shape[0]
    per_layer = lambda i, f: (layer, 0, 0)
    vmem = (3 * _nbytes((TM, D_MODEL), F32) + 2 * 3 * _nbytes((D_MODEL, TF), BF16)
            + _nbytes((TM, D_MODEL), BF16) + 3 * _nbytes((TM, TF), F32) + _nbytes((TM, D_MODEL), F32))
    return pl.pallas_call(
        _ffn_kernel,
        out_shape=jax.ShapeDtypeStruct((rows, D_MODEL), F32),
        grid=(rows // TM, D_FF // TF),
        in_specs=[
            _mod_spec(layer, 3, row_of_tile), _mod_spec(layer, 4, row_of_tile), _mod_spec(layer, 5, row_of_tile),
            pl.BlockSpec((TM, D_MODEL), lambda i, f: (i, 0), pipeline_mode=pl.Buffered(1)),
            pl.BlockSpec((None, D_MODEL, TF), lambda i, f: (layer, 0, f)),
            pl.BlockSpec((None, D_MODEL, TF), lambda i, f: (layer, 0, f)),
            pl.BlockSpec((None, TF, D_MODEL), lambda i, f: (layer, f, 0)),
            pl.BlockSpec((None, 1, D_MODEL), per_layer), pl.BlockSpec((None, 1, D_MODEL), per_layer),
        ],
        out_specs=pl.BlockSpec((TM, D_MODEL), lambda i, f: (i, 0)),
        scratch_shapes=[pltpu.VMEM((TM, D_MODEL), BF16)],
        compiler_params=_params(("parallel", "arbitrary"), vmem),
        name="ffn",
    )(mods, mods, mods, x, w_gate, w_up, w_down, ln_g, ln_b)


def _rotate_half_cols(w):
    a1, a2, b1, b2 = jnp.split(w, 4, axis=-1)
    return jnp.concatenate([-a2, a1, -b2, b1], axis=-1)


def _w_in_prep_src(piece):
    rope = OFF_ROPE // PREP_PIECE
    tail = Z_ROPE // PREP_PIECE
    return jnp.where(piece < rope, piece, jnp.where(piece < tail, piece + 1, rope))


def _w_in_prep_kernel(*refs):
    *piece_refs, o_ref = refs
    pieces = [r[...] for r in piece_refs]
    n = len(pieces)
    is_last = pl.program_id(1) == pl.num_programs(1) - 1
    rot_piece = Z_DIM // PREP_PIECE - 1
    for p in range(n):
        piece = (Z_PAD // PREP_COLS - 1) * n + p
        if piece == rot_piece:
            a1, a2, b1, b2 = jnp.split(pieces[p], 4, axis=0)
            pieces[p] = jnp.where(is_last, jnp.concatenate([-a2, a1, -b2, b1], axis=0), pieces[p])
        elif piece > rot_piece:
            pieces[p] = jnp.where(is_last, 0.0, pieces[p])
    o_ref[...] = jnp.concatenate(pieces, axis=0).T.astype(BF16)


def _w_in_prep(w_in):
    w_t = jnp.swapaxes(w_in, 1, 2)
    n = PREP_COLS // PREP_PIECE
    piece_spec = lambda p: pl.BlockSpec((None, PREP_PIECE, D_MODEL), lambda l, g: (l, _w_in_prep_src(g * n + p), 0))
    vmem = 2 * (_nbytes((PREP_COLS, D_MODEL), F32) + _nbytes((D_MODEL, PREP_COLS), BF16)) + 3 * _nbytes(
        (PREP_COLS, D_MODEL), F32)
    return pl.pallas_call(
        _w_in_prep_kernel,
        out_shape=jax.ShapeDtypeStruct((DEPTH, D_MODEL, Z_PAD), BF16),
        grid=(DEPTH, Z_PAD // PREP_COLS),
        in_specs=[piece_spec(p) for p in range(n)],
        out_specs=pl.BlockSpec((None, D_MODEL, PREP_COLS), lambda l, g: (l, 0, g)),
        compiler_params=_params(("parallel", "parallel"), vmem),
        name="w_in_prep",
    )(*([w_t] * n))


def _relayout_wq(w):
    w = w.reshape(DEPTH, MLA_Q_RANK, MLA_HEADS, MLA_QK)
    nope, rope = w[..., :MLA_NOPE], w[..., MLA_NOPE:]
    zeros = jnp.zeros_like(rope)
    cols = jnp.concatenate([nope, rope, zeros, _rotate_half_cols(rope), zeros], axis=-1)
    return cols.reshape(DEPTH, MLA_Q_RANK, MLA_HEADS * WQ_HEAD).astype(BF16)


def _rope_tables():
    t = jnp.arange(SEQ)
    row = (t // GRID_W).astype(F32)
    col = (t % GRID_W).astype(F32)
    n_freq = MLA_ROPE // 4
    inv = ROPE_THETA ** (-jnp.arange(n_freq, dtype=F32) / n_freq)
    ar = row[:, None] * inv
    ac = col[:, None] * inv
    ang = jnp.concatenate([ar, ar, ac, ac], axis=-1)
    cos, sin = jnp.cos(ang), jnp.sin(ang)
    ones, zeros = jnp.ones((TM_PROJ, MLA_ROPE), F32), jnp.zeros((TM_PROJ, MLA_ROPE), F32)
    lat = (jnp.concatenate([cos, sin], axis=1), jnp.concatenate([sin, cos], axis=1))
    ctx = (jnp.concatenate([ones, zeros], axis=1), jnp.concatenate([zeros, ones], axis=1))
    return lat, ctx


def kernel(x, c, ctx, c_ctx, ada_w, ada_b, w_in, mla_q_norm, mla_wq_b, mla_kv_norm, mla_wkv_b, conv_w, na_rpb,
           w_out, ln1_g, ln1_b, ffn_w_gate, ffn_w_up, ffn_w_down, ln2_g, ln2_b):
    assert x.shape == (BATCH, SEQ, D_MODEL) and ctx.shape == (BATCH, CTX_LEN, D_MODEL)
    cc = jnp.concatenate([c, c_ctx[None, :], jnp.zeros((MOD_ROWS - BATCH - 1, D_MODEL), F32)], axis=0)
    mods = _ada(cc, ada_w, ada_b).reshape(DEPTH * MOD_ROWS * N_MOD, 1, D_MODEL)
    (ta_lat, tb_lat), (ta_ctx, tb_ctx) = _rope_tables()
    rows3d = lambda v: v.reshape(DEPTH, 1, -1)

    w_in_p = _w_in_prep(w_in)
    wq_all = _relayout_wq(mla_wq_b)
    wkv = mla_wkv_b.astype(BF16)
    g_q, g_kv = rows3d(mla_q_norm), rows3d(mla_kv_norm)
    bias = _na_bias(na_rpb)
    w_o = w_out.astype(BF16)
    w_g, w_u, w_d = ffn_w_gate.astype(BF16), ffn_w_up.astype(BF16), ffn_w_down.astype(BF16)
    ln1 = (rows3d(ln1_g), rows3d(ln1_b))
    ln2 = (rows3d(ln2_g), rows3d(ln2_b))

    xl = x.reshape(BATCH * SEQ, D_MODEL)
    xc = ctx.reshape(BATCH * CTX_LEN, D_MODEL)
    lat_row = lambda tm: (lambda i: i // (SEQ // tm))
    ctx_row = lambda i: CTX_MOD_ROW
    for l in range(DEPTH):
        last = l == DEPTH - 1
        zl = _in_proj(xl, mods, w_in_p, l, lat_row(TM))
        zc = _in_proj(xc, mods, w_in_p, l, ctx_row)
        ql, kl, vl = _mla_proj(zl, g_q, g_kv, wq_all, wkv, ta_lat, tb_lat, l, lambda i: i % (SEQ // TM_PROJ))
        qc, kc, vc = _mla_proj(zc, g_q, g_kv, wq_all, wkv, ta_ctx, tb_ctx, l, lambda i: 0)
        a = _mla_attn(ql, kl, vl, qc, kc, vc, with_ctx=not last)
        n = _na_attn(zl, zc, bias, l, with_ctx=not last)
        cv_l = _conv(zl, conv_w, l, SEQ)
        x1 = _out_proj(a[0], cv_l, n[0], xl, mods, w_o, *ln1, l, lat_row(TM_OUT))
        xl = _ffn(x1, mods, w_g, w_u, w_d, *ln2, l, lat_row(TM))
        if not last:
            cv_c = _conv(zc, conv_w, l, CTX_LEN)
            x1c = _out_proj(a[1], cv_c, n[1], xc, mods, w_o, *ln1, l, ctx_row)
            xc = _ffn(x1c, mods, w_g, w_u, w_d, *ln2, l, ctx_row)
    return xl.reshape(BATCH, SEQ, D_MODEL)
```

```python
import functools

import numpy as np
import jax
import jax.numpy as jnp
from jax import lax
from jax.experimental import pallas as pl
from jax.experimental.pallas import tpu as pltpu

D_MODEL = 2048
BATCH = 4
SEQ = 2048
DEPTH = 2
CTX_LEN = 256
GRID_W = 64
GRID_ROWS = SEQ // GRID_W
MLA_HEADS = 6
MLA_Q_RANK = 512
MLA_KV_RANK = 512
MLA_NOPE = 128
MLA_ROPE = 64
MLA_V = 128
MLA_QK = MLA_NOPE + MLA_ROPE
CONV_DIM = 512
NA_HEADS = 6
NA_HEAD_DIM = 128
NA_DIM = NA_HEADS * NA_HEAD_DIM
NA_WIN_R = 8
NA_WIN_C = 16
OFF_MLA_Q = 0
OFF_MLA_KV = OFF_MLA_Q + MLA_Q_RANK
OFF_ROPE = OFF_MLA_KV + MLA_KV_RANK
OFF_CONV_B = OFF_ROPE + MLA_ROPE
OFF_CONV_C = OFF_CONV_B + CONV_DIM
OFF_CONV_H = OFF_CONV_C + CONV_DIM
OFF_NA_Q = OFF_CONV_H + CONV_DIM
OFF_NA_K = OFF_NA_Q + NA_DIM
OFF_NA_V = OFF_NA_K + NA_DIM
IN_DIM = OFF_NA_V + NA_DIM
D_FF = -(-8 * D_MODEL // (3 * 256)) * 256
ROPE_THETA = 10000.0
LN_EPS = 1e-6
RMS_EPS = 1e-6
DEEPNORM_ALPHA = (2 * DEPTH) ** 0.25
LOG2E = 1.4426950408889634
N_MOD = 6
MOD_ROWS = 8
CTX_MOD_ROW = BATCH

LANES = 128
V7X_VMEM_BYTES = 64 * 1024 * 1024
V7X_VMEM_BUDGET = 58 * 1024 * 1024

Z_Q = 0
Z_KV = Z_Q + MLA_Q_RANK
Z_CONV_B = Z_KV + MLA_KV_RANK
Z_CONV_C = Z_CONV_B + CONV_DIM
Z_CONV_H = Z_CONV_C + CONV_DIM
Z_NA_Q = Z_CONV_H + CONV_DIM
Z_NA_K = Z_NA_Q + NA_DIM
Z_NA_V = Z_NA_K + NA_DIM
Z_ROPE = Z_NA_V + NA_DIM
Z_DIM = Z_ROPE + 2 * MLA_ROPE
V7X_MXU_COLS = 256
Z_PAD = -(-Z_DIM // (5 * V7X_MXU_COLS)) * (5 * V7X_MXU_COLS)

QK_PAD = 2 * LANES
WQ_HEAD = 3 * LANES

TM = 1024
TM_OUT = 1024
TM_OUT_CHUNK = 256
TM_PROJ = 512
PREP_PIECE = MLA_ROPE
PREP_COLS = 5 * LANES
TN_IN = 5 * V7X_MXU_COLS
TN_ADA = 1024
TF = 512
TQ = 256
Q_TILE_UNROLL = 8
NA_RQ = 4
NA_RK = NA_RQ + NA_WIN_R - 1
NA_NQ = NA_RQ * GRID_W
NA_NK = NA_RK * GRID_W
NA_BLOCKS = GRID_ROWS // NA_RQ
FFN_CAST_BLOCKS = 22
OUT_CAST_BLOCKS = 16

F32 = jnp.float32
BF16 = jnp.bfloat16


def _params(semantics, vmem_bytes):
    assert vmem_bytes <= V7X_VMEM_BUDGET, vmem_bytes
    return pltpu.CompilerParams(dimension_semantics=semantics, vmem_limit_bytes=int(vmem_bytes))


def _nbytes(shape, dtype):
    return int(np.prod(shape)) * jnp.dtype(dtype).itemsize


def _dot(a, b):
    return jnp.dot(a, b, preferred_element_type=F32)


def _dot_nt(a, b):
    return lax.dot_general(a, b, (((1,), (1,)), ((), ())), preferred_element_type=F32)


def _mod_spec(layer, chunk, row_of_tile):
    base = layer * MOD_ROWS * N_MOD + chunk
    return pl.BlockSpec((None, 1, D_MODEL), lambda i, *_: (base + row_of_tile(i) * N_MOD, 0, 0))


def _layer_norm(r, g, b):
    mu = jnp.mean(r, axis=-1, keepdims=True)
    c = r - mu
    var = jnp.mean(c * c, axis=-1, keepdims=True)
    return c * lax.rsqrt(var + LN_EPS) * g + b


def _ada_kernel(cc_ref, w_ref, b_ref, o_ref):
    cc = cc_ref[...]
    s = (cc * jax.nn.sigmoid(cc)).astype(BF16)
    o_ref[...] = _dot(s, w_ref[...].astype(BF16)) + b_ref[...]


def _ada(cc, ada_w, ada_b):
    n = N_MOD * D_MODEL
    vmem = 2 * (_nbytes((D_MODEL, TN_ADA), F32) + _nbytes((MOD_ROWS, TN_ADA), F32) * 2
                + _nbytes((MOD_ROWS, D_MODEL), F32)) + _nbytes((D_MODEL, TN_ADA), BF16) * 2
    return pl.pallas_call(
        _ada_kernel,
        out_shape=jax.ShapeDtypeStruct((DEPTH, MOD_ROWS, n), F32),
        grid=(DEPTH, n // TN_ADA),
        in_specs=[
            pl.BlockSpec((MOD_ROWS, D_MODEL), lambda l, j: (0, 0)),
            pl.BlockSpec((None, D_MODEL, TN_ADA), lambda l, j: (l, 0, j)),
            pl.BlockSpec((None, 1, TN_ADA), lambda l, j: (l, 0, j)),
        ],
        out_specs=pl.BlockSpec((None, MOD_ROWS, TN_ADA), lambda l, j: (l, 0, j)),
        compiler_params=_params(("parallel", "parallel"), vmem),
        name="ada_mod",
    )(cc, ada_w, ada_b.reshape(DEPTH, 1, n))


def _in_proj_kernel(sh_ref, sc_ref, x_ref, w_ref, o_ref, xm_ref):
    @pl.when(pl.program_id(1) == 0)
    def _():
        xm_ref[...] = (x_ref[...] * (1.0 + sc_ref[...]) + sh_ref[...]).astype(BF16)

    o_ref[...] = _dot(xm_ref[...], w_ref[...]).astype(BF16)


def _in_proj(x, mods, w_in_p, layer, row_of_tile):
    rows = x.shape[0]
    vmem = (2 * (_nbytes((TM, D_MODEL), F32) + _nbytes((D_MODEL, TN_IN), BF16) + _nbytes((TM, TN_IN), BF16))
            + _nbytes((TM, D_MODEL), BF16) * 2 + _nbytes((TM, TN_IN), F32))
    return pl.pallas_call(
        _in_proj_kernel,
        out_shape=jax.ShapeDtypeStruct((rows, Z_PAD), BF16),
        grid=(rows // TM, Z_PAD // TN_IN),
        in_specs=[
            _mod_spec(layer, 0, row_of_tile),
            _mod_spec(layer, 1, row_of_tile),
            pl.BlockSpec((TM, D_MODEL), lambda i, j: (i, 0)),
            pl.BlockSpec((None, D_MODEL, TN_IN), lambda i, j: (layer, 0, j)),
        ],
        out_specs=pl.BlockSpec((TM, TN_IN), lambda i, j: (i, j)),
        scratch_shapes=[pltpu.VMEM((TM, D_MODEL), BF16)],
        compiler_params=_params(("parallel", "arbitrary"), vmem),
        name="in_proj",
    )(mods, mods, x, w_in_p)


def _rms_norm(x, g):
    return x * lax.rsqrt(jnp.mean(x * x, axis=-1, keepdims=True) + RMS_EPS) * g


def _mla_proj_kernel(z_ref, slot_ref, gq_ref, gkv_ref, wq_ref, wkv_ref, ta_ref, tb_ref, q_ref, k_ref, v_ref):
    cq = z_ref[:, Z_Q:Z_Q + MLA_Q_RANK].astype(F32)
    ckv = z_ref[:, Z_KV:Z_KV + MLA_KV_RANK].astype(F32)
    qa = _dot(_rms_norm(cq, gq_ref[...]).astype(BF16), wq_ref[...]) * (MLA_QK ** -0.5 * LOG2E)
    kv = _dot(_rms_norm(ckv, gkv_ref[...]).astype(BF16), wkv_ref[...])
    ta = ta_ref[...]
    tb = tb_ref[...]
    y = slot_ref[...].astype(F32) * ta
    kro = y + pltpu.roll(y, MLA_ROPE, 1)
    lane = lax.broadcasted_iota(jnp.int32, kro.shape, 1)
    kro = jnp.where(lane < MLA_ROPE, kro, 0.0).astype(BF16)
    for h in range(MLA_HEADS):
        qb = h * WQ_HEAD
        ob = h * QK_PAD
        q_ref[:, ob:ob + LANES] = qa[:, qb:qb + LANES].astype(BF16)
        q_ref[:, ob + LANES:ob + QK_PAD] = (
            qa[:, qb + LANES:qb + 2 * LANES] * ta + qa[:, qb + 2 * LANES:qb + 3 * LANES] * tb).astype(BF16)
        k_ref[:, ob:ob + LANES] = kv[:, ob:ob + LANES].astype(BF16)
        k_ref[:, ob + LANES:ob + QK_PAD] = kro
        v_ref[:, h * MLA_V:(h + 1) * MLA_V] = kv[:, ob + LANES:ob + QK_PAD].astype(BF16)


def _mla_proj(z, g_q, g_kv, wq_all, wkv, ta, tb, layer, table_tile):
    rows = z.shape[0]
    per_layer = lambda i: (layer, 0, 0)
    zw = Z_KV + MLA_KV_RANK
    hq = MLA_HEADS * QK_PAD
    hv = MLA_HEADS * MLA_V
    tm = TM_PROJ
    vmem = (2 * (_nbytes((tm, zw), BF16) + _nbytes((tm, LANES), BF16) + 2 * _nbytes((tm, LANES), F32)
                 + _nbytes(wq_all.shape[1:], BF16) + _nbytes(wkv.shape[1:], BF16)
                 + 2 * _nbytes((tm, hq), BF16) + _nbytes((tm, hv), BF16))
            + _nbytes((tm, MLA_HEADS * WQ_HEAD), F32) + _nbytes((tm, hq), F32) + 4 * _nbytes((tm, zw), F32))
    return pl.pallas_call(
        _mla_proj_kernel,
        out_shape=(jax.ShapeDtypeStruct((rows, hq), BF16), jax.ShapeDtypeStruct((rows, hq), BF16),
                   jax.ShapeDtypeStruct((rows, hv), BF16)),
        grid=(rows // tm,),
        in_specs=[
            pl.BlockSpec((tm, zw), lambda i: (i, 0)),
            pl.BlockSpec((tm, 2 * MLA_ROPE), lambda i: (i, Z_ROPE // (2 * MLA_ROPE))),
            pl.BlockSpec((None, 1, MLA_Q_RANK), per_layer),
            pl.BlockSpec((None, 1, MLA_KV_RANK), per_layer),
            pl.BlockSpec((None,) + wq_all.shape[1:], per_layer),
            pl.BlockSpec((None,) + wkv.shape[1:], per_layer),
            pl.BlockSpec((tm, LANES), lambda i: (table_tile(i), 0)),
            pl.BlockSpec((tm, LANES), lambda i: (table_tile(i), 0)),
        ],
        out_specs=(pl.BlockSpec((tm, hq), lambda i: (i, 0)), pl.BlockSpec((tm, hq), lambda i: (i, 0)),
                   pl.BlockSpec((tm, hv), lambda i: (i, 0))),
        compiler_params=_params(("parallel",), vmem),
        name="mla_proj",
    )(z, z, g_q, g_kv, wq_all, wkv, ta, tb)


def _rider_specs(w, layer, n_blocks, axis, n_heads):
    rows, cols = w.shape[1:]
    block = (rows // n_blocks, cols) if axis == 0 else (rows, cols // n_blocks)
    assert block[0] * block[1] * n_blocks == rows * cols and n_blocks <= BATCH * n_heads

    def at(b, h):
        step = jnp.minimum(b * n_heads + h, n_blocks - 1)
        return (step, 0) if axis == 0 else (0, step)

    in_spec = pl.BlockSpec((None,) + block, lambda b, h: (layer,) + at(b, h))
    return in_spec, pl.BlockSpec(block, at), jax.ShapeDtypeStruct((rows, cols), BF16), block


def _cast_riders(in_refs, out_refs):
    for i_ref, o_ref in zip(in_refs, out_refs):
        o_ref[...] = i_ref[...].astype(BF16)


def _split_refs(refs, *counts):
    out, pos = [], 0
    for c in counts:
        out.append(refs[pos:pos + c])
        pos += c
    assert pos == len(refs)
    return out


def _softmax_pv(scores, values_ones):
    m = functools.reduce(jnp.maximum, [jnp.max(s, axis=-1, keepdims=True) for s in scores])
    acc = functools.reduce(jnp.add, [_dot(jnp.exp2(s - m).astype(BF16), v) for s, v in zip(scores, values_ones)])
    return acc[:, :LANES] / acc[:, LANES:]


def _fill_values_ones(v_ref, vx_ref):
    vx_ref[:, :LANES] = v_ref[...]
    vx_ref[:, LANES:] = jnp.ones((vx_ref.shape[0], LANES), BF16)


def _scaled_q(q, head_dim):
    return (q.astype(F32) * (head_dim ** -0.5 * LOG2E)).astype(BF16)


def _mla_attn_kernel(*refs, with_ctx, n_riders):
    ins, rider_in, outs, rider_out, (vxl_ref, vxc_ref) = _split_refs(
        refs, 6 if with_ctx else 5, n_riders, 2 if with_ctx else 1, n_riders, 2)
    q_ref, kl_ref, vl_ref, kc_ref, vc_ref = ins[:5]
    o_ref = outs[0]
    _cast_riders(rider_in, rider_out)
    _fill_values_ones(vl_ref, vxl_ref)
    _fill_values_ones(vc_ref, vxc_ref)

    def q_tile(t, carry):
        r0 = pl.multiple_of(t * TQ, TQ)
        q = q_ref[pl.ds(r0, TQ), :]
        s_lat = _dot_nt(q, kl_ref[...])
        s_ctx = _dot_nt(q, kc_ref[...])
        o = _softmax_pv([s_lat, s_ctx], [vxl_ref[...], vxc_ref[...]])
        o_ref[pl.ds(r0, TQ), :] = o.astype(BF16)
        return carry

    lax.fori_loop(0, SEQ // TQ, q_tile, 0, unroll=Q_TILE_UNROLL)
    if with_ctx:
        s = _dot_nt(ins[5][...], kc_ref[...])
        outs[1][...] = _softmax_pv([s], [vxc_ref[...]]).astype(BF16)


def _mla_attn(ql, kl, vl, qc, kc, vc, with_ctx, layer, riders):
    head = lambda b, h: (b, h)
    in_specs = [
        pl.BlockSpec((SEQ, QK_PAD), head), pl.BlockSpec((SEQ, QK_PAD), head), pl.BlockSpec((SEQ, MLA_V), head),
        pl.BlockSpec((CTX_LEN, QK_PAD), head), pl.BlockSpec((CTX_LEN, MLA_V), head),
    ]
    args = [ql, kl, vl, kc, vc]
    out_shape = [jax.ShapeDtypeStruct((BATCH * SEQ, MLA_HEADS * MLA_V), BF16)]
    out_specs = [pl.BlockSpec((SEQ, MLA_V), head)]
    if with_ctx:
        in_specs.append(pl.BlockSpec((CTX_LEN, QK_PAD), head))
        args.append(qc)
        out_shape.append(jax.ShapeDtypeStruct((BATCH * CTX_LEN, MLA_HEADS * MLA_V), BF16))
        out_specs.append(pl.BlockSpec((CTX_LEN, MLA_V), head))
    vmem = (2 * (3 * _nbytes((SEQ, QK_PAD), BF16) + 2 * _nbytes((SEQ, MLA_V), BF16)) + 2 * _nbytes((SEQ, 2 * LANES), BF16)
            + Q_TILE_UNROLL * _nbytes((TQ, SEQ + CTX_LEN), F32))
    n_base = len(out_shape)
    for w, n_blocks, axis in riders:
        in_spec, out_spec, shape, block = _rider_specs(w, layer, n_blocks, axis, MLA_HEADS)
        in_specs.append(in_spec)
        args.append(w)
        out_specs.append(out_spec)
        out_shape.append(shape)
        vmem += 2 * (_nbytes(block, F32) + _nbytes(block, BF16))
    res = pl.pallas_call(
        functools.partial(_mla_attn_kernel, with_ctx=with_ctx, n_riders=len(riders)),
        out_shape=tuple(out_shape), grid=(BATCH, MLA_HEADS), in_specs=in_specs, out_specs=tuple(out_specs),
        scratch_shapes=[pltpu.VMEM((SEQ, 2 * LANES), BF16), pltpu.VMEM((CTX_LEN, 2 * LANES), BF16)],
        compiler_params=_params(("arbitrary", "arbitrary"), vmem),
        name="mla_attn_ctx" if with_ctx else "mla_attn",
    )(*args)
    return res[:n_base], res[n_base:]


def _conv_kernel(gb_ref, gc_ref, h_ref, w_ref, o_ref):
    n = o_ref.shape[0]
    u = gc_ref[...].astype(F32) * h_ref[...].astype(F32)
    row = lax.broadcasted_iota(jnp.int32, u.shape, 0)
    u_prev = jnp.where(row == 0, 0.0, pltpu.roll(u, 1, 0))
    u_next = jnp.where(row == n - 1, 0.0, pltpu.roll(u, n - 1, 0))
    y = u_prev * w_ref[0:1, :] + u * w_ref[1:2, :] + u_next * w_ref[2:3, :]
    o_ref[...] = (gb_ref[...].astype(F32) * y).astype(BF16)


def _conv(z, conv_w, layer, seq_len):
    rows = z.shape[0]
    blk = (seq_len, CONV_DIM)
    vmem = 2 * 4 * _nbytes(blk, BF16) + 8 * _nbytes(blk, F32)
    return pl.pallas_call(
        _conv_kernel,
        out_shape=jax.ShapeDtypeStruct((rows, CONV_DIM), BF16),
        grid=(rows // seq_len,),
        in_specs=[
            pl.BlockSpec(blk, lambda s: (s, Z_CONV_B // CONV_DIM)),
            pl.BlockSpec(blk, lambda s: (s, Z_CONV_C // CONV_DIM)),
            pl.BlockSpec(blk, lambda s: (s, Z_CONV_H // CONV_DIM)),
            pl.BlockSpec((None,) + conv_w.shape[1:], lambda s: (layer, 0, 0)),
        ],
        out_specs=pl.BlockSpec(blk, lambda s: (s, 0)),
        compiler_params=_params(("parallel",), vmem),
        name="short_conv",
    )(z, z, z, conv_w)


def _na_band_start(block):
    return min(max(block * NA_RQ - NA_WIN_R // 2, 0), GRID_ROWS - NA_RK)


def _na_bias_kind(block):
    return 0 if block == 0 else (2 if block == NA_BLOCKS - 1 else 1)


def _na_row_offsets():
    n_dr = 2 * NA_WIN_R - 1
    idx = np.full((3, NA_RQ, NA_RK), n_dr, np.int32)
    for kind, block in ((0, 0), (1, 2), (2, NA_BLOCKS - 1)):
        for qr in range(NA_RQ):
            r = block * NA_RQ + qr
            r0 = min(max(r - NA_WIN_R // 2, 0), GRID_ROWS - NA_WIN_R)
            for ki in range(NA_RK):
                kr = _na_band_start(block) + ki
                if r0 <= kr < r0 + NA_WIN_R:
                    idx[kind, qr, ki] = kr - r + NA_WIN_R - 1
    return idx


def _na_toeplitz(rpb):
    n_dr = 2 * NA_WIN_R - 1
    neg = -jnp.inf
    rpb = rpb.reshape(DEPTH * NA_HEADS, n_dr, 2 * NA_WIN_C - 1) * LOG2E
    side = GRID_W - 1 - (NA_WIN_C - 1)
    w = jnp.pad(rpb, ((0, 0), (0, 0), (side, side + 1)), constant_values=neg)
    flat = jnp.tile(w, (1, 1, GRID_W))
    lo = GRID_W - 1
    t = flat[:, :, lo:lo + GRID_W * (2 * GRID_W - 1)].reshape(-1, n_dr, GRID_W, 2 * GRID_W - 1)[..., :GRID_W]
    col = np.arange(GRID_W)
    c0 = np.clip(col - NA_WIN_C // 2, 0, GRID_W - NA_WIN_C)
    col_ok = (col[None, :] >= c0[:, None]) & (col[None, :] < c0[:, None] + NA_WIN_C)
    t = jnp.where(col_ok[None, None], t, neg)
    return jnp.pad(t, ((0, 0), (0, 1), (0, 0), (0, 0)), constant_values=neg)


def _na_bias_kernel(t_ref, o_ref):
    offsets = _na_row_offsets()
    for kind in range(3):
        for qr in range(NA_RQ):
            for ki in range(NA_RK):
                o_ref[kind, qr * GRID_W:(qr + 1) * GRID_W, ki * GRID_W:(ki + 1) * GRID_W] = (
                    t_ref[int(offsets[kind, qr, ki])])


def _na_bias(rpb):
    t = _na_toeplitz(rpb)
    vmem = 2 * (_nbytes((t.shape[1], GRID_W, LANES), F32) + _nbytes((3, NA_NQ, NA_NK + GRID_W), F32))
    return pl.pallas_call(
        _na_bias_kernel,
        out_shape=jax.ShapeDtypeStruct((t.shape[0], 3, NA_NQ, NA_NK), F32),
        grid=(t.shape[0],),
        in_specs=[pl.BlockSpec((None,) + t.shape[1:], lambda g: (g, 0, 0, 0))],
        out_specs=pl.BlockSpec((None, 3, NA_NQ, NA_NK), lambda g: (g, 0, 0, 0)),
        compiler_params=_params(("parallel",), vmem),
        name="na_bias",
    )(t)


def _na_kernel(*refs, with_ctx, n_riders):
    ins, rider_in, outs, rider_out, (vx_ref, vxc_ref) = _split_refs(
        refs, 7 if with_ctx else 6, n_riders, 2 if with_ctx else 1, n_riders, 2)
    q_ref, k_ref, v_ref, kc_ref, vc_ref, bias_ref = ins[:6]
    o_ref = outs[0]
    _cast_riders(rider_in, rider_out)
    _fill_values_ones(v_ref, vx_ref)
    _fill_values_ones(vc_ref, vxc_ref)
    for block in range(NA_BLOCKS):
        q = _scaled_q(q_ref[block * NA_NQ:(block + 1) * NA_NQ, :], NA_HEAD_DIM)
        k0 = _na_band_start(block) * GRID_W
        s_loc = _dot_nt(q, k_ref[k0:k0 + NA_NK, :]) + bias_ref[_na_bias_kind(block)]
        s_ctx = _dot_nt(q, kc_ref[...])
        o = _softmax_pv([s_loc, s_ctx], [vx_ref[k0:k0 + NA_NK, :], vxc_ref[...]])
        o_ref[block * NA_NQ:(block + 1) * NA_NQ, :] = o.astype(BF16)
    if with_ctx:
        s = _dot_nt(_scaled_q(ins[6][...], NA_HEAD_DIM), kc_ref[...])
        outs[1][...] = _softmax_pv([s], [vxc_ref[...]]).astype(BF16)


def _na_attn(zl, zc, bias, layer, with_ctx, riders):
    d = NA_HEAD_DIM
    col = lambda off: (lambda b, h: (b, off // d + h))
    in_specs = [
        pl.BlockSpec((SEQ, d), col(Z_NA_Q)), pl.BlockSpec((SEQ, d), col(Z_NA_K)), pl.BlockSpec((SEQ, d), col(Z_NA_V)),
        pl.BlockSpec((CTX_LEN, d), col(Z_NA_K)), pl.BlockSpec((CTX_LEN, d), col(Z_NA_V)),
        pl.BlockSpec((None, 3, NA_NQ, NA_NK), lambda b, h: (layer * NA_HEADS + h, 0, 0, 0)),
    ]
    args = [zl, zl, zl, zc, zc, bias]
    out_shape = [jax.ShapeDtypeStruct((BATCH * SEQ, NA_DIM), BF16)]
    out_specs = [pl.BlockSpec((SEQ, d), lambda b, h: (b, h))]
    if with_ctx:
        in_specs.append(pl.BlockSpec((CTX_LEN, d), col(Z_NA_Q)))
        args.append(zc)
        out_shape.append(jax.ShapeDtypeStruct((BATCH * CTX_LEN, NA_DIM), BF16))
        out_specs.append(pl.BlockSpec((CTX_LEN, d), lambda b, h: (b, h)))
    vmem = (2 * (5 * _nbytes((SEQ, d), BF16) + _nbytes((3, NA_NQ, NA_NK), F32)) + 2 * _nbytes((SEQ, 2 * LANES), BF16)
            + 8 * _nbytes((NA_NQ, NA_NK + CTX_LEN), F32))
    n_base = len(out_shape)
    for w, n_blocks, axis in riders:
        in_spec, out_spec, shape, block = _rider_specs(w, layer, n_blocks, axis, NA_HEADS)
        in_specs.append(in_spec)
        args.append(w)
        out_specs.append(out_spec)
        out_shape.append(shape)
        vmem += 2 * (_nbytes(block, F32) + _nbytes(block, BF16))
    res = pl.pallas_call(
        functools.partial(_na_kernel, with_ctx=with_ctx, n_riders=len(riders)),
        out_shape=tuple(out_shape), grid=(BATCH, NA_HEADS), in_specs=in_specs, out_specs=tuple(out_specs),
        scratch_shapes=[pltpu.VMEM((SEQ, 2 * LANES), BF16), pltpu.VMEM((CTX_LEN, 2 * LANES), BF16)],
        compiler_params=_params(("arbitrary", "arbitrary"), vmem),
        name="na_attn_ctx" if with_ctx else "na_attn",
    )(*args)
    return res[:n_base], res[n_base:]


def _out_proj_kernel(g_ref, a_ref, c_ref, n_ref, x_ref, w_ref, lng_ref, lnb_ref, o_ref):
    a_w = MLA_HEADS * MLA_V
    for r0 in range(0, o_ref.shape[0], TM_OUT_CHUNK):
        rows = slice(r0, r0 + TM_OUT_CHUNK)
        y = (_dot(a_ref[rows, :], w_ref[0:a_w, :]) + _dot(c_ref[rows, :], w_ref[a_w:a_w + CONV_DIM, :])
             + _dot(n_ref[rows, :], w_ref[a_w + CONV_DIM:, :]))
        r = DEEPNORM_ALPHA * x_ref[rows, :] + g_ref[...] * y
        o_ref[rows, :] = _layer_norm(r, lng_ref[...], lnb_ref[...])


def _out_proj(a, cv, n, x, mods, w_out, ln_g, ln_b, layer, row_of_tile):
    rows = x.shape[0]
    tm = TM_OUT
    row = lambda i: (i, 0)
    per_layer = lambda i: (layer, 0, 0)
    vmem = (2 * (2 * _nbytes((tm, D_MODEL), F32) + _nbytes((tm, D_MODEL), BF16)) + _nbytes(w_out.shape, BF16)
            + 3 * _nbytes((TM_OUT_CHUNK, D_MODEL), F32))
    return pl.pallas_call(
        _out_proj_kernel,
        out_shape=jax.ShapeDtypeStruct((rows, D_MODEL), F32),
        grid=(rows // tm,),
        in_specs=[
            _mod_spec(layer, 2, row_of_tile),
            pl.BlockSpec((tm, a.shape[1]), row), pl.BlockSpec((tm, cv.shape[1]), row),
            pl.BlockSpec((tm, n.shape[1]), row), pl.BlockSpec((tm, D_MODEL), row),
            pl.BlockSpec(w_out.shape, lambda i: (0, 0), pipeline_mode=pl.Buffered(1)),
            pl.BlockSpec((None, 1, D_MODEL), per_layer), pl.BlockSpec((None, 1, D_MODEL), per_layer),
        ],
        out_specs=pl.BlockSpec((tm, D_MODEL), row),
        compiler_params=_params(("parallel",), vmem),
        name="out_proj",
    )(mods, a, cv, n, x, w_out, ln_g, ln_b)


def _ffn_kernel(sh_ref, sc_ref, g_ref, x_ref, wg_ref, wu_ref, wd_ref, lng_ref, lnb_ref, o_ref, xm_ref):
    f = pl.program_id(1)

    @pl.when(f == 0)
    def _():
        x = x_ref[...]
        xm_ref[...] = (x * (1.0 + sc_ref[...]) + sh_ref[...]).astype(BF16)
        o_ref[...] = DEEPNORM_ALPHA * x

    xm = xm_ref[...]
    gate = _dot(xm, wg_ref[...])
    up = _dot(xm, wu_ref[...])
    hidden = (gate * jax.nn.sigmoid(gate) * up).astype(BF16)
    o_ref[...] += g_ref[...] * _dot(hidden, wd_ref[...])

    @pl.when(f == pl.num_programs(1) - 1)
    def _():
        o_ref[...] = _layer_norm(o_ref[...], lng_ref[...], lnb_ref[...])


def _ffn(x, mods, w_gate, w_up, w_down, ln_g, ln_b, layer, row_of_tile):
    rows = x.shape[0]
    per_layer = lambda i, f: (layer, 0, 0)
    vmem = (3 * _nbytes((TM, D_MODEL), F32) + 2 * 3 * _nbytes((D_MODEL, TF), BF16)
            + _nbytes((TM, D_MODEL), BF16) + 3 * _nbytes((TM, TF), F32) + _nbytes((TM, D_MODEL), F32))
    return pl.pallas_call(
        _ffn_kernel,
        out_shape=jax.ShapeDtypeStruct((rows, D_MODEL), F32),
        grid=(rows // TM, D_FF // TF),
        in_specs=[
            _mod_spec(layer, 3, row_of_tile), _mod_spec(layer, 4, row_of_tile), _mod_spec(layer, 5, row_of_tile),
            pl.BlockSpec((TM, D_MODEL), lambda i, f: (i, 0), pipeline_mode=pl.Buffered(1)),
            pl.BlockSpec((D_MODEL, TF), lambda i, f: (0, f)),
            pl.BlockSpec((D_MODEL, TF), lambda i, f: (0, f)),
            pl.BlockSpec((TF, D_MODEL), lambda i, f: (f, 0)),
            pl.BlockSpec((None, 1, D_MODEL), per_layer), pl.BlockSpec((None, 1, D_MODEL), per_layer),
        ],
        out_specs=pl.BlockSpec((TM, D_MODEL), lambda i, f: (i, 0)),
        scratch_shapes=[pltpu.VMEM((TM, D_MODEL), BF16)],
        compiler_params=_params(("parallel", "arbitrary"), vmem),
        name="ffn",
    )(mods, mods, mods, x, w_gate, w_up, w_down, ln_g, ln_b)


def _rotate_half_cols(w):
    a1, a2, b1, b2 = jnp.split(w, 4, axis=-1)
    return jnp.concatenate([-a2, a1, -b2, b1], axis=-1)


def _w_in_prep_src(piece):
    rope = OFF_ROPE // PREP_PIECE
    tail = Z_ROPE // PREP_PIECE
    return jnp.where(piece < rope, piece, jnp.where(piece < tail, piece + 1, rope))


def _w_in_prep_kernel(*refs):
    *piece_refs, o_ref = refs
    pieces = [r[...] for r in piece_refs]
    n = len(pieces)
    is_last = pl.program_id(1) == pl.num_programs(1) - 1
    rot_piece = Z_DIM // PREP_PIECE - 1
    for p in range(n):
        piece = (Z_PAD // PREP_COLS - 1) * n + p
        if piece == rot_piece:
            a1, a2, b1, b2 = jnp.split(pieces[p], 4, axis=0)
            pieces[p] = jnp.where(is_last, jnp.concatenate([-a2, a1, -b2, b1], axis=0), pieces[p])
        elif piece > rot_piece:
            pieces[p] = jnp.where(is_last, 0.0, pieces[p])
    o_ref[...] = jnp.concatenate(pieces, axis=0).T.astype(BF16)


def _w_in_prep(w_in):
    w_t = jnp.swapaxes(w_in, 1, 2)
    n = PREP_COLS // PREP_PIECE
    piece_spec = lambda p: pl.BlockSpec((None, PREP_PIECE, D_MODEL), lambda l, g: (l, _w_in_prep_src(g * n + p), 0))
    vmem = 2 * (_nbytes((PREP_COLS, D_MODEL), F32) + _nbytes((D_MODEL, PREP_COLS), BF16)) + 3 * _nbytes(
        (PREP_COLS, D_MODEL), F32)
    return pl.pallas_call(
        _w_in_prep_kernel,
        out_shape=jax.ShapeDtypeStruct((DEPTH, D_MODEL, Z_PAD), BF16),
        grid=(DEPTH, Z_PAD // PREP_COLS),
        in_specs=[piece_spec(p) for p in range(n)],
        out_specs=pl.BlockSpec((None, D_MODEL, PREP_COLS), lambda l, g: (l, 0, g)),
        compiler_params=_params(("parallel", "parallel"), vmem),
        name="w_in_prep",
    )(*([w_t] * n))


def _relayout_wq(w):
    w = w.reshape(DEPTH, MLA_Q_RANK, MLA_HEADS, MLA_QK)
    nope, rope = w[..., :MLA_NOPE], w[..., MLA_NOPE:]
    zeros = jnp.zeros_like(rope)
    cols = jnp.concatenate([nope, rope, zeros, _rotate_half_cols(rope), zeros], axis=-1)
    return cols.reshape(DEPTH, MLA_Q_RANK, MLA_HEADS * WQ_HEAD).astype(BF16)


def _rope_tables():
    t = jnp.arange(SEQ)
    row = (t // GRID_W).astype(F32)
    col = (t % GRID_W).astype(F32)
    n_freq = MLA_ROPE // 4
    inv = ROPE_THETA ** (-jnp.arange(n_freq, dtype=F32) / n_freq)
    ar = row[:, None] * inv
    ac = col[:, None] * inv
    ang = jnp.concatenate([ar, ar, ac, ac], axis=-1)
    cos, sin = jnp.cos(ang), jnp.sin(ang)
    ones, zeros = jnp.ones((TM_PROJ, MLA_ROPE), F32), jnp.zeros((TM_PROJ, MLA_ROPE), F32)
    lat = (jnp.concatenate([cos, sin], axis=1), jnp.concatenate([sin, cos], axis=1))
    ctx = (jnp.concatenate([ones, zeros], axis=1), jnp.concatenate([zeros, ones], axis=1))
    return lat, ctx


def kernel(x, c, ctx, c_ctx, ada_w, ada_b, w_in, mla_q_norm, mla_wq_b, mla_kv_norm, mla_wkv_b, conv_w, na_rpb,
           w_out, ln1_g, ln1_b, ffn_w_gate, ffn_w_up, ffn_w_down, ln2_g, ln2_b):
    assert x.shape == (BATCH, SEQ, D_MODEL) and ctx.shape == (BATCH, CTX_LEN, D_MODEL)
    cc = jnp.concatenate([c, c_ctx[None, :], jnp.zeros((MOD_ROWS - BATCH - 1, D_MODEL), F32)], axis=0)
    mods = _ada(cc, ada_w, ada_b).reshape(DEPTH * MOD_ROWS * N_MOD, 1, D_MODEL)
    (ta_lat, tb_lat), (ta_ctx, tb_ctx) = _rope_tables()
    rows3d = lambda v: v.reshape(DEPTH, 1, -1)

    w_in_p = _w_in_prep(w_in)
    wq_all = _relayout_wq(mla_wq_b)
    wkv = mla_wkv_b.astype(BF16)
    g_q, g_kv = rows3d(mla_q_norm), rows3d(mla_kv_norm)
    bias = _na_bias(na_rpb)
    ln1 = (rows3d(ln1_g), rows3d(ln1_b))
    ln2 = (rows3d(ln2_g), rows3d(ln2_b))

    xl = x.reshape(BATCH * SEQ, D_MODEL)
    xc = ctx.reshape(BATCH * CTX_LEN, D_MODEL)
    lat_row = lambda tm: (lambda i: i // (SEQ // tm))
    ctx_row = lambda i: CTX_MOD_ROW
    for l in range(DEPTH):
        last = l == DEPTH - 1
        zl = _in_proj(xl, mods, w_in_p, l, lat_row(TM))
        zc = _in_proj(xc, mods, w_in_p, l, ctx_row)
        ql, kl, vl = _mla_proj(zl, g_q, g_kv, wq_all, wkv, ta_lat, tb_lat, l, lambda i: i % (SEQ // TM_PROJ))
        qc, kc, vc = _mla_proj(zc, g_q, g_kv, wq_all, wkv, ta_ctx, tb_ctx, l, lambda i: 0)
        ffn_riders = [(ffn_w_gate, FFN_CAST_BLOCKS, 1), (ffn_w_up, FFN_CAST_BLOCKS, 1), (ffn_w_down, FFN_CAST_BLOCKS, 0)]
        a, (w_g, w_u, w_d) = _mla_attn(ql, kl, vl, qc, kc, vc, not last, l, ffn_riders)
        n, (w_o,) = _na_attn(zl, zc, bias, l, not last, [(w_out, OUT_CAST_BLOCKS, 0)])
        cv_l = _conv(zl, conv_w, l, SEQ)
        x1 = _out_proj(a[0], cv_l, n[0], xl, mods, w_o, *ln1, l, lat_row(TM_OUT))
        xl = _ffn(x1, mods, w_g, w_u, w_d, *ln2, l, lat_row(TM))
        if not last:
            cv_c = _conv(zc, conv_w, l, CTX_LEN)
            x1c = _out_proj(a[1], cv_c, n[1], xc, mods, w_o, *ln1, l, ctx_row)
            xc = _ffn(x1c, mods, w_g, w_u, w_d, *ln2, l, ctx_row)
    return xl.reshape(BATCH, SEQ, D_MODEL)
```

```python
import functools

import numpy as np
import jax
import jax.numpy as jnp
from jax import lax
from jax.experimental import pallas as pl
from jax.experimental.pallas import tpu as pltpu

D_MODEL = 2048
BATCH = 4
SEQ = 2048
DEPTH = 2
CTX_LEN = 256
GRID_W = 64
GRID_ROWS = SEQ // GRID_W
MLA_HEADS = 6
MLA_Q_RANK = 512
MLA_KV_RANK = 512
MLA_NOPE = 128
MLA_ROPE = 64
MLA_V = 128
MLA_QK = MLA_NOPE + MLA_ROPE
CONV_DIM = 512
NA_HEADS = 6
NA_HEAD_DIM = 128
NA_DIM = NA_HEADS * NA_HEAD_DIM
NA_WIN_R = 8
NA_WIN_C = 16
OFF_MLA_Q = 0
OFF_MLA_KV = OFF_MLA_Q + MLA_Q_RANK
OFF_ROPE = OFF_MLA_KV + MLA_KV_RANK
OFF_CONV_B = OFF_ROPE + MLA_ROPE
OFF_CONV_C = OFF_CONV_B + CONV_DIM
OFF_CONV_H = OFF_CONV_C + CONV_DIM
OFF_NA_Q = OFF_CONV_H + CONV_DIM
OFF_NA_K = OFF_NA_Q + NA_DIM
OFF_NA_V = OFF_NA_K + NA_DIM
IN_DIM = OFF_NA_V + NA_DIM
D_FF = -(-8 * D_MODEL // (3 * 256)) * 256
ROPE_THETA = 10000.0
LN_EPS = 1e-6
RMS_EPS = 1e-6
DEEPNORM_ALPHA = (2 * DEPTH) ** 0.25
LOG2E = 1.4426950408889634
N_MOD = 6
MOD_ROWS = 8
CTX_MOD_ROW = BATCH

LANES = 128
V7X_VMEM_BYTES = 64 * 1024 * 1024
V7X_VMEM_BUDGET = 58 * 1024 * 1024

Z_Q = 0
Z_KV = Z_Q + MLA_Q_RANK
Z_CONV_B = Z_KV + MLA_KV_RANK
Z_CONV_C = Z_CONV_B + CONV_DIM
Z_CONV_H = Z_CONV_C + CONV_DIM
Z_NA_Q = Z_CONV_H + CONV_DIM
Z_NA_K = Z_NA_Q + NA_DIM
Z_NA_V = Z_NA_K + NA_DIM
Z_ROPE = Z_NA_V + NA_DIM
Z_DIM = Z_ROPE + 2 * MLA_ROPE
V7X_MXU_COLS = 256
Z_PAD = -(-Z_DIM // (5 * V7X_MXU_COLS)) * (5 * V7X_MXU_COLS)

QK_PAD = 2 * LANES

TM = 1024
TM_OUT = 1024
TM_OUT_CHUNK = 256
TM_PROJ = 512
PREP_PIECE = MLA_ROPE
PREP_COLS = 5 * LANES
TN_IN = 5 * V7X_MXU_COLS
TN_ADA = 1024
TF = 512
TQ = 256
Q_TILE_UNROLL = 8
NA_RQ = 4
NA_RK = NA_RQ + NA_WIN_R - 1
NA_NQ = NA_RQ * GRID_W
NA_NK = NA_RK * GRID_W
NA_BLOCKS = GRID_ROWS // NA_RQ
FFN_CAST_BLOCKS = 22
OUT_CAST_BLOCKS = 8
NA_HEADS_PER_STEP = 2

F32 = jnp.float32
BF16 = jnp.bfloat16


def _params(semantics, vmem_bytes):
    assert vmem_bytes <= V7X_VMEM_BUDGET, vmem_bytes
    return pltpu.CompilerParams(dimension_semantics=semantics, vmem_limit_bytes=int(vmem_bytes))


def _nbytes(shape, dtype):
    return int(np.prod(shape)) * jnp.dtype(dtype).itemsize


def _dot(a, b):
    return jnp.dot(a, b, preferred_element_type=F32)


def _dot_nt(a, b):
    return lax.dot_general(a, b, (((1,), (1,)), ((), ())), preferred_element_type=F32)


def _mod_spec(layer, chunk, row_of_tile):
    base = layer * MOD_ROWS * N_MOD + chunk
    return pl.BlockSpec((None, 1, D_MODEL), lambda i, *_: (base + row_of_tile(i) * N_MOD, 0, 0))


def _layer_norm(r, g, b):
    mu = jnp.mean(r, axis=-1, keepdims=True)
    c = r - mu
    var = jnp.mean(c * c, axis=-1, keepdims=True)
    return c * lax.rsqrt(var + LN_EPS) * g + b


def _ada_kernel(cc_ref, w_ref, b_ref, o_ref):
    cc = cc_ref[...]
    s = (cc * jax.nn.sigmoid(cc)).astype(BF16)
    o_ref[...] = _dot(s, w_ref[...].astype(BF16)) + b_ref[...]


def _ada(cc, ada_w, ada_b):
    n = N_MOD * D_MODEL
    vmem = 2 * (_nbytes((D_MODEL, TN_ADA), F32) + _nbytes((MOD_ROWS, TN_ADA), F32) * 2
                + _nbytes((MOD_ROWS, D_MODEL), F32)) + _nbytes((D_MODEL, TN_ADA), BF16) * 2
    return pl.pallas_call(
        _ada_kernel,
        out_shape=jax.ShapeDtypeStruct((DEPTH, MOD_ROWS, n), F32),
        grid=(DEPTH, n // TN_ADA),
        in_specs=[
            pl.BlockSpec((MOD_ROWS, D_MODEL), lambda l, j: (0, 0)),
            pl.BlockSpec((None, D_MODEL, TN_ADA), lambda l, j: (l, 0, j)),
            pl.BlockSpec((None, 1, TN_ADA), lambda l, j: (l, 0, j)),
        ],
        out_specs=pl.BlockSpec((None, MOD_ROWS, TN_ADA), lambda l, j: (l, 0, j)),
        compiler_params=_params(("parallel", "parallel"), vmem),
        name="ada_mod",
    )(cc, ada_w, ada_b.reshape(DEPTH, 1, n))


def _in_proj_kernel(sh_ref, sc_ref, x_ref, w_ref, o_ref, xm_ref):
    @pl.when(pl.program_id(1) == 0)
    def _():
        xm_ref[...] = (x_ref[...] * (1.0 + sc_ref[...]) + sh_ref[...]).astype(BF16)

    o_ref[...] = _dot(xm_ref[...], w_ref[...]).astype(BF16)


def _in_proj(x, mods, w_in_p, layer, row_of_tile):
    rows = x.shape[0]
    vmem = (2 * (_nbytes((TM, D_MODEL), F32) + _nbytes((D_MODEL, TN_IN), BF16) + _nbytes((TM, TN_IN), BF16))
            + _nbytes((TM, D_MODEL), BF16) * 2 + _nbytes((TM, TN_IN), F32))
    return pl.pallas_call(
        _in_proj_kernel,
        out_shape=jax.ShapeDtypeStruct((rows, Z_PAD), BF16),
        grid=(rows // TM, Z_PAD // TN_IN),
        in_specs=[
            _mod_spec(layer, 0, row_of_tile),
            _mod_spec(layer, 1, row_of_tile),
            pl.BlockSpec((TM, D_MODEL), lambda i, j: (i, 0)),
            pl.BlockSpec((None, D_MODEL, TN_IN), lambda i, j: (layer, 0, j)),
        ],
        out_specs=pl.BlockSpec((TM, TN_IN), lambda i, j: (i, j)),
        scratch_shapes=[pltpu.VMEM((TM, D_MODEL), BF16)],
        compiler_params=_params(("parallel", "arbitrary"), vmem),
        name="in_proj",
    )(mods, mods, x, w_in_p)


def _rms_norm(x, g):
    return x * lax.rsqrt(jnp.mean(x * x, axis=-1, keepdims=True) + RMS_EPS) * g


def _rope_sum(pair, table, lane):
    y = pair * table
    return jnp.where(lane < MLA_ROPE, y + pltpu.roll(y, MLA_ROPE, 1), 0.0)


def _mla_proj_kernel(z_ref, slot_ref, gq_ref, gkv_ref, wq_ref, wkv_ref, ta_ref, q_ref, k_ref, v_ref):
    cq = z_ref[:, Z_Q:Z_Q + MLA_Q_RANK].astype(F32)
    ckv = z_ref[:, Z_KV:Z_KV + MLA_KV_RANK].astype(F32)
    qa = _dot(_rms_norm(cq, gq_ref[...]).astype(BF16), wq_ref[...]) * (MLA_QK ** -0.5 * LOG2E)
    kv = _dot(_rms_norm(ckv, gkv_ref[...]).astype(BF16), wkv_ref[...])
    ta = ta_ref[...]
    lane = lax.broadcasted_iota(jnp.int32, ta.shape, 1)
    kro = _rope_sum(slot_ref[...].astype(F32), ta, lane).astype(BF16)
    for h in range(MLA_HEADS):
        ob = h * QK_PAD
        q_ref[:, ob:ob + LANES] = qa[:, ob:ob + LANES].astype(BF16)
        q_ref[:, ob + LANES:ob + QK_PAD] = _rope_sum(qa[:, ob + LANES:ob + QK_PAD], ta, lane).astype(BF16)
        k_ref[:, ob:ob + LANES] = kv[:, ob:ob + LANES].astype(BF16)
        k_ref[:, ob + LANES:ob + QK_PAD] = kro
        v_ref[:, h * MLA_V:(h + 1) * MLA_V] = kv[:, ob + LANES:ob + QK_PAD].astype(BF16)


def _mla_proj(z, g_q, g_kv, wq_all, wkv, ta, layer, table_tile):
    rows = z.shape[0]
    per_layer = lambda i: (layer, 0, 0)
    zw = Z_KV + MLA_KV_RANK
    hq = MLA_HEADS * QK_PAD
    hv = MLA_HEADS * MLA_V
    tm = TM_PROJ
    vmem = (2 * (_nbytes((tm, zw), BF16) + _nbytes((tm, LANES), BF16) + _nbytes((tm, LANES), F32)
                 + _nbytes(wq_all.shape[1:], BF16) + _nbytes(wkv.shape[1:], BF16)
                 + 2 * _nbytes((tm, hq), BF16) + _nbytes((tm, hv), BF16))
            + 2 * _nbytes((tm, hq), F32) + 4 * _nbytes((tm, zw), F32))
    return pl.pallas_call(
        _mla_proj_kernel,
        out_shape=(jax.ShapeDtypeStruct((rows, hq), BF16), jax.ShapeDtypeStruct((rows, hq), BF16),
                   jax.ShapeDtypeStruct((rows, hv), BF16)),
        grid=(rows // tm,),
        in_specs=[
            pl.BlockSpec((tm, zw), lambda i: (i, 0)),
            pl.BlockSpec((tm, 2 * MLA_ROPE), lambda i: (i, Z_ROPE // (2 * MLA_ROPE))),
            pl.BlockSpec((None, 1, MLA_Q_RANK), per_layer),
            pl.BlockSpec((None, 1, MLA_KV_RANK), per_layer),
            pl.BlockSpec((None,) + wq_all.shape[1:], per_layer),
            pl.BlockSpec((None,) + wkv.shape[1:], per_layer),
            pl.BlockSpec((tm, LANES), lambda i: (table_tile(i), 0)),
        ],
        out_specs=(pl.BlockSpec((tm, hq), lambda i: (i, 0)), pl.BlockSpec((tm, hq), lambda i: (i, 0)),
                   pl.BlockSpec((tm, hv), lambda i: (i, 0))),
        compiler_params=_params(("parallel",), vmem),
        name="mla_proj",
    )(z, z, g_q, g_kv, wq_all, wkv, ta)


def _rider_specs(w, layer, n_blocks, axis, n_heads):
    rows, cols = w.shape[1:]
    block = (rows // n_blocks, cols) if axis == 0 else (rows, cols // n_blocks)
    assert block[0] * block[1] * n_blocks == rows * cols and n_blocks <= BATCH * n_heads

    def at(b, h):
        step = jnp.minimum(b * n_heads + h, n_blocks - 1)
        return (step, 0) if axis == 0 else (0, step)

    in_spec = pl.BlockSpec((None,) + block, lambda b, h: (layer,) + at(b, h))
    return in_spec, pl.BlockSpec(block, at), jax.ShapeDtypeStruct((rows, cols), BF16), block


def _cast_riders(in_refs, out_refs):
    for i_ref, o_ref in zip(in_refs, out_refs):
        o_ref[...] = i_ref[...].astype(BF16)


def _split_refs(refs, *counts):
    out, pos = [], 0
    for c in counts:
        out.append(refs[pos:pos + c])
        pos += c
    assert pos == len(refs)
    return out


def _softmax_pv(scores, values_ones):
    m = functools.reduce(jnp.maximum, [jnp.max(s, axis=-1, keepdims=True) for s in scores])
    acc = functools.reduce(jnp.add, [_dot(jnp.exp2(s - m).astype(BF16), v) for s, v in zip(scores, values_ones)])
    return acc[:, :LANES] / acc[:, LANES:]


def _fill_values_ones(v_ref, vx_ref):
    vx_ref[:, :LANES] = v_ref[...]
    vx_ref[:, LANES:] = jnp.ones((vx_ref.shape[0], LANES), BF16)


def _scaled_q(q, head_dim):
    return (q.astype(F32) * (head_dim ** -0.5 * LOG2E)).astype(BF16)


def _mla_attn_kernel(*refs, with_ctx, n_riders):
    ins, rider_in, outs, rider_out, (vxl_ref, vxc_ref) = _split_refs(
        refs, 6 if with_ctx else 5, n_riders, 2 if with_ctx else 1, n_riders, 2)
    q_ref, kl_ref, vl_ref, kc_ref, vc_ref = ins[:5]
    o_ref = outs[0]
    _cast_riders(rider_in, rider_out)
    _fill_values_ones(vl_ref, vxl_ref)
    _fill_values_ones(vc_ref, vxc_ref)

    def q_tile(t, carry):
        r0 = pl.multiple_of(t * TQ, TQ)
        q = q_ref[pl.ds(r0, TQ), :]
        s_lat = _dot_nt(q, kl_ref[...])
        s_ctx = _dot_nt(q, kc_ref[...])
        o = _softmax_pv([s_lat, s_ctx], [vxl_ref[...], vxc_ref[...]])
        o_ref[pl.ds(r0, TQ), :] = o.astype(BF16)
        return carry

    lax.fori_loop(0, SEQ // TQ, q_tile, 0, unroll=Q_TILE_UNROLL)
    if with_ctx:
        s = _dot_nt(ins[5][...], kc_ref[...])
        outs[1][...] = _softmax_pv([s], [vxc_ref[...]]).astype(BF16)


def _mla_attn(ql, kl, vl, qc, kc, vc, with_ctx, layer, riders):
    head = lambda b, h: (b, h)
    in_specs = [
        pl.BlockSpec((SEQ, QK_PAD), head), pl.BlockSpec((SEQ, QK_PAD), head), pl.BlockSpec((SEQ, MLA_V), head),
        pl.BlockSpec((CTX_LEN, QK_PAD), head), pl.BlockSpec((CTX_LEN, MLA_V), head),
    ]
    args = [ql, kl, vl, kc, vc]
    out_shape = [jax.ShapeDtypeStruct((BATCH * SEQ, MLA_HEADS * MLA_V), BF16)]
    out_specs = [pl.BlockSpec((SEQ, MLA_V), head)]
    if with_ctx:
        in_specs.append(pl.BlockSpec((CTX_LEN, QK_PAD), head))
        args.append(qc)
        out_shape.append(jax.ShapeDtypeStruct((BATCH * CTX_LEN, MLA_HEADS * MLA_V), BF16))
        out_specs.append(pl.BlockSpec((CTX_LEN, MLA_V), head))
    vmem = (2 * (3 * _nbytes((SEQ, QK_PAD), BF16) + 2 * _nbytes((SEQ, MLA_V), BF16)) + 2 * _nbytes((SEQ, 2 * LANES), BF16)
            + Q_TILE_UNROLL * _nbytes((TQ, SEQ + CTX_LEN), F32))
    n_base = len(out_shape)
    for w, n_blocks, axis in riders:
        in_spec, out_spec, shape, block = _rider_specs(w, layer, n_blocks, axis, MLA_HEADS)
        in_specs.append(in_spec)
        args.append(w)
        out_specs.append(out_spec)
        out_shape.append(shape)
        vmem += 2 * (_nbytes(block, F32) + _nbytes(block, BF16))
    res = pl.pallas_call(
        functools.partial(_mla_attn_kernel, with_ctx=with_ctx, n_riders=len(riders)),
        out_shape=tuple(out_shape), grid=(BATCH, MLA_HEADS), in_specs=in_specs, out_specs=tuple(out_specs),
        scratch_shapes=[pltpu.VMEM((SEQ, 2 * LANES), BF16), pltpu.VMEM((CTX_LEN, 2 * LANES), BF16)],
        compiler_params=_params(("arbitrary", "arbitrary"), vmem),
        name="mla_attn_ctx" if with_ctx else "mla_attn",
    )(*args)
    return res[:n_base], res[n_base:]


def _conv_kernel(gb_ref, gc_ref, h_ref, w_ref, o_ref):
    n = o_ref.shape[0]
    u = gc_ref[...].astype(F32) * h_ref[...].astype(F32)
    row = lax.broadcasted_iota(jnp.int32, u.shape, 0)
    u_prev = jnp.where(row == 0, 0.0, pltpu.roll(u, 1, 0))
    u_next = jnp.where(row == n - 1, 0.0, pltpu.roll(u, n - 1, 0))
    y = u_prev * w_ref[0:1, :] + u * w_ref[1:2, :] + u_next * w_ref[2:3, :]
    o_ref[...] = (gb_ref[...].astype(F32) * y).astype(BF16)


def _conv(z, conv_w, layer, seq_len):
    rows = z.shape[0]
    blk = (seq_len, CONV_DIM)
    vmem = 2 * 4 * _nbytes(blk, BF16) + 8 * _nbytes(blk, F32)
    return pl.pallas_call(
        _conv_kernel,
        out_shape=jax.ShapeDtypeStruct((rows, CONV_DIM), BF16),
        grid=(rows // seq_len,),
        in_specs=[
            pl.BlockSpec(blk, lambda s: (s, Z_CONV_B // CONV_DIM)),
            pl.BlockSpec(blk, lambda s: (s, Z_CONV_C // CONV_DIM)),
            pl.BlockSpec(blk, lambda s: (s, Z_CONV_H // CONV_DIM)),
            pl.BlockSpec((None,) + conv_w.shape[1:], lambda s: (layer, 0, 0)),
        ],
        out_specs=pl.BlockSpec(blk, lambda s: (s, 0)),
        compiler_params=_params(("parallel",), vmem),
        name="short_conv",
    )(z, z, z, conv_w)


def _na_band_start(block):
    return min(max(block * NA_RQ - NA_WIN_R // 2, 0), GRID_ROWS - NA_RK)


def _na_bias_kind(block):
    return 0 if block == 0 else (2 if block == NA_BLOCKS - 1 else 1)


def _na_row_offsets():
    n_dr = 2 * NA_WIN_R - 1
    idx = np.full((3, NA_RQ, NA_RK), n_dr, np.int32)
    for kind, block in ((0, 0), (1, 2), (2, NA_BLOCKS - 1)):
        for qr in range(NA_RQ):
            r = block * NA_RQ + qr
            r0 = min(max(r - NA_WIN_R // 2, 0), GRID_ROWS - NA_WIN_R)
            for ki in range(NA_RK):
                kr = _na_band_start(block) + ki
                if r0 <= kr < r0 + NA_WIN_R:
                    idx[kind, qr, ki] = kr - r + NA_WIN_R - 1
    return idx


def _na_bias_kernel(w_ref, o_ref):
    shape = (GRID_W, LANES)
    c = lax.broadcasted_iota(jnp.int32, shape, 0)
    kc = lax.broadcasted_iota(jnp.int32, shape, 1)
    c0 = jnp.clip(c - NA_WIN_C // 2, 0, GRID_W - NA_WIN_C)
    col_ok = (kc >= c0) & (kc < c0 + NA_WIN_C)
    neg = jnp.full((GRID_W, GRID_W), -jnp.inf, F32)
    offsets = _na_row_offsets()
    n_dr = 2 * NA_WIN_R - 1
    tiles = {n_dr: neg}
    for d in sorted(set(int(v) for v in offsets.reshape(-1)) - {n_dr}):
        row = jnp.broadcast_to(w_ref[d:d + 1, :], shape)
        skew = pltpu.roll(row, LANES - (GRID_W - 1), 1, stride=1, stride_axis=0)
        tiles[d] = jnp.where(col_ok, skew, -jnp.inf)[:, :GRID_W]
    for kind in range(3):
        for qr in range(NA_RQ):
            for ki in range(NA_RK):
                o_ref[kind, qr * GRID_W:(qr + 1) * GRID_W, ki * GRID_W:(ki + 1) * GRID_W] = (
                    tiles[int(offsets[kind, qr, ki])])


def _na_bias(rpb):
    n_dr, n_dc = 2 * NA_WIN_R - 1, 2 * NA_WIN_C - 1
    lo = GRID_W - 1 - (NA_WIN_C - 1)
    w = jnp.pad(rpb.reshape(DEPTH * NA_HEADS, n_dr, n_dc) * LOG2E, ((0, 0), (0, 0), (lo, LANES - lo - n_dc)),
                constant_values=-jnp.inf)
    vmem = 2 * (_nbytes((2 * 8, LANES), F32) + _nbytes((3, NA_NQ, NA_NK + GRID_W), F32)) + 16 * _nbytes(
        (GRID_W, LANES), F32)
    return pl.pallas_call(
        _na_bias_kernel,
        out_shape=jax.ShapeDtypeStruct((w.shape[0], 3, NA_NQ, NA_NK), F32),
        grid=(w.shape[0],),
        in_specs=[pl.BlockSpec((None, n_dr, LANES), lambda g: (g, 0, 0))],
        out_specs=pl.BlockSpec((None, 3, NA_NQ, NA_NK), lambda g: (g, 0, 0, 0)),
        compiler_params=_params(("parallel",), vmem),
        name="na_bias",
    )(w)


def _na_kernel(*refs, with_ctx, n_riders):
    ins, rider_in, outs, rider_out, (vx_ref, vxc_ref) = _split_refs(
        refs, 7 if with_ctx else 6, n_riders, 2 if with_ctx else 1, n_riders, 2)
    q_ref, k_ref, v_ref, kc_ref, vc_ref, bias_ref = ins[:6]
    o_ref = outs[0]
    _cast_riders(rider_in, rider_out)
    d = NA_HEAD_DIM
    for hh in range(NA_HEADS_PER_STEP):
        cols = slice(hh * d, (hh + 1) * d)
        _fill_values_ones(v_ref.at[:, cols], vx_ref.at[hh])
        _fill_values_ones(vc_ref.at[:, cols], vxc_ref.at[hh])
        for block in range(NA_BLOCKS):
            rows = slice(block * NA_NQ, (block + 1) * NA_NQ)
            q = _scaled_q(q_ref[rows, cols], d)
            k0 = _na_band_start(block) * GRID_W
            s_loc = _dot_nt(q, k_ref[k0:k0 + NA_NK, cols]) + bias_ref[hh, _na_bias_kind(block)]
            s_ctx = _dot_nt(q, kc_ref[:, cols])
            o = _softmax_pv([s_loc, s_ctx], [vx_ref[hh, k0:k0 + NA_NK, :], vxc_ref[hh]])
            o_ref[rows, cols] = o.astype(BF16)
        if with_ctx:
            s = _dot_nt(_scaled_q(ins[6][:, cols], d), kc_ref[:, cols])
            outs[1][:, cols] = _softmax_pv([s], [vxc_ref[hh]]).astype(BF16)


def _na_attn(zl, zc, bias, layer, with_ctx, riders):
    hps = NA_HEADS_PER_STEP
    w = hps * NA_HEAD_DIM
    steps = NA_HEADS // hps
    col = lambda off: (lambda b, g: (b, off // w + g))
    in_specs = [
        pl.BlockSpec((SEQ, w), col(Z_NA_Q)), pl.BlockSpec((SEQ, w), col(Z_NA_K)), pl.BlockSpec((SEQ, w), col(Z_NA_V)),
        pl.BlockSpec((CTX_LEN, w), col(Z_NA_K)), pl.BlockSpec((CTX_LEN, w), col(Z_NA_V)),
        pl.BlockSpec((hps, 3, NA_NQ, NA_NK), lambda b, g: (layer * steps + g, 0, 0, 0)),
    ]
    args = [zl, zl, zl, zc, zc, bias]
    out_shape = [jax.ShapeDtypeStruct((BATCH * SEQ, NA_DIM), BF16)]
    out_specs = [pl.BlockSpec((SEQ, w), lambda b, g: (b, g))]
    if with_ctx:
        in_specs.append(pl.BlockSpec((CTX_LEN, w), col(Z_NA_Q)))
        args.append(zc)
        out_shape.append(jax.ShapeDtypeStruct((BATCH * CTX_LEN, NA_DIM), BF16))
        out_specs.append(pl.BlockSpec((CTX_LEN, w), lambda b, g: (b, g)))
    vmem = (2 * (5 * _nbytes((SEQ, w), BF16) + _nbytes((hps, 3, NA_NQ, NA_NK), F32))
            + 2 * hps * _nbytes((SEQ, 2 * LANES), BF16) + 8 * hps * _nbytes((NA_NQ, NA_NK + CTX_LEN), F32))
    n_base = len(out_shape)
    for wt, n_blocks, axis in riders:
        in_spec, out_spec, shape, block = _rider_specs(wt, layer, n_blocks, axis, steps)
        in_specs.append(in_spec)
        args.append(wt)
        out_specs.append(out_spec)
        out_shape.append(shape)
        vmem += 2 * (_nbytes(block, F32) + _nbytes(block, BF16))
    res = pl.pallas_call(
        functools.partial(_na_kernel, with_ctx=with_ctx, n_riders=len(riders)),
        out_shape=tuple(out_shape), grid=(BATCH, steps), in_specs=in_specs, out_specs=tuple(out_specs),
        scratch_shapes=[pltpu.VMEM((hps, SEQ, 2 * LANES), BF16), pltpu.VMEM((hps, CTX_LEN, 2 * LANES), BF16)],
        compiler_params=_params(("arbitrary", "arbitrary"), vmem),
        name="na_attn_ctx" if with_ctx else "na_attn",
    )(*args)
    return res[:n_base], res[n_base:]


def _out_proj_kernel(g_ref, a_ref, c_ref, n_ref, x_ref, w_ref, lng_ref, lnb_ref, o_ref):
    a_w = MLA_HEADS * MLA_V
    for r0 in range(0, o_ref.shape[0], TM_OUT_CHUNK):
        rows = slice(r0, r0 + TM_OUT_CHUNK)
        y = (_dot(a_ref[rows, :], w_ref[0:a_w, :]) + _dot(c_ref[rows, :], w_ref[a_w:a_w + CONV_DIM, :])
             + _dot(n_ref[rows, :], w_ref[a_w + CONV_DIM:, :]))
        r = DEEPNORM_ALPHA * x_ref[rows, :] + g_ref[...] * y
        o_ref[rows, :] = _layer_norm(r, lng_ref[...], lnb_ref[...])


def _out_proj(a, cv, n, x, mods, w_out, ln_g, ln_b, layer, row_of_tile):
    rows = x.shape[0]
    tm = TM_OUT
    row = lambda i: (i, 0)
    per_layer = lambda i: (layer, 0, 0)
    vmem = (2 * (2 * _nbytes((tm, D_MODEL), F32) + _nbytes((tm, D_MODEL), BF16)) + _nbytes(w_out.shape, BF16)
            + 3 * _nbytes((TM_OUT_CHUNK, D_MODEL), F32))
    return pl.pallas_call(
        _out_proj_kernel,
        out_shape=jax.ShapeDtypeStruct((rows, D_MODEL), F32),
        grid=(rows // tm,),
        in_specs=[
            _mod_spec(layer, 2, row_of_tile),
            pl.BlockSpec((tm, a.shape[1]), row), pl.BlockSpec((tm, cv.shape[1]), row),
            pl.BlockSpec((tm, n.shape[1]), row), pl.BlockSpec((tm, D_MODEL), row),
            pl.BlockSpec(w_out.shape, lambda i: (0, 0), pipeline_mode=pl.Buffered(1)),
            pl.BlockSpec((None, 1, D_MODEL), per_layer), pl.BlockSpec((None, 1, D_MODEL), per_layer),
        ],
        out_specs=pl.BlockSpec((tm, D_MODEL), row),
        compiler_params=_params(("parallel",), vmem),
        name="out_proj",
    )(mods, a, cv, n, x, w_out, ln_g, ln_b)


def _ffn_kernel(sh_ref, sc_ref, g_ref, x_ref, wg_ref, wu_ref, wd_ref, lng_ref, lnb_ref, o_ref, xm_ref):
    f = pl.program_id(1)

    @pl.when(f == 0)
    def _():
        x = x_ref[...]
        xm_ref[...] = (x * (1.0 + sc_ref[...]) + sh_ref[...]).astype(BF16)
        o_ref[...] = DEEPNORM_ALPHA * x

    xm = xm_ref[...]
    gate = _dot(xm, wg_ref[...])
    up = _dot(xm, wu_ref[...])
    hidden = (gate * jax.nn.sigmoid(gate) * up).astype(BF16)
    o_ref[...] += g_ref[...] * _dot(hidden, wd_ref[...])

    @pl.when(f == pl.num_programs(1) - 1)
    def _():
        o_ref[...] = _layer_norm(o_ref[...], lng_ref[...], lnb_ref[...])


def _ffn(x, mods, w_gate, w_up, w_down, ln_g, ln_b, layer, row_of_tile):
    rows = x.shape[0]
    per_layer = lambda i, f: (layer, 0, 0)
    vmem = (3 * _nbytes((TM, D_MODEL), F32) + 2 * 3 * _nbytes((D_MODEL, TF), BF16)
            + _nbytes((TM, D_MODEL), BF16) + 3 * _nbytes((TM, TF), F32) + _nbytes((TM, D_MODEL), F32))
    return pl.pallas_call(
        _ffn_kernel,
        out_shape=jax.ShapeDtypeStruct((rows, D_MODEL), F32),
        grid=(rows // TM, D_FF // TF),
        in_specs=[
            _mod_spec(layer, 3, row_of_tile), _mod_spec(layer, 4, row_of_tile), _mod_spec(layer, 5, row_of_tile),
            pl.BlockSpec((TM, D_MODEL), lambda i, f: (i, 0), pipeline_mode=pl.Buffered(1)),
            pl.BlockSpec((D_MODEL, TF), lambda i, f: (0, f)),
            pl.BlockSpec((D_MODEL, TF), lambda i, f: (0, f)),
            pl.BlockSpec((TF, D_MODEL), lambda i, f: (f, 0)),
            pl.BlockSpec((None, 1, D_MODEL), per_layer), pl.BlockSpec((None, 1, D_MODEL), per_layer),
        ],
        out_specs=pl.BlockSpec((TM, D_MODEL), lambda i, f: (i, 0)),
        scratch_shapes=[pltpu.VMEM((TM, D_MODEL), BF16)],
        compiler_params=_params(("parallel", "arbitrary"), vmem),
        name="ffn",
    )(mods, mods, mods, x, w_gate, w_up, w_down, ln_g, ln_b)


def _rotate_half_cols(w):
    a1, a2, b1, b2 = jnp.split(w, 4, axis=-1)
    return jnp.concatenate([-a2, a1, -b2, b1], axis=-1)


def _w_in_prep_src(piece):
    rope = OFF_ROPE // PREP_PIECE
    tail = Z_ROPE // PREP_PIECE
    return jnp.where(piece < rope, piece, jnp.where(piece < tail, piece + 1, rope))


def _w_in_prep_kernel(*refs):
    *piece_refs, o_ref = refs
    pieces = [r[...] for r in piece_refs]
    n = len(pieces)
    is_last = pl.program_id(1) == pl.num_programs(1) - 1
    rot_piece = Z_DIM // PREP_PIECE - 1
    for p in range(n):
        piece = (Z_PAD // PREP_COLS - 1) * n + p
        if piece == rot_piece:
            a1, a2, b1, b2 = jnp.split(pieces[p], 4, axis=0)
            pieces[p] = jnp.where(is_last, jnp.concatenate([-a2, a1, -b2, b1], axis=0), pieces[p])
        elif piece > rot_piece:
            pieces[p] = jnp.where(is_last, 0.0, pieces[p])
    o_ref[...] = jnp.concatenate(pieces, axis=0).T.astype(BF16)


def _w_in_prep(w_in):
    w_t = jnp.swapaxes(w_in, 1, 2)
    n = PREP_COLS // PREP_PIECE
    piece_spec = lambda p: pl.BlockSpec((None, PREP_PIECE, D_MODEL), lambda l, g: (l, _w_in_prep_src(g * n + p), 0))
    vmem = 2 * (_nbytes((PREP_COLS, D_MODEL), F32) + _nbytes((D_MODEL, PREP_COLS), BF16)) + 3 * _nbytes(
        (PREP_COLS, D_MODEL), F32)
    return pl.pallas_call(
        _w_in_prep_kernel,
        out_shape=jax.ShapeDtypeStruct((DEPTH, D_MODEL, Z_PAD), BF16),
        grid=(DEPTH, Z_PAD // PREP_COLS),
        in_specs=[piece_spec(p) for p in range(n)],
        out_specs=pl.BlockSpec((None, D_MODEL, PREP_COLS), lambda l, g: (l, 0, g)),
        compiler_params=_params(("parallel", "parallel"), vmem),
        name="w_in_prep",
    )(*([w_t] * n))


def _relayout_wq(w):
    w = w.reshape(DEPTH, MLA_Q_RANK, MLA_HEADS, MLA_QK)
    rope = w[..., MLA_NOPE:]
    cols = jnp.concatenate([w, _rotate_half_cols(rope)], axis=-1)
    return cols.reshape(DEPTH, MLA_Q_RANK, MLA_HEADS * QK_PAD).astype(BF16)


def _rope_tables():
    t = jnp.arange(SEQ)
    row = (t // GRID_W).astype(F32)
    col = (t % GRID_W).astype(F32)
    n_freq = MLA_ROPE // 4
    inv = ROPE_THETA ** (-jnp.arange(n_freq, dtype=F32) / n_freq)
    ar = row[:, None] * inv
    ac = col[:, None] * inv
    ang = jnp.concatenate([ar, ar, ac, ac], axis=-1)
    lat = jnp.concatenate([jnp.cos(ang), jnp.sin(ang)], axis=1)
    ctx = jnp.concatenate([jnp.ones((TM_PROJ, MLA_ROPE), F32), jnp.zeros((TM_PROJ, MLA_ROPE), F32)], axis=1)
    return lat, ctx


def kernel(x, c, ctx, c_ctx, ada_w, ada_b, w_in, mla_q_norm, mla_wq_b, mla_kv_norm, mla_wkv_b, conv_w, na_rpb,
           w_out, ln1_g, ln1_b, ffn_w_gate, ffn_w_up, ffn_w_down, ln2_g, ln2_b):
    assert x.shape == (BATCH, SEQ, D_MODEL) and ctx.shape == (BATCH, CTX_LEN, D_MODEL)
    cc = jnp.concatenate([c, c_ctx[None, :], jnp.zeros((MOD_ROWS - BATCH - 1, D_MODEL), F32)], axis=0)
    mods = _ada(cc, ada_w, ada_b).reshape(DEPTH * MOD_ROWS * N_MOD, 1, D_MODEL)
    ta_lat, ta_ctx = _rope_tables()
    rows3d = lambda v: v.reshape(DEPTH, 1, -1)

    w_in_p = _w_in_prep(w_in)
    wq_all = _relayout_wq(mla_wq_b)
    wkv = mla_wkv_b.astype(BF16)
    g_q, g_kv = rows3d(mla_q_norm), rows3d(mla_kv_norm)
    bias = _na_bias(na_rpb)
    ln1 = (rows3d(ln1_g), rows3d(ln1_b))
    ln2 = (rows3d(ln2_g), rows3d(ln2_b))

    xl = x.reshape(BATCH * SEQ, D_MODEL)
    xc = ctx.reshape(BATCH * CTX_LEN, D_MODEL)
    lat_row = lambda tm: (lambda i: i // (SEQ // tm))
    ctx_row = lambda i: CTX_MOD_ROW
    for l in range(DEPTH):
        last = l == DEPTH - 1
        zl = _in_proj(xl, mods, w_in_p, l, lat_row(TM))
        zc = _in_proj(xc, mods, w_in_p, l, ctx_row)
        ql, kl, vl = _mla_proj(zl, g_q, g_kv, wq_all, wkv, ta_lat, l, lambda i: i % (SEQ // TM_PROJ))
        qc, kc, vc = _mla_proj(zc, g_q, g_kv, wq_all, wkv, ta_ctx, l, lambda i: 0)
        ffn_riders = [(ffn_w_gate, FFN_CAST_BLOCKS, 1), (ffn_w_up, FFN_CAST_BLOCKS, 1), (ffn_w_down, FFN_CAST_BLOCKS, 0)]
        a, (w_g, w_u, w_d) = _mla_attn(ql, kl, vl, qc, kc, vc, not last, l, ffn_riders)
        n, (w_o,) = _na_attn(zl, zc, bias, l, not last, [(w_out, OUT_CAST_BLOCKS, 0)])
        cv_l = _conv(zl, conv_w, l, SEQ)
        x1 = _out_proj(a[0], cv_l, n[0], xl, mods, w_o, *ln1, l, lat_row(TM_OUT))
        xl = _ffn(x1, mods, w_g, w_u, w_d, *ln2, l, lat_row(TM))
        if not last:
            cv_c = _conv(zc, conv_w, l, CTX_LEN)
            x1c = _out_proj(a[1], cv_c, n[1], xc, mods, w_o, *ln1, l, ctx_row)
            xc = _ffn(x1c, mods, w_g, w_u, w_d, *ln2, l, ctx_row)
    return xl.reshape(BATCH, SEQ, D_MODEL)
```

```python
import functools

import numpy as np
import jax
import jax.numpy as jnp
from jax import lax
from jax.experimental import pallas as pl
from jax.experimental.pallas import tpu as pltpu

D_MODEL = 2048
BATCH = 4
SEQ = 2048
DEPTH = 2
CTX_LEN = 256
GRID_W = 64
GRID_ROWS = SEQ // GRID_W
MLA_HEADS = 6
MLA_Q_RANK = 512
MLA_KV_RANK = 512
MLA_NOPE = 128
MLA_ROPE = 64
MLA_V = 128
MLA_QK = MLA_NOPE + MLA_ROPE
CONV_DIM = 512
NA_HEADS = 6
NA_HEAD_DIM = 128
NA_DIM = NA_HEADS * NA_HEAD_DIM
NA_WIN_R = 8
NA_WIN_C = 16
OFF_MLA_Q = 0
OFF_MLA_KV = OFF_MLA_Q + MLA_Q_RANK
OFF_ROPE = OFF_MLA_KV + MLA_KV_RANK
OFF_CONV_B = OFF_ROPE + MLA_ROPE
OFF_CONV_C = OFF_CONV_B + CONV_DIM
OFF_CONV_H = OFF_CONV_C + CONV_DIM
OFF_NA_Q = OFF_CONV_H + CONV_DIM
OFF_NA_K = OFF_NA_Q + NA_DIM
OFF_NA_V = OFF_NA_K + NA_DIM
IN_DIM = OFF_NA_V + NA_DIM
D_FF = -(-8 * D_MODEL // (3 * 256)) * 256
ROPE_THETA = 10000.0
LN_EPS = 1e-6
RMS_EPS = 1e-6
DEEPNORM_ALPHA = (2 * DEPTH) ** 0.25
LOG2E = 1.4426950408889634
N_MOD = 6
MOD_ROWS = 8
CTX_MOD_ROW = BATCH

LANES = 128
V7X_VMEM_BYTES = 64 * 1024 * 1024
V7X_VMEM_BUDGET = 62 * 1024 * 1024

Z_Q = 0
Z_KV = Z_Q + MLA_Q_RANK
Z_CONV_B = Z_KV + MLA_KV_RANK
Z_CONV_C = Z_CONV_B + CONV_DIM
Z_CONV_H = Z_CONV_C + CONV_DIM
Z_NA_Q = Z_CONV_H + CONV_DIM
Z_NA_K = Z_NA_Q + NA_DIM
Z_NA_V = Z_NA_K + NA_DIM
Z_ROPE = Z_NA_V + NA_DIM
Z_DIM = Z_ROPE + 2 * MLA_ROPE
V7X_MXU_COLS = 256
Z_PAD = -(-Z_DIM // (5 * V7X_MXU_COLS)) * (5 * V7X_MXU_COLS)

QK_PAD = 2 * LANES

TM = 1024
TM_OUT = 1024
TM_OUT_CHUNK = 256
TM_PROJ = 512
PREP_PIECE = MLA_ROPE
PREP_COLS = 5 * LANES
TN_IN = 5 * V7X_MXU_COLS
TN_ADA = 1024
TF = 512
TQ = 256
Q_TILE_UNROLL = 8
NA_RQ = 4
NA_RK = NA_RQ + NA_WIN_R - 1
NA_NQ = NA_RQ * GRID_W
NA_NK = NA_RK * GRID_W
NA_BLOCKS = GRID_ROWS // NA_RQ
FFN_CAST_BLOCKS = 22
OUT_CAST_BLOCKS = 8
NA_HEADS_PER_STEP = 2

F32 = jnp.float32
BF16 = jnp.bfloat16


def _params(semantics, vmem_bytes):
    assert vmem_bytes <= V7X_VMEM_BUDGET, vmem_bytes
    return pltpu.CompilerParams(dimension_semantics=semantics, vmem_limit_bytes=int(vmem_bytes))


def _nbytes(shape, dtype):
    return int(np.prod(shape)) * jnp.dtype(dtype).itemsize


def _dot(a, b):
    return jnp.dot(a, b, preferred_element_type=F32)


def _dot_nt(a, b):
    return lax.dot_general(a, b, (((1,), (1,)), ((), ())), preferred_element_type=F32)


def _mod_spec(layer, chunk, row_of_tile):
    base = layer * MOD_ROWS * N_MOD + chunk
    return pl.BlockSpec((None, 1, D_MODEL), lambda i, *_: (base + row_of_tile(i) * N_MOD, 0, 0))


def _layer_norm(r, g, b):
    mu = jnp.mean(r, axis=-1, keepdims=True)
    c = r - mu
    var = jnp.mean(c * c, axis=-1, keepdims=True)
    return c * lax.rsqrt(var + LN_EPS) * g + b


def _ada_kernel(cc_ref, w_ref, b_ref, o_ref):
    cc = cc_ref[...]
    s = (cc * jax.nn.sigmoid(cc)).astype(BF16)
    o_ref[...] = _dot(s, w_ref[...].astype(BF16)) + b_ref[...]


def _ada(cc, ada_w, ada_b):
    n = N_MOD * D_MODEL
    vmem = 2 * (_nbytes((D_MODEL, TN_ADA), F32) + _nbytes((MOD_ROWS, TN_ADA), F32) * 2
                + _nbytes((MOD_ROWS, D_MODEL), F32)) + _nbytes((D_MODEL, TN_ADA), BF16) * 2
    return pl.pallas_call(
        _ada_kernel,
        out_shape=jax.ShapeDtypeStruct((DEPTH, MOD_ROWS, n), F32),
        grid=(DEPTH, n // TN_ADA),
        in_specs=[
            pl.BlockSpec((MOD_ROWS, D_MODEL), lambda l, j: (0, 0)),
            pl.BlockSpec((None, D_MODEL, TN_ADA), lambda l, j: (l, 0, j)),
            pl.BlockSpec((None, 1, TN_ADA), lambda l, j: (l, 0, j)),
        ],
        out_specs=pl.BlockSpec((None, MOD_ROWS, TN_ADA), lambda l, j: (l, 0, j)),
        compiler_params=_params(("parallel", "parallel"), vmem),
        name="ada_mod",
    )(cc, ada_w, ada_b.reshape(DEPTH, 1, n))


def _in_proj_kernel(sh_ref, sc_ref, x_ref, w_ref, o_ref, xm_ref):
    @pl.when(pl.program_id(1) == 0)
    def _():
        xm_ref[...] = (x_ref[...] * (1.0 + sc_ref[...]) + sh_ref[...]).astype(BF16)

    o_ref[...] = _dot(xm_ref[...], w_ref[...]).astype(BF16)


def _in_proj(x, mods, w_in_p, layer, row_of_tile):
    rows = x.shape[0]
    vmem = (2 * (_nbytes((TM, D_MODEL), F32) + _nbytes((D_MODEL, TN_IN), BF16) + _nbytes((TM, TN_IN), BF16))
            + _nbytes((TM, D_MODEL), BF16) * 2 + _nbytes((TM, TN_IN), F32))
    return pl.pallas_call(
        _in_proj_kernel,
        out_shape=jax.ShapeDtypeStruct((rows, Z_PAD), BF16),
        grid=(rows // TM, Z_PAD // TN_IN),
        in_specs=[
            _mod_spec(layer, 0, row_of_tile),
            _mod_spec(layer, 1, row_of_tile),
            pl.BlockSpec((TM, D_MODEL), lambda i, j: (i, 0)),
            pl.BlockSpec((None, D_MODEL, TN_IN), lambda i, j: (layer, 0, j)),
        ],
        out_specs=pl.BlockSpec((TM, TN_IN), lambda i, j: (i, j)),
        scratch_shapes=[pltpu.VMEM((TM, D_MODEL), BF16)],
        compiler_params=_params(("parallel", "arbitrary"), vmem),
        name="in_proj",
    )(mods, mods, x, w_in_p)


def _rms_norm(x, g):
    return x * lax.rsqrt(jnp.mean(x * x, axis=-1, keepdims=True) + RMS_EPS) * g


def _rope_sum(pair, table, lane):
    y = pair * table
    return jnp.where(lane < MLA_ROPE, y + pltpu.roll(y, MLA_ROPE, 1), 0.0)


def _mla_proj_kernel(z_ref, slot_ref, gq_ref, gkv_ref, wq_ref, wkv_ref, ta_ref, q_ref, k_ref, v_ref):
    cq = z_ref[:, Z_Q:Z_Q + MLA_Q_RANK].astype(F32)
    ckv = z_ref[:, Z_KV:Z_KV + MLA_KV_RANK].astype(F32)
    qa = _dot(_rms_norm(cq, gq_ref[...]).astype(BF16), wq_ref[...]) * (MLA_QK ** -0.5 * LOG2E)
    kv = _dot(_rms_norm(ckv, gkv_ref[...]).astype(BF16), wkv_ref[...])
    ta = ta_ref[...]
    lane = lax.broadcasted_iota(jnp.int32, ta.shape, 1)
    kro = _rope_sum(slot_ref[...].astype(F32), ta, lane).astype(BF16)
    for h in range(MLA_HEADS):
        ob = h * QK_PAD
        q_ref[:, ob:ob + LANES] = qa[:, ob:ob + LANES].astype(BF16)
        q_ref[:, ob + LANES:ob + QK_PAD] = _rope_sum(qa[:, ob + LANES:ob + QK_PAD], ta, lane).astype(BF16)
        k_ref[:, ob:ob + LANES] = kv[:, ob:ob + LANES].astype(BF16)
        k_ref[:, ob + LANES:ob + QK_PAD] = kro
        v_ref[:, h * MLA_V:(h + 1) * MLA_V] = kv[:, ob + LANES:ob + QK_PAD].astype(BF16)


def _mla_proj(z, g_q, g_kv, wq_all, wkv, ta, layer, table_tile):
    rows = z.shape[0]
    per_layer = lambda i: (layer, 0, 0)
    zw = Z_KV + MLA_KV_RANK
    hq = MLA_HEADS * QK_PAD
    hv = MLA_HEADS * MLA_V
    tm = TM_PROJ
    vmem = (2 * (_nbytes((tm, zw), BF16) + _nbytes((tm, LANES), BF16) + _nbytes((tm, LANES), F32)
                 + _nbytes(wq_all.shape[1:], BF16) + _nbytes(wkv.shape[1:], BF16)
                 + 2 * _nbytes((tm, hq), BF16) + _nbytes((tm, hv), BF16))
            + 2 * _nbytes((tm, hq), F32) + 4 * _nbytes((tm, zw), F32))
    return pl.pallas_call(
        _mla_proj_kernel,
        out_shape=(jax.ShapeDtypeStruct((rows, hq), BF16), jax.ShapeDtypeStruct((rows, hq), BF16),
                   jax.ShapeDtypeStruct((rows, hv), BF16)),
        grid=(rows // tm,),
        in_specs=[
            pl.BlockSpec((tm, zw), lambda i: (i, 0)),
            pl.BlockSpec((tm, 2 * MLA_ROPE), lambda i: (i, Z_ROPE // (2 * MLA_ROPE))),
            pl.BlockSpec((None, 1, MLA_Q_RANK), per_layer),
            pl.BlockSpec((None, 1, MLA_KV_RANK), per_layer),
            pl.BlockSpec((None,) + wq_all.shape[1:], per_layer),
            pl.BlockSpec((None,) + wkv.shape[1:], per_layer),
            pl.BlockSpec((tm, LANES), lambda i: (table_tile(i), 0)),
        ],
        out_specs=(pl.BlockSpec((tm, hq), lambda i: (i, 0)), pl.BlockSpec((tm, hq), lambda i: (i, 0)),
                   pl.BlockSpec((tm, hv), lambda i: (i, 0))),
        compiler_params=_params(("parallel",), vmem),
        name="mla_proj",
    )(z, z, g_q, g_kv, wq_all, wkv, ta)


def _rider_specs(w, layer, n_blocks, axis, n_heads):
    rows, cols = w.shape[1:]
    block = (rows // n_blocks, cols) if axis == 0 else (rows, cols // n_blocks)
    assert block[0] * block[1] * n_blocks == rows * cols and n_blocks <= BATCH * n_heads

    def at(b, h):
        step = jnp.minimum(b * n_heads + h, n_blocks - 1)
        return (step, 0) if axis == 0 else (0, step)

    in_spec = pl.BlockSpec((None,) + block, lambda b, h: (layer,) + at(b, h))
    return in_spec, pl.BlockSpec(block, at), jax.ShapeDtypeStruct((rows, cols), BF16), block


def _cast_riders(in_refs, out_refs):
    for i_ref, o_ref in zip(in_refs, out_refs):
        o_ref[...] = i_ref[...].astype(BF16)


def _split_refs(refs, *counts):
    out, pos = [], 0
    for c in counts:
        out.append(refs[pos:pos + c])
        pos += c
    assert pos == len(refs)
    return out


def _softmax_pv(scores, values_ones):
    m = functools.reduce(jnp.maximum, [jnp.max(s, axis=-1, keepdims=True) for s in scores])
    acc = functools.reduce(jnp.add, [_dot(jnp.exp2(s - m).astype(BF16), v) for s, v in zip(scores, values_ones)])
    return acc[:, :LANES] / acc[:, LANES:]


def _fill_values_ones(v_ref, vx_ref):
    vx_ref[:, :LANES] = v_ref[...]
    vx_ref[:, LANES:] = jnp.ones((vx_ref.shape[0], LANES), BF16)


def _scaled_q(q, head_dim):
    return (q.astype(F32) * (head_dim ** -0.5 * LOG2E)).astype(BF16)


def _mla_attn_kernel(*refs, with_ctx, n_riders):
    ins, rider_in, outs, rider_out, (vxl_ref, vxc_ref) = _split_refs(
        refs, 6 if with_ctx else 5, n_riders, 2 if with_ctx else 1, n_riders, 2)
    q_ref, kl_ref, vl_ref, kc_ref, vc_ref = ins[:5]
    o_ref = outs[0]
    _cast_riders(rider_in, rider_out)
    _fill_values_ones(vl_ref, vxl_ref)
    _fill_values_ones(vc_ref, vxc_ref)

    def q_tile(t, carry):
        r0 = pl.multiple_of(t * TQ, TQ)
        q = q_ref[pl.ds(r0, TQ), :]
        s_lat = _dot_nt(q, kl_ref[...])
        s_ctx = _dot_nt(q, kc_ref[...])
        o = _softmax_pv([s_lat, s_ctx], [vxl_ref[...], vxc_ref[...]])
        o_ref[pl.ds(r0, TQ), :] = o.astype(BF16)
        return carry

    lax.fori_loop(0, SEQ // TQ, q_tile, 0, unroll=Q_TILE_UNROLL)
    if with_ctx:
        s = _dot_nt(ins[5][...], kc_ref[...])
        outs[1][...] = _softmax_pv([s], [vxc_ref[...]]).astype(BF16)


def _mla_attn(ql, kl, vl, qc, kc, vc, with_ctx, layer, riders):
    head = lambda b, h: (b, h)
    in_specs = [
        pl.BlockSpec((SEQ, QK_PAD), head), pl.BlockSpec((SEQ, QK_PAD), head), pl.BlockSpec((SEQ, MLA_V), head),
        pl.BlockSpec((CTX_LEN, QK_PAD), head), pl.BlockSpec((CTX_LEN, MLA_V), head),
    ]
    args = [ql, kl, vl, kc, vc]
    out_shape = [jax.ShapeDtypeStruct((BATCH * SEQ, MLA_HEADS * MLA_V), BF16)]
    out_specs = [pl.BlockSpec((SEQ, MLA_V), head)]
    if with_ctx:
        in_specs.append(pl.BlockSpec((CTX_LEN, QK_PAD), head))
        args.append(qc)
        out_shape.append(jax.ShapeDtypeStruct((BATCH * CTX_LEN, MLA_HEADS * MLA_V), BF16))
        out_specs.append(pl.BlockSpec((CTX_LEN, MLA_V), head))
    vmem = (2 * (3 * _nbytes((SEQ, QK_PAD), BF16) + 2 * _nbytes((SEQ, MLA_V), BF16)) + 2 * _nbytes((SEQ, 2 * LANES), BF16)
            + Q_TILE_UNROLL * _nbytes((TQ, SEQ + CTX_LEN), F32))
    n_base = len(out_shape)
    for w, n_blocks, axis in riders:
        in_spec, out_spec, shape, block = _rider_specs(w, layer, n_blocks, axis, MLA_HEADS)
        in_specs.append(in_spec)
        args.append(w)
        out_specs.append(out_spec)
        out_shape.append(shape)
        vmem += 2 * (_nbytes(block, F32) + _nbytes(block, BF16))
    res = pl.pallas_call(
        functools.partial(_mla_attn_kernel, with_ctx=with_ctx, n_riders=len(riders)),
        out_shape=tuple(out_shape), grid=(BATCH, MLA_HEADS), in_specs=in_specs, out_specs=tuple(out_specs),
        scratch_shapes=[pltpu.VMEM((SEQ, 2 * LANES), BF16), pltpu.VMEM((CTX_LEN, 2 * LANES), BF16)],
        compiler_params=_params(("arbitrary", "arbitrary"), vmem),
        name="mla_attn_ctx" if with_ctx else "mla_attn",
    )(*args)
    return res[:n_base], res[n_base:]


def _conv_kernel(gb_ref, gc_ref, h_ref, w_ref, o_ref):
    n = o_ref.shape[0]
    u = gc_ref[...].astype(F32) * h_ref[...].astype(F32)
    row = lax.broadcasted_iota(jnp.int32, u.shape, 0)
    u_prev = jnp.where(row == 0, 0.0, pltpu.roll(u, 1, 0))
    u_next = jnp.where(row == n - 1, 0.0, pltpu.roll(u, n - 1, 0))
    y = u_prev * w_ref[0:1, :] + u * w_ref[1:2, :] + u_next * w_ref[2:3, :]
    o_ref[...] = (gb_ref[...].astype(F32) * y).astype(BF16)


def _conv(z, conv_w, layer, seq_len):
    rows = z.shape[0]
    blk = (seq_len, CONV_DIM)
    vmem = 2 * 4 * _nbytes(blk, BF16) + 8 * _nbytes(blk, F32)
    return pl.pallas_call(
        _conv_kernel,
        out_shape=jax.ShapeDtypeStruct((rows, CONV_DIM), BF16),
        grid=(rows // seq_len,),
        in_specs=[
            pl.BlockSpec(blk, lambda s: (s, Z_CONV_B // CONV_DIM)),
            pl.BlockSpec(blk, lambda s: (s, Z_CONV_C // CONV_DIM)),
            pl.BlockSpec(blk, lambda s: (s, Z_CONV_H // CONV_DIM)),
            pl.BlockSpec((None,) + conv_w.shape[1:], lambda s: (layer, 0, 0)),
        ],
        out_specs=pl.BlockSpec(blk, lambda s: (s, 0)),
        compiler_params=_params(("parallel",), vmem),
        name="short_conv",
    )(z, z, z, conv_w)


def _na_band_start(block):
    return min(max(block * NA_RQ - NA_WIN_R // 2, 0), GRID_ROWS - NA_RK)


def _na_bias_kind(block):
    return 0 if block == 0 else (2 if block == NA_BLOCKS - 1 else 1)


def _na_row_offsets():
    n_dr = 2 * NA_WIN_R - 1
    idx = np.full((3, NA_RQ, NA_RK), n_dr, np.int32)
    for kind, block in ((0, 0), (1, 2), (2, NA_BLOCKS - 1)):
        for qr in range(NA_RQ):
            r = block * NA_RQ + qr
            r0 = min(max(r - NA_WIN_R // 2, 0), GRID_ROWS - NA_WIN_R)
            for ki in range(NA_RK):
                kr = _na_band_start(block) + ki
                if r0 <= kr < r0 + NA_WIN_R:
                    idx[kind, qr, ki] = kr - r + NA_WIN_R - 1
    return idx


def _na_bias_kernel(w_ref, o_ref):
    shape = (GRID_W, LANES)
    c = lax.broadcasted_iota(jnp.int32, shape, 0)
    kc = lax.broadcasted_iota(jnp.int32, shape, 1)
    c0 = jnp.clip(c - NA_WIN_C // 2, 0, GRID_W - NA_WIN_C)
    col_ok = (kc >= c0) & (kc < c0 + NA_WIN_C)
    neg = jnp.full((GRID_W, GRID_W), -jnp.inf, F32)
    offsets = _na_row_offsets()
    n_dr = 2 * NA_WIN_R - 1
    tiles = {n_dr: neg}
    for d in sorted(set(int(v) for v in offsets.reshape(-1)) - {n_dr}):
        row = jnp.broadcast_to(w_ref[d:d + 1, :], shape)
        skew = pltpu.roll(row, LANES - (GRID_W - 1), 1, stride=1, stride_axis=0)
        tiles[d] = jnp.where(col_ok, skew, -jnp.inf)[:, :GRID_W]
    for kind in range(3):
        for qr in range(NA_RQ):
            for ki in range(NA_RK):
                o_ref[kind, qr * GRID_W:(qr + 1) * GRID_W, ki * GRID_W:(ki + 1) * GRID_W] = (
                    tiles[int(offsets[kind, qr, ki])])


def _na_bias(rpb):
    n_dr, n_dc = 2 * NA_WIN_R - 1, 2 * NA_WIN_C - 1
    lo = GRID_W - 1 - (NA_WIN_C - 1)
    w = jnp.pad(rpb.reshape(DEPTH * NA_HEADS, n_dr, n_dc) * LOG2E, ((0, 0), (0, 0), (lo, LANES - lo - n_dc)),
                constant_values=-jnp.inf)
    vmem = 2 * (_nbytes((2 * 8, LANES), F32) + _nbytes((3, NA_NQ, NA_NK + GRID_W), F32)) + 16 * _nbytes(
        (GRID_W, LANES), F32)
    return pl.pallas_call(
        _na_bias_kernel,
        out_shape=jax.ShapeDtypeStruct((w.shape[0], 3, NA_NQ, NA_NK), F32),
        grid=(w.shape[0],),
        in_specs=[pl.BlockSpec((None, n_dr, LANES), lambda g: (g, 0, 0))],
        out_specs=pl.BlockSpec((None, 3, NA_NQ, NA_NK), lambda g: (g, 0, 0, 0)),
        compiler_params=_params(("parallel",), vmem),
        name="na_bias",
    )(w)


def _na_kernel(*refs, with_ctx, n_riders):
    ins, rider_in, outs, rider_out, (vx_ref, vxc_ref) = _split_refs(
        refs, 7 if with_ctx else 6, n_riders, 2 if with_ctx else 1, n_riders, 2)
    q_ref, k_ref, v_ref, kc_ref, vc_ref, bias_ref = ins[:6]
    o_ref = outs[0]
    _cast_riders(rider_in, rider_out)
    d = NA_HEAD_DIM
    for hh in range(NA_HEADS_PER_STEP):
        cols = slice(hh * d, (hh + 1) * d)
        _fill_values_ones(v_ref.at[:, cols], vx_ref.at[hh])
        _fill_values_ones(vc_ref.at[:, cols], vxc_ref.at[hh])
        for block in range(NA_BLOCKS):
            rows = slice(block * NA_NQ, (block + 1) * NA_NQ)
            q = _scaled_q(q_ref[rows, cols], d)
            k0 = _na_band_start(block) * GRID_W
            s_loc = _dot_nt(q, k_ref[k0:k0 + NA_NK, cols]) + bias_ref[hh, _na_bias_kind(block)]
            s_ctx = _dot_nt(q, kc_ref[:, cols])
            o = _softmax_pv([s_loc, s_ctx], [vx_ref[hh, k0:k0 + NA_NK, :], vxc_ref[hh]])
            o_ref[rows, cols] = o.astype(BF16)
        if with_ctx:
            s = _dot_nt(_scaled_q(ins[6][:, cols], d), kc_ref[:, cols])
            outs[1][:, cols] = _softmax_pv([s], [vxc_ref[hh]]).astype(BF16)


def _na_attn(zl, zc, bias, layer, with_ctx, riders):
    hps = NA_HEADS_PER_STEP
    w = hps * NA_HEAD_DIM
    steps = NA_HEADS // hps
    col = lambda off: (lambda b, g: (b, off // w + g))
    in_specs = [
        pl.BlockSpec((SEQ, w), col(Z_NA_Q)), pl.BlockSpec((SEQ, w), col(Z_NA_K)), pl.BlockSpec((SEQ, w), col(Z_NA_V)),
        pl.BlockSpec((CTX_LEN, w), col(Z_NA_K)), pl.BlockSpec((CTX_LEN, w), col(Z_NA_V)),
        pl.BlockSpec((hps, 3, NA_NQ, NA_NK), lambda b, g: (layer * steps + g, 0, 0, 0)),
    ]
    args = [zl, zl, zl, zc, zc, bias]
    out_shape = [jax.ShapeDtypeStruct((BATCH * SEQ, NA_DIM), BF16)]
    out_specs = [pl.BlockSpec((SEQ, w), lambda b, g: (b, g))]
    if with_ctx:
        in_specs.append(pl.BlockSpec((CTX_LEN, w), col(Z_NA_Q)))
        args.append(zc)
        out_shape.append(jax.ShapeDtypeStruct((BATCH * CTX_LEN, NA_DIM), BF16))
        out_specs.append(pl.BlockSpec((CTX_LEN, w), lambda b, g: (b, g)))
    vmem = (2 * (5 * _nbytes((SEQ, w), BF16) + _nbytes((hps, 3, NA_NQ, NA_NK), F32))
            + 2 * hps * _nbytes((SEQ, 2 * LANES), BF16) + 8 * hps * _nbytes((NA_NQ, NA_NK + CTX_LEN), F32))
    n_base = len(out_shape)
    for wt, n_blocks, axis in riders:
        in_spec, out_spec, shape, block = _rider_specs(wt, layer, n_blocks, axis, steps)
        in_specs.append(in_spec)
        args.append(wt)
        out_specs.append(out_spec)
        out_shape.append(shape)
        vmem += 2 * (_nbytes(block, F32) + _nbytes(block, BF16))
    res = pl.pallas_call(
        functools.partial(_na_kernel, with_ctx=with_ctx, n_riders=len(riders)),
        out_shape=tuple(out_shape), grid=(BATCH, steps), in_specs=in_specs, out_specs=tuple(out_specs),
        scratch_shapes=[pltpu.VMEM((hps, SEQ, 2 * LANES), BF16), pltpu.VMEM((hps, CTX_LEN, 2 * LANES), BF16)],
        compiler_params=_params(("arbitrary", "arbitrary"), vmem),
        name="na_attn_ctx" if with_ctx else "na_attn",
    )(*args)
    return res[:n_base], res[n_base:]


def _out_proj_kernel(g_ref, a_ref, c_ref, n_ref, x_ref, w_ref, lng_ref, lnb_ref, o_ref):
    a_w = MLA_HEADS * MLA_V
    for r0 in range(0, o_ref.shape[0], TM_OUT_CHUNK):
        rows = slice(r0, r0 + TM_OUT_CHUNK)
        y = (_dot(a_ref[rows, :], w_ref[0:a_w, :]) + _dot(c_ref[rows, :], w_ref[a_w:a_w + CONV_DIM, :])
             + _dot(n_ref[rows, :], w_ref[a_w + CONV_DIM:, :]))
        r = DEEPNORM_ALPHA * x_ref[rows, :] + g_ref[...] * y
        o_ref[rows, :] = _layer_norm(r, lng_ref[...], lnb_ref[...])


def _out_proj(a, cv, n, x, mods, w_out, ln_g, ln_b, layer, row_of_tile):
    rows = x.shape[0]
    tm = TM_OUT
    row = lambda i: (i, 0)
    per_layer = lambda i: (layer, 0, 0)
    vmem = (2 * (2 * _nbytes((tm, D_MODEL), F32) + _nbytes((tm, D_MODEL), BF16)) + _nbytes(w_out.shape, BF16)
            + 3 * _nbytes((TM_OUT_CHUNK, D_MODEL), F32))
    return pl.pallas_call(
        _out_proj_kernel,
        out_shape=jax.ShapeDtypeStruct((rows, D_MODEL), F32),
        grid=(rows // tm,),
        in_specs=[
            _mod_spec(layer, 2, row_of_tile),
            pl.BlockSpec((tm, a.shape[1]), row), pl.BlockSpec((tm, cv.shape[1]), row),
            pl.BlockSpec((tm, n.shape[1]), row), pl.BlockSpec((tm, D_MODEL), row),
            pl.BlockSpec(w_out.shape, lambda i: (0, 0), pipeline_mode=pl.Buffered(1)),
            pl.BlockSpec((None, 1, D_MODEL), per_layer), pl.BlockSpec((None, 1, D_MODEL), per_layer),
        ],
        out_specs=pl.BlockSpec((tm, D_MODEL), row),
        compiler_params=_params(("parallel",), vmem),
        name="out_proj",
    )(mods, a, cv, n, x, w_out, ln_g, ln_b)


def _ffn_kernel(sh_ref, sc_ref, g_ref, x_ref, wg_ref, wu_ref, wd_ref, lng_ref, lnb_ref, o_ref, xm_ref):
    f = pl.program_id(1)

    @pl.when(f == 0)
    def _():
        x = x_ref[...]
        xm_ref[...] = (x * (1.0 + sc_ref[...]) + sh_ref[...]).astype(BF16)
        o_ref[...] = DEEPNORM_ALPHA * x

    xm = xm_ref[...]
    gate = _dot(xm, wg_ref[...])
    up = _dot(xm, wu_ref[...])
    hidden = (gate * jax.nn.sigmoid(gate) * up).astype(BF16)
    o_ref[...] += g_ref[...] * _dot(hidden, wd_ref[...])

    @pl.when(f == pl.num_programs(1) - 1)
    def _():
        o_ref[...] = _layer_norm(o_ref[...], lng_ref[...], lnb_ref[...])


def _ffn(x, mods, w_gate, w_up, w_down, ln_g, ln_b, layer, row_of_tile):
    rows = x.shape[0]
    per_layer = lambda i, f: (layer, 0, 0)
    vmem = (4 * _nbytes((TM, D_MODEL), F32) + 2 * 3 * _nbytes((D_MODEL, TF), BF16)
            + _nbytes((TM, D_MODEL), BF16) + 3 * _nbytes((TM, TF), F32) + _nbytes((TM, D_MODEL), F32))
    return pl.pallas_call(
        _ffn_kernel,
        out_shape=jax.ShapeDtypeStruct((rows, D_MODEL), F32),
        grid=(rows // TM, D_FF // TF),
        in_specs=[
            _mod_spec(layer, 3, row_of_tile), _mod_spec(layer, 4, row_of_tile), _mod_spec(layer, 5, row_of_tile),
            pl.BlockSpec((TM, D_MODEL), lambda i, f: (i, 0)),
            pl.BlockSpec((D_MODEL, TF), lambda i, f: (0, f)),
            pl.BlockSpec((D_MODEL, TF), lambda i, f: (0, f)),
            pl.BlockSpec((TF, D_MODEL), lambda i, f: (f, 0)),
            pl.BlockSpec((None, 1, D_MODEL), per_layer), pl.BlockSpec((None, 1, D_MODEL), per_layer),
        ],
        out_specs=pl.BlockSpec((TM, D_MODEL), lambda i, f: (i, 0)),
        scratch_shapes=[pltpu.VMEM((TM, D_MODEL), BF16)],
        compiler_params=_params(("parallel", "arbitrary"), vmem),
        name="ffn",
    )(mods, mods, mods, x, w_gate, w_up, w_down, ln_g, ln_b)


def _rotate_half_cols(w):
    a1, a2, b1, b2 = jnp.split(w, 4, axis=-1)
    return jnp.concatenate([-a2, a1, -b2, b1], axis=-1)


def _w_in_prep_src(piece):
    rope = OFF_ROPE // PREP_PIECE
    tail = Z_ROPE // PREP_PIECE
    return jnp.where(piece < rope, piece, jnp.where(piece < tail, piece + 1, rope))


def _w_in_prep_kernel(*refs):
    *piece_refs, o_ref = refs
    pieces = [r[...] for r in piece_refs]
    n = len(pieces)
    is_last = pl.program_id(1) == pl.num_programs(1) - 1
    rot_piece = Z_DIM // PREP_PIECE - 1
    for p in range(n):
        piece = (Z_PAD // PREP_COLS - 1) * n + p
        if piece == rot_piece:
            a1, a2, b1, b2 = jnp.split(pieces[p], 4, axis=0)
            pieces[p] = jnp.where(is_last, jnp.concatenate([-a2, a1, -b2, b1], axis=0), pieces[p])
        elif piece > rot_piece:
            pieces[p] = jnp.where(is_last, 0.0, pieces[p])
    o_ref[...] = jnp.concatenate(pieces, axis=0).T.astype(BF16)


def _w_in_prep(w_in):
    w_t = jnp.swapaxes(w_in, 1, 2)
    n = PREP_COLS // PREP_PIECE
    piece_spec = lambda p: pl.BlockSpec((None, PREP_PIECE, D_MODEL), lambda l, g: (l, _w_in_prep_src(g * n + p), 0))
    vmem = 2 * (_nbytes((PREP_COLS, D_MODEL), F32) + _nbytes((D_MODEL, PREP_COLS), BF16)) + 3 * _nbytes(
        (PREP_COLS, D_MODEL), F32)
    return pl.pallas_call(
        _w_in_prep_kernel,
        out_shape=jax.ShapeDtypeStruct((DEPTH, D_MODEL, Z_PAD), BF16),
        grid=(DEPTH, Z_PAD // PREP_COLS),
        in_specs=[piece_spec(p) for p in range(n)],
        out_specs=pl.BlockSpec((None, D_MODEL, PREP_COLS), lambda l, g: (l, 0, g)),
        compiler_params=_params(("parallel", "parallel"), vmem),
        name="w_in_prep",
    )(*([w_t] * n))


def _relayout_wq(w):
    w = w.reshape(DEPTH, MLA_Q_RANK, MLA_HEADS, MLA_QK)
    rope = w[..., MLA_NOPE:]
    cols = jnp.concatenate([w, _rotate_half_cols(rope)], axis=-1)
    return cols.reshape(DEPTH, MLA_Q_RANK, MLA_HEADS * QK_PAD).astype(BF16)


def _rope_tables():
    t = jnp.arange(SEQ)
    row = (t // GRID_W).astype(F32)
    col = (t % GRID_W).astype(F32)
    n_freq = MLA_ROPE // 4
    inv = ROPE_THETA ** (-jnp.arange(n_freq, dtype=F32) / n_freq)
    ar = row[:, None] * inv
    ac = col[:, None] * inv
    ang = jnp.concatenate([ar, ar, ac, ac], axis=-1)
    lat = jnp.concatenate([jnp.cos(ang), jnp.sin(ang)], axis=1)
    ctx = jnp.concatenate([jnp.ones((TM_PROJ, MLA_ROPE), F32), jnp.zeros((TM_PROJ, MLA_ROPE), F32)], axis=1)
    return lat, ctx


def kernel(x, c, ctx, c_ctx, ada_w, ada_b, w_in, mla_q_norm, mla_wq_b, mla_kv_norm, mla_wkv_b, conv_w, na_rpb,
           w_out, ln1_g, ln1_b, ffn_w_gate, ffn_w_up, ffn_w_down, ln2_g, ln2_b):
    assert x.shape == (BATCH, SEQ, D_MODEL) and ctx.shape == (BATCH, CTX_LEN, D_MODEL)
    cc = jnp.concatenate([c, c_ctx[None, :], jnp.zeros((MOD_ROWS - BATCH - 1, D_MODEL), F32)], axis=0)
    mods = _ada(cc, ada_w, ada_b).reshape(DEPTH * MOD_ROWS * N_MOD, 1, D_MODEL)
    ta_lat, ta_ctx = _rope_tables()
    rows3d = lambda v: v.reshape(DEPTH, 1, -1)

    w_in_p = _w_in_prep(w_in)
    wq_all = _relayout_wq(mla_wq_b)
    wkv = mla_wkv_b.astype(BF16)
    g_q, g_kv = rows3d(mla_q_norm), rows3d(mla_kv_norm)
    bias = _na_bias(na_rpb)
    ln1 = (rows3d(ln1_g), rows3d(ln1_b))
    ln2 = (rows3d(ln2_g), rows3d(ln2_b))

    xl = x.reshape(BATCH * SEQ, D_MODEL)
    xc = ctx.reshape(BATCH * CTX_LEN, D_MODEL)
    lat_row = lambda tm: (lambda i: i // (SEQ // tm))
    ctx_row = lambda i: CTX_MOD_ROW
    for l in range(DEPTH):
        last = l == DEPTH - 1
        zl = _in_proj(xl, mods, w_in_p, l, lat_row(TM))
        zc = _in_proj(xc, mods, w_in_p, l, ctx_row)
        ql, kl, vl = _mla_proj(zl, g_q, g_kv, wq_all, wkv, ta_lat, l, lambda i: i % (SEQ // TM_PROJ))
        qc, kc, vc = _mla_proj(zc, g_q, g_kv, wq_all, wkv, ta_ctx, l, lambda i: 0)
        ffn_riders = [(ffn_w_gate, FFN_CAST_BLOCKS, 1), (ffn_w_up, FFN_CAST_BLOCKS, 1), (ffn_w_down, FFN_CAST_BLOCKS, 0)]
        a, (w_g, w_u, w_d) = _mla_attn(ql, kl, vl, qc, kc, vc, not last, l, ffn_riders)
        n, (w_o,) = _na_attn(zl, zc, bias, l, not last, [(w_out, OUT_CAST_BLOCKS, 0)])
        cv_l = _conv(zl, conv_w, l, SEQ)
        x1 = _out_proj(a[0], cv_l, n[0], xl, mods, w_o, *ln1, l, lat_row(TM_OUT))
        xl = _ffn(x1, mods, w_g, w_u, w_d, *ln2, l, lat_row(TM))
        if not last:
            cv_c = _conv(zc, conv_w, l, CTX_LEN)
            x1c = _out_proj(a[1], cv_c, n[1], xc, mods, w_o, *ln1, l, ctx_row)
            xc = _ffn(x1c, mods, w_g, w_u, w_d, *ln2, l, ctx_row)
    return xl.reshape(BATCH, SEQ, D_MODEL)
```

```python
import functools

import numpy as np
import jax
import jax.numpy as jnp
from jax import lax
from jax.experimental import pallas as pl
from jax.experimental.pallas import tpu as pltpu

D_MODEL = 2048
BATCH = 4
SEQ = 2048
DEPTH = 2
CTX_LEN = 256
GRID_W = 64
GRID_ROWS = SEQ // GRID_W
MLA_HEADS = 6
MLA_Q_RANK = 512
MLA_KV_RANK = 512
MLA_NOPE = 128
MLA_ROPE = 64
MLA_V = 128
MLA_QK = MLA_NOPE + MLA_ROPE
CONV_DIM = 512
NA_HEADS = 6
NA_HEAD_DIM = 128
NA_DIM = NA_HEADS * NA_HEAD_DIM
NA_WIN_R = 8
NA_WIN_C = 16
OFF_MLA_Q = 0
OFF_MLA_KV = OFF_MLA_Q + MLA_Q_RANK
OFF_ROPE = OFF_MLA_KV + MLA_KV_RANK
OFF_CONV_B = OFF_ROPE + MLA_ROPE
OFF_CONV_C = OFF_CONV_B + CONV_DIM
OFF_CONV_H = OFF_CONV_C + CONV_DIM
OFF_NA_Q = OFF_CONV_H + CONV_DIM
OFF_NA_K = OFF_NA_Q + NA_DIM
OFF_NA_V = OFF_NA_K + NA_DIM
IN_DIM = OFF_NA_V + NA_DIM
D_FF = -(-8 * D_MODEL // (3 * 256)) * 256
ROPE_THETA = 10000.0
LN_EPS = 1e-6
RMS_EPS = 1e-6
DEEPNORM_ALPHA = (2 * DEPTH) ** 0.25
LOG2E = 1.4426950408889634
N_MOD = 6
MOD_ROWS = 8
CTX_MOD_ROW = BATCH

LANES = 128
V7X_VMEM_BYTES = 64 * 1024 * 1024
V7X_VMEM_BUDGET = 62 * 1024 * 1024

Z_Q = 0
Z_KV = Z_Q + MLA_Q_RANK
Z_CONV_B = Z_KV + MLA_KV_RANK
Z_CONV_C = Z_CONV_B + CONV_DIM
Z_CONV_H = Z_CONV_C + CONV_DIM
Z_NA_Q = Z_CONV_H + CONV_DIM
Z_NA_K = Z_NA_Q + NA_DIM
Z_NA_V = Z_NA_K + NA_DIM
Z_ROPE = Z_NA_V + NA_DIM
Z_DIM = Z_ROPE + 2 * MLA_ROPE
V7X_MXU_COLS = 256
Z_PAD = -(-Z_DIM // (5 * V7X_MXU_COLS)) * (5 * V7X_MXU_COLS)

QK_PAD = 2 * LANES

TM = 1024
TM_OUT = 512
TM_OUT_CHUNK = 256
CONV_HALO = 16
TM_PROJ = 512
PREP_PIECE = MLA_ROPE
PREP_COLS = 5 * LANES
TN_IN = 5 * V7X_MXU_COLS
TN_ADA = 1024
TF = 512
TQ = 256
Q_TILE_UNROLL = 8
NA_RQ = 4
NA_RK = NA_RQ + NA_WIN_R - 1
NA_NQ = NA_RQ * GRID_W
NA_NK = NA_RK * GRID_W
NA_BLOCKS = GRID_ROWS // NA_RQ
FFN_CAST_BLOCKS = 22
OUT_CAST_BLOCKS = 8
NA_HEADS_PER_STEP = 2

F32 = jnp.float32
BF16 = jnp.bfloat16


def _params(semantics, vmem_bytes):
    assert vmem_bytes <= V7X_VMEM_BUDGET, vmem_bytes
    return pltpu.CompilerParams(dimension_semantics=semantics, vmem_limit_bytes=int(vmem_bytes))


def _nbytes(shape, dtype):
    return int(np.prod(shape)) * jnp.dtype(dtype).itemsize


def _dot(a, b):
    return jnp.dot(a, b, preferred_element_type=F32)


def _dot_nt(a, b):
    return lax.dot_general(a, b, (((1,), (1,)), ((), ())), preferred_element_type=F32)


def _mod_spec(layer, chunk, row_of_tile):
    base = layer * MOD_ROWS * N_MOD + chunk
    return pl.BlockSpec((None, 1, D_MODEL), lambda i, *_: (base + row_of_tile(i) * N_MOD, 0, 0))


def _layer_norm(r, g, b):
    mu = jnp.mean(r, axis=-1, keepdims=True)
    c = r - mu
    var = jnp.mean(c * c, axis=-1, keepdims=True)
    return c * lax.rsqrt(var + LN_EPS) * g + b


def _ada_kernel(cc_ref, w_ref, b_ref, o_ref):
    cc = cc_ref[...]
    s = (cc * jax.nn.sigmoid(cc)).astype(BF16)
    o_ref[...] = _dot(s, w_ref[...].astype(BF16)) + b_ref[...]


def _ada(cc, ada_w, ada_b):
    n = N_MOD * D_MODEL
    vmem = 2 * (_nbytes((D_MODEL, TN_ADA), F32) + _nbytes((MOD_ROWS, TN_ADA), F32) * 2
                + _nbytes((MOD_ROWS, D_MODEL), F32)) + _nbytes((D_MODEL, TN_ADA), BF16) * 2
    return pl.pallas_call(
        _ada_kernel,
        out_shape=jax.ShapeDtypeStruct((DEPTH, MOD_ROWS, n), F32),
        grid=(DEPTH, n // TN_ADA),
        in_specs=[
            pl.BlockSpec((MOD_ROWS, D_MODEL), lambda l, j: (0, 0)),
            pl.BlockSpec((None, D_MODEL, TN_ADA), lambda l, j: (l, 0, j)),
            pl.BlockSpec((None, 1, TN_ADA), lambda l, j: (l, 0, j)),
        ],
        out_specs=pl.BlockSpec((None, MOD_ROWS, TN_ADA), lambda l, j: (l, 0, j)),
        compiler_params=_params(("parallel", "parallel"), vmem),
        name="ada_mod",
    )(cc, ada_w, ada_b.reshape(DEPTH, 1, n))


def _in_proj_kernel(sh_ref, sc_ref, x_ref, w_ref, o_ref, xm_ref):
    @pl.when(pl.program_id(1) == 0)
    def _():
        xm_ref[...] = (x_ref[...] * (1.0 + sc_ref[...]) + sh_ref[...]).astype(BF16)

    o_ref[...] = _dot(xm_ref[...], w_ref[...]).astype(BF16)


def _in_proj(x, mods, w_in_p, layer, row_of_tile):
    rows = x.shape[0]
    vmem = (2 * (_nbytes((TM, D_MODEL), F32) + _nbytes((D_MODEL, TN_IN), BF16) + _nbytes((TM, TN_IN), BF16))
            + _nbytes((TM, D_MODEL), BF16) * 2 + _nbytes((TM, TN_IN), F32))
    return pl.pallas_call(
        _in_proj_kernel,
        out_shape=jax.ShapeDtypeStruct((rows, Z_PAD), BF16),
        grid=(rows // TM, Z_PAD // TN_IN),
        in_specs=[
            _mod_spec(layer, 0, row_of_tile),
            _mod_spec(layer, 1, row_of_tile),
            pl.BlockSpec((TM, D_MODEL), lambda i, j: (i, 0)),
            pl.BlockSpec((None, D_MODEL, TN_IN), lambda i, j: (layer, 0, j)),
        ],
        out_specs=pl.BlockSpec((TM, TN_IN), lambda i, j: (i, j)),
        scratch_shapes=[pltpu.VMEM((TM, D_MODEL), BF16)],
        compiler_params=_params(("parallel", "arbitrary"), vmem),
        name="in_proj",
    )(mods, mods, x, w_in_p)


def _rms_norm(x, g):
    return x * lax.rsqrt(jnp.mean(x * x, axis=-1, keepdims=True) + RMS_EPS) * g


def _rope_sum(pair, table, lane):
    y = pair * table
    return jnp.where(lane < MLA_ROPE, y + pltpu.roll(y, MLA_ROPE, 1), 0.0)


def _mla_proj_kernel(z_ref, slot_ref, gq_ref, gkv_ref, wq_ref, wkv_ref, ta_ref, q_ref, k_ref, v_ref):
    cq = z_ref[:, Z_Q:Z_Q + MLA_Q_RANK].astype(F32)
    ckv = z_ref[:, Z_KV:Z_KV + MLA_KV_RANK].astype(F32)
    qa = _dot(_rms_norm(cq, gq_ref[...]).astype(BF16), wq_ref[...]) * (MLA_QK ** -0.5 * LOG2E)
    kv = _dot(_rms_norm(ckv, gkv_ref[...]).astype(BF16), wkv_ref[...])
    ta = ta_ref[...]
    lane = lax.broadcasted_iota(jnp.int32, ta.shape, 1)
    kro = _rope_sum(slot_ref[...].astype(F32), ta, lane).astype(BF16)
    for h in range(MLA_HEADS):
        ob = h * QK_PAD
        q_ref[:, ob:ob + LANES] = qa[:, ob:ob + LANES].astype(BF16)
        q_ref[:, ob + LANES:ob + QK_PAD] = _rope_sum(qa[:, ob + LANES:ob + QK_PAD], ta, lane).astype(BF16)
        k_ref[:, ob:ob + LANES] = kv[:, ob:ob + LANES].astype(BF16)
        k_ref[:, ob + LANES:ob + QK_PAD] = kro
        v_ref[:, h * MLA_V:(h + 1) * MLA_V] = kv[:, ob + LANES:ob + QK_PAD].astype(BF16)


def _mla_proj(z, g_q, g_kv, wq_all, wkv, ta, layer, table_tile):
    rows = z.shape[0]
    per_layer = lambda i: (layer, 0, 0)
    zw = Z_KV + MLA_KV_RANK
    hq = MLA_HEADS * QK_PAD
    hv = MLA_HEADS * MLA_V
    tm = TM_PROJ
    vmem = (2 * (_nbytes((tm, zw), BF16) + _nbytes((tm, LANES), BF16) + _nbytes((tm, LANES), F32)
                 + _nbytes(wq_all.shape[1:], BF16) + _nbytes(wkv.shape[1:], BF16)
                 + 2 * _nbytes((tm, hq), BF16) + _nbytes((tm, hv), BF16))
            + 2 * _nbytes((tm, hq), F32) + 4 * _nbytes((tm, zw), F32))
    return pl.pallas_call(
        _mla_proj_kernel,
        out_shape=(jax.ShapeDtypeStruct((rows, hq), BF16), jax.ShapeDtypeStruct((rows, hq), BF16),
                   jax.ShapeDtypeStruct((rows, hv), BF16)),
        grid=(rows // tm,),
        in_specs=[
            pl.BlockSpec((tm, zw), lambda i: (i, 0)),
            pl.BlockSpec((tm, 2 * MLA_ROPE), lambda i: (i, Z_ROPE // (2 * MLA_ROPE))),
            pl.BlockSpec((None, 1, MLA_Q_RANK), per_layer),
            pl.BlockSpec((None, 1, MLA_KV_RANK), per_layer),
            pl.BlockSpec((None,) + wq_all.shape[1:], per_layer),
            pl.BlockSpec((None,) + wkv.shape[1:], per_layer),
            pl.BlockSpec((tm, LANES), lambda i: (table_tile(i), 0)),
        ],
        out_specs=(pl.BlockSpec((tm, hq), lambda i: (i, 0)), pl.BlockSpec((tm, hq), lambda i: (i, 0)),
                   pl.BlockSpec((tm, hv), lambda i: (i, 0))),
        compiler_params=_params(("parallel",), vmem),
        name="mla_proj",
    )(z, z, g_q, g_kv, wq_all, wkv, ta)


def _rider_specs(w, layer, n_blocks, axis, grid):
    rows, cols = w.shape[1:]
    block = (rows // n_blocks, cols) if axis == 0 else (rows, cols // n_blocks)
    assert block[0] * block[1] * n_blocks == rows * cols and n_blocks <= grid[0] * grid[1]

    def at(outer, inner):
        step = jnp.minimum(outer * grid[1] + inner, n_blocks - 1)
        return (step, 0) if axis == 0 else (0, step)

    in_spec = pl.BlockSpec((None,) + block, lambda outer, inner: (layer,) + at(outer, inner))
    return in_spec, pl.BlockSpec(block, at), jax.ShapeDtypeStruct((rows, cols), BF16), block


def _cast_riders(in_refs, out_refs):
    for i_ref, o_ref in zip(in_refs, out_refs):
        o_ref[...] = i_ref[...].astype(BF16)


def _split_refs(refs, *counts):
    out, pos = [], 0
    for c in counts:
        out.append(refs[pos:pos + c])
        pos += c
    assert pos == len(refs)
    return out


def _softmax_pv(scores, values_ones):
    m = functools.reduce(jnp.maximum, [jnp.max(s, axis=-1, keepdims=True) for s in scores])
    acc = functools.reduce(jnp.add, [_dot(jnp.exp2(s - m).astype(BF16), v) for s, v in zip(scores, values_ones)])
    return acc[:, :LANES] / acc[:, LANES:]


def _fill_values_ones(v_ref, vx_ref):
    vx_ref[:, :LANES] = v_ref[...]
    vx_ref[:, LANES:] = jnp.ones((vx_ref.shape[0], LANES), BF16)


def _scaled_q(q, head_dim):
    return (q.astype(F32) * (head_dim ** -0.5 * LOG2E)).astype(BF16)


def _mla_attn_kernel(*refs, with_ctx, n_riders):
    ins, rider_in, outs, rider_out, (vxl_ref, vxc_ref) = _split_refs(
        refs, 6 if with_ctx else 5, n_riders, 2 if with_ctx else 1, n_riders, 2)
    q_ref, kl_ref, vl_ref, kc_ref, vc_ref = ins[:5]
    o_ref = outs[0]
    _cast_riders(rider_in, rider_out)
    _fill_values_ones(vl_ref, vxl_ref)
    _fill_values_ones(vc_ref, vxc_ref)

    def q_tile(t, carry):
        r0 = pl.multiple_of(t * TQ, TQ)
        q = q_ref[pl.ds(r0, TQ), :]
        s_lat = _dot_nt(q, kl_ref[...])
        s_ctx = _dot_nt(q, kc_ref[...])
        o = _softmax_pv([s_lat, s_ctx], [vxl_ref[...], vxc_ref[...]])
        o_ref[pl.ds(r0, TQ), :] = o.astype(BF16)
        return carry

    lax.fori_loop(0, SEQ // TQ, q_tile, 0, unroll=Q_TILE_UNROLL)
    if with_ctx:
        s = _dot_nt(ins[5][...], kc_ref[...])
        outs[1][...] = _softmax_pv([s], [vxc_ref[...]]).astype(BF16)


def _mla_attn(ql, kl, vl, qc, kc, vc, with_ctx, layer, riders):
    head = lambda b, h: (b, h)
    in_specs = [
        pl.BlockSpec((SEQ, QK_PAD), head), pl.BlockSpec((SEQ, QK_PAD), head), pl.BlockSpec((SEQ, MLA_V), head),
        pl.BlockSpec((CTX_LEN, QK_PAD), head), pl.BlockSpec((CTX_LEN, MLA_V), head),
    ]
    args = [ql, kl, vl, kc, vc]
    out_shape = [jax.ShapeDtypeStruct((BATCH * SEQ, MLA_HEADS * MLA_V), BF16)]
    out_specs = [pl.BlockSpec((SEQ, MLA_V), head)]
    if with_ctx:
        in_specs.append(pl.BlockSpec((CTX_LEN, QK_PAD), head))
        args.append(qc)
        out_shape.append(jax.ShapeDtypeStruct((BATCH * CTX_LEN, MLA_HEADS * MLA_V), BF16))
        out_specs.append(pl.BlockSpec((CTX_LEN, MLA_V), head))
    vmem = (2 * (3 * _nbytes((SEQ, QK_PAD), BF16) + 2 * _nbytes((SEQ, MLA_V), BF16)) + 2 * _nbytes((SEQ, 2 * LANES), BF16)
            + Q_TILE_UNROLL * _nbytes((TQ, SEQ + CTX_LEN), F32))
    n_base = len(out_shape)
    for w, n_blocks, axis in riders:
        in_spec, out_spec, shape, block = _rider_specs(w, layer, n_blocks, axis, (BATCH, MLA_HEADS))
        in_specs.append(in_spec)
        args.append(w)
        out_specs.append(out_spec)
        out_shape.append(shape)
        vmem += 2 * (_nbytes(block, F32) + _nbytes(block, BF16))
    res = pl.pallas_call(
        functools.partial(_mla_attn_kernel, with_ctx=with_ctx, n_riders=len(riders)),
        out_shape=tuple(out_shape), grid=(BATCH, MLA_HEADS), in_specs=in_specs, out_specs=tuple(out_specs),
        scratch_shapes=[pltpu.VMEM((SEQ, 2 * LANES), BF16), pltpu.VMEM((CTX_LEN, 2 * LANES), BF16)],
        compiler_params=_params(("arbitrary", "arbitrary"), vmem),
        name="mla_attn_ctx" if with_ctx else "mla_attn",
    )(*args)
    return res[:n_base], res[n_base:]


def _na_band_start(block):
    return min(max(block * NA_RQ - NA_WIN_R // 2, 0), GRID_ROWS - NA_RK)


def _na_bias_kind(block):
    return 0 if block == 0 else (2 if block == NA_BLOCKS - 1 else 1)


def _na_row_offsets():
    n_dr = 2 * NA_WIN_R - 1
    idx = np.full((3, NA_RQ, NA_RK), n_dr, np.int32)
    for kind, block in ((0, 0), (1, 2), (2, NA_BLOCKS - 1)):
        for qr in range(NA_RQ):
            r = block * NA_RQ + qr
            r0 = min(max(r - NA_WIN_R // 2, 0), GRID_ROWS - NA_WIN_R)
            for ki in range(NA_RK):
                kr = _na_band_start(block) + ki
                if r0 <= kr < r0 + NA_WIN_R:
                    idx[kind, qr, ki] = kr - r + NA_WIN_R - 1
    return idx


def _na_bias_kernel(w_ref, o_ref):
    shape = (GRID_W, LANES)
    c = lax.broadcasted_iota(jnp.int32, shape, 0)
    kc = lax.broadcasted_iota(jnp.int32, shape, 1)
    c0 = jnp.clip(c - NA_WIN_C // 2, 0, GRID_W - NA_WIN_C)
    col_ok = (kc >= c0) & (kc < c0 + NA_WIN_C)
    neg = jnp.full((GRID_W, GRID_W), -jnp.inf, F32)
    offsets = _na_row_offsets()
    n_dr = 2 * NA_WIN_R - 1
    tiles = {n_dr: neg}
    for d in sorted(set(int(v) for v in offsets.reshape(-1)) - {n_dr}):
        row = jnp.broadcast_to(w_ref[d:d + 1, :], shape)
        skew = pltpu.roll(row, LANES - (GRID_W - 1), 1, stride=1, stride_axis=0)
        tiles[d] = jnp.where(col_ok, skew, -jnp.inf)[:, :GRID_W]
    for kind in range(3):
        for qr in range(NA_RQ):
            for ki in range(NA_RK):
                o_ref[kind, qr * GRID_W:(qr + 1) * GRID_W, ki * GRID_W:(ki + 1) * GRID_W] = (
                    tiles[int(offsets[kind, qr, ki])])


def _na_bias(rpb):
    n_dr, n_dc = 2 * NA_WIN_R - 1, 2 * NA_WIN_C - 1
    lo = GRID_W - 1 - (NA_WIN_C - 1)
    w = jnp.pad(rpb.reshape(DEPTH * NA_HEADS, n_dr, n_dc) * LOG2E, ((0, 0), (0, 0), (lo, LANES - lo - n_dc)),
                constant_values=-jnp.inf)
    vmem = 2 * (_nbytes((2 * 8, LANES), F32) + _nbytes((3, NA_NQ, NA_NK + GRID_W), F32)) + 16 * _nbytes(
        (GRID_W, LANES), F32)
    return pl.pallas_call(
        _na_bias_kernel,
        out_shape=jax.ShapeDtypeStruct((w.shape[0], 3, NA_NQ, NA_NK), F32),
        grid=(w.shape[0],),
        in_specs=[pl.BlockSpec((None, n_dr, LANES), lambda g: (g, 0, 0))],
        out_specs=pl.BlockSpec((None, 3, NA_NQ, NA_NK), lambda g: (g, 0, 0, 0)),
        compiler_params=_params(("parallel",), vmem),
        name="na_bias",
    )(w)


def _na_kernel(*refs, with_ctx, n_riders):
    ins, rider_in, outs, rider_out, (vx_ref, vxc_ref) = _split_refs(
        refs, 7 if with_ctx else 6, n_riders, 2 if with_ctx else 1, n_riders, 2)
    q_ref, k_ref, v_ref, kc_ref, vc_ref, bias_ref = ins[:6]
    o_ref = outs[0]
    _cast_riders(rider_in, rider_out)
    d = NA_HEAD_DIM
    for hh in range(NA_HEADS_PER_STEP):
        cols = slice(hh * d, (hh + 1) * d)
        _fill_values_ones(v_ref.at[:, cols], vx_ref.at[hh])
        _fill_values_ones(vc_ref.at[:, cols], vxc_ref.at[hh])
        for block in range(NA_BLOCKS):
            rows = slice(block * NA_NQ, (block + 1) * NA_NQ)
            q = _scaled_q(q_ref[rows, cols], d)
            k0 = _na_band_start(block) * GRID_W
            s_loc = _dot_nt(q, k_ref[k0:k0 + NA_NK, cols]) + bias_ref[hh, _na_bias_kind(block)]
            s_ctx = _dot_nt(q, kc_ref[:, cols])
            o = _softmax_pv([s_loc, s_ctx], [vx_ref[hh, k0:k0 + NA_NK, :], vxc_ref[hh]])
            o_ref[rows, cols] = o.astype(BF16)
        if with_ctx:
            s = _dot_nt(_scaled_q(ins[6][:, cols], d), kc_ref[:, cols])
            outs[1][:, cols] = _softmax_pv([s], [vxc_ref[hh]]).astype(BF16)


def _na_attn(zl, zc, bias, layer, with_ctx, riders):
    hps = NA_HEADS_PER_STEP
    w = hps * NA_HEAD_DIM
    steps = NA_HEADS // hps
    col = lambda off: (lambda g, b: (b, off // w + g))
    in_specs = [
        pl.BlockSpec((SEQ, w), col(Z_NA_Q)), pl.BlockSpec((SEQ, w), col(Z_NA_K)), pl.BlockSpec((SEQ, w), col(Z_NA_V)),
        pl.BlockSpec((CTX_LEN, w), col(Z_NA_K)), pl.BlockSpec((CTX_LEN, w), col(Z_NA_V)),
        pl.BlockSpec((hps, 3, NA_NQ, NA_NK), lambda g, b: (layer * steps + g, 0, 0, 0)),
    ]
    args = [zl, zl, zl, zc, zc, bias]
    out_shape = [jax.ShapeDtypeStruct((BATCH * SEQ, NA_DIM), BF16)]
    out_specs = [pl.BlockSpec((SEQ, w), lambda g, b: (b, g))]
    if with_ctx:
        in_specs.append(pl.BlockSpec((CTX_LEN, w), col(Z_NA_Q)))
        args.append(zc)
        out_shape.append(jax.ShapeDtypeStruct((BATCH * CTX_LEN, NA_DIM), BF16))
        out_specs.append(pl.BlockSpec((CTX_LEN, w), lambda g, b: (b, g)))
    vmem = (2 * (5 * _nbytes((SEQ, w), BF16) + _nbytes((hps, 3, NA_NQ, NA_NK), F32))
            + 2 * hps * _nbytes((SEQ, 2 * LANES), BF16) + 8 * hps * _nbytes((NA_NQ, NA_NK + CTX_LEN), F32))
    n_base = len(out_shape)
    for wt, n_blocks, axis in riders:
        in_spec, out_spec, shape, block = _rider_specs(wt, layer, n_blocks, axis, (steps, BATCH))
        in_specs.append(in_spec)
        args.append(wt)
        out_specs.append(out_spec)
        out_shape.append(shape)
        vmem += 2 * (_nbytes(block, F32) + _nbytes(block, BF16))
    res = pl.pallas_call(
        functools.partial(_na_kernel, with_ctx=with_ctx, n_riders=len(riders)),
        out_shape=tuple(out_shape), grid=(steps, BATCH), in_specs=in_specs, out_specs=tuple(out_specs),
        scratch_shapes=[pltpu.VMEM((hps, SEQ, 2 * LANES), BF16), pltpu.VMEM((hps, CTX_LEN, 2 * LANES), BF16)],
        compiler_params=_params(("arbitrary", "arbitrary"), vmem),
        name="na_attn_ctx" if with_ctx else "na_attn",
    )(*args)
    return res[:n_base], res[n_base:]


def _out_proj_kernel(g_ref, a_ref, n_ref, x_ref, gb_ref, gc_ref, h_ref, gcp_ref, hp_ref, gcn_ref, hn_ref, cw_ref,
                     w_ref, lng_ref, lnb_ref, o_ref, mix_ref, *, seq_len):
    tm = o_ref.shape[0]
    u = gc_ref[...].astype(F32) * h_ref[...].astype(F32)
    u_before = gcp_ref[CONV_HALO - 1:, :].astype(F32) * hp_ref[CONV_HALO - 1:, :].astype(F32)
    u_after = gcn_ref[:1, :].astype(F32) * hn_ref[:1, :].astype(F32)
    row = lax.broadcasted_iota(jnp.int32, u.shape, 0)
    pos = (pl.program_id(0) * tm + row) % seq_len
    u_prev = jnp.where(row == 0, u_before, pltpu.roll(u, 1, 0))
    u_prev = jnp.where(pos == 0, 0.0, u_prev)
    u_next = jnp.where(row == tm - 1, u_after, pltpu.roll(u, tm - 1, 0))
    u_next = jnp.where(pos == seq_len - 1, 0.0, u_next)
    y = u_prev * cw_ref[0:1, :] + u * cw_ref[1:2, :] + u_next * cw_ref[2:3, :]
    a_w = MLA_HEADS * MLA_V
    mix_ref[:, :a_w] = a_ref[...]
    mix_ref[:, a_w:a_w + CONV_DIM] = (gb_ref[...].astype(F32) * y).astype(BF16)
    mix_ref[:, a_w + CONV_DIM:] = n_ref[...]

    for r0 in range(0, tm, TM_OUT_CHUNK):
        rows = slice(r0, r0 + TM_OUT_CHUNK)
        r = DEEPNORM_ALPHA * x_ref[rows, :] + g_ref[...] * _dot(mix_ref[rows, :], w_ref[...])
        o_ref[rows, :] = _layer_norm(r, lng_ref[...], lnb_ref[...])


def _out_proj(a, n, z, x, mods, conv_w, w_out, ln_g, ln_b, layer, row_of_tile, seq_len):
    rows = x.shape[0]
    tm = TM_OUT
    assert tm % seq_len == 0 or seq_len % tm == 0
    row = lambda i: (i, 0)
    per_layer = lambda i: (layer, 0, 0)
    conv_blk = lambda off: pl.BlockSpec((tm, CONV_DIM), lambda i: (i, off // CONV_DIM))
    halo = tm // CONV_HALO
    before = lambda off: pl.BlockSpec((CONV_HALO, CONV_DIM), lambda i: (jnp.maximum(i * halo - 1, 0), off // CONV_DIM))
    after = lambda off: pl.BlockSpec(
        (CONV_HALO, CONV_DIM), lambda i: (jnp.minimum((i + 1) * halo, rows // CONV_HALO - 1), off // CONV_DIM))
    vmem = (2 * (2 * _nbytes((tm, D_MODEL), F32) + _nbytes((tm, D_MODEL), BF16) + _nbytes((tm, CONV_DIM), BF16))
            + _nbytes((tm, D_MODEL), BF16) + _nbytes(w_out.shape, BF16)
            + 3 * _nbytes((TM_OUT_CHUNK, D_MODEL), F32) + 4 * _nbytes((tm, CONV_DIM), F32))
    return pl.pallas_call(
        functools.partial(_out_proj_kernel, seq_len=seq_len),
        out_shape=jax.ShapeDtypeStruct((rows, D_MODEL), F32),
        grid=(rows // tm,),
        in_specs=[
            _mod_spec(layer, 2, row_of_tile),
            pl.BlockSpec((tm, a.shape[1]), row), pl.BlockSpec((tm, n.shape[1]), row), pl.BlockSpec((tm, D_MODEL), row),
            conv_blk(Z_CONV_B), conv_blk(Z_CONV_C), conv_blk(Z_CONV_H),
            before(Z_CONV_C), before(Z_CONV_H), after(Z_CONV_C), after(Z_CONV_H),
            pl.BlockSpec((None,) + conv_w.shape[1:], per_layer),
            pl.BlockSpec(w_out.shape, lambda i: (0, 0), pipeline_mode=pl.Buffered(1)),
            pl.BlockSpec((None, 1, D_MODEL), per_layer), pl.BlockSpec((None, 1, D_MODEL), per_layer),
        ],
        out_specs=pl.BlockSpec((tm, D_MODEL), row),
        scratch_shapes=[pltpu.VMEM((tm, D_MODEL), BF16)],
        compiler_params=_params(("parallel",), vmem),
        name="out_proj",
    )(mods, a, n, x, z, z, z, z, z, z, z, conv_w, w_out, ln_g, ln_b)


def _ffn_kernel(sh_ref, sc_ref, g_ref, x_ref, wg_ref, wu_ref, wd_ref, lng_ref, lnb_ref, o_ref, xm_ref):
    f = pl.program_id(1)

    @pl.when(f == 0)
    def _():
        x = x_ref[...]
        xm_ref[...] = (x * (1.0 + sc_ref[...]) + sh_ref[...]).astype(BF16)
        o_ref[...] = DEEPNORM_ALPHA * x

    xm = xm_ref[...]
    gate = _dot(xm, wg_ref[...])
    up = _dot(xm, wu_ref[...])
    hidden = (gate * jax.nn.sigmoid(gate) * up).astype(BF16)
    o_ref[...] += g_ref[...] * _dot(hidden, wd_ref[...])

    @pl.when(f == pl.num_programs(1) - 1)
    def _():
        o_ref[...] = _layer_norm(o_ref[...], lng_ref[...], lnb_ref[...])


def _ffn(x, mods, w_gate, w_up, w_down, ln_g, ln_b, layer, row_of_tile):
    rows = x.shape[0]
    per_layer = lambda i, f: (layer, 0, 0)
    vmem = (4 * _nbytes((TM, D_MODEL), F32) + 2 * 3 * _nbytes((D_MODEL, TF), BF16)
            + _nbytes((TM, D_MODEL), BF16) + 3 * _nbytes((TM, TF), F32) + _nbytes((TM, D_MODEL), F32))
    return pl.pallas_call(
        _ffn_kernel,
        out_shape=jax.ShapeDtypeStruct((rows, D_MODEL), F32),
        grid=(rows // TM, D_FF // TF),
        in_specs=[
            _mod_spec(layer, 3, row_of_tile), _mod_spec(layer, 4, row_of_tile), _mod_spec(layer, 5, row_of_tile),
            pl.BlockSpec((TM, D_MODEL), lambda i, f: (i, 0)),
            pl.BlockSpec((D_MODEL, TF), lambda i, f: (0, f)),
            pl.BlockSpec((D_MODEL, TF), lambda i, f: (0, f)),
            pl.BlockSpec((TF, D_MODEL), lambda i, f: (f, 0)),
            pl.BlockSpec((None, 1, D_MODEL), per_layer), pl.BlockSpec((None, 1, D_MODEL), per_layer),
        ],
        out_specs=pl.BlockSpec((TM, D_MODEL), lambda i, f: (i, 0)),
        scratch_shapes=[pltpu.VMEM((TM, D_MODEL), BF16)],
        compiler_params=_params(("parallel", "arbitrary"), vmem),
        name="ffn",
    )(mods, mods, mods, x, w_gate, w_up, w_down, ln_g, ln_b)


def _rotate_half_cols(w):
    a1, a2, b1, b2 = jnp.split(w, 4, axis=-1)
    return jnp.concatenate([-a2, a1, -b2, b1], axis=-1)


def _w_in_prep_src(piece):
    rope = OFF_ROPE // PREP_PIECE
    tail = Z_ROPE // PREP_PIECE
    return jnp.where(piece < rope, piece, jnp.where(piece < tail, piece + 1, rope))


def _w_in_prep_kernel(*refs):
    *piece_refs, o_ref = refs
    pieces = [r[...] for r in piece_refs]
    n = len(pieces)
    is_last = pl.program_id(1) == pl.num_programs(1) - 1
    rot_piece = Z_DIM // PREP_PIECE - 1
    for p in range(n):
        piece = (Z_PAD // PREP_COLS - 1) * n + p
        if piece == rot_piece:
            a1, a2, b1, b2 = jnp.split(pieces[p], 4, axis=0)
            pieces[p] = jnp.where(is_last, jnp.concatenate([-a2, a1, -b2, b1], axis=0), pieces[p])
        elif piece > rot_piece:
            pieces[p] = jnp.where(is_last, 0.0, pieces[p])
    o_ref[...] = jnp.concatenate(pieces, axis=0).T.astype(BF16)


def _w_in_prep(w_in):
    w_t = jnp.swapaxes(w_in, 1, 2)
    n = PREP_COLS // PREP_PIECE
    piece_spec = lambda p: pl.BlockSpec((None, PREP_PIECE, D_MODEL), lambda l, g: (l, _w_in_prep_src(g * n + p), 0))
    vmem = 2 * (_nbytes((PREP_COLS, D_MODEL), F32) + _nbytes((D_MODEL, PREP_COLS), BF16)) + 3 * _nbytes(
        (PREP_COLS, D_MODEL), F32)
    return pl.pallas_call(
        _w_in_prep_kernel,
        out_shape=jax.ShapeDtypeStruct((DEPTH, D_MODEL, Z_PAD), BF16),
        grid=(DEPTH, Z_PAD // PREP_COLS),
        in_specs=[piece_spec(p) for p in range(n)],
        out_specs=pl.BlockSpec((None, D_MODEL, PREP_COLS), lambda l, g: (l, 0, g)),
        compiler_params=_params(("parallel", "parallel"), vmem),
        name="w_in_prep",
    )(*([w_t] * n))


def _relayout_wq(w):
    w = w.reshape(DEPTH, MLA_Q_RANK, MLA_HEADS, MLA_QK)
    rope = w[..., MLA_NOPE:]
    cols = jnp.concatenate([w, _rotate_half_cols(rope)], axis=-1)
    return cols.reshape(DEPTH, MLA_Q_RANK, MLA_HEADS * QK_PAD).astype(BF16)


def _rope_tables():
    t = jnp.arange(SEQ)
    row = (t // GRID_W).astype(F32)
    col = (t % GRID_W).astype(F32)
    n_freq = MLA_ROPE // 4
    inv = ROPE_THETA ** (-jnp.arange(n_freq, dtype=F32) / n_freq)
    ar = row[:, None] * inv
    ac = col[:, None] * inv
    ang = jnp.concatenate([ar, ar, ac, ac], axis=-1)
    lat = jnp.concatenate([jnp.cos(ang), jnp.sin(ang)], axis=1)
    ctx = jnp.concatenate([jnp.ones((TM_PROJ, MLA_ROPE), F32), jnp.zeros((TM_PROJ, MLA_ROPE), F32)], axis=1)
    return lat, ctx


def kernel(x, c, ctx, c_ctx, ada_w, ada_b, w_in, mla_q_norm, mla_wq_b, mla_kv_norm, mla_wkv_b, conv_w, na_rpb,
           w_out, ln1_g, ln1_b, ffn_w_gate, ffn_w_up, ffn_w_down, ln2_g, ln2_b):
    assert x.shape == (BATCH, SEQ, D_MODEL) and ctx.shape == (BATCH, CTX_LEN, D_MODEL)
    cc = jnp.concatenate([c, c_ctx[None, :], jnp.zeros((MOD_ROWS - BATCH - 1, D_MODEL), F32)], axis=0)
    mods = _ada(cc, ada_w, ada_b).reshape(DEPTH * MOD_ROWS * N_MOD, 1, D_MODEL)
    ta_lat, ta_ctx = _rope_tables()
    rows3d = lambda v: v.reshape(DEPTH, 1, -1)

    w_in_p = _w_in_prep(w_in)
    wq_all = _relayout_wq(mla_wq_b)
    wkv = mla_wkv_b.astype(BF16)
    g_q, g_kv = rows3d(mla_q_norm), rows3d(mla_kv_norm)
    bias = _na_bias(na_rpb)
    ln1 = (rows3d(ln1_g), rows3d(ln1_b))
    ln2 = (rows3d(ln2_g), rows3d(ln2_b))

    xl = x.reshape(BATCH * SEQ, D_MODEL)
    xc = ctx.reshape(BATCH * CTX_LEN, D_MODEL)
    lat_row = lambda tm: (lambda i: i // (SEQ // tm))
    ctx_row = lambda i: CTX_MOD_ROW
    for l in range(DEPTH):
        last = l == DEPTH - 1
        zl = _in_proj(xl, mods, w_in_p, l, lat_row(TM))
        zc = _in_proj(xc, mods, w_in_p, l, ctx_row)
        ql, kl, vl = _mla_proj(zl, g_q, g_kv, wq_all, wkv, ta_lat, l, lambda i: i % (SEQ // TM_PROJ))
        qc, kc, vc = _mla_proj(zc, g_q, g_kv, wq_all, wkv, ta_ctx, l, lambda i: 0)
        ffn_riders = [(ffn_w_gate, FFN_CAST_BLOCKS, 1), (ffn_w_up, FFN_CAST_BLOCKS, 1), (ffn_w_down, FFN_CAST_BLOCKS, 0)]
        a, (w_g, w_u, w_d) = _mla_attn(ql, kl, vl, qc, kc, vc, not last, l, ffn_riders)
        n, (w_o,) = _na_attn(zl, zc, bias, l, not last, [(w_out, OUT_CAST_BLOCKS, 0)])
        x1 = _out_proj(a[0], n[0], zl, xl, mods, conv_w, w_o, *ln1, l, lat_row(TM_OUT), SEQ)
        xl = _ffn(x1, mods, w_g, w_u, w_d, *ln2, l, lat_row(TM))
        if not last:
            x1c = _out_proj(a[1], n[1], zc, xc, mods, conv_w, w_o, *ln1, l, ctx_row, CTX_LEN)
            xc = _ffn(x1c, mods, w_g, w_u, w_d, *ln2, l, ctx_row)
    return xl.reshape(BATCH, SEQ, D_MODEL)
```

```python
import functools

import numpy as np
import jax
import jax.numpy as jnp
from jax import lax
from jax.experimental import pallas as pl
from jax.experimental.pallas import tpu as pltpu

D_MODEL = 2048
BATCH = 4
SEQ = 2048
DEPTH = 2
CTX_LEN = 256
GRID_W = 64
GRID_ROWS = SEQ // GRID_W
MLA_HEADS = 6
MLA_Q_RANK = 512
MLA_KV_RANK = 512
MLA_NOPE = 128
MLA_ROPE = 64
MLA_V = 128
MLA_QK = MLA_NOPE + MLA_ROPE
CONV_DIM = 512
NA_HEADS = 6
NA_HEAD_DIM = 128
NA_DIM = NA_HEADS * NA_HEAD_DIM
NA_WIN_R = 8
NA_WIN_C = 16
OFF_MLA_Q = 0
OFF_MLA_KV = OFF_MLA_Q + MLA_Q_RANK
OFF_ROPE = OFF_MLA_KV + MLA_KV_RANK
OFF_CONV_B = OFF_ROPE + MLA_ROPE
OFF_CONV_C = OFF_CONV_B + CONV_DIM
OFF_CONV_H = OFF_CONV_C + CONV_DIM
OFF_NA_Q = OFF_CONV_H + CONV_DIM
OFF_NA_K = OFF_NA_Q + NA_DIM
OFF_NA_V = OFF_NA_K + NA_DIM
IN_DIM = OFF_NA_V + NA_DIM
D_FF = -(-8 * D_MODEL // (3 * 256)) * 256
ROPE_THETA = 10000.0
LN_EPS = 1e-6
RMS_EPS = 1e-6
DEEPNORM_ALPHA = (2 * DEPTH) ** 0.25
LOG2E = 1.4426950408889634
N_MOD = 6
MOD_ROWS = 8
CTX_MOD_ROW = BATCH

LANES = 128
V7X_VMEM_BYTES = 64 * 1024 * 1024
V7X_VMEM_BUDGET = 62 * 1024 * 1024

Z_Q = 0
Z_KV = Z_Q + MLA_Q_RANK
Z_CONV_B = Z_KV + MLA_KV_RANK
Z_CONV_C = Z_CONV_B + CONV_DIM
Z_CONV_H = Z_CONV_C + CONV_DIM
Z_NA_Q = Z_CONV_H + CONV_DIM
Z_NA_K = Z_NA_Q + NA_DIM
Z_NA_V = Z_NA_K + NA_DIM
Z_ROPE = Z_NA_V + NA_DIM
Z_DIM = Z_ROPE + 2 * MLA_ROPE
V7X_MXU_COLS = 256
Z_PAD = -(-Z_DIM // (5 * V7X_MXU_COLS)) * (5 * V7X_MXU_COLS)

QK_PAD = 2 * LANES

TM = 1024
TM_OUT = 512
TM_OUT_CHUNK = 256
CONV_HALO = 16
TM_PROJ = 512
PREP_PIECE = MLA_ROPE
PREP_COLS = 5 * LANES
TN_IN = 5 * V7X_MXU_COLS
TN_ADA = 1024
TF = 512
TQ = 256
Q_TILE_UNROLL = 8
NA_RQ = 4
NA_RK = NA_RQ + NA_WIN_R - 1
NA_NQ = NA_RQ * GRID_W
NA_NK = NA_RK * GRID_W
NA_BLOCKS = GRID_ROWS // NA_RQ
FFN_CAST_BLOCKS = 11
MLA_HEADS_PER_STEP = 2
OUT_CAST_BLOCKS = 8
NA_HEADS_PER_STEP = 2

F32 = jnp.float32
BF16 = jnp.bfloat16


def _params(semantics, vmem_bytes):
    assert vmem_bytes <= V7X_VMEM_BUDGET, vmem_bytes
    return pltpu.CompilerParams(dimension_semantics=semantics, vmem_limit_bytes=int(vmem_bytes))


def _nbytes(shape, dtype):
    return int(np.prod(shape)) * jnp.dtype(dtype).itemsize


def _dot(a, b):
    return jnp.dot(a, b, preferred_element_type=F32)


def _dot_nt(a, b):
    return lax.dot_general(a, b, (((1,), (1,)), ((), ())), preferred_element_type=F32)


def _mod_spec(layer, chunk, row_of_tile):
    base = layer * MOD_ROWS * N_MOD + chunk
    return pl.BlockSpec((None, 1, D_MODEL), lambda i, *_: (base + row_of_tile(i) * N_MOD, 0, 0))


def _layer_norm(r, g, b):
    mu = jnp.mean(r, axis=-1, keepdims=True)
    c = r - mu
    var = jnp.mean(c * c, axis=-1, keepdims=True)
    return c * lax.rsqrt(var + LN_EPS) * g + b


def _ada_kernel(cc_ref, w_ref, b_ref, o_ref):
    cc = cc_ref[...]
    s = (cc * jax.nn.sigmoid(cc)).astype(BF16)
    o_ref[...] = _dot(s, w_ref[...].astype(BF16)) + b_ref[...]


def _ada(cc, ada_w, ada_b):
    n = N_MOD * D_MODEL
    vmem = 2 * (_nbytes((D_MODEL, TN_ADA), F32) + _nbytes((MOD_ROWS, TN_ADA), F32) * 2
                + _nbytes((MOD_ROWS, D_MODEL), F32)) + _nbytes((D_MODEL, TN_ADA), BF16) * 2
    return pl.pallas_call(
        _ada_kernel,
        out_shape=jax.ShapeDtypeStruct((DEPTH, MOD_ROWS, n), F32),
        grid=(DEPTH, n // TN_ADA),
        in_specs=[
            pl.BlockSpec((MOD_ROWS, D_MODEL), lambda l, j: (0, 0)),
            pl.BlockSpec((None, D_MODEL, TN_ADA), lambda l, j: (l, 0, j)),
            pl.BlockSpec((None, 1, TN_ADA), lambda l, j: (l, 0, j)),
        ],
        out_specs=pl.BlockSpec((None, MOD_ROWS, TN_ADA), lambda l, j: (l, 0, j)),
        compiler_params=_params(("parallel", "parallel"), vmem),
        name="ada_mod",
    )(cc, ada_w, ada_b.reshape(DEPTH, 1, n))


def _in_proj_kernel(sh_ref, sc_ref, x_ref, w_ref, o_ref, xm_ref):
    @pl.when(pl.program_id(1) == 0)
    def _():
        xm_ref[...] = (x_ref[...] * (1.0 + sc_ref[...]) + sh_ref[...]).astype(BF16)

    o_ref[...] = _dot(xm_ref[...], w_ref[...]).astype(BF16)


def _in_proj(x, mods, w_in_p, layer, row_of_tile):
    rows = x.shape[0]
    vmem = (2 * (_nbytes((TM, D_MODEL), F32) + _nbytes((D_MODEL, TN_IN), BF16) + _nbytes((TM, TN_IN), BF16))
            + _nbytes((TM, D_MODEL), BF16) * 2 + _nbytes((TM, TN_IN), F32))
    return pl.pallas_call(
        _in_proj_kernel,
        out_shape=jax.ShapeDtypeStruct((rows, Z_PAD), BF16),
        grid=(rows // TM, Z_PAD // TN_IN),
        in_specs=[
            _mod_spec(layer, 0, row_of_tile),
            _mod_spec(layer, 1, row_of_tile),
            pl.BlockSpec((TM, D_MODEL), lambda i, j: (i, 0)),
            pl.BlockSpec((None, D_MODEL, TN_IN), lambda i, j: (layer, 0, j)),
        ],
        out_specs=pl.BlockSpec((TM, TN_IN), lambda i, j: (i, j)),
        scratch_shapes=[pltpu.VMEM((TM, D_MODEL), BF16)],
        compiler_params=_params(("parallel", "arbitrary"), vmem),
        name="in_proj",
    )(mods, mods, x, w_in_p)


def _rms_norm(x, g):
    return x * lax.rsqrt(jnp.mean(x * x, axis=-1, keepdims=True) + RMS_EPS) * g


def _rope_sum(pair, table, lane):
    y = pair * table
    return jnp.where(lane < MLA_ROPE, y + pltpu.roll(y, MLA_ROPE, 1), 0.0)


def _mla_proj_kernel(z_ref, slot_ref, gq_ref, gkv_ref, wq_ref, wkv_ref, ta_ref, q_ref, k_ref, v_ref):
    cq = z_ref[:, Z_Q:Z_Q + MLA_Q_RANK].astype(F32)
    ckv = z_ref[:, Z_KV:Z_KV + MLA_KV_RANK].astype(F32)
    qa = _dot(_rms_norm(cq, gq_ref[...]).astype(BF16), wq_ref[...]) * (MLA_QK ** -0.5 * LOG2E)
    kv = _dot(_rms_norm(ckv, gkv_ref[...]).astype(BF16), wkv_ref[...])
    ta = ta_ref[...]
    lane = lax.broadcasted_iota(jnp.int32, ta.shape, 1)
    kro = _rope_sum(slot_ref[...].astype(F32), ta, lane).astype(BF16)
    for h in range(MLA_HEADS):
        ob = h * QK_PAD
        q_ref[:, ob:ob + LANES] = qa[:, ob:ob + LANES].astype(BF16)
        q_ref[:, ob + LANES:ob + QK_PAD] = _rope_sum(qa[:, ob + LANES:ob + QK_PAD], ta, lane).astype(BF16)
        k_ref[:, ob:ob + LANES] = kv[:, ob:ob + LANES].astype(BF16)
        k_ref[:, ob + LANES:ob + QK_PAD] = kro
        v_ref[:, h * MLA_V:(h + 1) * MLA_V] = kv[:, ob + LANES:ob + QK_PAD].astype(BF16)


def _mla_proj(z, g_q, g_kv, wq_all, wkv, ta, layer, table_tile):
    rows = z.shape[0]
    per_layer = lambda i: (layer, 0, 0)
    zw = Z_KV + MLA_KV_RANK
    hq = MLA_HEADS * QK_PAD
    hv = MLA_HEADS * MLA_V
    tm = TM_PROJ
    vmem = (2 * (_nbytes((tm, zw), BF16) + _nbytes((tm, LANES), BF16) + _nbytes((tm, LANES), F32)
                 + _nbytes(wq_all.shape[1:], BF16) + _nbytes(wkv.shape[1:], BF16)
                 + 2 * _nbytes((tm, hq), BF16) + _nbytes((tm, hv), BF16))
            + 2 * _nbytes((tm, hq), F32) + 4 * _nbytes((tm, zw), F32))
    return pl.pallas_call(
        _mla_proj_kernel,
        out_shape=(jax.ShapeDtypeStruct((rows, hq), BF16), jax.ShapeDtypeStruct((rows, hq), BF16),
                   jax.ShapeDtypeStruct((rows, hv), BF16)),
        grid=(rows // tm,),
        in_specs=[
            pl.BlockSpec((tm, zw), lambda i: (i, 0)),
            pl.BlockSpec((tm, 2 * MLA_ROPE), lambda i: (i, Z_ROPE // (2 * MLA_ROPE))),
            pl.BlockSpec((None, 1, MLA_Q_RANK), per_layer),
            pl.BlockSpec((None, 1, MLA_KV_RANK), per_layer),
            pl.BlockSpec((None,) + wq_all.shape[1:], per_layer),
            pl.BlockSpec((None,) + wkv.shape[1:], per_layer),
            pl.BlockSpec((tm, LANES), lambda i: (table_tile(i), 0)),
        ],
        out_specs=(pl.BlockSpec((tm, hq), lambda i: (i, 0)), pl.BlockSpec((tm, hq), lambda i: (i, 0)),
                   pl.BlockSpec((tm, hv), lambda i: (i, 0))),
        compiler_params=_params(("parallel",), vmem),
        name="mla_proj",
    )(z, z, g_q, g_kv, wq_all, wkv, ta)


def _rider_specs(w, layer, n_blocks, axis, grid):
    rows, cols = w.shape[1:]
    block = (rows // n_blocks, cols) if axis == 0 else (rows, cols // n_blocks)
    assert block[0] * block[1] * n_blocks == rows * cols and n_blocks <= grid[0] * grid[1]

    def at(outer, inner):
        step = jnp.minimum(outer * grid[1] + inner, n_blocks - 1)
        return (step, 0) if axis == 0 else (0, step)

    in_spec = pl.BlockSpec((None,) + block, lambda outer, inner: (layer,) + at(outer, inner))
    return in_spec, pl.BlockSpec(block, at), jax.ShapeDtypeStruct((rows, cols), BF16), block


def _cast_riders(in_refs, out_refs):
    for i_ref, o_ref in zip(in_refs, out_refs):
        o_ref[...] = i_ref[...].astype(BF16)


def _split_refs(refs, *counts):
    out, pos = [], 0
    for c in counts:
        out.append(refs[pos:pos + c])
        pos += c
    assert pos == len(refs)
    return out


def _softmax_pv(scores, values_ones):
    m = functools.reduce(jnp.maximum, [jnp.max(s, axis=-1, keepdims=True) for s in scores])
    acc = functools.reduce(jnp.add, [_dot(jnp.exp2(s - m).astype(BF16), v) for s, v in zip(scores, values_ones)])
    return acc[:, :LANES] / acc[:, LANES:]


def _fill_values_ones(v_ref, vx_ref):
    vx_ref[:, :LANES] = v_ref[...]
    vx_ref[:, LANES:] = jnp.ones((vx_ref.shape[0], LANES), BF16)


def _scaled_q(q, head_dim):
    return (q.astype(F32) * (head_dim ** -0.5 * LOG2E)).astype(BF16)


def _mla_attn_kernel(*refs, with_ctx, n_riders):
    ins, rider_in, outs, rider_out, (vxl_ref, vxc_ref) = _split_refs(
        refs, 6 if with_ctx else 5, n_riders, 2 if with_ctx else 1, n_riders, 2)
    q_ref, kl_ref, vl_ref, kc_ref, vc_ref = ins[:5]
    o_ref = outs[0]
    _cast_riders(rider_in, rider_out)
    for hh in range(MLA_HEADS_PER_STEP):
        qk = slice(hh * QK_PAD, (hh + 1) * QK_PAD)
        vcols = slice(hh * MLA_V, (hh + 1) * MLA_V)
        _fill_values_ones(vl_ref.at[:, vcols], vxl_ref.at[hh])
        _fill_values_ones(vc_ref.at[:, vcols], vxc_ref.at[hh])

        def q_tile(t, carry, hh=hh, qk=qk, vcols=vcols):
            r0 = pl.multiple_of(t * TQ, TQ)
            q = q_ref[pl.ds(r0, TQ), qk]
            s_lat = _dot_nt(q, kl_ref[:, qk])
            s_ctx = _dot_nt(q, kc_ref[:, qk])
            o = _softmax_pv([s_lat, s_ctx], [vxl_ref[hh], vxc_ref[hh]])
            o_ref[pl.ds(r0, TQ), vcols] = o.astype(BF16)
            return carry

        lax.fori_loop(0, SEQ // TQ, q_tile, 0, unroll=Q_TILE_UNROLL)
        if with_ctx:
            s = _dot_nt(ins[5][:, qk], kc_ref[:, qk])
            outs[1][:, vcols] = _softmax_pv([s], [vxc_ref[hh]]).astype(BF16)


def _mla_attn(ql, kl, vl, qc, kc, vc, with_ctx, layer, riders):
    hps = MLA_HEADS_PER_STEP
    grid = (BATCH, MLA_HEADS // hps)
    head = lambda b, g: (b, g)
    wqk, wv = hps * QK_PAD, hps * MLA_V
    in_specs = [
        pl.BlockSpec((SEQ, wqk), head), pl.BlockSpec((SEQ, wqk), head), pl.BlockSpec((SEQ, wv), head),
        pl.BlockSpec((CTX_LEN, wqk), head), pl.BlockSpec((CTX_LEN, wv), head),
    ]
    args = [ql, kl, vl, kc, vc]
    out_shape = [jax.ShapeDtypeStruct((BATCH * SEQ, MLA_HEADS * MLA_V), BF16)]
    out_specs = [pl.BlockSpec((SEQ, wv), head)]
    if with_ctx:
        in_specs.append(pl.BlockSpec((CTX_LEN, wqk), head))
        args.append(qc)
        out_shape.append(jax.ShapeDtypeStruct((BATCH * CTX_LEN, MLA_HEADS * MLA_V), BF16))
        out_specs.append(pl.BlockSpec((CTX_LEN, wv), head))
    vmem = (2 * (3 * _nbytes((SEQ, wqk), BF16) + 2 * _nbytes((SEQ, wv), BF16))
            + 2 * hps * _nbytes((SEQ, 2 * LANES), BF16) + hps * Q_TILE_UNROLL * _nbytes((TQ, SEQ + CTX_LEN), BF16))
    n_base = len(out_shape)
    for w, n_blocks, axis in riders:
        in_spec, out_spec, shape, block = _rider_specs(w, layer, n_blocks, axis, grid)
        in_specs.append(in_spec)
        args.append(w)
        out_specs.append(out_spec)
        out_shape.append(shape)
        vmem += 2 * (_nbytes(block, F32) + _nbytes(block, BF16))
    res = pl.pallas_call(
        functools.partial(_mla_attn_kernel, with_ctx=with_ctx, n_riders=len(riders)),
        out_shape=tuple(out_shape), grid=grid, in_specs=in_specs, out_specs=tuple(out_specs),
        scratch_shapes=[pltpu.VMEM((hps, SEQ, 2 * LANES), BF16), pltpu.VMEM((hps, CTX_LEN, 2 * LANES), BF16)],
        compiler_params=_params(("arbitrary", "arbitrary"), vmem),
        name="mla_attn_ctx" if with_ctx else "mla_attn",
    )(*args)
    return res[:n_base], res[n_base:]


def _na_band_start(block):
    return min(max(block * NA_RQ - NA_WIN_R // 2, 0), GRID_ROWS - NA_RK)


def _na_bias_kind(block):
    return 0 if block == 0 else (2 if block == NA_BLOCKS - 1 else 1)


def _na_row_offsets():
    n_dr = 2 * NA_WIN_R - 1
    idx = np.full((3, NA_RQ, NA_RK), n_dr, np.int32)
    for kind, block in ((0, 0), (1, 2), (2, NA_BLOCKS - 1)):
        for qr in range(NA_RQ):
            r = block * NA_RQ + qr
            r0 = min(max(r - NA_WIN_R // 2, 0), GRID_ROWS - NA_WIN_R)
            for ki in range(NA_RK):
                kr = _na_band_start(block) + ki
                if r0 <= kr < r0 + NA_WIN_R:
                    idx[kind, qr, ki] = kr - r + NA_WIN_R - 1
    return idx


def _na_bias_kernel(w_ref, o_ref):
    shape = (GRID_W, LANES)
    c = lax.broadcasted_iota(jnp.int32, shape, 0)
    kc = lax.broadcasted_iota(jnp.int32, shape, 1)
    c0 = jnp.clip(c - NA_WIN_C // 2, 0, GRID_W - NA_WIN_C)
    col_ok = (kc >= c0) & (kc < c0 + NA_WIN_C)
    neg = jnp.full((GRID_W, GRID_W), -jnp.inf, F32)
    offsets = _na_row_offsets()
    n_dr = 2 * NA_WIN_R - 1
    tiles = {n_dr: neg}
    for d in sorted(set(int(v) for v in offsets.reshape(-1)) - {n_dr}):
        row = jnp.broadcast_to(w_ref[d:d + 1, :], shape)
        skew = pltpu.roll(row, LANES - (GRID_W - 1), 1, stride=1, stride_axis=0)
        tiles[d] = jnp.where(col_ok, skew, -jnp.inf)[:, :GRID_W]
    for kind in range(3):
        for qr in range(NA_RQ):
            for ki in range(NA_RK):
                o_ref[kind, qr * GRID_W:(qr + 1) * GRID_W, ki * GRID_W:(ki + 1) * GRID_W] = (
                    tiles[int(offsets[kind, qr, ki])])


def _na_bias(rpb):
    n_dr, n_dc = 2 * NA_WIN_R - 1, 2 * NA_WIN_C - 1
    lo = GRID_W - 1 - (NA_WIN_C - 1)
    w = jnp.pad(rpb.reshape(DEPTH * NA_HEADS, n_dr, n_dc) * LOG2E, ((0, 0), (0, 0), (lo, LANES - lo - n_dc)),
                constant_values=-jnp.inf)
    vmem = 2 * (_nbytes((2 * 8, LANES), F32) + _nbytes((3, NA_NQ, NA_NK + GRID_W), F32)) + 16 * _nbytes(
        (GRID_W, LANES), F32)
    return pl.pallas_call(
        _na_bias_kernel,
        out_shape=jax.ShapeDtypeStruct((w.shape[0], 3, NA_NQ, NA_NK), F32),
        grid=(w.shape[0],),
        in_specs=[pl.BlockSpec((None, n_dr, LANES), lambda g: (g, 0, 0))],
        out_specs=pl.BlockSpec((None, 3, NA_NQ, NA_NK), lambda g: (g, 0, 0, 0)),
        compiler_params=_params(("parallel",), vmem),
        name="na_bias",
    )(w)


def _na_kernel(*refs, with_ctx, n_riders):
    ins, rider_in, outs, rider_out, (vx_ref, vxc_ref) = _split_refs(
        refs, 7 if with_ctx else 6, n_riders, 2 if with_ctx else 1, n_riders, 2)
    q_ref, k_ref, v_ref, kc_ref, vc_ref, bias_ref = ins[:6]
    o_ref = outs[0]
    _cast_riders(rider_in, rider_out)
    d = NA_HEAD_DIM
    for hh in range(NA_HEADS_PER_STEP):
        cols = slice(hh * d, (hh + 1) * d)
        _fill_values_ones(v_ref.at[:, cols], vx_ref.at[hh])
        _fill_values_ones(vc_ref.at[:, cols], vxc_ref.at[hh])
        for block in range(NA_BLOCKS):
            rows = slice(block * NA_NQ, (block + 1) * NA_NQ)
            q = _scaled_q(q_ref[rows, cols], d)
            k0 = _na_band_start(block) * GRID_W
            s_loc = _dot_nt(q, k_ref[k0:k0 + NA_NK, cols]) + bias_ref[hh, _na_bias_kind(block)]
            s_ctx = _dot_nt(q, kc_ref[:, cols])
            o = _softmax_pv([s_loc, s_ctx], [vx_ref[hh, k0:k0 + NA_NK, :], vxc_ref[hh]])
            o_ref[rows, cols] = o.astype(BF16)
        if with_ctx:
            s = _dot_nt(_scaled_q(ins[6][:, cols], d), kc_ref[:, cols])
            outs[1][:, cols] = _softmax_pv([s], [vxc_ref[hh]]).astype(BF16)


def _na_attn(zl, zc, bias, layer, with_ctx, riders):
    hps = NA_HEADS_PER_STEP
    w = hps * NA_HEAD_DIM
    steps = NA_HEADS // hps
    col = lambda off: (lambda g, b: (b, off // w + g))
    in_specs = [
        pl.BlockSpec((SEQ, w), col(Z_NA_Q)), pl.BlockSpec((SEQ, w), col(Z_NA_K)), pl.BlockSpec((SEQ, w), col(Z_NA_V)),
        pl.BlockSpec((CTX_LEN, w), col(Z_NA_K)), pl.BlockSpec((CTX_LEN, w), col(Z_NA_V)),
        pl.BlockSpec((hps, 3, NA_NQ, NA_NK), lambda g, b: (layer * steps + g, 0, 0, 0)),
    ]
    args = [zl, zl, zl, zc, zc, bias]
    out_shape = [jax.ShapeDtypeStruct((BATCH * SEQ, NA_DIM), BF16)]
    out_specs = [pl.BlockSpec((SEQ, w), lambda g, b: (b, g))]
    if with_ctx:
        in_specs.append(pl.BlockSpec((CTX_LEN, w), col(Z_NA_Q)))
        args.append(zc)
        out_shape.append(jax.ShapeDtypeStruct((BATCH * CTX_LEN, NA_DIM), BF16))
        out_specs.append(pl.BlockSpec((CTX_LEN, w), lambda g, b: (b, g)))
    vmem = (2 * (5 * _nbytes((SEQ, w), BF16) + _nbytes((hps, 3, NA_NQ, NA_NK), F32))
            + 2 * hps * _nbytes((SEQ, 2 * LANES), BF16) + 4 * hps * _nbytes((NA_NQ, NA_NK + CTX_LEN), F32))
    n_base = len(out_shape)
    for wt, n_blocks, axis in riders:
        in_spec, out_spec, shape, block = _rider_specs(wt, layer, n_blocks, axis, (steps, BATCH))
        in_specs.append(in_spec)
        args.append(wt)
        out_specs.append(out_spec)
        out_shape.append(shape)
        vmem += 2 * (_nbytes(block, F32) + _nbytes(block, BF16))
    res = pl.pallas_call(
        functools.partial(_na_kernel, with_ctx=with_ctx, n_riders=len(riders)),
        out_shape=tuple(out_shape), grid=(steps, BATCH), in_specs=in_specs, out_specs=tuple(out_specs),
        scratch_shapes=[pltpu.VMEM((hps, SEQ, 2 * LANES), BF16), pltpu.VMEM((hps, CTX_LEN, 2 * LANES), BF16)],
        compiler_params=_params(("arbitrary", "arbitrary"), vmem),
        name="na_attn_ctx" if with_ctx else "na_attn",
    )(*args)
    return res[:n_base], res[n_base:]


def _out_proj_kernel(g_ref, a_ref, n_ref, x_ref, gb_ref, gc_ref, h_ref, gcp_ref, hp_ref, gcn_ref, hn_ref, cw_ref,
                     w_ref, lng_ref, lnb_ref, o_ref, mix_ref, *, seq_len):
    tm = o_ref.shape[0]
    u = gc_ref[...].astype(F32) * h_ref[...].astype(F32)
    u_before = gcp_ref[CONV_HALO - 1:, :].astype(F32) * hp_ref[CONV_HALO - 1:, :].astype(F32)
    u_after = gcn_ref[:1, :].astype(F32) * hn_ref[:1, :].astype(F32)
    row = lax.broadcasted_iota(jnp.int32, u.shape, 0)
    pos = (pl.program_id(0) * tm + row) % seq_len
    u_prev = jnp.where(row == 0, u_before, pltpu.roll(u, 1, 0))
    u_prev = jnp.where(pos == 0, 0.0, u_prev)
    u_next = jnp.where(row == tm - 1, u_after, pltpu.roll(u, tm - 1, 0))
    u_next = jnp.where(pos == seq_len - 1, 0.0, u_next)
    y = u_prev * cw_ref[0:1, :] + u * cw_ref[1:2, :] + u_next * cw_ref[2:3, :]
    a_w = MLA_HEADS * MLA_V
    mix_ref[:, :a_w] = a_ref[...]
    mix_ref[:, a_w:a_w + CONV_DIM] = (gb_ref[...].astype(F32) * y).astype(BF16)
    mix_ref[:, a_w + CONV_DIM:] = n_ref[...]

    for r0 in range(0, tm, TM_OUT_CHUNK):
        rows = slice(r0, r0 + TM_OUT_CHUNK)
        r = DEEPNORM_ALPHA * x_ref[rows, :] + g_ref[...] * _dot(mix_ref[rows, :], w_ref[...])
        o_ref[rows, :] = _layer_norm(r, lng_ref[...], lnb_ref[...])


def _out_proj(a, n, z, x, mods, conv_w, w_out, ln_g, ln_b, layer, row_of_tile, seq_len):
    rows = x.shape[0]
    tm = TM_OUT
    assert tm % seq_len == 0 or seq_len % tm == 0
    row = lambda i: (i, 0)
    per_layer = lambda i: (layer, 0, 0)
    conv_blk = lambda off: pl.BlockSpec((tm, CONV_DIM), lambda i: (i, off // CONV_DIM))
    halo = tm // CONV_HALO
    before = lambda off: pl.BlockSpec((CONV_HALO, CONV_DIM), lambda i: (jnp.maximum(i * halo - 1, 0), off // CONV_DIM))
    after = lambda off: pl.BlockSpec(
        (CONV_HALO, CONV_DIM), lambda i: (jnp.minimum((i + 1) * halo, rows // CONV_HALO - 1), off // CONV_DIM))
    vmem = (2 * (2 * _nbytes((tm, D_MODEL), F32) + _nbytes((tm, D_MODEL), BF16) + _nbytes((tm, CONV_DIM), BF16))
            + _nbytes((tm, D_MODEL), BF16) + _nbytes(w_out.shape, BF16)
            + 3 * _nbytes((TM_OUT_CHUNK, D_MODEL), F32) + 4 * _nbytes((tm, CONV_DIM), F32))
    return pl.pallas_call(
        functools.partial(_out_proj_kernel, seq_len=seq_len),
        out_shape=jax.ShapeDtypeStruct((rows, D_MODEL), F32),
        grid=(rows // tm,),
        in_specs=[
            _mod_spec(layer, 2, row_of_tile),
            pl.BlockSpec((tm, a.shape[1]), row), pl.BlockSpec((tm, n.shape[1]), row), pl.BlockSpec((tm, D_MODEL), row),
            conv_blk(Z_CONV_B), conv_blk(Z_CONV_C), conv_blk(Z_CONV_H),
            before(Z_CONV_C), before(Z_CONV_H), after(Z_CONV_C), after(Z_CONV_H),
            pl.BlockSpec((None,) + conv_w.shape[1:], per_layer),
            pl.BlockSpec(w_out.shape, lambda i: (0, 0), pipeline_mode=pl.Buffered(1)),
            pl.BlockSpec((None, 1, D_MODEL), per_layer), pl.BlockSpec((None, 1, D_MODEL), per_layer),
        ],
        out_specs=pl.BlockSpec((tm, D_MODEL), row),
        scratch_shapes=[pltpu.VMEM((tm, D_MODEL), BF16)],
        compiler_params=_params(("parallel",), vmem),
        name="out_proj",
    )(mods, a, n, x, z, z, z, z, z, z, z, conv_w, w_out, ln_g, ln_b)


def _ffn_kernel(sh_ref, sc_ref, g_ref, x_ref, wg_ref, wu_ref, wd_ref, lng_ref, lnb_ref, o_ref, xm_ref):
    f = pl.program_id(1)

    @pl.when(f == 0)
    def _():
        x = x_ref[...]
        xm_ref[...] = (x * (1.0 + sc_ref[...]) + sh_ref[...]).astype(BF16)
        o_ref[...] = DEEPNORM_ALPHA * x

    xm = xm_ref[...]
    gate = _dot(xm, wg_ref[...])
    up = _dot(xm, wu_ref[...])
    hidden = (gate * jax.nn.sigmoid(gate) * up).astype(BF16)
    o_ref[...] += g_ref[...] * _dot(hidden, wd_ref[...])

    @pl.when(f == pl.num_programs(1) - 1)
    def _():
        o_ref[...] = _layer_norm(o_ref[...], lng_ref[...], lnb_ref[...])


def _ffn(x, mods, w_gate, w_up, w_down, ln_g, ln_b, layer, row_of_tile):
    rows = x.shape[0]
    per_layer = lambda i, f: (layer, 0, 0)
    vmem = (4 * _nbytes((TM, D_MODEL), F32) + 2 * 3 * _nbytes((D_MODEL, TF), BF16)
            + _nbytes((TM, D_MODEL), BF16) + 3 * _nbytes((TM, TF), F32) + _nbytes((TM, D_MODEL), F32))
    return pl.pallas_call(
        _ffn_kernel,
        out_shape=jax.ShapeDtypeStruct((rows, D_MODEL), F32),
        grid=(rows // TM, D_FF // TF),
        in_specs=[
            _mod_spec(layer, 3, row_of_tile), _mod_spec(layer, 4, row_of_tile), _mod_spec(layer, 5, row_of_tile),
            pl.BlockSpec((TM, D_MODEL), lambda i, f: (i, 0)),
            pl.BlockSpec((D_MODEL, TF), lambda i, f: (0, f)),
            pl.BlockSpec((D_MODEL, TF), lambda i, f: (0, f)),
            pl.BlockSpec((TF, D_MODEL), lambda i, f: (f, 0)),
            pl.BlockSpec((None, 1, D_MODEL), per_layer), pl.BlockSpec((None, 1, D_MODEL), per_layer),
        ],
        out_specs=pl.BlockSpec((TM, D_MODEL), lambda i, f: (i, 0)),
        scratch_shapes=[pltpu.VMEM((TM, D_MODEL), BF16)],
        compiler_params=_params(("parallel", "arbitrary"), vmem),
        name="ffn",
    )(mods, mods, mods, x, w_gate, w_up, w_down, ln_g, ln_b)


def _rotate_half_cols(w):
    a1, a2, b1, b2 = jnp.split(w, 4, axis=-1)
    return jnp.concatenate([-a2, a1, -b2, b1], axis=-1)


def _w_in_prep_src(piece):
    rope = OFF_ROPE // PREP_PIECE
    tail = Z_ROPE // PREP_PIECE
    return jnp.where(piece < rope, piece, jnp.where(piece < tail, piece + 1, rope))


def _w_in_prep_kernel(*refs):
    *piece_refs, o_ref = refs
    pieces = [r[...] for r in piece_refs]
    n = len(pieces)
    is_last = pl.program_id(1) == pl.num_programs(1) - 1
    rot_piece = Z_DIM // PREP_PIECE - 1
    for p in range(n):
        piece = (Z_PAD // PREP_COLS - 1) * n + p
        if piece == rot_piece:
            a1, a2, b1, b2 = jnp.split(pieces[p], 4, axis=0)
            pieces[p] = jnp.where(is_last, jnp.concatenate([-a2, a1, -b2, b1], axis=0), pieces[p])
        elif piece > rot_piece:
            pieces[p] = jnp.where(is_last, 0.0, pieces[p])
    o_ref[...] = jnp.concatenate(pieces, axis=0).T.astype(BF16)


def _w_in_prep(w_in):
    w_t = jnp.swapaxes(w_in, 1, 2)
    n = PREP_COLS // PREP_PIECE
    piece_spec = lambda p: pl.BlockSpec((None, PREP_PIECE, D_MODEL), lambda l, g: (l, _w_in_prep_src(g * n + p), 0))
    vmem = 2 * (_nbytes((PREP_COLS, D_MODEL), F32) + _nbytes((D_MODEL, PREP_COLS), BF16)) + 3 * _nbytes(
        (PREP_COLS, D_MODEL), F32)
    return pl.pallas_call(
        _w_in_prep_kernel,
        out_shape=jax.ShapeDtypeStruct((DEPTH, D_MODEL, Z_PAD), BF16),
        grid=(DEPTH, Z_PAD // PREP_COLS),
        in_specs=[piece_spec(p) for p in range(n)],
        out_specs=pl.BlockSpec((None, D_MODEL, PREP_COLS), lambda l, g: (l, 0, g)),
        compiler_params=_params(("parallel", "parallel"), vmem),
        name="w_in_prep",
    )(*([w_t] * n))


def _relayout_wq(w):
    w = w.reshape(DEPTH, MLA_Q_RANK, MLA_HEADS, MLA_QK)
    rope = w[..., MLA_NOPE:]
    cols = jnp.concatenate([w, _rotate_half_cols(rope)], axis=-1)
    return cols.reshape(DEPTH, MLA_Q_RANK, MLA_HEADS * QK_PAD).astype(BF16)


def _rope_tables():
    t = jnp.arange(SEQ)
    row = (t // GRID_W).astype(F32)
    col = (t % GRID_W).astype(F32)
    n_freq = MLA_ROPE // 4
    inv = ROPE_THETA ** (-jnp.arange(n_freq, dtype=F32) / n_freq)
    ar = row[:, None] * inv
    ac = col[:, None] * inv
    ang = jnp.concatenate([ar, ar, ac, ac], axis=-1)
    lat = jnp.concatenate([jnp.cos(ang), jnp.sin(ang)], axis=1)
    ctx = jnp.concatenate([jnp.ones((TM_PROJ, MLA_ROPE), F32), jnp.zeros((TM_PROJ, MLA_ROPE), F32)], axis=1)
    return lat, ctx


def kernel(x, c, ctx, c_ctx, ada_w, ada_b, w_in, mla_q_norm, mla_wq_b, mla_kv_norm, mla_wkv_b, conv_w, na_rpb,
           w_out, ln1_g, ln1_b, ffn_w_gate, ffn_w_up, ffn_w_down, ln2_g, ln2_b):
    assert x.shape == (BATCH, SEQ, D_MODEL) and ctx.shape == (BATCH, CTX_LEN, D_MODEL)
    cc = jnp.concatenate([c, c_ctx[None, :], jnp.zeros((MOD_ROWS - BATCH - 1, D_MODEL), F32)], axis=0)
    mods = _ada(cc, ada_w, ada_b).reshape(DEPTH * MOD_ROWS * N_MOD, 1, D_MODEL)
    ta_lat, ta_ctx = _rope_tables()
    rows3d = lambda v: v.reshape(DEPTH, 1, -1)

    w_in_p = _w_in_prep(w_in)
    wq_all = _relayout_wq(mla_wq_b)
    wkv = mla_wkv_b.astype(BF16)
    g_q, g_kv = rows3d(mla_q_norm), rows3d(mla_kv_norm)
    bias = _na_bias(na_rpb)
    ln1 = (rows3d(ln1_g), rows3d(ln1_b))
    ln2 = (rows3d(ln2_g), rows3d(ln2_b))

    xl = x.reshape(BATCH * SEQ, D_MODEL)
    xc = ctx.reshape(BATCH * CTX_LEN, D_MODEL)
    lat_row = lambda tm: (lambda i: i // (SEQ // tm))
    ctx_row = lambda i: CTX_MOD_ROW
    for l in range(DEPTH):
        last = l == DEPTH - 1
        zl = _in_proj(xl, mods, w_in_p, l, lat_row(TM))
        zc = _in_proj(xc, mods, w_in_p, l, ctx_row)
        ql, kl, vl = _mla_proj(zl, g_q, g_kv, wq_all, wkv, ta_lat, l, lambda i: i % (SEQ // TM_PROJ))
        qc, kc, vc = _mla_proj(zc, g_q, g_kv, wq_all, wkv, ta_ctx, l, lambda i: 0)
        a, (w_g,) = _mla_attn(ql, kl, vl, qc, kc, vc, not last, l, [(ffn_w_gate, FFN_CAST_BLOCKS, 1)])
        n, (w_u, w_d, w_o) = _na_attn(
            zl, zc, bias, l, not last,
            [(ffn_w_up, FFN_CAST_BLOCKS, 1), (ffn_w_down, FFN_CAST_BLOCKS, 0), (w_out, OUT_CAST_BLOCKS, 0)])
        x1 = _out_proj(a[0], n[0], zl, xl, mods, conv_w, w_o, *ln1, l, lat_row(TM_OUT), SEQ)
        xl = _ffn(x1, mods, w_g, w_u, w_d, *ln2, l, lat_row(TM))
        if not last:
            x1c = _out_proj(a[1], n[1], zc, xc, mods, conv_w, w_o, *ln1, l, ctx_row, CTX_LEN)
            xc = _ffn(x1c, mods, w_g, w_u, w_d, *ln2, l, ctx_row)
    return xl.reshape(BATCH, SEQ, D_MODEL)
```

```python
import functools

import numpy as np
import jax
import jax.numpy as jnp
from jax import lax
from jax.experimental import pallas as pl
from jax.experimental.pallas import tpu as pltpu

D_MODEL = 2048
BATCH = 4
SEQ = 2048
DEPTH = 2
CTX_LEN = 256
GRID_W = 64
GRID_ROWS = SEQ // GRID_W
MLA_HEADS = 6
MLA_Q_RANK = 512
MLA_KV_RANK = 512
MLA_NOPE = 128
MLA_ROPE = 64
MLA_V = 128
MLA_QK = MLA_NOPE + MLA_ROPE
CONV_DIM = 512
NA_HEADS = 6
NA_HEAD_DIM = 128
NA_DIM = NA_HEADS * NA_HEAD_DIM
NA_WIN_R = 8
NA_WIN_C = 16
OFF_MLA_Q = 0
OFF_MLA_KV = OFF_MLA_Q + MLA_Q_RANK
OFF_ROPE = OFF_MLA_KV + MLA_KV_RANK
OFF_CONV_B = OFF_ROPE + MLA_ROPE
OFF_CONV_C = OFF_CONV_B + CONV_DIM
OFF_CONV_H = OFF_CONV_C + CONV_DIM
OFF_NA_Q = OFF_CONV_H + CONV_DIM
OFF_NA_K = OFF_NA_Q + NA_DIM
OFF_NA_V = OFF_NA_K + NA_DIM
IN_DIM = OFF_NA_V + NA_DIM
D_FF = -(-8 * D_MODEL // (3 * 256)) * 256
ROPE_THETA = 10000.0
LN_EPS = 1e-6
RMS_EPS = 1e-6
DEEPNORM_ALPHA = (2 * DEPTH) ** 0.25
LOG2E = 1.4426950408889634
N_MOD = 6
MOD_ROWS = 8
CTX_MOD_ROW = BATCH

LANES = 128
V7X_VMEM_BYTES = 64 * 1024 * 1024
V7X_VMEM_BUDGET = 62 * 1024 * 1024

Z_Q = 0
Z_KV = Z_Q + MLA_Q_RANK
Z_CONV_B = Z_KV + MLA_KV_RANK
Z_CONV_C = Z_CONV_B + CONV_DIM
Z_CONV_H = Z_CONV_C + CONV_DIM
Z_NA_Q = Z_CONV_H + CONV_DIM
Z_NA_K = Z_NA_Q + NA_DIM
Z_NA_V = Z_NA_K + NA_DIM
Z_ROPE = Z_NA_V + NA_DIM
Z_DIM = Z_ROPE + 2 * MLA_ROPE
V7X_MXU_COLS = 256
Z_PAD = -(-Z_DIM // (5 * V7X_MXU_COLS)) * (5 * V7X_MXU_COLS)

QK_PAD = 2 * LANES

TM = 1024
TM_OUT = 512
TM_OUT_CHUNK = 256
CONV_HALO = 16
TM_PROJ = 512
PREP_PIECE = MLA_ROPE
PREP_COLS = 5 * LANES
TN_IN = 5 * V7X_MXU_COLS
TN_ADA = 1024
TN_ADA_RIDE = 512
TF = 512
TQ = 256
Q_TILE_UNROLL = 8
NA_RQ = 4
NA_RK = NA_RQ + NA_WIN_R - 1
NA_NQ = NA_RQ * GRID_W
NA_NK = NA_RK * GRID_W
NA_BLOCKS = GRID_ROWS // NA_RQ
FFN_CAST_BLOCKS = 11
MLA_HEADS_PER_STEP = 2
OUT_CAST_BLOCKS = 8
NA_HEADS_PER_STEP = 2

F32 = jnp.float32
BF16 = jnp.bfloat16


def _params(semantics, vmem_bytes):
    assert vmem_bytes <= V7X_VMEM_BUDGET, vmem_bytes
    return pltpu.CompilerParams(dimension_semantics=semantics, vmem_limit_bytes=int(vmem_bytes))


def _nbytes(shape, dtype):
    return int(np.prod(shape)) * jnp.dtype(dtype).itemsize


def _dot(a, b):
    return jnp.dot(a, b, preferred_element_type=F32)


def _dot_nt(a, b):
    return lax.dot_general(a, b, (((1,), (1,)), ((), ())), preferred_element_type=F32)


def _mod_spec(chunk, row_of_tile):
    return pl.BlockSpec((None, 1, D_MODEL), lambda i, *_: (chunk + row_of_tile(i) * N_MOD, 0, 0))


def _layer_norm(r, g, b):
    mu = jnp.mean(r, axis=-1, keepdims=True)
    c = r - mu
    var = jnp.mean(c * c, axis=-1, keepdims=True)
    return c * lax.rsqrt(var + LN_EPS) * g + b


def _ada_block(cc_ref, w_ref, b_ref):
    cc = cc_ref[...]
    s = (cc * jax.nn.sigmoid(cc)).astype(BF16)
    return _dot(s, w_ref[...].astype(BF16)) + b_ref[...]


def _ada_kernel(cc_ref, w_ref, b_ref, o_ref):
    o_ref[...] = _ada_block(cc_ref, w_ref, b_ref)


def _ada(cc, ada_w, ada_b, layer):
    n = N_MOD * D_MODEL
    vmem = 2 * (_nbytes((D_MODEL, TN_ADA), F32) + _nbytes((MOD_ROWS, TN_ADA), F32) * 2
                + _nbytes((MOD_ROWS, D_MODEL), F32)) + _nbytes((D_MODEL, TN_ADA), BF16) * 2
    return pl.pallas_call(
        _ada_kernel,
        out_shape=jax.ShapeDtypeStruct((MOD_ROWS, n), F32),
        grid=(n // TN_ADA,),
        in_specs=[
            pl.BlockSpec((MOD_ROWS, D_MODEL), lambda j: (0, 0)),
            pl.BlockSpec((None, D_MODEL, TN_ADA), lambda j: (layer, 0, j)),
            pl.BlockSpec((None, 1, TN_ADA), lambda j: (layer, 0, j)),
        ],
        out_specs=pl.BlockSpec((MOD_ROWS, TN_ADA), lambda j: (0, j)),
        compiler_params=_params(("parallel",), vmem),
        name="ada_mod",
    )(cc, ada_w, ada_b)


def _in_proj_kernel(sh_ref, sc_ref, x_ref, w_ref, *rest):
    if len(rest) == 2:
        ada_in, (o_ref, xm_ref), ada_out = (), rest, None
    else:
        ada_in, (o_ref, ada_out, xm_ref) = rest[:3], rest[3:]

    @pl.when(pl.program_id(1) == 0)
    def _():
        xm_ref[...] = (x_ref[...] * (1.0 + sc_ref[...]) + sh_ref[...]).astype(BF16)

    o_ref[...] = _dot(xm_ref[...], w_ref[...]).astype(BF16)
    if ada_out is not None:
        ada_out[...] = _ada_block(*ada_in)


def _in_proj(x, mods, w_in_p, layer, row_of_tile, next_ada=None):
    rows = x.shape[0]
    grid = (rows // TM, Z_PAD // TN_IN)
    vmem = (2 * (_nbytes((TM, D_MODEL), F32) + _nbytes((D_MODEL, TN_IN), BF16) + _nbytes((TM, TN_IN), BF16))
            + _nbytes((TM, D_MODEL), BF16) * 2 + _nbytes((TM, TN_IN), F32))
    in_specs = [
        _mod_spec(0, row_of_tile),
        _mod_spec(1, row_of_tile),
        pl.BlockSpec((TM, D_MODEL), lambda i, j: (i, 0)),
        pl.BlockSpec((None, D_MODEL, TN_IN), lambda i, j: (layer, 0, j)),
    ]
    args = [mods, mods, x, w_in_p]
    out_shape = [jax.ShapeDtypeStruct((rows, Z_PAD), BF16)]
    out_specs = [pl.BlockSpec((TM, TN_IN), lambda i, j: (i, j))]
    if next_ada is not None:
        n_blocks = N_MOD * D_MODEL // TN_ADA_RIDE
        assert n_blocks <= grid[0] * grid[1]
        blk = lambda i, j: jnp.minimum(i * grid[1] + j, n_blocks - 1)
        in_specs += [
            pl.BlockSpec((MOD_ROWS, D_MODEL), lambda i, j: (0, 0)),
            pl.BlockSpec((None, D_MODEL, TN_ADA_RIDE), lambda i, j: (layer + 1, 0, blk(i, j))),
            pl.BlockSpec((None, 1, TN_ADA_RIDE), lambda i, j: (layer + 1, 0, blk(i, j))),
        ]
        args += list(next_ada)
        out_shape.append(jax.ShapeDtypeStruct((MOD_ROWS, N_MOD * D_MODEL), F32))
        out_specs.append(pl.BlockSpec((MOD_ROWS, TN_ADA_RIDE), lambda i, j: (0, blk(i, j))))
        vmem += 2 * _nbytes((D_MODEL, TN_ADA_RIDE), F32) + _nbytes((D_MODEL, TN_ADA_RIDE), BF16)
    res = pl.pallas_call(
        _in_proj_kernel,
        out_shape=tuple(out_shape), grid=grid, in_specs=in_specs, out_specs=tuple(out_specs),
        scratch_shapes=[pltpu.VMEM((TM, D_MODEL), BF16)],
        compiler_params=_params(("arbitrary", "arbitrary"), vmem),
        name="in_proj",
    )(*args)
    return res if next_ada is not None else res[0]


def _rms_norm(x, g):
    return x * lax.rsqrt(jnp.mean(x * x, axis=-1, keepdims=True) + RMS_EPS) * g


def _rope_sum(pair, table, lane):
    y = pair * table
    return jnp.where(lane < MLA_ROPE, y + pltpu.roll(y, MLA_ROPE, 1), 0.0)


def _mla_proj_kernel(z_ref, slot_ref, gq_ref, gkv_ref, wq_ref, wkv_ref, ta_ref, q_ref, k_ref, v_ref):
    cq = z_ref[:, Z_Q:Z_Q + MLA_Q_RANK].astype(F32)
    ckv = z_ref[:, Z_KV:Z_KV + MLA_KV_RANK].astype(F32)
    qa = _dot(_rms_norm(cq, gq_ref[...]).astype(BF16), wq_ref[...]) * (MLA_QK ** -0.5 * LOG2E)
    kv = _dot(_rms_norm(ckv, gkv_ref[...]).astype(BF16), wkv_ref[...])
    ta = ta_ref[...]
    lane = lax.broadcasted_iota(jnp.int32, ta.shape, 1)
    kro = _rope_sum(slot_ref[...].astype(F32), ta, lane).astype(BF16)
    for h in range(MLA_HEADS):
        ob = h * QK_PAD
        q_ref[:, ob:ob + LANES] = qa[:, ob:ob + LANES].astype(BF16)
        q_ref[:, ob + LANES:ob + QK_PAD] = _rope_sum(qa[:, ob + LANES:ob + QK_PAD], ta, lane).astype(BF16)
        k_ref[:, ob:ob + LANES] = kv[:, ob:ob + LANES].astype(BF16)
        k_ref[:, ob + LANES:ob + QK_PAD] = kro
        v_ref[:, h * MLA_V:(h + 1) * MLA_V] = kv[:, ob + LANES:ob + QK_PAD].astype(BF16)


def _mla_proj(z, g_q, g_kv, wq_all, wkv, ta, layer, table_tile):
    rows = z.shape[0]
    per_layer = lambda i: (layer, 0, 0)
    zw = Z_KV + MLA_KV_RANK
    hq = MLA_HEADS * QK_PAD
    hv = MLA_HEADS * MLA_V
    tm = TM_PROJ
    vmem = (2 * (_nbytes((tm, zw), BF16) + _nbytes((tm, LANES), BF16) + _nbytes((tm, LANES), F32)
                 + _nbytes(wq_all.shape[1:], BF16) + _nbytes(wkv.shape[1:], BF16)
                 + 2 * _nbytes((tm, hq), BF16) + _nbytes((tm, hv), BF16))
            + 2 * _nbytes((tm, hq), F32) + 4 * _nbytes((tm, zw), F32))
    return pl.pallas_call(
        _mla_proj_kernel,
        out_shape=(jax.ShapeDtypeStruct((rows, hq), BF16), jax.ShapeDtypeStruct((rows, hq), BF16),
                   jax.ShapeDtypeStruct((rows, hv), BF16)),
        grid=(rows // tm,),
        in_specs=[
            pl.BlockSpec((tm, zw), lambda i: (i, 0)),
            pl.BlockSpec((tm, 2 * MLA_ROPE), lambda i: (i, Z_ROPE // (2 * MLA_ROPE))),
            pl.BlockSpec((None, 1, MLA_Q_RANK), per_layer),
            pl.BlockSpec((None, 1, MLA_KV_RANK), per_layer),
            pl.BlockSpec((None,) + wq_all.shape[1:], per_layer),
            pl.BlockSpec((None,) + wkv.shape[1:], per_layer),
            pl.BlockSpec((tm, LANES), lambda i: (table_tile(i), 0)),
        ],
        out_specs=(pl.BlockSpec((tm, hq), lambda i: (i, 0)), pl.BlockSpec((tm, hq), lambda i: (i, 0)),
                   pl.BlockSpec((tm, hv), lambda i: (i, 0))),
        compiler_params=_params(("parallel",), vmem),
        name="mla_proj",
    )(z, z, g_q, g_kv, wq_all, wkv, ta)


def _rider_specs(w, layer, n_blocks, axis, grid):
    rows, cols = w.shape[1:]
    block = (rows // n_blocks, cols) if axis == 0 else (rows, cols // n_blocks)
    assert block[0] * block[1] * n_blocks == rows * cols and n_blocks <= grid[0] * grid[1]

    def at(outer, inner):
        step = jnp.minimum(outer * grid[1] + inner, n_blocks - 1)
        return (step, 0) if axis == 0 else (0, step)

    in_spec = pl.BlockSpec((None,) + block, lambda outer, inner: (layer,) + at(outer, inner))
    return in_spec, pl.BlockSpec(block, at), jax.ShapeDtypeStruct((rows, cols), BF16), block


def _cast_riders(in_refs, out_refs):
    for i_ref, o_ref in zip(in_refs, out_refs):
        o_ref[...] = i_ref[...].astype(BF16)


def _split_refs(refs, *counts):
    out, pos = [], 0
    for c in counts:
        out.append(refs[pos:pos + c])
        pos += c
    assert pos == len(refs)
    return out


def _softmax_pv(scores, values_ones):
    m = functools.reduce(jnp.maximum, [jnp.max(s, axis=-1, keepdims=True) for s in scores])
    acc = functools.reduce(jnp.add, [_dot(jnp.exp2(s - m).astype(BF16), v) for s, v in zip(scores, values_ones)])
    return acc[:, :LANES] / acc[:, LANES:]


def _fill_values_ones(v_ref, vx_ref):
    vx_ref[:, :LANES] = v_ref[...]
    vx_ref[:, LANES:] = jnp.ones((vx_ref.shape[0], LANES), BF16)


def _scaled_q(q, head_dim):
    return (q.astype(F32) * (head_dim ** -0.5 * LOG2E)).astype(BF16)


def _mla_attn_kernel(*refs, with_ctx, n_riders):
    ins, rider_in, outs, rider_out, (vxl_ref, vxc_ref) = _split_refs(
        refs, 6 if with_ctx else 5, n_riders, 2 if with_ctx else 1, n_riders, 2)
    q_ref, kl_ref, vl_ref, kc_ref, vc_ref = ins[:5]
    o_ref = outs[0]
    _cast_riders(rider_in, rider_out)
    for hh in range(MLA_HEADS_PER_STEP):
        qk = slice(hh * QK_PAD, (hh + 1) * QK_PAD)
        vcols = slice(hh * MLA_V, (hh + 1) * MLA_V)
        _fill_values_ones(vl_ref.at[:, vcols], vxl_ref.at[hh])
        _fill_values_ones(vc_ref.at[:, vcols], vxc_ref.at[hh])

        def q_tile(t, carry, hh=hh, qk=qk, vcols=vcols):
            r0 = pl.multiple_of(t * TQ, TQ)
            q = q_ref[pl.ds(r0, TQ), qk]
            s_lat = _dot_nt(q, kl_ref[:, qk])
            s_ctx = _dot_nt(q, kc_ref[:, qk])
            o = _softmax_pv([s_lat, s_ctx], [vxl_ref[hh], vxc_ref[hh]])
            o_ref[pl.ds(r0, TQ), vcols] = o.astype(BF16)
            return carry

        lax.fori_loop(0, SEQ // TQ, q_tile, 0, unroll=Q_TILE_UNROLL)
        if with_ctx:
            s = _dot_nt(ins[5][:, qk], kc_ref[:, qk])
            outs[1][:, vcols] = _softmax_pv([s], [vxc_ref[hh]]).astype(BF16)


def _mla_attn(ql, kl, vl, qc, kc, vc, with_ctx, layer, riders):
    hps = MLA_HEADS_PER_STEP
    grid = (BATCH, MLA_HEADS // hps)
    head = lambda b, g: (b, g)
    wqk, wv = hps * QK_PAD, hps * MLA_V
    in_specs = [
        pl.BlockSpec((SEQ, wqk), head), pl.BlockSpec((SEQ, wqk), head), pl.BlockSpec((SEQ, wv), head),
        pl.BlockSpec((CTX_LEN, wqk), head), pl.BlockSpec((CTX_LEN, wv), head),
    ]
    args = [ql, kl, vl, kc, vc]
    out_shape = [jax.ShapeDtypeStruct((BATCH * SEQ, MLA_HEADS * MLA_V), BF16)]
    out_specs = [pl.BlockSpec((SEQ, wv), head)]
    if with_ctx:
        in_specs.append(pl.BlockSpec((CTX_LEN, wqk), head))
        args.append(qc)
        out_shape.append(jax.ShapeDtypeStruct((BATCH * CTX_LEN, MLA_HEADS * MLA_V), BF16))
        out_specs.append(pl.BlockSpec((CTX_LEN, wv), head))
    vmem = (2 * (3 * _nbytes((SEQ, wqk), BF16) + 2 * _nbytes((SEQ, wv), BF16))
            + 2 * hps * _nbytes((SEQ, 2 * LANES), BF16) + Q_TILE_UNROLL * _nbytes((TQ, SEQ + CTX_LEN), BF16))
    n_base = len(out_shape)
    for w, n_blocks, axis in riders:
        in_spec, out_spec, shape, block = _rider_specs(w, layer, n_blocks, axis, grid)
        in_specs.append(in_spec)
        args.append(w)
        out_specs.append(out_spec)
        out_shape.append(shape)
        vmem += 2 * (_nbytes(block, F32) + _nbytes(block, BF16))
    res = pl.pallas_call(
        functools.partial(_mla_attn_kernel, with_ctx=with_ctx, n_riders=len(riders)),
        out_shape=tuple(out_shape), grid=grid, in_specs=in_specs, out_specs=tuple(out_specs),
        scratch_shapes=[pltpu.VMEM((hps, SEQ, 2 * LANES), BF16), pltpu.VMEM((hps, CTX_LEN, 2 * LANES), BF16)],
        compiler_params=_params(("arbitrary", "arbitrary"), vmem),
        name="mla_attn_ctx" if with_ctx else "mla_attn",
    )(*args)
    return res[:n_base], res[n_base:]


def _na_band_start(block):
    return min(max(block * NA_RQ - NA_WIN_R // 2, 0), GRID_ROWS - NA_RK)


def _na_bias_kind(block):
    return 0 if block == 0 else (2 if block == NA_BLOCKS - 1 else 1)


def _na_row_offsets():
    n_dr = 2 * NA_WIN_R - 1
    idx = np.full((3, NA_RQ, NA_RK), n_dr, np.int32)
    for kind, block in ((0, 0), (1, 2), (2, NA_BLOCKS - 1)):
        for qr in range(NA_RQ):
            r = block * NA_RQ + qr
            r0 = min(max(r - NA_WIN_R // 2, 0), GRID_ROWS - NA_WIN_R)
            for ki in range(NA_RK):
                kr = _na_band_start(block) + ki
                if r0 <= kr < r0 + NA_WIN_R:
                    idx[kind, qr, ki] = kr - r + NA_WIN_R - 1
    return idx


def _na_bias_kernel(w_ref, o_ref):
    shape = (GRID_W, LANES)
    c = lax.broadcasted_iota(jnp.int32, shape, 0)
    kc = lax.broadcasted_iota(jnp.int32, shape, 1)
    c0 = jnp.clip(c - NA_WIN_C // 2, 0, GRID_W - NA_WIN_C)
    col_ok = (kc >= c0) & (kc < c0 + NA_WIN_C)
    neg = jnp.full((GRID_W, GRID_W), -jnp.inf, F32)
    offsets = _na_row_offsets()
    n_dr = 2 * NA_WIN_R - 1
    tiles = {n_dr: neg}
    for d in sorted(set(int(v) for v in offsets.reshape(-1)) - {n_dr}):
        row = jnp.broadcast_to(w_ref[d:d + 1, :], shape)
        skew = pltpu.roll(row, LANES - (GRID_W - 1), 1, stride=1, stride_axis=0)
        tiles[d] = jnp.where(col_ok, skew, -jnp.inf)[:, :GRID_W]
    for kind in range(3):
        for qr in range(NA_RQ):
            for ki in range(NA_RK):
                o_ref[kind, qr * GRID_W:(qr + 1) * GRID_W, ki * GRID_W:(ki + 1) * GRID_W] = (
                    tiles[int(offsets[kind, qr, ki])])


def _na_bias(rpb):
    n_dr, n_dc = 2 * NA_WIN_R - 1, 2 * NA_WIN_C - 1
    lo = GRID_W - 1 - (NA_WIN_C - 1)
    w = jnp.pad(rpb.reshape(DEPTH * NA_HEADS, n_dr, n_dc) * LOG2E, ((0, 0), (0, 0), (lo, LANES - lo - n_dc)),
                constant_values=-jnp.inf)
    vmem = 2 * (_nbytes((2 * 8, LANES), F32) + _nbytes((3, NA_NQ, NA_NK + GRID_W), F32)) + 16 * _nbytes(
        (GRID_W, LANES), F32)
    return pl.pallas_call(
        _na_bias_kernel,
        out_shape=jax.ShapeDtypeStruct((w.shape[0], 3, NA_NQ, NA_NK), F32),
        grid=(w.shape[0],),
        in_specs=[pl.BlockSpec((None, n_dr, LANES), lambda g: (g, 0, 0))],
        out_specs=pl.BlockSpec((None, 3, NA_NQ, NA_NK), lambda g: (g, 0, 0, 0)),
        compiler_params=_params(("parallel",), vmem),
        name="na_bias",
    )(w)


def _na_kernel(*refs, with_ctx, n_riders):
    ins, rider_in, outs, rider_out, (vx_ref, vxc_ref) = _split_refs(
        refs, 7 if with_ctx else 6, n_riders, 2 if with_ctx else 1, n_riders, 2)
    q_ref, k_ref, v_ref, kc_ref, vc_ref, bias_ref = ins[:6]
    o_ref = outs[0]
    _cast_riders(rider_in, rider_out)
    d = NA_HEAD_DIM
    for hh in range(NA_HEADS_PER_STEP):
        cols = slice(hh * d, (hh + 1) * d)
        _fill_values_ones(v_ref.at[:, cols], vx_ref.at[hh])
        _fill_values_ones(vc_ref.at[:, cols], vxc_ref.at[hh])
        for block in range(NA_BLOCKS):
            rows = slice(block * NA_NQ, (block + 1) * NA_NQ)
            q = _scaled_q(q_ref[rows, cols], d)
            k0 = _na_band_start(block) * GRID_W
            s_loc = _dot_nt(q, k_ref[k0:k0 + NA_NK, cols]) + bias_ref[hh, _na_bias_kind(block)]
            s_ctx = _dot_nt(q, kc_ref[:, cols])
            o = _softmax_pv([s_loc, s_ctx], [vx_ref[hh, k0:k0 + NA_NK, :], vxc_ref[hh]])
            o_ref[rows, cols] = o.astype(BF16)
        if with_ctx:
            s = _dot_nt(_scaled_q(ins[6][:, cols], d), kc_ref[:, cols])
            outs[1][:, cols] = _softmax_pv([s], [vxc_ref[hh]]).astype(BF16)


def _na_attn(zl, zc, bias, layer, with_ctx, riders):
    hps = NA_HEADS_PER_STEP
    w = hps * NA_HEAD_DIM
    steps = NA_HEADS // hps
    col = lambda off: (lambda g, b: (b, off // w + g))
    in_specs = [
        pl.BlockSpec((SEQ, w), col(Z_NA_Q)), pl.BlockSpec((SEQ, w), col(Z_NA_K)), pl.BlockSpec((SEQ, w), col(Z_NA_V)),
        pl.BlockSpec((CTX_LEN, w), col(Z_NA_K)), pl.BlockSpec((CTX_LEN, w), col(Z_NA_V)),
        pl.BlockSpec((hps, 3, NA_NQ, NA_NK), lambda g, b: (layer * steps + g, 0, 0, 0)),
    ]
    args = [zl, zl, zl, zc, zc, bias]
    out_shape = [jax.ShapeDtypeStruct((BATCH * SEQ, NA_DIM), BF16)]
    out_specs = [pl.BlockSpec((SEQ, w), lambda g, b: (b, g))]
    if with_ctx:
        in_specs.append(pl.BlockSpec((CTX_LEN, w), col(Z_NA_Q)))
        args.append(zc)
        out_shape.append(jax.ShapeDtypeStruct((BATCH * CTX_LEN, NA_DIM), BF16))
        out_specs.append(pl.BlockSpec((CTX_LEN, w), lambda g, b: (b, g)))
    vmem = (2 * (5 * _nbytes((SEQ, w), BF16) + _nbytes((hps, 3, NA_NQ, NA_NK), F32))
            + 2 * hps * _nbytes((SEQ, 2 * LANES), BF16) + 4 * hps * _nbytes((NA_NQ, NA_NK + CTX_LEN), F32))
    n_base = len(out_shape)
    for wt, n_blocks, axis in riders:
        in_spec, out_spec, shape, block = _rider_specs(wt, layer, n_blocks, axis, (steps, BATCH))
        in_specs.append(in_spec)
        args.append(wt)
        out_specs.append(out_spec)
        out_shape.append(shape)
        vmem += 2 * (_nbytes(block, F32) + _nbytes(block, BF16))
    res = pl.pallas_call(
        functools.partial(_na_kernel, with_ctx=with_ctx, n_riders=len(riders)),
        out_shape=tuple(out_shape), grid=(steps, BATCH), in_specs=in_specs, out_specs=tuple(out_specs),
        scratch_shapes=[pltpu.VMEM((hps, SEQ, 2 * LANES), BF16), pltpu.VMEM((hps, CTX_LEN, 2 * LANES), BF16)],
        compiler_params=_params(("arbitrary", "arbitrary"), vmem),
        name="na_attn_ctx" if with_ctx else "na_attn",
    )(*args)
    return res[:n_base], res[n_base:]


def _out_proj_kernel(g_ref, a_ref, n_ref, x_ref, gb_ref, gc_ref, h_ref, gcp_ref, hp_ref, gcn_ref, hn_ref, cw_ref,
                     w_ref, lng_ref, lnb_ref, o_ref, mix_ref, *, seq_len):
    tm = o_ref.shape[0]
    u = gc_ref[...].astype(F32) * h_ref[...].astype(F32)
    u_before = gcp_ref[CONV_HALO - 1:, :].astype(F32) * hp_ref[CONV_HALO - 1:, :].astype(F32)
    u_after = gcn_ref[:1, :].astype(F32) * hn_ref[:1, :].astype(F32)
    row = lax.broadcasted_iota(jnp.int32, u.shape, 0)
    pos = (pl.program_id(0) * tm + row) % seq_len
    u_prev = jnp.where(row == 0, u_before, pltpu.roll(u, 1, 0))
    u_prev = jnp.where(pos == 0, 0.0, u_prev)
    u_next = jnp.where(row == tm - 1, u_after, pltpu.roll(u, tm - 1, 0))
    u_next = jnp.where(pos == seq_len - 1, 0.0, u_next)
    y = u_prev * cw_ref[0:1, :] + u * cw_ref[1:2, :] + u_next * cw_ref[2:3, :]
    a_w = MLA_HEADS * MLA_V
    mix_ref[:, :a_w] = a_ref[...]
    mix_ref[:, a_w:a_w + CONV_DIM] = (gb_ref[...].astype(F32) * y).astype(BF16)
    mix_ref[:, a_w + CONV_DIM:] = n_ref[...]

    for r0 in range(0, tm, TM_OUT_CHUNK):
        rows = slice(r0, r0 + TM_OUT_CHUNK)
        r = DEEPNORM_ALPHA * x_ref[rows, :] + g_ref[...] * _dot(mix_ref[rows, :], w_ref[...])
        o_ref[rows, :] = _layer_norm(r, lng_ref[...], lnb_ref[...])


def _out_proj(a, n, z, x, mods, conv_w, w_out, ln_g, ln_b, layer, row_of_tile, seq_len):
    rows = x.shape[0]
    tm = TM_OUT
    assert tm % seq_len == 0 or seq_len % tm == 0
    row = lambda i: (i, 0)
    per_layer = lambda i: (layer, 0, 0)
    conv_blk = lambda off: pl.BlockSpec((tm, CONV_DIM), lambda i: (i, off // CONV_DIM))
    halo = tm // CONV_HALO
    before = lambda off: pl.BlockSpec((CONV_HALO, CONV_DIM), lambda i: (jnp.maximum(i * halo - 1, 0), off // CONV_DIM))
    after = lambda off: pl.BlockSpec(
        (CONV_HALO, CONV_DIM), lambda i: (jnp.minimum((i + 1) * halo, rows // CONV_HALO - 1), off // CONV_DIM))
    vmem = (2 * (2 * _nbytes((tm, D_MODEL), F32) + _nbytes((tm, D_MODEL), BF16) + _nbytes((tm, CONV_DIM), BF16))
            + _nbytes((tm, D_MODEL), BF16) + _nbytes(w_out.shape, BF16)
            + 3 * _nbytes((TM_OUT_CHUNK, D_MODEL), F32) + 4 * _nbytes((tm, CONV_DIM), F32))
    return pl.pallas_call(
        functools.partial(_out_proj_kernel, seq_len=seq_len),
        out_shape=jax.ShapeDtypeStruct((rows, D_MODEL), F32),
        grid=(rows // tm,),
        in_specs=[
            _mod_spec(2, row_of_tile),
            pl.BlockSpec((tm, a.shape[1]), row), pl.BlockSpec((tm, n.shape[1]), row), pl.BlockSpec((tm, D_MODEL), row),
            conv_blk(Z_CONV_B), conv_blk(Z_CONV_C), conv_blk(Z_CONV_H),
            before(Z_CONV_C), before(Z_CONV_H), after(Z_CONV_C), after(Z_CONV_H),
            pl.BlockSpec((None,) + conv_w.shape[1:], per_layer),
            pl.BlockSpec(w_out.shape, lambda i: (0, 0), pipeline_mode=pl.Buffered(1)),
            pl.BlockSpec((None, 1, D_MODEL), per_layer), pl.BlockSpec((None, 1, D_MODEL), per_layer),
        ],
        out_specs=pl.BlockSpec((tm, D_MODEL), row),
        scratch_shapes=[pltpu.VMEM((tm, D_MODEL), BF16)],
        compiler_params=_params(("parallel",), vmem),
        name="out_proj",
    )(mods, a, n, x, z, z, z, z, z, z, z, conv_w, w_out, ln_g, ln_b)


def _ffn_kernel(sh_ref, sc_ref, g_ref, x_ref, wg_ref, wu_ref, wd_ref, lng_ref, lnb_ref, o_ref, xm_ref):
    f = pl.program_id(1)

    @pl.when(f == 0)
    def _():
        x = x_ref[...]
        xm_ref[...] = (x * (1.0 + sc_ref[...]) + sh_ref[...]).astype(BF16)
        o_ref[...] = DEEPNORM_ALPHA * x

    xm = xm_ref[...]
    gate = _dot(xm, wg_ref[...])
    up = _dot(xm, wu_ref[...])
    hidden = (gate * jax.nn.sigmoid(gate) * up).astype(BF16)
    o_ref[...] += g_ref[...] * _dot(hidden, wd_ref[...])

    @pl.when(f == pl.num_programs(1) - 1)
    def _():
        o_ref[...] = _layer_norm(o_ref[...], lng_ref[...], lnb_ref[...])


def _ffn(x, mods, w_gate, w_up, w_down, ln_g, ln_b, layer, row_of_tile):
    rows = x.shape[0]
    per_layer = lambda i, f: (layer, 0, 0)
    vmem = (4 * _nbytes((TM, D_MODEL), F32) + 2 * 3 * _nbytes((D_MODEL, TF), BF16)
            + _nbytes((TM, D_MODEL), BF16) + 3 * _nbytes((TM, TF), F32) + _nbytes((TM, D_MODEL), F32))
    return pl.pallas_call(
        _ffn_kernel,
        out_shape=jax.ShapeDtypeStruct((rows, D_MODEL), F32),
        grid=(rows // TM, D_FF // TF),
        in_specs=[
            _mod_spec(3, row_of_tile), _mod_spec(4, row_of_tile), _mod_spec(5, row_of_tile),
            pl.BlockSpec((TM, D_MODEL), lambda i, f: (i, 0)),
            pl.BlockSpec((D_MODEL, TF), lambda i, f: (0, f)),
            pl.BlockSpec((D_MODEL, TF), lambda i, f: (0, f)),
            pl.BlockSpec((TF, D_MODEL), lambda i, f: (f, 0)),
            pl.BlockSpec((None, 1, D_MODEL), per_layer), pl.BlockSpec((None, 1, D_MODEL), per_layer),
        ],
        out_specs=pl.BlockSpec((TM, D_MODEL), lambda i, f: (i, 0)),
        scratch_shapes=[pltpu.VMEM((TM, D_MODEL), BF16)],
        compiler_params=_params(("parallel", "arbitrary"), vmem),
        name="ffn",
    )(mods, mods, mods, x, w_gate, w_up, w_down, ln_g, ln_b)


def _rotate_half_cols(w):
    a1, a2, b1, b2 = jnp.split(w, 4, axis=-1)
    return jnp.concatenate([-a2, a1, -b2, b1], axis=-1)


def _w_in_prep_src(piece):
    rope = OFF_ROPE // PREP_PIECE
    tail = Z_ROPE // PREP_PIECE
    return jnp.where(piece < rope, piece, jnp.where(piece < tail, piece + 1, rope))


def _w_in_prep_kernel(*refs):
    *piece_refs, o_ref = refs
    pieces = [r[...] for r in piece_refs]
    n = len(pieces)
    is_last = pl.program_id(1) == pl.num_programs(1) - 1
    rot_piece = Z_DIM // PREP_PIECE - 1
    for p in range(n):
        piece = (Z_PAD // PREP_COLS - 1) * n + p
        if piece == rot_piece:
            a1, a2, b1, b2 = jnp.split(pieces[p], 4, axis=0)
            pieces[p] = jnp.where(is_last, jnp.concatenate([-a2, a1, -b2, b1], axis=0), pieces[p])
        elif piece > rot_piece:
            pieces[p] = jnp.where(is_last, 0.0, pieces[p])
    o_ref[...] = jnp.concatenate(pieces, axis=0).T.astype(BF16)


def _w_in_prep(w_in):
    w_t = jnp.swapaxes(w_in, 1, 2)
    n = PREP_COLS // PREP_PIECE
    piece_spec = lambda p: pl.BlockSpec((None, PREP_PIECE, D_MODEL), lambda l, g: (l, _w_in_prep_src(g * n + p), 0))
    vmem = 2 * (_nbytes((PREP_COLS, D_MODEL), F32) + _nbytes((D_MODEL, PREP_COLS), BF16)) + 3 * _nbytes(
        (PREP_COLS, D_MODEL), F32)
    return pl.pallas_call(
        _w_in_prep_kernel,
        out_shape=jax.ShapeDtypeStruct((DEPTH, D_MODEL, Z_PAD), BF16),
        grid=(DEPTH, Z_PAD // PREP_COLS),
        in_specs=[piece_spec(p) for p in range(n)],
        out_specs=pl.BlockSpec((None, D_MODEL, PREP_COLS), lambda l, g: (l, 0, g)),
        compiler_params=_params(("parallel", "parallel"), vmem),
        name="w_in_prep",
    )(*([w_t] * n))


def _relayout_wq(w):
    w = w.reshape(DEPTH, MLA_Q_RANK, MLA_HEADS, MLA_QK)
    rope = w[..., MLA_NOPE:]
    cols = jnp.concatenate([w, _rotate_half_cols(rope)], axis=-1)
    return cols.reshape(DEPTH, MLA_Q_RANK, MLA_HEADS * QK_PAD).astype(BF16)


def _rope_tables():
    t = jnp.arange(SEQ)
    row = (t // GRID_W).astype(F32)
    col = (t % GRID_W).astype(F32)
    n_freq = MLA_ROPE // 4
    inv = ROPE_THETA ** (-jnp.arange(n_freq, dtype=F32) / n_freq)
    ar = row[:, None] * inv
    ac = col[:, None] * inv
    ang = jnp.concatenate([ar, ar, ac, ac], axis=-1)
    lat = jnp.concatenate([jnp.cos(ang), jnp.sin(ang)], axis=1)
    ctx = jnp.concatenate([jnp.ones((TM_PROJ, MLA_ROPE), F32), jnp.zeros((TM_PROJ, MLA_ROPE), F32)], axis=1)
    return lat, ctx


def kernel(x, c, ctx, c_ctx, ada_w, ada_b, w_in, mla_q_norm, mla_wq_b, mla_kv_norm, mla_wkv_b, conv_w, na_rpb,
           w_out, ln1_g, ln1_b, ffn_w_gate, ffn_w_up, ffn_w_down, ln2_g, ln2_b):
    assert x.shape == (BATCH, SEQ, D_MODEL) and ctx.shape == (BATCH, CTX_LEN, D_MODEL)
    cc = jnp.concatenate([c, c_ctx[None, :], jnp.zeros((MOD_ROWS - BATCH - 1, D_MODEL), F32)], axis=0)
    ada_b = ada_b.reshape(DEPTH, 1, N_MOD * D_MODEL)
    as_table = lambda m: m.reshape(MOD_ROWS * N_MOD, 1, D_MODEL)
    mods = as_table(_ada(cc, ada_w, ada_b, 0))
    ta_lat, ta_ctx = _rope_tables()
    rows3d = lambda v: v.reshape(DEPTH, 1, -1)

    w_in_p = _w_in_prep(w_in)
    wq_all = _relayout_wq(mla_wq_b)
    wkv = mla_wkv_b.astype(BF16)
    g_q, g_kv = rows3d(mla_q_norm), rows3d(mla_kv_norm)
    bias = _na_bias(na_rpb)
    ln1 = (rows3d(ln1_g), rows3d(ln1_b))
    ln2 = (rows3d(ln2_g), rows3d(ln2_b))

    xl = x.reshape(BATCH * SEQ, D_MODEL)
    xc = ctx.reshape(BATCH * CTX_LEN, D_MODEL)
    lat_row = lambda tm: (lambda i: i // (SEQ // tm))
    ctx_row = lambda i: CTX_MOD_ROW
    for l in range(DEPTH):
        last = l == DEPTH - 1
        if last:
            zl = _in_proj(xl, mods, w_in_p, l, lat_row(TM))
        else:
            zl, next_mods = _in_proj(xl, mods, w_in_p, l, lat_row(TM), next_ada=(cc, ada_w, ada_b))
        zc = _in_proj(xc, mods, w_in_p, l, ctx_row)
        ql, kl, vl = _mla_proj(zl, g_q, g_kv, wq_all, wkv, ta_lat, l, lambda i: i % (SEQ // TM_PROJ))
        qc, kc, vc = _mla_proj(zc, g_q, g_kv, wq_all, wkv, ta_ctx, l, lambda i: 0)
        a, (w_g, w_u) = _mla_attn(ql, kl, vl, qc, kc, vc, not last, l,
                                  [(ffn_w_gate, FFN_CAST_BLOCKS, 1), (ffn_w_up, FFN_CAST_BLOCKS, 1)])
        n, (w_d, w_o) = _na_attn(zl, zc, bias, l, not last,
                                 [(ffn_w_down, FFN_CAST_BLOCKS, 0), (w_out, OUT_CAST_BLOCKS, 0)])
        x1 = _out_proj(a[0], n[0], zl, xl, mods, conv_w, w_o, *ln1, l, lat_row(TM_OUT), SEQ)
        xl = _ffn(x1, mods, w_g, w_u, w_d, *ln2, l, lat_row(TM))
        if not last:
            x1c = _out_proj(a[1], n[1], zc, xc, mods, conv_w, w_o, *ln1, l, ctx_row, CTX_LEN)
            xc = _ffn(x1c, mods, w_g, w_u, w_d, *ln2, l, ctx_row)
            mods = as_table(next_mods)
    return xl.reshape(BATCH, SEQ, D_MODEL)
```

```python
import functools

import numpy as np
import jax
import jax.numpy as jnp
from jax import lax
from jax.experimental import pallas as pl
from jax.experimental.pallas import tpu as pltpu

D_MODEL = 2048
BATCH = 4
SEQ = 2048
DEPTH = 2
CTX_LEN = 256
GRID_W = 64
GRID_ROWS = SEQ // GRID_W
MLA_HEADS = 6
MLA_Q_RANK = 512
MLA_KV_RANK = 512
MLA_NOPE = 128
MLA_ROPE = 64
MLA_V = 128
MLA_QK = MLA_NOPE + MLA_ROPE
CONV_DIM = 512
NA_HEADS = 6
NA_HEAD_DIM = 128
NA_DIM = NA_HEADS * NA_HEAD_DIM
NA_WIN_R = 8
NA_WIN_C = 16
OFF_MLA_Q = 0
OFF_MLA_KV = OFF_MLA_Q + MLA_Q_RANK
OFF_ROPE = OFF_MLA_KV + MLA_KV_RANK
OFF_CONV_B = OFF_ROPE + MLA_ROPE
OFF_CONV_C = OFF_CONV_B + CONV_DIM
OFF_CONV_H = OFF_CONV_C + CONV_DIM
OFF_NA_Q = OFF_CONV_H + CONV_DIM
OFF_NA_K = OFF_NA_Q + NA_DIM
OFF_NA_V = OFF_NA_K + NA_DIM
IN_DIM = OFF_NA_V + NA_DIM
D_FF = -(-8 * D_MODEL // (3 * 256)) * 256
ROPE_THETA = 10000.0
LN_EPS = 1e-6
RMS_EPS = 1e-6
DEEPNORM_ALPHA = (2 * DEPTH) ** 0.25
LOG2E = 1.4426950408889634
N_MOD = 6
MOD_ROWS = 8
CTX_MOD_ROW = BATCH

LANES = 128
V7X_VMEM_BYTES = 64 * 1024 * 1024
V7X_VMEM_BUDGET = V7X_VMEM_BYTES - 2 * 1024 * 1024

Z_Q = 0
Z_KV = Z_Q + MLA_Q_RANK
Z_CONV_B = Z_KV + MLA_KV_RANK
Z_CONV_C = Z_CONV_B + CONV_DIM
Z_CONV_H = Z_CONV_C + CONV_DIM
Z_NA_Q = Z_CONV_H + CONV_DIM
Z_NA_K = Z_NA_Q + NA_DIM
Z_NA_V = Z_NA_K + NA_DIM
Z_ROPE = Z_NA_V + NA_DIM
Z_DIM = Z_ROPE + 2 * MLA_ROPE
V7X_MXU_COLS = 256
Z_PAD = -(-Z_DIM // (5 * V7X_MXU_COLS)) * (5 * V7X_MXU_COLS)

QK_PAD = 2 * LANES

TM = 1024
TM_OUT = 512
TM_OUT_CHUNK = 256
CONV_HALO = 16
TM_PROJ = 1024
PREP_PIECE = MLA_ROPE
PREP_COLS = 5 * LANES
TN_IN = 5 * V7X_MXU_COLS
TN_ADA = 1024
TF = 512
TQ = 256
Q_TILE_UNROLL = 8
NA_RQ = 4
NA_RK = NA_RQ + NA_WIN_R - 1
NA_NQ = NA_RQ * GRID_W
NA_NK = NA_RK * GRID_W
NA_BLOCKS = GRID_ROWS // NA_RQ
FFN_CAST_BLOCKS = 11
MLA_HEADS_PER_STEP = 2
OUT_CAST_BLOCKS = 8
NA_HEADS_PER_STEP = 2

F32 = jnp.float32
BF16 = jnp.bfloat16


def _params(semantics, vmem_bytes):
    assert vmem_bytes <= V7X_VMEM_BUDGET, vmem_bytes
    return pltpu.CompilerParams(dimension_semantics=semantics, vmem_limit_bytes=int(vmem_bytes))


def _nbytes(shape, dtype):
    return int(np.prod(shape)) * jnp.dtype(dtype).itemsize


def _dot(a, b):
    return jnp.dot(a, b, preferred_element_type=F32)


def _dot_nt(a, b):
    return lax.dot_general(a, b, (((1,), (1,)), ((), ())), preferred_element_type=F32)


def _mod_spec(chunk, row_of_tile):
    return pl.BlockSpec((None, 1, D_MODEL), lambda i, *_: (chunk + row_of_tile(i) * N_MOD, 0, 0))


def _layer_norm(r, g, b):
    mu = jnp.mean(r, axis=-1, keepdims=True)
    c = r - mu
    var = jnp.mean(c * c, axis=-1, keepdims=True)
    return c * lax.rsqrt(var + LN_EPS) * g + b


def _ada_block(cc_ref, w_ref, b_ref):
    cc = cc_ref[...]
    s = (cc * jax.nn.sigmoid(cc)).astype(BF16)
    return _dot(s, w_ref[...].astype(BF16)) + b_ref[...]


def _ada_kernel(cc_ref, w_ref, b_ref, o_ref):
    o_ref[...] = _ada_block(cc_ref, w_ref, b_ref)


def _ada(cc, ada_w, ada_b, layer):
    n = N_MOD * D_MODEL
    vmem = 2 * (_nbytes((D_MODEL, TN_ADA), F32) + _nbytes((MOD_ROWS, TN_ADA), F32) * 2
                + _nbytes((MOD_ROWS, D_MODEL), F32)) + _nbytes((D_MODEL, TN_ADA), BF16) * 2
    return pl.pallas_call(
        _ada_kernel,
        out_shape=jax.ShapeDtypeStruct((MOD_ROWS, n), F32),
        grid=(n // TN_ADA,),
        in_specs=[
            pl.BlockSpec((MOD_ROWS, D_MODEL), lambda j: (0, 0)),
            pl.BlockSpec((None, D_MODEL, TN_ADA), lambda j: (layer, 0, j)),
            pl.BlockSpec((None, 1, TN_ADA), lambda j: (layer, 0, j)),
        ],
        out_specs=pl.BlockSpec((MOD_ROWS, TN_ADA), lambda j: (0, j)),
        compiler_params=_params(("parallel",), vmem),
        name="ada_mod",
    )(cc, ada_w, ada_b)


def _in_proj_kernel(sh_ref, sc_ref, x_ref, w_ref, o_ref, xm_ref):
    @pl.when(pl.program_id(1) == 0)
    def _():
        xm_ref[...] = (x_ref[...] * (1.0 + sc_ref[...]) + sh_ref[...]).astype(BF16)

    o_ref[...] = _dot(xm_ref[...], w_ref[...]).astype(BF16)


def _in_proj(x, mods, w_in_p, layer, row_of_tile):
    rows = x.shape[0]
    vmem = (2 * (_nbytes((TM, D_MODEL), F32) + _nbytes((D_MODEL, TN_IN), BF16) + _nbytes((TM, TN_IN), BF16))
            + _nbytes((TM, D_MODEL), BF16) * 2 + _nbytes((TM, TN_IN), F32))
    return pl.pallas_call(
        _in_proj_kernel,
        out_shape=jax.ShapeDtypeStruct((rows, Z_PAD), BF16),
        grid=(rows // TM, Z_PAD // TN_IN),
        in_specs=[
            _mod_spec(0, row_of_tile),
            _mod_spec(1, row_of_tile),
            pl.BlockSpec((TM, D_MODEL), lambda i, j: (i, 0)),
            pl.BlockSpec((None, D_MODEL, TN_IN), lambda i, j: (layer, 0, j)),
        ],
        out_specs=pl.BlockSpec((TM, TN_IN), lambda i, j: (i, j)),
        scratch_shapes=[pltpu.VMEM((TM, D_MODEL), BF16)],
        compiler_params=_params(("parallel", "arbitrary"), vmem),
        name="in_proj",
    )(mods, mods, x, w_in_p)


def _rms_norm(x, g):
    return x * lax.rsqrt(jnp.mean(x * x, axis=-1, keepdims=True) + RMS_EPS) * g


def _rope_sum(pair, table, lane):
    y = pair * table
    return jnp.where(lane < MLA_ROPE, y + pltpu.roll(y, MLA_ROPE, 1), 0.0)


def _mla_proj_kernel(z_ref, slot_ref, gq_ref, gkv_ref, wq_ref, wkv_ref, ta_ref, q_ref, k_ref, v_ref):
    cq = z_ref[:, Z_Q:Z_Q + MLA_Q_RANK].astype(F32)
    ckv = z_ref[:, Z_KV:Z_KV + MLA_KV_RANK].astype(F32)
    qa = _dot(_rms_norm(cq, gq_ref[...]).astype(BF16), wq_ref[...]) * (MLA_QK ** -0.5 * LOG2E)
    kv = _dot(_rms_norm(ckv, gkv_ref[...]).astype(BF16), wkv_ref[...])
    ta = ta_ref[...]
    lane = lax.broadcasted_iota(jnp.int32, ta.shape, 1)
    kro = _rope_sum(slot_ref[...].astype(F32), ta, lane).astype(BF16)
    for h in range(MLA_HEADS):
        ob = h * QK_PAD
        q_ref[:, ob:ob + LANES] = qa[:, ob:ob + LANES].astype(BF16)
        q_ref[:, ob + LANES:ob + QK_PAD] = _rope_sum(qa[:, ob + LANES:ob + QK_PAD], ta, lane).astype(BF16)
        k_ref[:, ob:ob + LANES] = kv[:, ob:ob + LANES].astype(BF16)
        k_ref[:, ob + LANES:ob + QK_PAD] = kro
        v_ref[:, h * MLA_V:(h + 1) * MLA_V] = kv[:, ob + LANES:ob + QK_PAD].astype(BF16)


def _mla_proj(z, g_q, g_kv, wq_all, wkv, ta, layer, table_tile):
    rows = z.shape[0]
    per_layer = lambda i: (layer, 0, 0)
    zw = Z_KV + MLA_KV_RANK
    hq = MLA_HEADS * QK_PAD
    hv = MLA_HEADS * MLA_V
    tm = TM_PROJ
    vmem = (2 * (_nbytes((tm, zw), BF16) + _nbytes((tm, LANES), BF16) + _nbytes((tm, LANES), F32)
                 + _nbytes(wq_all.shape[1:], BF16) + _nbytes(wkv.shape[1:], BF16)
                 + 2 * _nbytes((tm, hq), BF16) + _nbytes((tm, hv), BF16))
            + 2 * _nbytes((tm, hq), F32) + 4 * _nbytes((tm, zw), F32))
    return pl.pallas_call(
        _mla_proj_kernel,
        out_shape=(jax.ShapeDtypeStruct((rows, hq), BF16), jax.ShapeDtypeStruct((rows, hq), BF16),
                   jax.ShapeDtypeStruct((rows, hv), BF16)),
        grid=(rows // tm,),
        in_specs=[
            pl.BlockSpec((tm, zw), lambda i: (i, 0)),
            pl.BlockSpec((tm, 2 * MLA_ROPE), lambda i: (i, Z_ROPE // (2 * MLA_ROPE))),
            pl.BlockSpec((None, 1, MLA_Q_RANK), per_layer),
            pl.BlockSpec((None, 1, MLA_KV_RANK), per_layer),
            pl.BlockSpec((None,) + wq_all.shape[1:], per_layer),
            pl.BlockSpec((None,) + wkv.shape[1:], per_layer),
            pl.BlockSpec((tm, LANES), lambda i: (table_tile(i), 0)),
        ],
        out_specs=(pl.BlockSpec((tm, hq), lambda i: (i, 0)), pl.BlockSpec((tm, hq), lambda i: (i, 0)),
                   pl.BlockSpec((tm, hv), lambda i: (i, 0))),
        compiler_params=_params(("parallel",), vmem),
        name="mla_proj",
    )(z, z, g_q, g_kv, wq_all, wkv, ta)


def _rider_specs(w, layer, n_blocks, axis, grid):
    rows, cols = w.shape[1:]
    block = (rows // n_blocks, cols) if axis == 0 else (rows, cols // n_blocks)
    assert block[0] * block[1] * n_blocks == rows * cols and n_blocks <= grid[0] * grid[1]

    def at(outer, inner):
        step = jnp.minimum(outer * grid[1] + inner, n_blocks - 1)
        return (step, 0) if axis == 0 else (0, step)

    in_spec = pl.BlockSpec((None,) + block, lambda outer, inner: (layer,) + at(outer, inner))
    return in_spec, pl.BlockSpec(block, at), jax.ShapeDtypeStruct((rows, cols), BF16), block


def _cast_riders(in_refs, out_refs):
    for i_ref, o_ref in zip(in_refs, out_refs):
        o_ref[...] = i_ref[...].astype(BF16)


def _split_refs(refs, *counts):
    out, pos = [], 0
    for c in counts:
        out.append(refs[pos:pos + c])
        pos += c
    assert pos == len(refs)
    return out


def _softmax_pv(scores, values_ones):
    m = functools.reduce(jnp.maximum, [jnp.max(s, axis=-1, keepdims=True) for s in scores])
    acc = functools.reduce(jnp.add, [_dot(jnp.exp2(s - m).astype(BF16), v) for s, v in zip(scores, values_ones)])
    return acc[:, :LANES] / acc[:, LANES:]


def _fill_values_ones(v_ref, vx_ref):
    vx_ref[:, :LANES] = v_ref[...]
    vx_ref[:, LANES:] = jnp.ones((vx_ref.shape[0], LANES), BF16)


def _scaled_q(q, head_dim):
    return (q.astype(F32) * (head_dim ** -0.5 * LOG2E)).astype(BF16)


def _mla_attn_kernel(*refs, with_ctx, n_riders):
    ins, rider_in, outs, rider_out, (vxl_ref, vxc_ref) = _split_refs(
        refs, 6 if with_ctx else 5, n_riders, 2 if with_ctx else 1, n_riders, 2)
    q_ref, kl_ref, vl_ref, kc_ref, vc_ref = ins[:5]
    o_ref = outs[0]
    _cast_riders(rider_in, rider_out)
    for hh in range(MLA_HEADS_PER_STEP):
        qk = slice(hh * QK_PAD, (hh + 1) * QK_PAD)
        vcols = slice(hh * MLA_V, (hh + 1) * MLA_V)
        _fill_values_ones(vl_ref.at[:, vcols], vxl_ref.at[hh])
        _fill_values_ones(vc_ref.at[:, vcols], vxc_ref.at[hh])

        def q_tile(t, carry, hh=hh, qk=qk, vcols=vcols):
            r0 = pl.multiple_of(t * TQ, TQ)
            q = q_ref[pl.ds(r0, TQ), qk]
            s_lat = _dot_nt(q, kl_ref[:, qk])
            s_ctx = _dot_nt(q, kc_ref[:, qk])
            o = _softmax_pv([s_lat, s_ctx], [vxl_ref[hh], vxc_ref[hh]])
            o_ref[pl.ds(r0, TQ), vcols] = o.astype(BF16)
            return carry

        lax.fori_loop(0, SEQ // TQ, q_tile, 0, unroll=Q_TILE_UNROLL)
        if with_ctx:
            s = _dot_nt(ins[5][:, qk], kc_ref[:, qk])
            outs[1][:, vcols] = _softmax_pv([s], [vxc_ref[hh]]).astype(BF16)


def _mla_attn(ql, kl, vl, qc, kc, vc, with_ctx, layer, riders):
    hps = MLA_HEADS_PER_STEP
    grid = (BATCH, MLA_HEADS // hps)
    head = lambda b, g: (b, g)
    wqk, wv = hps * QK_PAD, hps * MLA_V
    in_specs = [
        pl.BlockSpec((SEQ, wqk), head), pl.BlockSpec((SEQ, wqk), head), pl.BlockSpec((SEQ, wv), head),
        pl.BlockSpec((CTX_LEN, wqk), head), pl.BlockSpec((CTX_LEN, wv), head),
    ]
    args = [ql, kl, vl, kc, vc]
    out_shape = [jax.ShapeDtypeStruct((BATCH * SEQ, MLA_HEADS * MLA_V), BF16)]
    out_specs = [pl.BlockSpec((SEQ, wv), head)]
    if with_ctx:
        in_specs.append(pl.BlockSpec((CTX_LEN, wqk), head))
        args.append(qc)
        out_shape.append(jax.ShapeDtypeStruct((BATCH * CTX_LEN, MLA_HEADS * MLA_V), BF16))
        out_specs.append(pl.BlockSpec((CTX_LEN, wv), head))
    vmem = (2 * (3 * _nbytes((SEQ, wqk), BF16) + 2 * _nbytes((SEQ, wv), BF16))
            + 2 * hps * _nbytes((SEQ, 2 * LANES), BF16) + Q_TILE_UNROLL * _nbytes((TQ, SEQ + CTX_LEN), BF16))
    n_base = len(out_shape)
    for w, n_blocks, axis in riders:
        in_spec, out_spec, shape, block = _rider_specs(w, layer, n_blocks, axis, grid)
        in_specs.append(in_spec)
        args.append(w)
        out_specs.append(out_spec)
        out_shape.append(shape)
        vmem += 2 * (_nbytes(block, F32) + _nbytes(block, BF16))
    res = pl.pallas_call(
        functools.partial(_mla_attn_kernel, with_ctx=with_ctx, n_riders=len(riders)),
        out_shape=tuple(out_shape), grid=grid, in_specs=in_specs, out_specs=tuple(out_specs),
        scratch_shapes=[pltpu.VMEM((hps, SEQ, 2 * LANES), BF16), pltpu.VMEM((hps, CTX_LEN, 2 * LANES), BF16)],
        compiler_params=_params(("arbitrary", "arbitrary"), vmem),
        name="mla_attn_ctx" if with_ctx else "mla_attn",
    )(*args)
    return res[:n_base], res[n_base:]


def _na_band_start(block):
    return min(max(block * NA_RQ - NA_WIN_R // 2, 0), GRID_ROWS - NA_RK)


def _na_bias_kind(block):
    return 0 if block == 0 else (2 if block == NA_BLOCKS - 1 else 1)


def _na_row_offsets():
    n_dr = 2 * NA_WIN_R - 1
    idx = np.full((3, NA_RQ, NA_RK), n_dr, np.int32)
    for kind, block in ((0, 0), (1, 2), (2, NA_BLOCKS - 1)):
        for qr in range(NA_RQ):
            r = block * NA_RQ + qr
            r0 = min(max(r - NA_WIN_R // 2, 0), GRID_ROWS - NA_WIN_R)
            for ki in range(NA_RK):
                kr = _na_band_start(block) + ki
                if r0 <= kr < r0 + NA_WIN_R:
                    idx[kind, qr, ki] = kr - r + NA_WIN_R - 1
    return idx


def _na_bias_kernel(w_ref, o_ref):
    shape = (GRID_W, LANES)
    c = lax.broadcasted_iota(jnp.int32, shape, 0)
    kc = lax.broadcasted_iota(jnp.int32, shape, 1)
    c0 = jnp.clip(c - NA_WIN_C // 2, 0, GRID_W - NA_WIN_C)
    col_ok = (kc >= c0) & (kc < c0 + NA_WIN_C)
    neg = jnp.full((GRID_W, GRID_W), -jnp.inf, F32)
    offsets = _na_row_offsets()
    n_dr = 2 * NA_WIN_R - 1
    tiles = {n_dr: neg}
    for d in sorted(set(int(v) for v in offsets.reshape(-1)) - {n_dr}):
        row = jnp.broadcast_to(w_ref[d:d + 1, :], shape)
        skew = pltpu.roll(row, LANES - (GRID_W - 1), 1, stride=1, stride_axis=0)
        tiles[d] = jnp.where(col_ok, skew, -jnp.inf)[:, :GRID_W]
    for kind in range(3):
        for qr in range(NA_RQ):
            for ki in range(NA_RK):
                o_ref[kind, qr * GRID_W:(qr + 1) * GRID_W, ki * GRID_W:(ki + 1) * GRID_W] = (
                    tiles[int(offsets[kind, qr, ki])])


def _na_bias(rpb):
    n_dr, n_dc = 2 * NA_WIN_R - 1, 2 * NA_WIN_C - 1
    lo = GRID_W - 1 - (NA_WIN_C - 1)
    w = jnp.pad(rpb.reshape(DEPTH * NA_HEADS, n_dr, n_dc) * LOG2E, ((0, 0), (0, 0), (lo, LANES - lo - n_dc)),
                constant_values=-jnp.inf)
    vmem = 2 * (_nbytes((2 * 8, LANES), F32) + _nbytes((3, NA_NQ, NA_NK + GRID_W), F32)) + 16 * _nbytes(
        (GRID_W, LANES), F32)
    return pl.pallas_call(
        _na_bias_kernel,
        out_shape=jax.ShapeDtypeStruct((w.shape[0], 3, NA_NQ, NA_NK), F32),
        grid=(w.shape[0],),
        in_specs=[pl.BlockSpec((None, n_dr, LANES), lambda g: (g, 0, 0))],
        out_specs=pl.BlockSpec((None, 3, NA_NQ, NA_NK), lambda g: (g, 0, 0, 0)),
        compiler_params=_params(("parallel",), vmem),
        name="na_bias",
    )(w)


def _na_kernel(*refs, with_ctx, n_riders):
    ins, rider_in, outs, rider_out, (vx_ref, vxc_ref) = _split_refs(
        refs, 7 if with_ctx else 6, n_riders, 2 if with_ctx else 1, n_riders, 2)
    q_ref, k_ref, v_ref, kc_ref, vc_ref, bias_ref = ins[:6]
    o_ref = outs[0]
    _cast_riders(rider_in, rider_out)
    d = NA_HEAD_DIM
    for hh in range(NA_HEADS_PER_STEP):
        cols = slice(hh * d, (hh + 1) * d)
        _fill_values_ones(v_ref.at[:, cols], vx_ref.at[hh])
        _fill_values_ones(vc_ref.at[:, cols], vxc_ref.at[hh])
        for block in range(NA_BLOCKS):
            rows = slice(block * NA_NQ, (block + 1) * NA_NQ)
            q = _scaled_q(q_ref[rows, cols], d)
            k0 = _na_band_start(block) * GRID_W
            s_loc = _dot_nt(q, k_ref[k0:k0 + NA_NK, cols]) + bias_ref[hh, _na_bias_kind(block)]
            s_ctx = _dot_nt(q, kc_ref[:, cols])
            o = _softmax_pv([s_loc, s_ctx], [vx_ref[hh, k0:k0 + NA_NK, :], vxc_ref[hh]])
            o_ref[rows, cols] = o.astype(BF16)
        if with_ctx:
            s = _dot_nt(_scaled_q(ins[6][:, cols], d), kc_ref[:, cols])
            outs[1][:, cols] = _softmax_pv([s], [vxc_ref[hh]]).astype(BF16)


def _na_attn(zl, zc, bias, layer, with_ctx, riders):
    hps = NA_HEADS_PER_STEP
    w = hps * NA_HEAD_DIM
    steps = NA_HEADS // hps
    col = lambda off: (lambda g, b: (b, off // w + g))
    in_specs = [
        pl.BlockSpec((SEQ, w), col(Z_NA_Q)), pl.BlockSpec((SEQ, w), col(Z_NA_K)), pl.BlockSpec((SEQ, w), col(Z_NA_V)),
        pl.BlockSpec((CTX_LEN, w), col(Z_NA_K)), pl.BlockSpec((CTX_LEN, w), col(Z_NA_V)),
        pl.BlockSpec((hps, 3, NA_NQ, NA_NK), lambda g, b: (layer * steps + g, 0, 0, 0)),
    ]
    args = [zl, zl, zl, zc, zc, bias]
    out_shape = [jax.ShapeDtypeStruct((BATCH * SEQ, NA_DIM), BF16)]
    out_specs = [pl.BlockSpec((SEQ, w), lambda g, b: (b, g))]
    if with_ctx:
        in_specs.append(pl.BlockSpec((CTX_LEN, w), col(Z_NA_Q)))
        args.append(zc)
        out_shape.append(jax.ShapeDtypeStruct((BATCH * CTX_LEN, NA_DIM), BF16))
        out_specs.append(pl.BlockSpec((CTX_LEN, w), lambda g, b: (b, g)))
    vmem = (2 * (5 * _nbytes((SEQ, w), BF16) + _nbytes((hps, 3, NA_NQ, NA_NK), F32))
            + 2 * hps * _nbytes((SEQ, 2 * LANES), BF16) + 4 * hps * _nbytes((NA_NQ, NA_NK + CTX_LEN), F32))
    n_base = len(out_shape)
    for wt, n_blocks, axis in riders:
        in_spec, out_spec, shape, block = _rider_specs(wt, layer, n_blocks, axis, (steps, BATCH))
        in_specs.append(in_spec)
        args.append(wt)
        out_specs.append(out_spec)
        out_shape.append(shape)
        vmem += 2 * (_nbytes(block, F32) + _nbytes(block, BF16))
    res = pl.pallas_call(
        functools.partial(_na_kernel, with_ctx=with_ctx, n_riders=len(riders)),
        out_shape=tuple(out_shape), grid=(steps, BATCH), in_specs=in_specs, out_specs=tuple(out_specs),
        scratch_shapes=[pltpu.VMEM((hps, SEQ, 2 * LANES), BF16), pltpu.VMEM((hps, CTX_LEN, 2 * LANES), BF16)],
        compiler_params=_params(("arbitrary", "arbitrary"), vmem),
        name="na_attn_ctx" if with_ctx else "na_attn",
    )(*args)
    return res[:n_base], res[n_base:]


def _out_proj_kernel(g_ref, a_ref, n_ref, x_ref, gb_ref, gc_ref, h_ref, gcp_ref, hp_ref, gcn_ref, hn_ref, cw_ref,
                     w_ref, lng_ref, lnb_ref, *rest, seq_len):
    if len(rest) == 2:
        o_ref, mix_ref = rest
    else:
        *ada_in, o_ref, ada_out, mix_ref = rest
        ada_out[...] = _ada_block(*ada_in)
    tm = o_ref.shape[0]
    u = gc_ref[...].astype(F32) * h_ref[...].astype(F32)
    u_before = gcp_ref[CONV_HALO - 1:, :].astype(F32) * hp_ref[CONV_HALO - 1:, :].astype(F32)
    u_after = gcn_ref[:1, :].astype(F32) * hn_ref[:1, :].astype(F32)
    row = lax.broadcasted_iota(jnp.int32, u.shape, 0)
    pos = (pl.program_id(0) * tm + row) % seq_len
    u_prev = jnp.where(row == 0, u_before, pltpu.roll(u, 1, 0))
    u_prev = jnp.where(pos == 0, 0.0, u_prev)
    u_next = jnp.where(row == tm - 1, u_after, pltpu.roll(u, tm - 1, 0))
    u_next = jnp.where(pos == seq_len - 1, 0.0, u_next)
    y = u_prev * cw_ref[0:1, :] + u * cw_ref[1:2, :] + u_next * cw_ref[2:3, :]
    a_w = MLA_HEADS * MLA_V
    mix_ref[:, :a_w] = a_ref[...]
    mix_ref[:, a_w:a_w + CONV_DIM] = (gb_ref[...].astype(F32) * y).astype(BF16)
    mix_ref[:, a_w + CONV_DIM:] = n_ref[...]

    for r0 in range(0, tm, TM_OUT_CHUNK):
        rows = slice(r0, r0 + TM_OUT_CHUNK)
        r = DEEPNORM_ALPHA * x_ref[rows, :] + g_ref[...] * _dot(mix_ref[rows, :], w_ref[...])
        o_ref[rows, :] = _layer_norm(r, lng_ref[...], lnb_ref[...])


def _out_proj(a, n, z, x, mods, conv_w, w_out, ln_g, ln_b, layer, row_of_tile, seq_len, next_ada=None):
    rows = x.shape[0]
    tm = TM_OUT
    steps = rows // tm
    assert tm % seq_len == 0 or seq_len % tm == 0
    row = lambda i: (i, 0)
    per_layer = lambda i: (layer, 0, 0)
    conv_blk = lambda off: pl.BlockSpec((tm, CONV_DIM), lambda i: (i, off // CONV_DIM))
    halo = tm // CONV_HALO
    before = lambda off: pl.BlockSpec((CONV_HALO, CONV_DIM), lambda i: (jnp.maximum(i * halo - 1, 0), off // CONV_DIM))
    after = lambda off: pl.BlockSpec(
        (CONV_HALO, CONV_DIM), lambda i: (jnp.minimum((i + 1) * halo, rows // CONV_HALO - 1), off // CONV_DIM))
    vmem = (2 * (2 * _nbytes((tm, D_MODEL), F32) + _nbytes((tm, D_MODEL), BF16) + _nbytes((tm, CONV_DIM), BF16))
            + _nbytes((tm, D_MODEL), BF16) + _nbytes(w_out.shape, BF16)
            + 3 * _nbytes((TM_OUT_CHUNK, D_MODEL), F32) + 4 * _nbytes((tm, CONV_DIM), F32))
    in_specs = [
        _mod_spec(2, row_of_tile),
        pl.BlockSpec((tm, a.shape[1]), row), pl.BlockSpec((tm, n.shape[1]), row), pl.BlockSpec((tm, D_MODEL), row),
        conv_blk(Z_CONV_B), conv_blk(Z_CONV_C), conv_blk(Z_CONV_H),
        before(Z_CONV_C), before(Z_CONV_H), after(Z_CONV_C), after(Z_CONV_H),
        pl.BlockSpec((None,) + conv_w.shape[1:], per_layer),
        pl.BlockSpec(w_out.shape, lambda i: (0, 0), pipeline_mode=pl.Buffered(1)),
        pl.BlockSpec((None, 1, D_MODEL), per_layer), pl.BlockSpec((None, 1, D_MODEL), per_layer),
    ]
    args = [mods, a, n, x, z, z, z, z, z, z, z, conv_w, w_out, ln_g, ln_b]
    out_shape = [jax.ShapeDtypeStruct((rows, D_MODEL), F32)]
    out_specs = [pl.BlockSpec((tm, D_MODEL), row)]
    if next_ada is not None:
        n_blocks = N_MOD * D_MODEL // TN_ADA
        assert n_blocks <= steps
        blk = lambda i: jnp.minimum(i, n_blocks - 1)
        in_specs += [
            pl.BlockSpec((MOD_ROWS, D_MODEL), lambda i: (0, 0)),
            pl.BlockSpec((None, D_MODEL, TN_ADA), lambda i: (layer + 1, 0, blk(i))),
            pl.BlockSpec((None, 1, TN_ADA), lambda i: (layer + 1, 0, blk(i))),
        ]
        args += list(next_ada)
        out_shape.append(jax.ShapeDtypeStruct((MOD_ROWS, N_MOD * D_MODEL), F32))
        out_specs.append(pl.BlockSpec((MOD_ROWS, TN_ADA), lambda i: (0, blk(i))))
        vmem += 2 * _nbytes((D_MODEL, TN_ADA), F32) + _nbytes((D_MODEL, TN_ADA), BF16)
    res = pl.pallas_call(
        functools.partial(_out_proj_kernel, seq_len=seq_len),
        out_shape=tuple(out_shape), grid=(steps,), in_specs=in_specs, out_specs=tuple(out_specs),
        scratch_shapes=[pltpu.VMEM((tm, D_MODEL), BF16)],
        compiler_params=_params(("arbitrary",), vmem),
        name="out_proj",
    )(*args)
    return res if next_ada is not None else res[0]


def _ffn_kernel(sh_ref, sc_ref, g_ref, x_ref, wg_ref, wu_ref, wd_ref, lng_ref, lnb_ref, o_ref, xm_ref):
    f = pl.program_id(1)

    @pl.when(f == 0)
    def _():
        x = x_ref[...]
        xm_ref[...] = (x * (1.0 + sc_ref[...]) + sh_ref[...]).astype(BF16)
        o_ref[...] = DEEPNORM_ALPHA * x

    xm = xm_ref[...]
    gate = _dot(xm, wg_ref[...])
    up = _dot(xm, wu_ref[...])
    hidden = (gate * jax.nn.sigmoid(gate) * up).astype(BF16)
    o_ref[...] += g_ref[...] * _dot(hidden, wd_ref[...])

    @pl.when(f == pl.num_programs(1) - 1)
    def _():
        o_ref[...] = _layer_norm(o_ref[...], lng_ref[...], lnb_ref[...])


def _ffn(x, mods, w_gate, w_up, w_down, ln_g, ln_b, layer, row_of_tile):
    rows = x.shape[0]
    per_layer = lambda i, f: (layer, 0, 0)
    vmem = (4 * _nbytes((TM, D_MODEL), F32) + 2 * 3 * _nbytes((D_MODEL, TF), BF16)
            + _nbytes((TM, D_MODEL), BF16) + 3 * _nbytes((TM, TF), F32) + _nbytes((TM, D_MODEL), F32))
    return pl.pallas_call(
        _ffn_kernel,
        out_shape=jax.ShapeDtypeStruct((rows, D_MODEL), F32),
        grid=(rows // TM, D_FF // TF),
        in_specs=[
            _mod_spec(3, row_of_tile), _mod_spec(4, row_of_tile), _mod_spec(5, row_of_tile),
            pl.BlockSpec((TM, D_MODEL), lambda i, f: (i, 0)),
            pl.BlockSpec((D_MODEL, TF), lambda i, f: (0, f)),
            pl.BlockSpec((D_MODEL, TF), lambda i, f: (0, f)),
            pl.BlockSpec((TF, D_MODEL), lambda i, f: (f, 0)),
            pl.BlockSpec((None, 1, D_MODEL), per_layer), pl.BlockSpec((None, 1, D_MODEL), per_layer),
        ],
        out_specs=pl.BlockSpec((TM, D_MODEL), lambda i, f: (i, 0)),
        scratch_shapes=[pltpu.VMEM((TM, D_MODEL), BF16)],
        compiler_params=_params(("parallel", "arbitrary"), vmem),
        name="ffn",
    )(mods, mods, mods, x, w_gate, w_up, w_down, ln_g, ln_b)


def _rotate_half_cols(w):
    a1, a2, b1, b2 = jnp.split(w, 4, axis=-1)
    return jnp.concatenate([-a2, a1, -b2, b1], axis=-1)


def _w_in_prep_src(piece):
    rope = OFF_ROPE // PREP_PIECE
    tail = Z_ROPE // PREP_PIECE
    return jnp.where(piece < rope, piece, jnp.where(piece < tail, piece + 1, rope))


def _w_in_prep_kernel(*refs):
    *piece_refs, o_ref = refs
    pieces = [r[...] for r in piece_refs]
    n = len(pieces)
    is_last = pl.program_id(1) == pl.num_programs(1) - 1
    rot_piece = Z_DIM // PREP_PIECE - 1
    for p in range(n):
        piece = (Z_PAD // PREP_COLS - 1) * n + p
        if piece == rot_piece:
            a1, a2, b1, b2 = jnp.split(pieces[p], 4, axis=0)
            pieces[p] = jnp.where(is_last, jnp.concatenate([-a2, a1, -b2, b1], axis=0), pieces[p])
        elif piece > rot_piece:
            pieces[p] = jnp.where(is_last, 0.0, pieces[p])
    o_ref[...] = jnp.concatenate(pieces, axis=0).T.astype(BF16)


def _w_in_prep(w_in):
    w_t = jnp.swapaxes(w_in, 1, 2)
    n = PREP_COLS // PREP_PIECE
    piece_spec = lambda p: pl.BlockSpec((None, PREP_PIECE, D_MODEL), lambda l, g: (l, _w_in_prep_src(g * n + p), 0))
    vmem = 2 * (_nbytes((PREP_COLS, D_MODEL), F32) + _nbytes((D_MODEL, PREP_COLS), BF16)) + 3 * _nbytes(
        (PREP_COLS, D_MODEL), F32)
    return pl.pallas_call(
        _w_in_prep_kernel,
        out_shape=jax.ShapeDtypeStruct((DEPTH, D_MODEL, Z_PAD), BF16),
        grid=(DEPTH, Z_PAD // PREP_COLS),
        in_specs=[piece_spec(p) for p in range(n)],
        out_specs=pl.BlockSpec((None, D_MODEL, PREP_COLS), lambda l, g: (l, 0, g)),
        compiler_params=_params(("parallel", "parallel"), vmem),
        name="w_in_prep",
    )(*([w_t] * n))


def _relayout_wq(w):
    w = w.reshape(DEPTH, MLA_Q_RANK, MLA_HEADS, MLA_QK)
    rope = w[..., MLA_NOPE:]
    cols = jnp.concatenate([w, _rotate_half_cols(rope)], axis=-1)
    return cols.reshape(DEPTH, MLA_Q_RANK, MLA_HEADS * QK_PAD).astype(BF16)


def _rope_tables():
    t = jnp.arange(SEQ)
    row = (t // GRID_W).astype(F32)
    col = (t % GRID_W).astype(F32)
    n_freq = MLA_ROPE // 4
    inv = ROPE_THETA ** (-jnp.arange(n_freq, dtype=F32) / n_freq)
    ar = row[:, None] * inv
    ac = col[:, None] * inv
    ang = jnp.concatenate([ar, ar, ac, ac], axis=-1)
    lat = jnp.concatenate([jnp.cos(ang), jnp.sin(ang)], axis=1)
    ctx = jnp.concatenate([jnp.ones((TM_PROJ, MLA_ROPE), F32), jnp.zeros((TM_PROJ, MLA_ROPE), F32)], axis=1)
    return lat, ctx


def kernel(x, c, ctx, c_ctx, ada_w, ada_b, w_in, mla_q_norm, mla_wq_b, mla_kv_norm, mla_wkv_b, conv_w, na_rpb,
           w_out, ln1_g, ln1_b, ffn_w_gate, ffn_w_up, ffn_w_down, ln2_g, ln2_b):
    assert x.shape == (BATCH, SEQ, D_MODEL) and ctx.shape == (BATCH, CTX_LEN, D_MODEL)
    cc = jnp.concatenate([c, c_ctx[None, :], jnp.zeros((MOD_ROWS - BATCH - 1, D_MODEL), F32)], axis=0)
    ada_b = ada_b.reshape(DEPTH, 1, N_MOD * D_MODEL)
    as_table = lambda m: m.reshape(MOD_ROWS * N_MOD, 1, D_MODEL)
    mods = as_table(_ada(cc, ada_w, ada_b, 0))
    ta_lat, ta_ctx = _rope_tables()
    rows3d = lambda v: v.reshape(DEPTH, 1, -1)

    w_in_p = _w_in_prep(w_in)
    wq_all = _relayout_wq(mla_wq_b)
    wkv = mla_wkv_b.astype(BF16)
    g_q, g_kv = rows3d(mla_q_norm), rows3d(mla_kv_norm)
    bias = _na_bias(na_rpb)
    ln1 = (rows3d(ln1_g), rows3d(ln1_b))
    ln2 = (rows3d(ln2_g), rows3d(ln2_b))

    xl = x.reshape(BATCH * SEQ, D_MODEL)
    xc = ctx.reshape(BATCH * CTX_LEN, D_MODEL)
    lat_row = lambda tm: (lambda i: i // (SEQ // tm))
    ctx_row = lambda i: CTX_MOD_ROW
    for l in range(DEPTH):
        last = l == DEPTH - 1
        zl = _in_proj(xl, mods, w_in_p, l, lat_row(TM))
        zc = _in_proj(xc, mods, w_in_p, l, ctx_row)
        ql, kl, vl = _mla_proj(zl, g_q, g_kv, wq_all, wkv, ta_lat, l, lambda i: i % (SEQ // TM_PROJ))
        qc, kc, vc = _mla_proj(zc, g_q, g_kv, wq_all, wkv, ta_ctx, l, lambda i: 0)
        a, (w_g, w_u) = _mla_attn(ql, kl, vl, qc, kc, vc, not last, l,
                                  [(ffn_w_gate, FFN_CAST_BLOCKS, 1), (ffn_w_up, FFN_CAST_BLOCKS, 1)])
        n, (w_d, w_o) = _na_attn(zl, zc, bias, l, not last,
                                 [(ffn_w_down, FFN_CAST_BLOCKS, 0), (w_out, OUT_CAST_BLOCKS, 0)])
        if last:
            x1 = _out_proj(a[0], n[0], zl, xl, mods, conv_w, w_o, *ln1, l, lat_row(TM_OUT), SEQ)
        else:
            x1, next_mods = _out_proj(a[0], n[0], zl, xl, mods, conv_w, w_o, *ln1, l, lat_row(TM_OUT), SEQ,
                                      next_ada=(cc, ada_w, ada_b))
        xl = _ffn(x1, mods, w_g, w_u, w_d, *ln2, l, lat_row(TM))
        if not last:
            x1c = _out_proj(a[1], n[1], zc, xc, mods, conv_w, w_o, *ln1, l, ctx_row, CTX_LEN)
            xc = _ffn(x1c, mods, w_g, w_u, w_d, *ln2, l, ctx_row)
            mods = as_table(next_mods)
    return xl.reshape(BATCH, SEQ, D_MODEL)
```

```python
import functools

import numpy as np
import jax
import jax.numpy as jnp
from jax import lax
from jax.experimental import pallas as pl
from jax.experimental.pallas import tpu as pltpu

D_MODEL = 2048
BATCH = 4
SEQ = 2048
DEPTH = 2
CTX_LEN = 256
GRID_W = 64
GRID_ROWS = SEQ // GRID_W
MLA_HEADS = 6
MLA_Q_RANK = 512
MLA_KV_RANK = 512
MLA_NOPE = 128
MLA_ROPE = 64
MLA_V = 128
MLA_QK = MLA_NOPE + MLA_ROPE
CONV_DIM = 512
NA_HEADS = 6
NA_HEAD_DIM = 128
NA_DIM = NA_HEADS * NA_HEAD_DIM
NA_WIN_R = 8
NA_WIN_C = 16
OFF_MLA_Q = 0
OFF_MLA_KV = OFF_MLA_Q + MLA_Q_RANK
OFF_ROPE = OFF_MLA_KV + MLA_KV_RANK
OFF_CONV_B = OFF_ROPE + MLA_ROPE
OFF_CONV_C = OFF_CONV_B + CONV_DIM
OFF_CONV_H = OFF_CONV_C + CONV_DIM
OFF_NA_Q = OFF_CONV_H + CONV_DIM
OFF_NA_K = OFF_NA_Q + NA_DIM
OFF_NA_V = OFF_NA_K + NA_DIM
IN_DIM = OFF_NA_V + NA_DIM
D_FF = -(-8 * D_MODEL // (3 * 256)) * 256
ROPE_THETA = 10000.0
LN_EPS = 1e-6
RMS_EPS = 1e-6
DEEPNORM_ALPHA = (2 * DEPTH) ** 0.25
LOG2E = 1.4426950408889634
N_MOD = 6
MOD_ROWS = 8
CTX_MOD_ROW = BATCH

LANES = 128
V7X_VMEM_BYTES = 64 * 1024 * 1024
V7X_VMEM_BUDGET = V7X_VMEM_BYTES - 2 * 1024 * 1024

Z_Q = 0
Z_KV = Z_Q + MLA_Q_RANK
Z_CONV_B = Z_KV + MLA_KV_RANK
Z_CONV_C = Z_CONV_B + CONV_DIM
Z_CONV_H = Z_CONV_C + CONV_DIM
Z_NA_Q = Z_CONV_H + CONV_DIM
Z_NA_K = Z_NA_Q + NA_DIM
Z_NA_V = Z_NA_K + NA_DIM
Z_ROPE = Z_NA_V + NA_DIM
Z_DIM = Z_ROPE + 2 * MLA_ROPE
V7X_MXU_COLS = 256
Z_PAD = -(-Z_DIM // (5 * V7X_MXU_COLS)) * (5 * V7X_MXU_COLS)

QK_PAD = 2 * LANES

TM = 1024
TM_OUT = 512
TM_OUT_CHUNK = 256
CONV_HALO = 16
TM_PROJ = 512
PREP_PIECE = MLA_ROPE
PREP_COLS = 5 * LANES
TN_IN = 5 * V7X_MXU_COLS
TN_ADA = 1024
TF = 512
TQ = 256
Q_TILE_UNROLL = 8
NA_RQ = 4
NA_RK = NA_RQ + NA_WIN_R - 1
NA_NQ = NA_RQ * GRID_W
NA_NK = NA_RK * GRID_W
NA_BLOCKS = GRID_ROWS // NA_RQ
FFN_CAST_BLOCKS = 11
MLA_HEADS_PER_STEP = 2
OUT_CAST_BLOCKS = 8
NA_HEADS_PER_STEP = 2

F32 = jnp.float32
BF16 = jnp.bfloat16


def _params(semantics, vmem_bytes):
    assert vmem_bytes <= V7X_VMEM_BUDGET, vmem_bytes
    return pltpu.CompilerParams(dimension_semantics=semantics, vmem_limit_bytes=int(vmem_bytes))


def _nbytes(shape, dtype):
    return int(np.prod(shape)) * jnp.dtype(dtype).itemsize


def _dot(a, b):
    return jnp.dot(a, b, preferred_element_type=F32)


def _dot_nt(a, b):
    return lax.dot_general(a, b, (((1,), (1,)), ((), ())), preferred_element_type=F32)


def _mod_spec(chunk, row_of_tile):
    return pl.BlockSpec((None, 1, D_MODEL), lambda i, *_: (chunk + row_of_tile(i) * N_MOD, 0, 0))


def _layer_norm(r, g, b):
    mu = jnp.mean(r, axis=-1, keepdims=True)
    c = r - mu
    var = jnp.mean(c * c, axis=-1, keepdims=True)
    return c * lax.rsqrt(var + LN_EPS) * g + b


def _ada_block(cc_ref, w_ref, b_ref):
    cc = cc_ref[...]
    s = (cc * jax.nn.sigmoid(cc)).astype(BF16)
    return _dot(s, w_ref[...].astype(BF16)) + b_ref[...]


def _ada_kernel(cc_ref, w_ref, b_ref, o_ref):
    o_ref[...] = _ada_block(cc_ref, w_ref, b_ref)


def _ada(cc, ada_w, ada_b, layer):
    n = N_MOD * D_MODEL
    vmem = 2 * (_nbytes((D_MODEL, TN_ADA), F32) + _nbytes((MOD_ROWS, TN_ADA), F32) * 2
                + _nbytes((MOD_ROWS, D_MODEL), F32)) + _nbytes((D_MODEL, TN_ADA), BF16) * 2
    return pl.pallas_call(
        _ada_kernel,
        out_shape=jax.ShapeDtypeStruct((MOD_ROWS, n), F32),
        grid=(n // TN_ADA,),
        in_specs=[
            pl.BlockSpec((MOD_ROWS, D_MODEL), lambda j: (0, 0)),
            pl.BlockSpec((None, D_MODEL, TN_ADA), lambda j: (layer, 0, j)),
            pl.BlockSpec((None, 1, TN_ADA), lambda j: (layer, 0, j)),
        ],
        out_specs=pl.BlockSpec((MOD_ROWS, TN_ADA), lambda j: (0, j)),
        compiler_params=_params(("parallel",), vmem),
        name="ada_mod",
    )(cc, ada_w, ada_b)


def _in_proj_kernel(sh_ref, sc_ref, x_ref, w_ref, o_ref, xm_ref):
    @pl.when(pl.program_id(1) == 0)
    def _():
        xm_ref[...] = (x_ref[...] * (1.0 + sc_ref[...]) + sh_ref[...]).astype(BF16)

    o_ref[...] = _dot(xm_ref[...], w_ref[...]).astype(BF16)


def _in_proj(x, mods, w_in_p, layer, row_of_tile):
    rows = x.shape[0]
    vmem = (2 * (_nbytes((TM, D_MODEL), F32) + _nbytes((D_MODEL, TN_IN), BF16) + _nbytes((TM, TN_IN), BF16))
            + _nbytes((TM, D_MODEL), BF16) * 2 + _nbytes((TM, TN_IN), F32))
    return pl.pallas_call(
        _in_proj_kernel,
        out_shape=jax.ShapeDtypeStruct((rows, Z_PAD), BF16),
        grid=(rows // TM, Z_PAD // TN_IN),
        in_specs=[
            _mod_spec(0, row_of_tile),
            _mod_spec(1, row_of_tile),
            pl.BlockSpec((TM, D_MODEL), lambda i, j: (i, 0)),
            pl.BlockSpec((None, D_MODEL, TN_IN), lambda i, j: (layer, 0, j)),
        ],
        out_specs=pl.BlockSpec((TM, TN_IN), lambda i, j: (i, j)),
        scratch_shapes=[pltpu.VMEM((TM, D_MODEL), BF16)],
        compiler_params=_params(("parallel", "arbitrary"), vmem),
        name="in_proj",
    )(mods, mods, x, w_in_p)


def _rms_norm(x, g):
    return x * lax.rsqrt(jnp.mean(x * x, axis=-1, keepdims=True) + RMS_EPS) * g


def _rope_sum(pair, table, lane):
    y = pair * table
    return jnp.where(lane < MLA_ROPE, y + pltpu.roll(y, MLA_ROPE, 1), 0.0)


def _mla_proj_kernel(z_ref, slot_ref, gq_ref, gkv_ref, wq_ref, wkv_ref, ta_ref, q_ref, k_ref, v_ref):
    cq = z_ref[:, Z_Q:Z_Q + MLA_Q_RANK].astype(F32)
    ckv = z_ref[:, Z_KV:Z_KV + MLA_KV_RANK].astype(F32)
    qa = _dot(_rms_norm(cq, gq_ref[...]).astype(BF16), wq_ref[...]) * (MLA_QK ** -0.5 * LOG2E)
    kv = _dot(_rms_norm(ckv, gkv_ref[...]).astype(BF16), wkv_ref[...])
    ta = ta_ref[...]
    lane = lax.broadcasted_iota(jnp.int32, ta.shape, 1)
    kro = _rope_sum(slot_ref[...].astype(F32), ta, lane).astype(BF16)
    for h in range(MLA_HEADS):
        ob = h * QK_PAD
        q_ref[:, ob:ob + LANES] = qa[:, ob:ob + LANES].astype(BF16)
        q_ref[:, ob + LANES:ob + QK_PAD] = _rope_sum(qa[:, ob + LANES:ob + QK_PAD], ta, lane).astype(BF16)
        k_ref[:, ob:ob + LANES] = kv[:, ob:ob + LANES].astype(BF16)
        k_ref[:, ob + LANES:ob + QK_PAD] = kro
        v_ref[:, h * MLA_V:(h + 1) * MLA_V] = kv[:, ob + LANES:ob + QK_PAD].astype(BF16)


def _mla_proj(z, g_q, g_kv, wq_all, wkv, ta, layer, table_tile):
    rows = z.shape[0]
    per_layer = lambda i: (layer, 0, 0)
    zw = Z_KV + MLA_KV_RANK
    hq = MLA_HEADS * QK_PAD
    hv = MLA_HEADS * MLA_V
    tm = TM_PROJ
    vmem = (2 * (_nbytes((tm, zw), BF16) + _nbytes((tm, LANES), BF16) + _nbytes((tm, LANES), F32)
                 + _nbytes(wq_all.shape[1:], BF16) + _nbytes(wkv.shape[1:], BF16)
                 + 2 * _nbytes((tm, hq), BF16) + _nbytes((tm, hv), BF16))
            + 2 * _nbytes((tm, hq), F32) + 4 * _nbytes((tm, zw), F32))
    return pl.pallas_call(
        _mla_proj_kernel,
        out_shape=(jax.ShapeDtypeStruct((rows, hq), BF16), jax.ShapeDtypeStruct((rows, hq), BF16),
                   jax.ShapeDtypeStruct((rows, hv), BF16)),
        grid=(rows // tm,),
        in_specs=[
            pl.BlockSpec((tm, zw), lambda i: (i, 0)),
            pl.BlockSpec((tm, 2 * MLA_ROPE), lambda i: (i, Z_ROPE // (2 * MLA_ROPE))),
            pl.BlockSpec((None, 1, MLA_Q_RANK), per_layer),
            pl.BlockSpec((None, 1, MLA_KV_RANK), per_layer),
            pl.BlockSpec((None,) + wq_all.shape[1:], per_layer),
            pl.BlockSpec((None,) + wkv.shape[1:], per_layer),
            pl.BlockSpec((tm, LANES), lambda i: (table_tile(i), 0)),
        ],
        out_specs=(pl.BlockSpec((tm, hq), lambda i: (i, 0)), pl.BlockSpec((tm, hq), lambda i: (i, 0)),
                   pl.BlockSpec((tm, hv), lambda i: (i, 0))),
        compiler_params=_params(("parallel",), vmem),
        name="mla_proj",
    )(z, z, g_q, g_kv, wq_all, wkv, ta)


def _rider_specs(w, layer, n_blocks, axis, grid):
    rows, cols = w.shape[1:]
    block = (rows // n_blocks, cols) if axis == 0 else (rows, cols // n_blocks)
    assert block[0] * block[1] * n_blocks == rows * cols and n_blocks <= grid[0] * grid[1]

    def at(outer, inner):
        step = jnp.minimum(outer * grid[1] + inner, n_blocks - 1)
        return (step, 0) if axis == 0 else (0, step)

    in_spec = pl.BlockSpec((None,) + block, lambda outer, inner: (layer,) + at(outer, inner))
    return in_spec, pl.BlockSpec(block, at), jax.ShapeDtypeStruct((rows, cols), BF16), block


def _cast_riders(in_refs, out_refs):
    for i_ref, o_ref in zip(in_refs, out_refs):
        o_ref[...] = i_ref[...].astype(BF16)


def _split_refs(refs, *counts):
    out, pos = [], 0
    for c in counts:
        out.append(refs[pos:pos + c])
        pos += c
    assert pos == len(refs)
    return out


def _softmax_pv(scores, values_ones):
    m = functools.reduce(jnp.maximum, [jnp.max(s, axis=-1, keepdims=True) for s in scores])
    acc = functools.reduce(jnp.add, [_dot(jnp.exp2(s - m).astype(BF16), v) for s, v in zip(scores, values_ones)])
    return acc[:, :LANES] / acc[:, LANES:]


def _fill_values_ones(v_ref, vx_ref):
    vx_ref[:, :LANES] = v_ref[...]
    vx_ref[:, LANES:] = jnp.ones((vx_ref.shape[0], LANES), BF16)


def _scaled_q(q, head_dim):
    return (q.astype(F32) * (head_dim ** -0.5 * LOG2E)).astype(BF16)


def _mla_attn_kernel(*refs, with_ctx, n_riders):
    ins, rider_in, outs, rider_out, (vxl_ref, vxc_ref) = _split_refs(
        refs, 6 if with_ctx else 5, n_riders, 2 if with_ctx else 1, n_riders, 2)
    q_ref, kl_ref, vl_ref, kc_ref, vc_ref = ins[:5]
    o_ref = outs[0]
    _cast_riders(rider_in, rider_out)
    for hh in range(MLA_HEADS_PER_STEP):
        qk = slice(hh * QK_PAD, (hh + 1) * QK_PAD)
        vcols = slice(hh * MLA_V, (hh + 1) * MLA_V)
        _fill_values_ones(vl_ref.at[:, vcols], vxl_ref.at[hh])
        _fill_values_ones(vc_ref.at[:, vcols], vxc_ref.at[hh])

        def q_tile(t, carry, hh=hh, qk=qk, vcols=vcols):
            r0 = pl.multiple_of(t * TQ, TQ)
            q = q_ref[pl.ds(r0, TQ), qk]
            s_lat = _dot_nt(q, kl_ref[:, qk])
            s_ctx = _dot_nt(q, kc_ref[:, qk])
            o = _softmax_pv([s_lat, s_ctx], [vxl_ref[hh], vxc_ref[hh]])
            o_ref[pl.ds(r0, TQ), vcols] = o.astype(BF16)
            return carry

        lax.fori_loop(0, SEQ // TQ, q_tile, 0, unroll=Q_TILE_UNROLL)
        if with_ctx:
            s = _dot_nt(ins[5][:, qk], kc_ref[:, qk])
            outs[1][:, vcols] = _softmax_pv([s], [vxc_ref[hh]]).astype(BF16)


def _mla_attn(ql, kl, vl, qc, kc, vc, with_ctx, layer, riders):
    hps = MLA_HEADS_PER_STEP
    grid = (BATCH, MLA_HEADS // hps)
    head = lambda b, g: (b, g)
    wqk, wv = hps * QK_PAD, hps * MLA_V
    in_specs = [
        pl.BlockSpec((SEQ, wqk), head), pl.BlockSpec((SEQ, wqk), head), pl.BlockSpec((SEQ, wv), head),
        pl.BlockSpec((CTX_LEN, wqk), head), pl.BlockSpec((CTX_LEN, wv), head),
    ]
    args = [ql, kl, vl, kc, vc]
    out_shape = [jax.ShapeDtypeStruct((BATCH * SEQ, MLA_HEADS * MLA_V), BF16)]
    out_specs = [pl.BlockSpec((SEQ, wv), head)]
    if with_ctx:
        in_specs.append(pl.BlockSpec((CTX_LEN, wqk), head))
        args.append(qc)
        out_shape.append(jax.ShapeDtypeStruct((BATCH * CTX_LEN, MLA_HEADS * MLA_V), BF16))
        out_specs.append(pl.BlockSpec((CTX_LEN, wv), head))
    vmem = (2 * (3 * _nbytes((SEQ, wqk), BF16) + 2 * _nbytes((SEQ, wv), BF16))
            + 2 * hps * _nbytes((SEQ, 2 * LANES), BF16) + Q_TILE_UNROLL * _nbytes((TQ, SEQ + CTX_LEN), BF16))
    n_base = len(out_shape)
    for w, n_blocks, axis in riders:
        in_spec, out_spec, shape, block = _rider_specs(w, layer, n_blocks, axis, grid)
        in_specs.append(in_spec)
        args.append(w)
        out_specs.append(out_spec)
        out_shape.append(shape)
        vmem += 2 * (_nbytes(block, F32) + _nbytes(block, BF16))
    res = pl.pallas_call(
        functools.partial(_mla_attn_kernel, with_ctx=with_ctx, n_riders=len(riders)),
        out_shape=tuple(out_shape), grid=grid, in_specs=in_specs, out_specs=tuple(out_specs),
        scratch_shapes=[pltpu.VMEM((hps, SEQ, 2 * LANES), BF16), pltpu.VMEM((hps, CTX_LEN, 2 * LANES), BF16)],
        compiler_params=_params(("arbitrary", "arbitrary"), vmem),
        name="mla_attn_ctx" if with_ctx else "mla_attn",
    )(*args)
    return res[:n_base], res[n_base:]


def _na_band_start(block):
    return min(max(block * NA_RQ - NA_WIN_R // 2, 0), GRID_ROWS - NA_RK)


def _na_bias_kind(block):
    return 0 if block == 0 else (2 if block == NA_BLOCKS - 1 else 1)


def _na_row_offsets():
    n_dr = 2 * NA_WIN_R - 1
    idx = np.full((3, NA_RQ, NA_RK), n_dr, np.int32)
    for kind, block in ((0, 0), (1, 2), (2, NA_BLOCKS - 1)):
        for qr in range(NA_RQ):
            r = block * NA_RQ + qr
            r0 = min(max(r - NA_WIN_R // 2, 0), GRID_ROWS - NA_WIN_R)
            for ki in range(NA_RK):
                kr = _na_band_start(block) + ki
                if r0 <= kr < r0 + NA_WIN_R:
                    idx[kind, qr, ki] = kr - r + NA_WIN_R - 1
    return idx


def _na_bias_kernel(w_ref, o_ref):
    shape = (GRID_W, LANES)
    c = lax.broadcasted_iota(jnp.int32, shape, 0)
    kc = lax.broadcasted_iota(jnp.int32, shape, 1)
    c0 = jnp.clip(c - NA_WIN_C // 2, 0, GRID_W - NA_WIN_C)
    col_ok = (kc >= c0) & (kc < c0 + NA_WIN_C)
    neg = jnp.full((GRID_W, GRID_W), -jnp.inf, F32)
    offsets = _na_row_offsets()
    n_dr = 2 * NA_WIN_R - 1
    tiles = {n_dr: neg}
    for d in sorted(set(int(v) for v in offsets.reshape(-1)) - {n_dr}):
        row = jnp.broadcast_to(w_ref[d:d + 1, :], shape)
        skew = pltpu.roll(row, LANES - (GRID_W - 1), 1, stride=1, stride_axis=0)
        tiles[d] = jnp.where(col_ok, skew, -jnp.inf)[:, :GRID_W]
    for kind in range(3):
        for qr in range(NA_RQ):
            for ki in range(NA_RK):
                o_ref[kind, qr * GRID_W:(qr + 1) * GRID_W, ki * GRID_W:(ki + 1) * GRID_W] = (
                    tiles[int(offsets[kind, qr, ki])])


def _na_bias(rpb):
    n_dr, n_dc = 2 * NA_WIN_R - 1, 2 * NA_WIN_C - 1
    lo = GRID_W - 1 - (NA_WIN_C - 1)
    w = jnp.pad(rpb.reshape(DEPTH * NA_HEADS, n_dr, n_dc) * LOG2E, ((0, 0), (0, 0), (lo, LANES - lo - n_dc)),
                constant_values=-jnp.inf)
    vmem = 2 * (_nbytes((2 * 8, LANES), F32) + _nbytes((3, NA_NQ, NA_NK + GRID_W), F32)) + 16 * _nbytes(
        (GRID_W, LANES), F32)
    return pl.pallas_call(
        _na_bias_kernel,
        out_shape=jax.ShapeDtypeStruct((w.shape[0], 3, NA_NQ, NA_NK), F32),
        grid=(w.shape[0],),
        in_specs=[pl.BlockSpec((None, n_dr, LANES), lambda g: (g, 0, 0))],
        out_specs=pl.BlockSpec((None, 3, NA_NQ, NA_NK), lambda g: (g, 0, 0, 0)),
        compiler_params=_params(("parallel",), vmem),
        name="na_bias",
    )(w)


def _na_kernel(*refs, with_ctx, n_riders):
    ins, rider_in, outs, rider_out, (vx_ref, vxc_ref) = _split_refs(
        refs, 7 if with_ctx else 6, n_riders, 2 if with_ctx else 1, n_riders, 2)
    q_ref, k_ref, v_ref, kc_ref, vc_ref, bias_ref = ins[:6]
    o_ref = outs[0]
    _cast_riders(rider_in, rider_out)
    d = NA_HEAD_DIM
    for hh in range(NA_HEADS_PER_STEP):
        cols = slice(hh * d, (hh + 1) * d)
        _fill_values_ones(v_ref.at[:, cols], vx_ref.at[hh])
        _fill_values_ones(vc_ref.at[:, cols], vxc_ref.at[hh])
        for block in range(NA_BLOCKS):
            rows = slice(block * NA_NQ, (block + 1) * NA_NQ)
            q = _scaled_q(q_ref[rows, cols], d)
            k0 = _na_band_start(block) * GRID_W
            s_loc = _dot_nt(q, k_ref[k0:k0 + NA_NK, cols]) + bias_ref[hh, _na_bias_kind(block)]
            s_ctx = _dot_nt(q, kc_ref[:, cols])
            o = _softmax_pv([s_loc, s_ctx], [vx_ref[hh, k0:k0 + NA_NK, :], vxc_ref[hh]])
            o_ref[rows, cols] = o.astype(BF16)
        if with_ctx:
            s = _dot_nt(_scaled_q(ins[6][:, cols], d), kc_ref[:, cols])
            outs[1][:, cols] = _softmax_pv([s], [vxc_ref[hh]]).astype(BF16)


def _na_attn(zl, zc, bias, layer, with_ctx, riders):
    hps = NA_HEADS_PER_STEP
    w = hps * NA_HEAD_DIM
    steps = NA_HEADS // hps
    col = lambda off: (lambda g, b: (b, off // w + g))
    in_specs = [
        pl.BlockSpec((SEQ, w), col(Z_NA_Q)), pl.BlockSpec((SEQ, w), col(Z_NA_K)), pl.BlockSpec((SEQ, w), col(Z_NA_V)),
        pl.BlockSpec((CTX_LEN, w), col(Z_NA_K)), pl.BlockSpec((CTX_LEN, w), col(Z_NA_V)),
        pl.BlockSpec((hps, 3, NA_NQ, NA_NK), lambda g, b: (layer * steps + g, 0, 0, 0)),
    ]
    args = [zl, zl, zl, zc, zc, bias]
    out_shape = [jax.ShapeDtypeStruct((BATCH * SEQ, NA_DIM), BF16)]
    out_specs = [pl.BlockSpec((SEQ, w), lambda g, b: (b, g))]
    if with_ctx:
        in_specs.append(pl.BlockSpec((CTX_LEN, w), col(Z_NA_Q)))
        args.append(zc)
        out_shape.append(jax.ShapeDtypeStruct((BATCH * CTX_LEN, NA_DIM), BF16))
        out_specs.append(pl.BlockSpec((CTX_LEN, w), lambda g, b: (b, g)))
    vmem = (2 * (5 * _nbytes((SEQ, w), BF16) + _nbytes((hps, 3, NA_NQ, NA_NK), F32))
            + 2 * hps * _nbytes((SEQ, 2 * LANES), BF16) + 4 * hps * _nbytes((NA_NQ, NA_NK + CTX_LEN), F32))
    n_base = len(out_shape)
    for wt, n_blocks, axis in riders:
        in_spec, out_spec, shape, block = _rider_specs(wt, layer, n_blocks, axis, (steps, BATCH))
        in_specs.append(in_spec)
        args.append(wt)
        out_specs.append(out_spec)
        out_shape.append(shape)
        vmem += 2 * (_nbytes(block, F32) + _nbytes(block, BF16))
    res = pl.pallas_call(
        functools.partial(_na_kernel, with_ctx=with_ctx, n_riders=len(riders)),
        out_shape=tuple(out_shape), grid=(steps, BATCH), in_specs=in_specs, out_specs=tuple(out_specs),
        scratch_shapes=[pltpu.VMEM((hps, SEQ, 2 * LANES), BF16), pltpu.VMEM((hps, CTX_LEN, 2 * LANES), BF16)],
        compiler_params=_params(("arbitrary", "arbitrary"), vmem),
        name="na_attn_ctx" if with_ctx else "na_attn",
    )(*args)
    return res[:n_base], res[n_base:]


def _out_proj_kernel(g_ref, a_ref, n_ref, x_ref, gb_ref, gc_ref, h_ref, gcp_ref, hp_ref, gcn_ref, hn_ref, cw_ref,
                     w_ref, lng_ref, lnb_ref, *rest, seq_len):
    if len(rest) == 2:
        o_ref, mix_ref = rest
    else:
        *ada_in, o_ref, ada_out, mix_ref = rest
        ada_out[...] = _ada_block(*ada_in)
    tm = o_ref.shape[0]
    u = gc_ref[...].astype(F32) * h_ref[...].astype(F32)
    u_before = gcp_ref[CONV_HALO - 1:, :].astype(F32) * hp_ref[CONV_HALO - 1:, :].astype(F32)
    u_after = gcn_ref[:1, :].astype(F32) * hn_ref[:1, :].astype(F32)
    row = lax.broadcasted_iota(jnp.int32, u.shape, 0)
    pos = (pl.program_id(0) * tm + row) % seq_len
    u_prev = jnp.where(row == 0, u_before, pltpu.roll(u, 1, 0))
    u_prev = jnp.where(pos == 0, 0.0, u_prev)
    u_next = jnp.where(row == tm - 1, u_after, pltpu.roll(u, tm - 1, 0))
    u_next = jnp.where(pos == seq_len - 1, 0.0, u_next)
    y = u_prev * cw_ref[0:1, :] + u * cw_ref[1:2, :] + u_next * cw_ref[2:3, :]
    a_w = MLA_HEADS * MLA_V
    mix_ref[:, :a_w] = a_ref[...]
    mix_ref[:, a_w:a_w + CONV_DIM] = (gb_ref[...].astype(F32) * y).astype(BF16)
    mix_ref[:, a_w + CONV_DIM:] = n_ref[...]

    for r0 in range(0, tm, TM_OUT_CHUNK):
        rows = slice(r0, r0 + TM_OUT_CHUNK)
        r = DEEPNORM_ALPHA * x_ref[rows, :] + g_ref[...] * _dot(mix_ref[rows, :], w_ref[...])
        o_ref[rows, :] = _layer_norm(r, lng_ref[...], lnb_ref[...])


def _out_proj(a, n, z, x, mods, conv_w, w_out, ln_g, ln_b, layer, row_of_tile, seq_len, next_ada=None):
    rows = x.shape[0]
    tm = TM_OUT
    steps = rows // tm
    assert tm % seq_len == 0 or seq_len % tm == 0
    row = lambda i: (i, 0)
    per_layer = lambda i: (layer, 0, 0)
    conv_blk = lambda off: pl.BlockSpec((tm, CONV_DIM), lambda i: (i, off // CONV_DIM))
    halo = tm // CONV_HALO
    before = lambda off: pl.BlockSpec((CONV_HALO, CONV_DIM), lambda i: (jnp.maximum(i * halo - 1, 0), off // CONV_DIM))
    after = lambda off: pl.BlockSpec(
        (CONV_HALO, CONV_DIM), lambda i: (jnp.minimum((i + 1) * halo, rows // CONV_HALO - 1), off // CONV_DIM))
    vmem = (2 * (2 * _nbytes((tm, D_MODEL), F32) + _nbytes((tm, D_MODEL), BF16) + _nbytes((tm, CONV_DIM), BF16))
            + _nbytes((tm, D_MODEL), BF16) + _nbytes(w_out.shape, BF16)
            + 3 * _nbytes((TM_OUT_CHUNK, D_MODEL), F32) + 4 * _nbytes((tm, CONV_DIM), F32))
    in_specs = [
        _mod_spec(2, row_of_tile),
        pl.BlockSpec((tm, a.shape[1]), row), pl.BlockSpec((tm, n.shape[1]), row), pl.BlockSpec((tm, D_MODEL), row),
        conv_blk(Z_CONV_B), conv_blk(Z_CONV_C), conv_blk(Z_CONV_H),
        before(Z_CONV_C), before(Z_CONV_H), after(Z_CONV_C), after(Z_CONV_H),
        pl.BlockSpec((None,) + conv_w.shape[1:], per_layer),
        pl.BlockSpec(w_out.shape, lambda i: (0, 0), pipeline_mode=pl.Buffered(1)),
        pl.BlockSpec((None, 1, D_MODEL), per_layer), pl.BlockSpec((None, 1, D_MODEL), per_layer),
    ]
    args = [mods, a, n, x, z, z, z, z, z, z, z, conv_w, w_out, ln_g, ln_b]
    out_shape = [jax.ShapeDtypeStruct((rows, D_MODEL), F32)]
    out_specs = [pl.BlockSpec((tm, D_MODEL), row)]
    if next_ada is not None:
        tn = N_MOD * D_MODEL // steps
        assert tn % LANES == 0 and tn * steps == N_MOD * D_MODEL
        in_specs += [
            pl.BlockSpec((MOD_ROWS, D_MODEL), lambda i: (0, 0)),
            pl.BlockSpec((None, D_MODEL, tn), lambda i: (layer + 1, 0, i)),
            pl.BlockSpec((None, 1, tn), lambda i: (layer + 1, 0, i)),
        ]
        args += list(next_ada)
        out_shape.append(jax.ShapeDtypeStruct((MOD_ROWS, N_MOD * D_MODEL), F32))
        out_specs.append(pl.BlockSpec((MOD_ROWS, tn), lambda i: (0, i)))
        vmem += 2 * _nbytes((D_MODEL, tn), F32) + _nbytes((D_MODEL, tn), BF16)
    res = pl.pallas_call(
        functools.partial(_out_proj_kernel, seq_len=seq_len),
        out_shape=tuple(out_shape), grid=(steps,), in_specs=in_specs, out_specs=tuple(out_specs),
        scratch_shapes=[pltpu.VMEM((tm, D_MODEL), BF16)],
        compiler_params=_params(("arbitrary",), vmem),
        name="out_proj",
    )(*args)
    return res if next_ada is not None else res[0]


def _ffn_kernel(sh_ref, sc_ref, g_ref, x_ref, wg_ref, wu_ref, wd_ref, lng_ref, lnb_ref, o_ref, xm_ref):
    f = pl.program_id(1)

    @pl.when(f == 0)
    def _():
        x = x_ref[...]
        xm_ref[...] = (x * (1.0 + sc_ref[...]) + sh_ref[...]).astype(BF16)
        o_ref[...] = DEEPNORM_ALPHA * x

    xm = xm_ref[...]
    gate = _dot(xm, wg_ref[...])
    up = _dot(xm, wu_ref[...])
    hidden = (gate * jax.nn.sigmoid(gate) * up).astype(BF16)
    o_ref[...] += g_ref[...] * _dot(hidden, wd_ref[...])

    @pl.when(f == pl.num_programs(1) - 1)
    def _():
        o_ref[...] = _layer_norm(o_ref[...], lng_ref[...], lnb_ref[...])


def _ffn(x, mods, w_gate, w_up, w_down, ln_g, ln_b, layer, row_of_tile):
    rows = x.shape[0]
    per_layer = lambda i, f: (layer, 0, 0)
    vmem = (4 * _nbytes((TM, D_MODEL), F32) + 2 * 3 * _nbytes((D_MODEL, TF), BF16)
            + _nbytes((TM, D_MODEL), BF16) + 3 * _nbytes((TM, TF), F32) + _nbytes((TM, D_MODEL), F32))
    return pl.pallas_call(
        _ffn_kernel,
        out_shape=jax.ShapeDtypeStruct((rows, D_MODEL), F32),
        grid=(rows // TM, D_FF // TF),
        in_specs=[
            _mod_spec(3, row_of_tile), _mod_spec(4, row_of_tile), _mod_spec(5, row_of_tile),
            pl.BlockSpec((TM, D_MODEL), lambda i, f: (i, 0)),
            pl.BlockSpec((D_MODEL, TF), lambda i, f: (0, f)),
            pl.BlockSpec((D_MODEL, TF), lambda i, f: (0, f)),
            pl.BlockSpec((TF, D_MODEL), lambda i, f: (f, 0)),
            pl.BlockSpec((None, 1, D_MODEL), per_layer), pl.BlockSpec((None, 1, D_MODEL), per_layer),
        ],
        out_specs=pl.BlockSpec((TM, D_MODEL), lambda i, f: (i, 0)),
        scratch_shapes=[pltpu.VMEM((TM, D_MODEL), BF16)],
        compiler_params=_params(("parallel", "arbitrary"), vmem),
        name="ffn",
    )(mods, mods, mods, x, w_gate, w_up, w_down, ln_g, ln_b)


def _rotate_half_cols(w):
    a1, a2, b1, b2 = jnp.split(w, 4, axis=-1)
    return jnp.concatenate([-a2, a1, -b2, b1], axis=-1)


def _w_in_prep_src(piece):
    rope = OFF_ROPE // PREP_PIECE
    tail = Z_ROPE // PREP_PIECE
    return jnp.where(piece < rope, piece, jnp.where(piece < tail, piece + 1, rope))


def _w_in_prep_kernel(*refs):
    *piece_refs, o_ref = refs
    pieces = [r[...] for r in piece_refs]
    n = len(pieces)
    is_last = pl.program_id(1) == pl.num_programs(1) - 1
    rot_piece = Z_DIM // PREP_PIECE - 1
    for p in range(n):
        piece = (Z_PAD // PREP_COLS - 1) * n + p
        if piece == rot_piece:
            a1, a2, b1, b2 = jnp.split(pieces[p], 4, axis=0)
            pieces[p] = jnp.where(is_last, jnp.concatenate([-a2, a1, -b2, b1], axis=0), pieces[p])
        elif piece > rot_piece:
            pieces[p] = jnp.where(is_last, 0.0, pieces[p])
    o_ref[...] = jnp.concatenate(pieces, axis=0).T.astype(BF16)


def _w_in_prep(w_in):
    w_t = jnp.swapaxes(w_in, 1, 2)
    n = PREP_COLS // PREP_PIECE
    piece_spec = lambda p: pl.BlockSpec((None, PREP_PIECE, D_MODEL), lambda l, g: (l, _w_in_prep_src(g * n + p), 0))
    vmem = 2 * (_nbytes((PREP_COLS, D_MODEL), F32) + _nbytes((D_MODEL, PREP_COLS), BF16)) + 3 * _nbytes(
        (PREP_COLS, D_MODEL), F32)
    return pl.pallas_call(
        _w_in_prep_kernel,
        out_shape=jax.ShapeDtypeStruct((DEPTH, D_MODEL, Z_PAD), BF16),
        grid=(DEPTH, Z_PAD // PREP_COLS),
        in_specs=[piece_spec(p) for p in range(n)],
        out_specs=pl.BlockSpec((None, D_MODEL, PREP_COLS), lambda l, g: (l, 0, g)),
        compiler_params=_params(("parallel", "parallel"), vmem),
        name="w_in_prep",
    )(*([w_t] * n))


def _relayout_wq(w):
    w = w.reshape(DEPTH, MLA_Q_RANK, MLA_HEADS, MLA_QK)
    rope = w[..., MLA_NOPE:]
    cols = jnp.concatenate([w, _rotate_half_cols(rope)], axis=-1)
    return cols.reshape(DEPTH, MLA_Q_RANK, MLA_HEADS * QK_PAD).astype(BF16)


def _rope_tables():
    t = jnp.arange(SEQ)
    row = (t // GRID_W).astype(F32)
    col = (t % GRID_W).astype(F32)
    n_freq = MLA_ROPE // 4
    inv = ROPE_THETA ** (-jnp.arange(n_freq, dtype=F32) / n_freq)
    ar = row[:, None] * inv
    ac = col[:, None] * inv
    ang = jnp.concatenate([ar, ar, ac, ac], axis=-1)
    lat = jnp.concatenate([jnp.cos(ang), jnp.sin(ang)], axis=1)
    ctx = jnp.concatenate([jnp.ones((TM_PROJ, MLA_ROPE), F32), jnp.zeros((TM_PROJ, MLA_ROPE), F32)], axis=1)
    return lat, ctx


def kernel(x, c, ctx, c_ctx, ada_w, ada_b, w_in, mla_q_norm, mla_wq_b, mla_kv_norm, mla_wkv_b, conv_w, na_rpb,
           w_out, ln1_g, ln1_b, ffn_w_gate, ffn_w_up, ffn_w_down, ln2_g, ln2_b):
    assert x.shape == (BATCH, SEQ, D_MODEL) and ctx.shape == (BATCH, CTX_LEN, D_MODEL)
    cc = jnp.concatenate([c, c_ctx[None, :], jnp.zeros((MOD_ROWS - BATCH - 1, D_MODEL), F32)], axis=0)
    ada_b = ada_b.reshape(DEPTH, 1, N_MOD * D_MODEL)
    as_table = lambda m: m.reshape(MOD_ROWS * N_MOD, 1, D_MODEL)
    mods = as_table(_ada(cc, ada_w, ada_b, 0))
    ta_lat, ta_ctx = _rope_tables()
    rows3d = lambda v: v.reshape(DEPTH, 1, -1)

    w_in_p = _w_in_prep(w_in)
    wq_all = _relayout_wq(mla_wq_b)
    wkv = mla_wkv_b.astype(BF16)
    g_q, g_kv = rows3d(mla_q_norm), rows3d(mla_kv_norm)
    bias = _na_bias(na_rpb)
    ln1 = (rows3d(ln1_g), rows3d(ln1_b))
    ln2 = (rows3d(ln2_g), rows3d(ln2_b))

    xl = x.reshape(BATCH * SEQ, D_MODEL)
    xc = ctx.reshape(BATCH * CTX_LEN, D_MODEL)
    lat_row = lambda tm: (lambda i: i // (SEQ // tm))
    ctx_row = lambda i: CTX_MOD_ROW
    for l in range(DEPTH):
        last = l == DEPTH - 1
        zl = _in_proj(xl, mods, w_in_p, l, lat_row(TM))
        zc = _in_proj(xc, mods, w_in_p, l, ctx_row)
        ql, kl, vl = _mla_proj(zl, g_q, g_kv, wq_all, wkv, ta_lat, l, lambda i: i % (SEQ // TM_PROJ))
        qc, kc, vc = _mla_proj(zc, g_q, g_kv, wq_all, wkv, ta_ctx, l, lambda i: 0)
        a, (w_g, w_u) = _mla_attn(ql, kl, vl, qc, kc, vc, not last, l,
                                  [(ffn_w_gate, FFN_CAST_BLOCKS, 1), (ffn_w_up, FFN_CAST_BLOCKS, 1)])
        n, (w_d, w_o) = _na_attn(zl, zc, bias, l, not last,
                                 [(ffn_w_down, FFN_CAST_BLOCKS, 0), (w_out, OUT_CAST_BLOCKS, 0)])
        if last:
            x1 = _out_proj(a[0], n[0], zl, xl, mods, conv_w, w_o, *ln1, l, lat_row(TM_OUT), SEQ)
        else:
            x1, next_mods = _out_proj(a[0], n[0], zl, xl, mods, conv_w, w_o, *ln1, l, lat_row(TM_OUT), SEQ,
                                      next_ada=(cc, ada_w, ada_b))
        xl = _ffn(x1, mods, w_g, w_u, w_d, *ln2, l, lat_row(TM))
        if not last:
            x1c = _out_proj(a[1], n[1], zc, xc, mods, conv_w, w_o, *ln1, l, ctx_row, CTX_LEN)
            xc = _ffn(x1c, mods, w_g, w_u, w_d, *ln2, l, ctx_row)
            mods = as_table(next_mods)
    return xl.reshape(BATCH, SEQ, D_MODEL)
```

```python
import functools

import numpy as np
import jax
import jax.numpy as jnp
from jax import lax
from jax.experimental import pallas as pl
from jax.experimental.pallas import tpu as pltpu

D_MODEL = 2048
BATCH = 4
SEQ = 2048
DEPTH = 2
CTX_LEN = 256
GRID_W = 64
GRID_ROWS = SEQ // GRID_W
MLA_HEADS = 6
MLA_Q_RANK = 512
MLA_KV_RANK = 512
MLA_NOPE = 128
MLA_ROPE = 64
MLA_V = 128
MLA_QK = MLA_NOPE + MLA_ROPE
CONV_DIM = 512
NA_HEADS = 6
NA_HEAD_DIM = 128
NA_DIM = NA_HEADS * NA_HEAD_DIM
NA_WIN_R = 8
NA_WIN_C = 16
OFF_MLA_Q = 0
OFF_MLA_KV = OFF_MLA_Q + MLA_Q_RANK
OFF_ROPE = OFF_MLA_KV + MLA_KV_RANK
OFF_CONV_B = OFF_ROPE + MLA_ROPE
OFF_CONV_C = OFF_CONV_B + CONV_DIM
OFF_CONV_H = OFF_CONV_C + CONV_DIM
OFF_NA_Q = OFF_CONV_H + CONV_DIM
OFF_NA_K = OFF_NA_Q + NA_DIM
OFF_NA_V = OFF_NA_K + NA_DIM
IN_DIM = OFF_NA_V + NA_DIM
D_FF = -(-8 * D_MODEL // (3 * 256)) * 256
ROPE_THETA = 10000.0
LN_EPS = 1e-6
RMS_EPS = 1e-6
DEEPNORM_ALPHA = (2 * DEPTH) ** 0.25
LOG2E = 1.4426950408889634
N_MOD = 6
MOD_ROWS = 8
CTX_MOD_ROW = BATCH

LANES = 128
V7X_VMEM_BYTES = 64 * 1024 * 1024
V7X_VMEM_BUDGET = V7X_VMEM_BYTES - 2 * 1024 * 1024

Z_Q = 0
Z_KV = Z_Q + MLA_Q_RANK
Z_CONV_B = Z_KV + MLA_KV_RANK
Z_CONV_C = Z_CONV_B + CONV_DIM
Z_CONV_H = Z_CONV_C + CONV_DIM
Z_NA_Q = Z_CONV_H + CONV_DIM
Z_NA_K = Z_NA_Q + NA_DIM
Z_NA_V = Z_NA_K + NA_DIM
Z_ROPE = Z_NA_V + NA_DIM
Z_DIM = Z_ROPE + 2 * MLA_ROPE
V7X_MXU_COLS = 256
Z_PAD = -(-Z_DIM // (5 * V7X_MXU_COLS)) * (5 * V7X_MXU_COLS)

QK_PAD = 2 * LANES

TM = 1024
TM_OUT = 512
TM_OUT_CHUNK = 256
CONV_HALO = 16
TM_PROJ = 512
PREP_PIECE = MLA_ROPE
PREP_COLS = 5 * LANES
TN_IN = 5 * V7X_MXU_COLS
TN_ADA = 1024
TF = 512
TQ = 256
Q_TILE_UNROLL = 8
NA_RQ = 4
NA_RK = NA_RQ + NA_WIN_R - 1
NA_NQ = NA_RQ * GRID_W
NA_NK = NA_RK * GRID_W
NA_BLOCKS = GRID_ROWS // NA_RQ
FFN_CAST_BLOCKS = 11
MLA_HEADS_PER_STEP = 2
OUT_CAST_BLOCKS = 8
NA_HEADS_PER_STEP = 2

F32 = jnp.float32
BF16 = jnp.bfloat16


def _params(semantics, vmem_bytes):
    assert vmem_bytes <= V7X_VMEM_BUDGET, vmem_bytes
    return pltpu.CompilerParams(dimension_semantics=semantics, vmem_limit_bytes=V7X_VMEM_BUDGET)


def _nbytes(shape, dtype):
    return int(np.prod(shape)) * jnp.dtype(dtype).itemsize


def _dot(a, b):
    return jnp.dot(a, b, preferred_element_type=F32)


def _dot_nt(a, b):
    return lax.dot_general(a, b, (((1,), (1,)), ((), ())), preferred_element_type=F32)


def _mod_spec(chunk, row_of_tile):
    return pl.BlockSpec((None, 1, D_MODEL), lambda i, *_: (chunk + row_of_tile(i) * N_MOD, 0, 0))


def _layer_norm(r, g, b):
    mu = jnp.mean(r, axis=-1, keepdims=True)
    c = r - mu
    var = jnp.mean(c * c, axis=-1, keepdims=True)
    return c * lax.rsqrt(var + LN_EPS) * g + b


def _ada_block(cc_ref, w_ref, b_ref):
    cc = cc_ref[...]
    s = (cc * jax.nn.sigmoid(cc)).astype(BF16)
    return _dot(s, w_ref[...].astype(BF16)) + b_ref[...]


def _ada_kernel(cc_ref, w_ref, b_ref, o_ref):
    o_ref[...] = _ada_block(cc_ref, w_ref, b_ref)


def _ada(cc, ada_w, ada_b, layer):
    n = N_MOD * D_MODEL
    vmem = 2 * (_nbytes((D_MODEL, TN_ADA), F32) + _nbytes((MOD_ROWS, TN_ADA), F32) * 2
                + _nbytes((MOD_ROWS, D_MODEL), F32)) + _nbytes((D_MODEL, TN_ADA), BF16) * 2
    return pl.pallas_call(
        _ada_kernel,
        out_shape=jax.ShapeDtypeStruct((MOD_ROWS, n), F32),
        grid=(n // TN_ADA,),
        in_specs=[
            pl.BlockSpec((MOD_ROWS, D_MODEL), lambda j: (0, 0)),
            pl.BlockSpec((None, D_MODEL, TN_ADA), lambda j: (layer, 0, j)),
            pl.BlockSpec((None, 1, TN_ADA), lambda j: (layer, 0, j)),
        ],
        out_specs=pl.BlockSpec((MOD_ROWS, TN_ADA), lambda j: (0, j)),
        compiler_params=_params(("parallel",), vmem),
        name="ada_mod",
    )(cc, ada_w, ada_b)


def _in_proj_kernel(sh_ref, sc_ref, x_ref, w_ref, o_ref, xm_ref):
    @pl.when(pl.program_id(1) == 0)
    def _():
        xm_ref[...] = (x_ref[...] * (1.0 + sc_ref[...]) + sh_ref[...]).astype(BF16)

    o_ref[...] = _dot(xm_ref[...], w_ref[...]).astype(BF16)


def _in_proj(x, mods, w_in_p, layer, row_of_tile):
    rows = x.shape[0]
    vmem = (2 * (_nbytes((TM, D_MODEL), F32) + _nbytes((D_MODEL, TN_IN), BF16) + _nbytes((TM, TN_IN), BF16))
            + _nbytes((TM, D_MODEL), BF16) * 2 + _nbytes((TM, TN_IN), F32))
    return pl.pallas_call(
        _in_proj_kernel,
        out_shape=jax.ShapeDtypeStruct((rows, Z_PAD), BF16),
        grid=(rows // TM, Z_PAD // TN_IN),
        in_specs=[
            _mod_spec(0, row_of_tile),
            _mod_spec(1, row_of_tile),
            pl.BlockSpec((TM, D_MODEL), lambda i, j: (i, 0)),
            pl.BlockSpec((None, D_MODEL, TN_IN), lambda i, j: (layer, 0, j)),
        ],
        out_specs=pl.BlockSpec((TM, TN_IN), lambda i, j: (i, j)),
        scratch_shapes=[pltpu.VMEM((TM, D_MODEL), BF16)],
        compiler_params=_params(("parallel", "arbitrary"), vmem),
        name="in_proj",
    )(mods, mods, x, w_in_p)


def _rms_norm(x, g):
    return x * lax.rsqrt(jnp.mean(x * x, axis=-1, keepdims=True) + RMS_EPS) * g


def _rope_sum(pair, table, lane):
    y = pair * table
    return jnp.where(lane < MLA_ROPE, y + pltpu.roll(y, MLA_ROPE, 1), 0.0)


def _mla_proj_kernel(z_ref, slot_ref, gq_ref, gkv_ref, wq_ref, wkv_ref, ta_ref, q_ref, k_ref, v_ref):
    cq = z_ref[:, Z_Q:Z_Q + MLA_Q_RANK].astype(F32)
    ckv = z_ref[:, Z_KV:Z_KV + MLA_KV_RANK].astype(F32)
    qa = _dot(_rms_norm(cq, gq_ref[...]).astype(BF16), wq_ref[...]) * (MLA_QK ** -0.5 * LOG2E)
    kv = _dot(_rms_norm(ckv, gkv_ref[...]).astype(BF16), wkv_ref[...])
    ta = ta_ref[...]
    lane = lax.broadcasted_iota(jnp.int32, ta.shape, 1)
    kro = _rope_sum(slot_ref[...].astype(F32), ta, lane).astype(BF16)
    for h in range(MLA_HEADS):
        ob = h * QK_PAD
        q_ref[:, ob:ob + LANES] = qa[:, ob:ob + LANES].astype(BF16)
        q_ref[:, ob + LANES:ob + QK_PAD] = _rope_sum(qa[:, ob + LANES:ob + QK_PAD], ta, lane).astype(BF16)
        k_ref[:, ob:ob + LANES] = kv[:, ob:ob + LANES].astype(BF16)
        k_ref[:, ob + LANES:ob + QK_PAD] = kro
        v_ref[:, h * MLA_V:(h + 1) * MLA_V] = kv[:, ob + LANES:ob + QK_PAD].astype(BF16)


def _mla_proj(z, g_q, g_kv, wq_all, wkv, ta, layer, table_tile):
    rows = z.shape[0]
    per_layer = lambda i: (layer, 0, 0)
    zw = Z_KV + MLA_KV_RANK
    hq = MLA_HEADS * QK_PAD
    hv = MLA_HEADS * MLA_V
    tm = TM_PROJ
    vmem = (2 * (_nbytes((tm, zw), BF16) + _nbytes((tm, LANES), BF16) + _nbytes((tm, LANES), F32)
                 + _nbytes(wq_all.shape[1:], BF16) + _nbytes(wkv.shape[1:], BF16)
                 + 2 * _nbytes((tm, hq), BF16) + _nbytes((tm, hv), BF16))
            + 2 * _nbytes((tm, hq), F32) + 4 * _nbytes((tm, zw), F32))
    return pl.pallas_call(
        _mla_proj_kernel,
        out_shape=(jax.ShapeDtypeStruct((rows, hq), BF16), jax.ShapeDtypeStruct((rows, hq), BF16),
                   jax.ShapeDtypeStruct((rows, hv), BF16)),
        grid=(rows // tm,),
        in_specs=[
            pl.BlockSpec((tm, zw), lambda i: (i, 0)),
            pl.BlockSpec((tm, 2 * MLA_ROPE), lambda i: (i, Z_ROPE // (2 * MLA_ROPE))),
            pl.BlockSpec((None, 1, MLA_Q_RANK), per_layer),
            pl.BlockSpec((None, 1, MLA_KV_RANK), per_layer),
            pl.BlockSpec((None,) + wq_all.shape[1:], per_layer),
            pl.BlockSpec((None,) + wkv.shape[1:], per_layer),
            pl.BlockSpec((tm, LANES), lambda i: (table_tile(i), 0)),
        ],
        out_specs=(pl.BlockSpec((tm, hq), lambda i: (i, 0)), pl.BlockSpec((tm, hq), lambda i: (i, 0)),
                   pl.BlockSpec((tm, hv), lambda i: (i, 0))),
        compiler_params=_params(("parallel",), vmem),
        name="mla_proj",
    )(z, z, g_q, g_kv, wq_all, wkv, ta)


def _rider_specs(w, layer, n_blocks, axis, grid):
    rows, cols = w.shape[1:]
    block = (rows // n_blocks, cols) if axis == 0 else (rows, cols // n_blocks)
    assert block[0] * block[1] * n_blocks == rows * cols and n_blocks <= grid[0] * grid[1]

    def at(outer, inner):
        step = jnp.minimum(outer * grid[1] + inner, n_blocks - 1)
        return (step, 0) if axis == 0 else (0, step)

    in_spec = pl.BlockSpec((None,) + block, lambda outer, inner: (layer,) + at(outer, inner))
    return in_spec, pl.BlockSpec(block, at), jax.ShapeDtypeStruct((rows, cols), BF16), block


def _cast_riders(in_refs, out_refs):
    for i_ref, o_ref in zip(in_refs, out_refs):
        o_ref[...] = i_ref[...].astype(BF16)


def _split_refs(refs, *counts):
    out, pos = [], 0
    for c in counts:
        out.append(refs[pos:pos + c])
        pos += c
    assert pos == len(refs)
    return out


def _softmax_pv(scores, values_ones):
    m = functools.reduce(jnp.maximum, [jnp.max(s, axis=-1, keepdims=True) for s in scores])
    acc = functools.reduce(jnp.add, [_dot(jnp.exp2(s - m).astype(BF16), v) for s, v in zip(scores, values_ones)])
    return acc[:, :LANES] / acc[:, LANES:]


def _fill_values_ones(v_ref, vx_ref):
    vx_ref[:, :LANES] = v_ref[...]
    vx_ref[:, LANES:] = jnp.ones((vx_ref.shape[0], LANES), BF16)


def _scaled_q(q, head_dim):
    return (q.astype(F32) * (head_dim ** -0.5 * LOG2E)).astype(BF16)


def _mla_attn_kernel(*refs, with_ctx, n_riders):
    ins, rider_in, outs, rider_out, (vxl_ref, vxc_ref) = _split_refs(
        refs, 6 if with_ctx else 5, n_riders, 2 if with_ctx else 1, n_riders, 2)
    q_ref, kl_ref, vl_ref, kc_ref, vc_ref = ins[:5]
    o_ref = outs[0]
    _cast_riders(rider_in, rider_out)
    for hh in range(MLA_HEADS_PER_STEP):
        qk = slice(hh * QK_PAD, (hh + 1) * QK_PAD)
        vcols = slice(hh * MLA_V, (hh + 1) * MLA_V)
        _fill_values_ones(vl_ref.at[:, vcols], vxl_ref.at[hh])
        _fill_values_ones(vc_ref.at[:, vcols], vxc_ref.at[hh])

        def q_tile(t, carry, hh=hh, qk=qk, vcols=vcols):
            r0 = pl.multiple_of(t * TQ, TQ)
            q = q_ref[pl.ds(r0, TQ), qk]
            s_lat = _dot_nt(q, kl_ref[:, qk])
            s_ctx = _dot_nt(q, kc_ref[:, qk])
            o = _softmax_pv([s_lat, s_ctx], [vxl_ref[hh], vxc_ref[hh]])
            o_ref[pl.ds(r0, TQ), vcols] = o.astype(BF16)
            return carry

        lax.fori_loop(0, SEQ // TQ, q_tile, 0, unroll=Q_TILE_UNROLL)
        if with_ctx:
            s = _dot_nt(ins[5][:, qk], kc_ref[:, qk])
            outs[1][:, vcols] = _softmax_pv([s], [vxc_ref[hh]]).astype(BF16)


def _mla_attn(ql, kl, vl, qc, kc, vc, with_ctx, layer, riders):
    hps = MLA_HEADS_PER_STEP
    grid = (BATCH, MLA_HEADS // hps)
    head = lambda b, g: (b, g)
    wqk, wv = hps * QK_PAD, hps * MLA_V
    in_specs = [
        pl.BlockSpec((SEQ, wqk), head), pl.BlockSpec((SEQ, wqk), head), pl.BlockSpec((SEQ, wv), head),
        pl.BlockSpec((CTX_LEN, wqk), head), pl.BlockSpec((CTX_LEN, wv), head),
    ]
    args = [ql, kl, vl, kc, vc]
    out_shape = [jax.ShapeDtypeStruct((BATCH * SEQ, MLA_HEADS * MLA_V), BF16)]
    out_specs = [pl.BlockSpec((SEQ, wv), head)]
    if with_ctx:
        in_specs.append(pl.BlockSpec((CTX_LEN, wqk), head))
        args.append(qc)
        out_shape.append(jax.ShapeDtypeStruct((BATCH * CTX_LEN, MLA_HEADS * MLA_V), BF16))
        out_specs.append(pl.BlockSpec((CTX_LEN, wv), head))
    vmem = (2 * (3 * _nbytes((SEQ, wqk), BF16) + 2 * _nbytes((SEQ, wv), BF16))
            + 2 * hps * _nbytes((SEQ, 2 * LANES), BF16) + Q_TILE_UNROLL * _nbytes((TQ, SEQ + CTX_LEN), BF16))
    n_base = len(out_shape)
    for w, n_blocks, axis in riders:
        in_spec, out_spec, shape, block = _rider_specs(w, layer, n_blocks, axis, grid)
        in_specs.append(in_spec)
        args.append(w)
        out_specs.append(out_spec)
        out_shape.append(shape)
        vmem += 2 * (_nbytes(block, F32) + _nbytes(block, BF16))
    res = pl.pallas_call(
        functools.partial(_mla_attn_kernel, with_ctx=with_ctx, n_riders=len(riders)),
        out_shape=tuple(out_shape), grid=grid, in_specs=in_specs, out_specs=tuple(out_specs),
        scratch_shapes=[pltpu.VMEM((hps, SEQ, 2 * LANES), BF16), pltpu.VMEM((hps, CTX_LEN, 2 * LANES), BF16)],
        compiler_params=_params(("arbitrary", "arbitrary"), vmem),
        name="mla_attn_ctx" if with_ctx else "mla_attn",
    )(*args)
    return res[:n_base], res[n_base:]


def _na_band_start(block):
    return min(max(block * NA_RQ - NA_WIN_R // 2, 0), GRID_ROWS - NA_RK)


def _na_bias_kind(block):
    return 0 if block == 0 else (2 if block == NA_BLOCKS - 1 else 1)


def _na_row_offsets():
    n_dr = 2 * NA_WIN_R - 1
    idx = np.full((3, NA_RQ, NA_RK), n_dr, np.int32)
    for kind, block in ((0, 0), (1, 2), (2, NA_BLOCKS - 1)):
        for qr in range(NA_RQ):
            r = block * NA_RQ + qr
            r0 = min(max(r - NA_WIN_R // 2, 0), GRID_ROWS - NA_WIN_R)
            for ki in range(NA_RK):
                kr = _na_band_start(block) + ki
                if r0 <= kr < r0 + NA_WIN_R:
                    idx[kind, qr, ki] = kr - r + NA_WIN_R - 1
    return idx


def _na_bias_kernel(w_ref, o_ref):
    shape = (GRID_W, LANES)
    c = lax.broadcasted_iota(jnp.int32, shape, 0)
    kc = lax.broadcasted_iota(jnp.int32, shape, 1)
    c0 = jnp.clip(c - NA_WIN_C // 2, 0, GRID_W - NA_WIN_C)
    col_ok = (kc >= c0) & (kc < c0 + NA_WIN_C)
    neg = jnp.full((GRID_W, GRID_W), -jnp.inf, F32)
    offsets = _na_row_offsets()
    n_dr = 2 * NA_WIN_R - 1
    tiles = {n_dr: neg}
    for d in sorted(set(int(v) for v in offsets.reshape(-1)) - {n_dr}):
        row = jnp.broadcast_to(w_ref[d:d + 1, :], shape)
        skew = pltpu.roll(row, LANES - (GRID_W - 1), 1, stride=1, stride_axis=0)
        tiles[d] = jnp.where(col_ok, skew, -jnp.inf)[:, :GRID_W]
    for kind in range(3):
        for qr in range(NA_RQ):
            for ki in range(NA_RK):
                o_ref[kind, qr * GRID_W:(qr + 1) * GRID_W, ki * GRID_W:(ki + 1) * GRID_W] = (
                    tiles[int(offsets[kind, qr, ki])])


def _na_bias(rpb):
    n_dr, n_dc = 2 * NA_WIN_R - 1, 2 * NA_WIN_C - 1
    lo = GRID_W - 1 - (NA_WIN_C - 1)
    w = jnp.pad(rpb.reshape(DEPTH * NA_HEADS, n_dr, n_dc) * LOG2E, ((0, 0), (0, 0), (lo, LANES - lo - n_dc)),
                constant_values=-jnp.inf)
    vmem = 2 * (_nbytes((2 * 8, LANES), F32) + _nbytes((3, NA_NQ, NA_NK + GRID_W), F32)) + 16 * _nbytes(
        (GRID_W, LANES), F32)
    return pl.pallas_call(
        _na_bias_kernel,
        out_shape=jax.ShapeDtypeStruct((w.shape[0], 3, NA_NQ, NA_NK), F32),
        grid=(w.shape[0],),
        in_specs=[pl.BlockSpec((None, n_dr, LANES), lambda g: (g, 0, 0))],
        out_specs=pl.BlockSpec((None, 3, NA_NQ, NA_NK), lambda g: (g, 0, 0, 0)),
        compiler_params=_params(("parallel",), vmem),
        name="na_bias",
    )(w)


def _na_kernel(*refs, with_ctx, n_riders):
    ins, rider_in, outs, rider_out, (vx_ref, vxc_ref) = _split_refs(
        refs, 7 if with_ctx else 6, n_riders, 2 if with_ctx else 1, n_riders, 2)
    q_ref, k_ref, v_ref, kc_ref, vc_ref, bias_ref = ins[:6]
    o_ref = outs[0]
    _cast_riders(rider_in, rider_out)
    d = NA_HEAD_DIM
    for hh in range(NA_HEADS_PER_STEP):
        cols = slice(hh * d, (hh + 1) * d)
        _fill_values_ones(v_ref.at[:, cols], vx_ref.at[hh])
        _fill_values_ones(vc_ref.at[:, cols], vxc_ref.at[hh])
        for block in range(NA_BLOCKS):
            rows = slice(block * NA_NQ, (block + 1) * NA_NQ)
            q = _scaled_q(q_ref[rows, cols], d)
            k0 = _na_band_start(block) * GRID_W
            s_loc = _dot_nt(q, k_ref[k0:k0 + NA_NK, cols]) + bias_ref[hh, _na_bias_kind(block)]
            s_ctx = _dot_nt(q, kc_ref[:, cols])
            o = _softmax_pv([s_loc, s_ctx], [vx_ref[hh, k0:k0 + NA_NK, :], vxc_ref[hh]])
            o_ref[rows, cols] = o.astype(BF16)
        if with_ctx:
            s = _dot_nt(_scaled_q(ins[6][:, cols], d), kc_ref[:, cols])
            outs[1][:, cols] = _softmax_pv([s], [vxc_ref[hh]]).astype(BF16)


def _na_attn(zl, zc, bias, layer, with_ctx, riders):
    hps = NA_HEADS_PER_STEP
    w = hps * NA_HEAD_DIM
    steps = NA_HEADS // hps
    col = lambda off: (lambda g, b: (b, off // w + g))
    in_specs = [
        pl.BlockSpec((SEQ, w), col(Z_NA_Q)), pl.BlockSpec((SEQ, w), col(Z_NA_K)), pl.BlockSpec((SEQ, w), col(Z_NA_V)),
        pl.BlockSpec((CTX_LEN, w), col(Z_NA_K)), pl.BlockSpec((CTX_LEN, w), col(Z_NA_V)),
        pl.BlockSpec((hps, 3, NA_NQ, NA_NK), lambda g, b: (layer * steps + g, 0, 0, 0)),
    ]
    args = [zl, zl, zl, zc, zc, bias]
    out_shape = [jax.ShapeDtypeStruct((BATCH * SEQ, NA_DIM), BF16)]
    out_specs = [pl.BlockSpec((SEQ, w), lambda g, b: (b, g))]
    if with_ctx:
        in_specs.append(pl.BlockSpec((CTX_LEN, w), col(Z_NA_Q)))
        args.append(zc)
        out_shape.append(jax.ShapeDtypeStruct((BATCH * CTX_LEN, NA_DIM), BF16))
        out_specs.append(pl.BlockSpec((CTX_LEN, w), lambda g, b: (b, g)))
    vmem = (2 * (5 * _nbytes((SEQ, w), BF16) + _nbytes((hps, 3, NA_NQ, NA_NK), F32))
            + 2 * hps * _nbytes((SEQ, 2 * LANES), BF16) + 4 * hps * _nbytes((NA_NQ, NA_NK + CTX_LEN), F32))
    n_base = len(out_shape)
    for wt, n_blocks, axis in riders:
        in_spec, out_spec, shape, block = _rider_specs(wt, layer, n_blocks, axis, (steps, BATCH))
        in_specs.append(in_spec)
        args.append(wt)
        out_specs.append(out_spec)
        out_shape.append(shape)
        vmem += 2 * (_nbytes(block, F32) + _nbytes(block, BF16))
    res = pl.pallas_call(
        functools.partial(_na_kernel, with_ctx=with_ctx, n_riders=len(riders)),
        out_shape=tuple(out_shape), grid=(steps, BATCH), in_specs=in_specs, out_specs=tuple(out_specs),
        scratch_shapes=[pltpu.VMEM((hps, SEQ, 2 * LANES), BF16), pltpu.VMEM((hps, CTX_LEN, 2 * LANES), BF16)],
        compiler_params=_params(("arbitrary", "arbitrary"), vmem),
        name="na_attn_ctx" if with_ctx else "na_attn",
    )(*args)
    return res[:n_base], res[n_base:]


def _out_proj_kernel(g_ref, a_ref, n_ref, x_ref, gb_ref, gc_ref, h_ref, gcp_ref, hp_ref, gcn_ref, hn_ref, cw_ref,
                     w_ref, lng_ref, lnb_ref, *rest, seq_len):
    if len(rest) == 2:
        o_ref, mix_ref = rest
    else:
        *ada_in, o_ref, ada_out, mix_ref = rest
        ada_out[...] = _ada_block(*ada_in)
    tm = o_ref.shape[0]
    u = gc_ref[...].astype(F32) * h_ref[...].astype(F32)
    u_before = gcp_ref[CONV_HALO - 1:, :].astype(F32) * hp_ref[CONV_HALO - 1:, :].astype(F32)
    u_after = gcn_ref[:1, :].astype(F32) * hn_ref[:1, :].astype(F32)
    row = lax.broadcasted_iota(jnp.int32, u.shape, 0)
    pos = (pl.program_id(0) * tm + row) % seq_len
    u_prev = jnp.where(row == 0, u_before, pltpu.roll(u, 1, 0))
    u_prev = jnp.where(pos == 0, 0.0, u_prev)
    u_next = jnp.where(row == tm - 1, u_after, pltpu.roll(u, tm - 1, 0))
    u_next = jnp.where(pos == seq_len - 1, 0.0, u_next)
    y = u_prev * cw_ref[0:1, :] + u * cw_ref[1:2, :] + u_next * cw_ref[2:3, :]
    a_w = MLA_HEADS * MLA_V
    mix_ref[:, :a_w] = a_ref[...]
    mix_ref[:, a_w:a_w + CONV_DIM] = (gb_ref[...].astype(F32) * y).astype(BF16)
    mix_ref[:, a_w + CONV_DIM:] = n_ref[...]

    for r0 in range(0, tm, TM_OUT_CHUNK):
        rows = slice(r0, r0 + TM_OUT_CHUNK)
        r = DEEPNORM_ALPHA * x_ref[rows, :] + g_ref[...] * _dot(mix_ref[rows, :], w_ref[...])
        o_ref[rows, :] = _layer_norm(r, lng_ref[...], lnb_ref[...])


def _out_proj(a, n, z, x, mods, conv_w, w_out, ln_g, ln_b, layer, row_of_tile, seq_len, next_ada=None):
    rows = x.shape[0]
    tm = TM_OUT
    steps = rows // tm
    assert tm % seq_len == 0 or seq_len % tm == 0
    row = lambda i: (i, 0)
    per_layer = lambda i: (layer, 0, 0)
    conv_blk = lambda off: pl.BlockSpec((tm, CONV_DIM), lambda i: (i, off // CONV_DIM))
    halo = tm // CONV_HALO
    before = lambda off: pl.BlockSpec((CONV_HALO, CONV_DIM), lambda i: (jnp.maximum(i * halo - 1, 0), off // CONV_DIM))
    after = lambda off: pl.BlockSpec(
        (CONV_HALO, CONV_DIM), lambda i: (jnp.minimum((i + 1) * halo, rows // CONV_HALO - 1), off // CONV_DIM))
    vmem = (2 * (2 * _nbytes((tm, D_MODEL), F32) + _nbytes((tm, D_MODEL), BF16) + _nbytes((tm, CONV_DIM), BF16))
            + _nbytes((tm, D_MODEL), BF16) + _nbytes(w_out.shape, BF16)
            + 3 * _nbytes((TM_OUT_CHUNK, D_MODEL), F32) + 4 * _nbytes((tm, CONV_DIM), F32))
    in_specs = [
        _mod_spec(2, row_of_tile),
        pl.BlockSpec((tm, a.shape[1]), row), pl.BlockSpec((tm, n.shape[1]), row), pl.BlockSpec((tm, D_MODEL), row),
        conv_blk(Z_CONV_B), conv_blk(Z_CONV_C), conv_blk(Z_CONV_H),
        before(Z_CONV_C), before(Z_CONV_H), after(Z_CONV_C), after(Z_CONV_H),
        pl.BlockSpec((None,) + conv_w.shape[1:], per_layer),
        pl.BlockSpec(w_out.shape, lambda i: (0, 0), pipeline_mode=pl.Buffered(1)),
        pl.BlockSpec((None, 1, D_MODEL), per_layer), pl.BlockSpec((None, 1, D_MODEL), per_layer),
    ]
    args = [mods, a, n, x, z, z, z, z, z, z, z, conv_w, w_out, ln_g, ln_b]
    out_shape = [jax.ShapeDtypeStruct((rows, D_MODEL), F32)]
    out_specs = [pl.BlockSpec((tm, D_MODEL), row)]
    if next_ada is not None:
        tn = N_MOD * D_MODEL // steps
        assert tn % LANES == 0 and tn * steps == N_MOD * D_MODEL
        in_specs += [
            pl.BlockSpec((MOD_ROWS, D_MODEL), lambda i: (0, 0)),
            pl.BlockSpec((None, D_MODEL, tn), lambda i: (layer + 1, 0, i)),
            pl.BlockSpec((None, 1, tn), lambda i: (layer + 1, 0, i)),
        ]
        args += list(next_ada)
        out_shape.append(jax.ShapeDtypeStruct((MOD_ROWS, N_MOD * D_MODEL), F32))
        out_specs.append(pl.BlockSpec((MOD_ROWS, tn), lambda i: (0, i)))
        vmem += 2 * _nbytes((D_MODEL, tn), F32) + _nbytes((D_MODEL, tn), BF16)
    res = pl.pallas_call(
        functools.partial(_out_proj_kernel, seq_len=seq_len),
        out_shape=tuple(out_shape), grid=(steps,), in_specs=in_specs, out_specs=tuple(out_specs),
        scratch_shapes=[pltpu.VMEM((tm, D_MODEL), BF16)],
        compiler_params=_params(("arbitrary",), vmem),
        name="out_proj",
    )(*args)
    return res if next_ada is not None else res[0]


def _ffn_kernel(sh_ref, sc_ref, g_ref, x_ref, wg_ref, wu_ref, wd_ref, lng_ref, lnb_ref, o_ref, xm_ref):
    f = pl.program_id(1)

    @pl.when(f == 0)
    def _():
        x = x_ref[...]
        xm_ref[...] = (x * (1.0 + sc_ref[...]) + sh_ref[...]).astype(BF16)
        o_ref[...] = DEEPNORM_ALPHA * x

    xm = xm_ref[...]
    gate = _dot(xm, wg_ref[...])
    up = _dot(xm, wu_ref[...])
    hidden = (gate * jax.nn.sigmoid(gate) * up).astype(BF16)
    o_ref[...] += g_ref[...] * _dot(hidden, wd_ref[...])

    @pl.when(f == pl.num_programs(1) - 1)
    def _():
        o_ref[...] = _layer_norm(o_ref[...], lng_ref[...], lnb_ref[...])


def _ffn(x, mods, w_gate, w_up, w_down, ln_g, ln_b, layer, row_of_tile):
    rows = x.shape[0]
    per_layer = lambda i, f: (layer, 0, 0)
    vmem = (4 * _nbytes((TM, D_MODEL), F32) + 2 * 3 * _nbytes((D_MODEL, TF), BF16)
            + _nbytes((TM, D_MODEL), BF16) + 3 * _nbytes((TM, TF), F32) + _nbytes((TM, D_MODEL), F32))
    return pl.pallas_call(
        _ffn_kernel,
        out_shape=jax.ShapeDtypeStruct((rows, D_MODEL), F32),
        grid=(rows // TM, D_FF // TF),
        in_specs=[
            _mod_spec(3, row_of_tile), _mod_spec(4, row_of_tile), _mod_spec(5, row_of_tile),
            pl.BlockSpec((TM, D_MODEL), lambda i, f: (i, 0)),
            pl.BlockSpec((D_MODEL, TF), lambda i, f: (0, f)),
            pl.BlockSpec((D_MODEL, TF), lambda i, f: (0, f)),
            pl.BlockSpec((TF, D_MODEL), lambda i, f: (f, 0)),
            pl.BlockSpec((None, 1, D_MODEL), per_layer), pl.BlockSpec((None, 1, D_MODEL), per_layer),
        ],
        out_specs=pl.BlockSpec((TM, D_MODEL), lambda i, f: (i, 0)),
        scratch_shapes=[pltpu.VMEM((TM, D_MODEL), BF16)],
        compiler_params=_params(("parallel", "arbitrary"), vmem),
        name="ffn",
    )(mods, mods, mods, x, w_gate, w_up, w_down, ln_g, ln_b)


def _rotate_half_cols(w):
    a1, a2, b1, b2 = jnp.split(w, 4, axis=-1)
    return jnp.concatenate([-a2, a1, -b2, b1], axis=-1)


def _w_in_prep_src(piece):
    rope = OFF_ROPE // PREP_PIECE
    tail = Z_ROPE // PREP_PIECE
    return jnp.where(piece < rope, piece, jnp.where(piece < tail, piece + 1, rope))


def _w_in_prep_kernel(*refs):
    *piece_refs, o_ref = refs
    pieces = [r[...] for r in piece_refs]
    n = len(pieces)
    is_last = pl.program_id(1) == pl.num_programs(1) - 1
    rot_piece = Z_DIM // PREP_PIECE - 1
    for p in range(n):
        piece = (Z_PAD // PREP_COLS - 1) * n + p
        if piece == rot_piece:
            a1, a2, b1, b2 = jnp.split(pieces[p], 4, axis=0)
            pieces[p] = jnp.where(is_last, jnp.concatenate([-a2, a1, -b2, b1], axis=0), pieces[p])
        elif piece > rot_piece:
            pieces[p] = jnp.where(is_last, 0.0, pieces[p])
    o_ref[...] = jnp.concatenate(pieces, axis=0).T.astype(BF16)


def _w_in_prep(w_in):
    w_t = jnp.swapaxes(w_in, 1, 2)
    n = PREP_COLS // PREP_PIECE
    piece_spec = lambda p: pl.BlockSpec((None, PREP_PIECE, D_MODEL), lambda l, g: (l, _w_in_prep_src(g * n + p), 0))
    vmem = 2 * (_nbytes((PREP_COLS, D_MODEL), F32) + _nbytes((D_MODEL, PREP_COLS), BF16)) + 3 * _nbytes(
        (PREP_COLS, D_MODEL), F32)
    return pl.pallas_call(
        _w_in_prep_kernel,
        out_shape=jax.ShapeDtypeStruct((DEPTH, D_MODEL, Z_PAD), BF16),
        grid=(DEPTH, Z_PAD // PREP_COLS),
        in_specs=[piece_spec(p) for p in range(n)],
        out_specs=pl.BlockSpec((None, D_MODEL, PREP_COLS), lambda l, g: (l, 0, g)),
        compiler_params=_params(("parallel", "parallel"), vmem),
        name="w_in_prep",
    )(*([w_t] * n))


def _relayout_wq(w):
    w = w.reshape(DEPTH, MLA_Q_RANK, MLA_HEADS, MLA_QK)
    rope = w[..., MLA_NOPE:]
    cols = jnp.concatenate([w, _rotate_half_cols(rope)], axis=-1)
    return cols.reshape(DEPTH, MLA_Q_RANK, MLA_HEADS * QK_PAD).astype(BF16)


def _rope_tables():
    t = jnp.arange(SEQ)
    row = (t // GRID_W).astype(F32)
    col = (t % GRID_W).astype(F32)
    n_freq = MLA_ROPE // 4
    inv = ROPE_THETA ** (-jnp.arange(n_freq, dtype=F32) / n_freq)
    ar = row[:, None] * inv
    ac = col[:, None] * inv
    ang = jnp.concatenate([ar, ar, ac, ac], axis=-1)
    lat = jnp.concatenate([jnp.cos(ang), jnp.sin(ang)], axis=1)
    ctx = jnp.concatenate([jnp.ones((TM_PROJ, MLA_ROPE), F32), jnp.zeros((TM_PROJ, MLA_ROPE), F32)], axis=1)
    return lat, ctx


def kernel(x, c, ctx, c_ctx, ada_w, ada_b, w_in, mla_q_norm, mla_wq_b, mla_kv_norm, mla_wkv_b, conv_w, na_rpb,
           w_out, ln1_g, ln1_b, ffn_w_gate, ffn_w_up, ffn_w_down, ln2_g, ln2_b):
    assert x.shape == (BATCH, SEQ, D_MODEL) and ctx.shape == (BATCH, CTX_LEN, D_MODEL)
    cc = jnp.concatenate([c, c_ctx[None, :], jnp.zeros((MOD_ROWS - BATCH - 1, D_MODEL), F32)], axis=0)
    ada_b = ada_b.reshape(DEPTH, 1, N_MOD * D_MODEL)
    as_table = lambda m: m.reshape(MOD_ROWS * N_MOD, 1, D_MODEL)
    mods = as_table(_ada(cc, ada_w, ada_b, 0))
    ta_lat, ta_ctx = _rope_tables()
    rows3d = lambda v: v.reshape(DEPTH, 1, -1)

    w_in_p = _w_in_prep(w_in)
    wq_all = _relayout_wq(mla_wq_b)
    wkv = mla_wkv_b.astype(BF16)
    g_q, g_kv = rows3d(mla_q_norm), rows3d(mla_kv_norm)
    bias = _na_bias(na_rpb)
    ln1 = (rows3d(ln1_g), rows3d(ln1_b))
    ln2 = (rows3d(ln2_g), rows3d(ln2_b))

    xl = x.reshape(BATCH * SEQ, D_MODEL)
    xc = ctx.reshape(BATCH * CTX_LEN, D_MODEL)
    lat_row = lambda tm: (lambda i: i // (SEQ // tm))
    ctx_row = lambda i: CTX_MOD_ROW
    for l in range(DEPTH):
        last = l == DEPTH - 1
        zl = _in_proj(xl, mods, w_in_p, l, lat_row(TM))
        zc = _in_proj(xc, mods, w_in_p, l, ctx_row)
        ql, kl, vl = _mla_proj(zl, g_q, g_kv, wq_all, wkv, ta_lat, l, lambda i: i % (SEQ // TM_PROJ))
        qc, kc, vc = _mla_proj(zc, g_q, g_kv, wq_all, wkv, ta_ctx, l, lambda i: 0)
        a, (w_g, w_u) = _mla_attn(ql, kl, vl, qc, kc, vc, not last, l,
                                  [(ffn_w_gate, FFN_CAST_BLOCKS, 1), (ffn_w_up, FFN_CAST_BLOCKS, 1)])
        n, (w_d, w_o) = _na_attn(zl, zc, bias, l, not last,
                                 [(ffn_w_down, FFN_CAST_BLOCKS, 0), (w_out, OUT_CAST_BLOCKS, 0)])
        if last:
            x1 = _out_proj(a[0], n[0], zl, xl, mods, conv_w, w_o, *ln1, l, lat_row(TM_OUT), SEQ)
        else:
            x1, next_mods = _out_proj(a[0], n[0], zl, xl, mods, conv_w, w_o, *ln1, l, lat_row(TM_OUT), SEQ,
                                      next_ada=(cc, ada_w, ada_b))
        xl = _ffn(x1, mods, w_g, w_u, w_d, *ln2, l, lat_row(TM))
        if not last:
            x1c = _out_proj(a[1], n[1], zc, xc, mods, conv_w, w_o, *ln1, l, ctx_row, CTX_LEN)
            xc = _ffn(x1c, mods, w_g, w_u, w_d, *ln2, l, ctx_row)
            mods = as_table(next_mods)
    return xl.reshape(BATCH, SEQ, D_MODEL)
```

```python
import functools

import numpy as np
import jax
import jax.numpy as jnp
from jax import lax
from jax.experimental import pallas as pl
from jax.experimental.pallas import tpu as pltpu

D_MODEL = 2048
BATCH = 4
SEQ = 2048
DEPTH = 2
CTX_LEN = 256
GRID_W = 64
GRID_ROWS = SEQ // GRID_W
MLA_HEADS = 6
MLA_Q_RANK = 512
MLA_KV_RANK = 512
MLA_NOPE = 128
MLA_ROPE = 64
MLA_V = 128
MLA_QK = MLA_NOPE + MLA_ROPE
CONV_DIM = 512
NA_HEADS = 6
NA_HEAD_DIM = 128
NA_DIM = NA_HEADS * NA_HEAD_DIM
NA_WIN_R = 8
NA_WIN_C = 16
OFF_MLA_Q = 0
OFF_MLA_KV = OFF_MLA_Q + MLA_Q_RANK
OFF_ROPE = OFF_MLA_KV + MLA_KV_RANK
OFF_CONV_B = OFF_ROPE + MLA_ROPE
OFF_CONV_C = OFF_CONV_B + CONV_DIM
OFF_CONV_H = OFF_CONV_C + CONV_DIM
OFF_NA_Q = OFF_CONV_H + CONV_DIM
OFF_NA_K = OFF_NA_Q + NA_DIM
OFF_NA_V = OFF_NA_K + NA_DIM
IN_DIM = OFF_NA_V + NA_DIM
D_FF = -(-8 * D_MODEL // (3 * 256)) * 256
ROPE_THETA = 10000.0
LN_EPS = 1e-6
RMS_EPS = 1e-6
DEEPNORM_ALPHA = (2 * DEPTH) ** 0.25
LOG2E = 1.4426950408889634
N_MOD = 6
MOD_ROWS = 8
CTX_MOD_ROW = BATCH

LANES = 128
V7X_VMEM_BYTES = 64 * 1024 * 1024
V7X_VMEM_BUDGET = V7X_VMEM_BYTES - 2 * 1024 * 1024

Z_Q = 0
Z_KV = Z_Q + MLA_Q_RANK
Z_CONV_B = Z_KV + MLA_KV_RANK
Z_CONV_C = Z_CONV_B + CONV_DIM
Z_CONV_H = Z_CONV_C + CONV_DIM
Z_NA_Q = Z_CONV_H + CONV_DIM
Z_NA_K = Z_NA_Q + NA_DIM
Z_NA_V = Z_NA_K + NA_DIM
Z_ROPE = Z_NA_V + NA_DIM
Z_DIM = Z_ROPE + 2 * MLA_ROPE
V7X_MXU_COLS = 256
Z_PAD = -(-Z_DIM // (5 * V7X_MXU_COLS)) * (5 * V7X_MXU_COLS)

QK_PAD = 2 * LANES

TM = 1024
TM_OUT = 512
TM_OUT_CHUNK = 256
CONV_HALO = 16
TM_PROJ = 1024
PREP_PIECE = MLA_ROPE
PREP_COLS = 5 * LANES
TN_IN = 5 * V7X_MXU_COLS
TN_ADA = 1024
TF = 512
TQ = 256
Q_TILE_UNROLL = 8
NA_RQ = 4
NA_RK = NA_RQ + NA_WIN_R - 1
NA_NQ = NA_RQ * GRID_W
NA_NK = NA_RK * GRID_W
NA_BLOCKS = GRID_ROWS // NA_RQ
FFN_CAST_BLOCKS = 11
MLA_HEADS_PER_STEP = 2
OUT_CAST_BLOCKS = 8
NA_HEADS_PER_STEP = 2

F32 = jnp.float32
BF16 = jnp.bfloat16


def _params(semantics, vmem_bytes):
    assert vmem_bytes <= V7X_VMEM_BUDGET, vmem_bytes
    return pltpu.CompilerParams(dimension_semantics=semantics, vmem_limit_bytes=int(vmem_bytes))


def _nbytes(shape, dtype):
    return int(np.prod(shape)) * jnp.dtype(dtype).itemsize


def _dot(a, b):
    return jnp.dot(a, b, preferred_element_type=F32)


def _dot_nt(a, b):
    return lax.dot_general(a, b, (((1,), (1,)), ((), ())), preferred_element_type=F32)


def _mod_spec(chunk, row_of_tile):
    return pl.BlockSpec((None, 1, D_MODEL), lambda i, *_: (chunk + row_of_tile(i) * N_MOD, 0, 0))


def _layer_norm(r, g, b):
    mu = jnp.mean(r, axis=-1, keepdims=True)
    c = r - mu
    var = jnp.mean(c * c, axis=-1, keepdims=True)
    return c * lax.rsqrt(var + LN_EPS) * g + b


def _ada_block(cc_ref, w_ref, b_ref):
    cc = cc_ref[...]
    s = (cc * jax.nn.sigmoid(cc)).astype(BF16)
    return _dot(s, w_ref[...].astype(BF16)) + b_ref[...]


def _ada_kernel(cc_ref, w_ref, b_ref, o_ref):
    o_ref[...] = _ada_block(cc_ref, w_ref, b_ref)


def _ada(cc, ada_w, ada_b, layer):
    n = N_MOD * D_MODEL
    vmem = 2 * (_nbytes((D_MODEL, TN_ADA), F32) + _nbytes((MOD_ROWS, TN_ADA), F32) * 2
                + _nbytes((MOD_ROWS, D_MODEL), F32)) + _nbytes((D_MODEL, TN_ADA), BF16) * 2
    return pl.pallas_call(
        _ada_kernel,
        out_shape=jax.ShapeDtypeStruct((MOD_ROWS, n), F32),
        grid=(n // TN_ADA,),
        in_specs=[
            pl.BlockSpec((MOD_ROWS, D_MODEL), lambda j: (0, 0)),
            pl.BlockSpec((None, D_MODEL, TN_ADA), lambda j: (layer, 0, j)),
            pl.BlockSpec((None, 1, TN_ADA), lambda j: (layer, 0, j)),
        ],
        out_specs=pl.BlockSpec((MOD_ROWS, TN_ADA), lambda j: (0, j)),
        compiler_params=_params(("parallel",), vmem),
        name="ada_mod",
    )(cc, ada_w, ada_b)


def _in_proj_kernel(sh_ref, sc_ref, x_ref, w_ref, o_ref, xm_ref):
    @pl.when(pl.program_id(1) == 0)
    def _():
        xm_ref[...] = (x_ref[...] * (1.0 + sc_ref[...]) + sh_ref[...]).astype(BF16)

    o_ref[...] = _dot(xm_ref[...], w_ref[...]).astype(BF16)


def _in_proj(x, mods, w_in_p, layer, row_of_tile):
    rows = x.shape[0]
    vmem = (2 * (_nbytes((TM, D_MODEL), F32) + _nbytes((D_MODEL, TN_IN), BF16) + _nbytes((TM, TN_IN), BF16))
            + _nbytes((TM, D_MODEL), BF16) + _nbytes((TM, TN_IN), F32))
    return pl.pallas_call(
        _in_proj_kernel,
        out_shape=jax.ShapeDtypeStruct((rows, Z_PAD), BF16),
        grid=(rows // TM, Z_PAD // TN_IN),
        in_specs=[
            _mod_spec(0, row_of_tile),
            _mod_spec(1, row_of_tile),
            pl.BlockSpec((TM, D_MODEL), lambda i, j: (i, 0)),
            pl.BlockSpec((None, D_MODEL, TN_IN), lambda i, j: (layer, 0, j)),
        ],
        out_specs=pl.BlockSpec((TM, TN_IN), lambda i, j: (i, j)),
        scratch_shapes=[pltpu.VMEM((TM, D_MODEL), BF16)],
        compiler_params=_params(("parallel", "arbitrary"), vmem),
        name="in_proj",
    )(mods, mods, x, w_in_p)


def _rms_norm(x, g):
    return x * lax.rsqrt(jnp.mean(x * x, axis=-1, keepdims=True) + RMS_EPS) * g


def _rope_sum(pair, table, lane):
    y = pair * table
    return jnp.where(lane < MLA_ROPE, y + pltpu.roll(y, MLA_ROPE, 1), 0.0)


def _mla_proj_kernel(z_ref, slot_ref, gq_ref, gkv_ref, wq_ref, wkv_ref, ta_ref, q_ref, k_ref, v_ref):
    cq = z_ref[:, Z_Q:Z_Q + MLA_Q_RANK].astype(F32)
    ckv = z_ref[:, Z_KV:Z_KV + MLA_KV_RANK].astype(F32)
    qa = _dot(_rms_norm(cq, gq_ref[...]).astype(BF16), wq_ref[...]) * (MLA_QK ** -0.5 * LOG2E)
    kv = _dot(_rms_norm(ckv, gkv_ref[...]).astype(BF16), wkv_ref[...])
    ta = ta_ref[...]
    lane = lax.broadcasted_iota(jnp.int32, ta.shape, 1)
    kro = _rope_sum(slot_ref[...].astype(F32), ta, lane).astype(BF16)
    for h in range(MLA_HEADS):
        ob = h * QK_PAD
        q_ref[:, ob:ob + LANES] = qa[:, ob:ob + LANES].astype(BF16)
        q_ref[:, ob + LANES:ob + QK_PAD] = _rope_sum(qa[:, ob + LANES:ob + QK_PAD], ta, lane).astype(BF16)
        k_ref[:, ob:ob + LANES] = kv[:, ob:ob + LANES].astype(BF16)
        k_ref[:, ob + LANES:ob + QK_PAD] = kro
        v_ref[:, h * MLA_V:(h + 1) * MLA_V] = kv[:, ob + LANES:ob + QK_PAD].astype(BF16)


def _mla_proj(z, g_q, g_kv, wq_all, wkv, ta, layer, table_tile):
    rows = z.shape[0]
    per_layer = lambda i: (layer, 0, 0)
    zw = Z_KV + MLA_KV_RANK
    hq = MLA_HEADS * QK_PAD
    hv = MLA_HEADS * MLA_V
    tm = TM_PROJ
    vmem = (2 * (_nbytes((tm, zw), BF16) + _nbytes((tm, LANES), BF16) + _nbytes((tm, LANES), F32)
                 + _nbytes(wq_all.shape[1:], BF16) + _nbytes(wkv.shape[1:], BF16)
                 + 2 * _nbytes((tm, hq), BF16) + _nbytes((tm, hv), BF16))
            + _nbytes((tm, hq), F32) + _nbytes((tm, zw), F32))
    return pl.pallas_call(
        _mla_proj_kernel,
        out_shape=(jax.ShapeDtypeStruct((rows, hq), BF16), jax.ShapeDtypeStruct((rows, hq), BF16),
                   jax.ShapeDtypeStruct((rows, hv), BF16)),
        grid=(rows // tm,),
        in_specs=[
            pl.BlockSpec((tm, zw), lambda i: (i, 0)),
            pl.BlockSpec((tm, 2 * MLA_ROPE), lambda i: (i, Z_ROPE // (2 * MLA_ROPE))),
            pl.BlockSpec((None, 1, MLA_Q_RANK), per_layer),
            pl.BlockSpec((None, 1, MLA_KV_RANK), per_layer),
            pl.BlockSpec((None,) + wq_all.shape[1:], per_layer),
            pl.BlockSpec((None,) + wkv.shape[1:], per_layer),
            pl.BlockSpec((tm, LANES), lambda i: (table_tile(i), 0)),
        ],
        out_specs=(pl.BlockSpec((tm, hq), lambda i: (i, 0)), pl.BlockSpec((tm, hq), lambda i: (i, 0)),
                   pl.BlockSpec((tm, hv), lambda i: (i, 0))),
        compiler_params=_params(("parallel",), vmem),
        name="mla_proj",
    )(z, z, g_q, g_kv, wq_all, wkv, ta)


def _rider_specs(w, layer, n_blocks, axis, grid):
    rows, cols = w.shape[1:]
    block = (rows // n_blocks, cols) if axis == 0 else (rows, cols // n_blocks)
    assert block[0] * block[1] * n_blocks == rows * cols and n_blocks <= grid[0] * grid[1]

    def at(outer, inner):
        step = jnp.minimum(outer * grid[1] + inner, n_blocks - 1)
        return (step, 0) if axis == 0 else (0, step)

    in_spec = pl.BlockSpec((None,) + block, lambda outer, inner: (layer,) + at(outer, inner))
    return in_spec, pl.BlockSpec(block, at), jax.ShapeDtypeStruct((rows, cols), BF16), block


def _cast_riders(in_refs, out_refs):
    for i_ref, o_ref in zip(in_refs, out_refs):
        o_ref[...] = i_ref[...].astype(BF16)


def _split_refs(refs, *counts):
    out, pos = [], 0
    for c in counts:
        out.append(refs[pos:pos + c])
        pos += c
    assert pos == len(refs)
    return out


def _softmax_pv(scores, values_ones):
    m = functools.reduce(jnp.maximum, [jnp.max(s, axis=-1, keepdims=True) for s in scores])
    acc = functools.reduce(jnp.add, [_dot(jnp.exp2(s - m).astype(BF16), v) for s, v in zip(scores, values_ones)])
    return acc[:, :LANES] / acc[:, LANES:]


def _fill_values_ones(v_ref, vx_ref):
    vx_ref[:, :LANES] = v_ref[...]
    vx_ref[:, LANES:] = jnp.ones((vx_ref.shape[0], LANES), BF16)


def _scaled_q(q, head_dim):
    return (q.astype(F32) * (head_dim ** -0.5 * LOG2E)).astype(BF16)


def _mla_attn_kernel(*refs, with_ctx, n_riders):
    ins, rider_in, outs, rider_out, (vxl_ref, vxc_ref) = _split_refs(
        refs, 6 if with_ctx else 5, n_riders, 2 if with_ctx else 1, n_riders, 2)
    q_ref, kl_ref, vl_ref, kc_ref, vc_ref = ins[:5]
    o_ref = outs[0]
    _cast_riders(rider_in, rider_out)
    for hh in range(MLA_HEADS_PER_STEP):
        qk = slice(hh * QK_PAD, (hh + 1) * QK_PAD)
        vcols = slice(hh * MLA_V, (hh + 1) * MLA_V)
        _fill_values_ones(vl_ref.at[:, vcols], vxl_ref.at[hh])
        _fill_values_ones(vc_ref.at[:, vcols], vxc_ref.at[hh])

        def q_tile(t, carry, hh=hh, qk=qk, vcols=vcols):
            r0 = pl.multiple_of(t * TQ, TQ)
            q = q_ref[pl.ds(r0, TQ), qk]
            s_lat = _dot_nt(q, kl_ref[:, qk])
            s_ctx = _dot_nt(q, kc_ref[:, qk])
            o = _softmax_pv([s_lat, s_ctx], [vxl_ref[hh], vxc_ref[hh]])
            o_ref[pl.ds(r0, TQ), vcols] = o.astype(BF16)
            return carry

        lax.fori_loop(0, SEQ // TQ, q_tile, 0, unroll=Q_TILE_UNROLL)
        if with_ctx:
            s = _dot_nt(ins[5][:, qk], kc_ref[:, qk])
            outs[1][:, vcols] = _softmax_pv([s], [vxc_ref[hh]]).astype(BF16)


def _mla_attn(ql, kl, vl, qc, kc, vc, with_ctx, layer, riders):
    hps = MLA_HEADS_PER_STEP
    grid = (BATCH, MLA_HEADS // hps)
    head = lambda b, g: (b, g)
    wqk, wv = hps * QK_PAD, hps * MLA_V
    in_specs = [
        pl.BlockSpec((SEQ, wqk), head), pl.BlockSpec((SEQ, wqk), head), pl.BlockSpec((SEQ, wv), head),
        pl.BlockSpec((CTX_LEN, wqk), head), pl.BlockSpec((CTX_LEN, wv), head),
    ]
    args = [ql, kl, vl, kc, vc]
    out_shape = [jax.ShapeDtypeStruct((BATCH * SEQ, MLA_HEADS * MLA_V), BF16)]
    out_specs = [pl.BlockSpec((SEQ, wv), head)]
    if with_ctx:
        in_specs.append(pl.BlockSpec((CTX_LEN, wqk), head))
        args.append(qc)
        out_shape.append(jax.ShapeDtypeStruct((BATCH * CTX_LEN, MLA_HEADS * MLA_V), BF16))
        out_specs.append(pl.BlockSpec((CTX_LEN, wv), head))
    vmem = (2 * (3 * _nbytes((SEQ, wqk), BF16) + 2 * _nbytes((SEQ, wv), BF16))
            + 2 * hps * _nbytes((SEQ, 2 * LANES), BF16) + Q_TILE_UNROLL * _nbytes((TQ, SEQ + CTX_LEN), BF16))
    n_base = len(out_shape)
    for w, n_blocks, axis in riders:
        in_spec, out_spec, shape, block = _rider_specs(w, layer, n_blocks, axis, grid)
        in_specs.append(in_spec)
        args.append(w)
        out_specs.append(out_spec)
        out_shape.append(shape)
        vmem += 2 * (_nbytes(block, F32) + _nbytes(block, BF16))
    res = pl.pallas_call(
        functools.partial(_mla_attn_kernel, with_ctx=with_ctx, n_riders=len(riders)),
        out_shape=tuple(out_shape), grid=grid, in_specs=in_specs, out_specs=tuple(out_specs),
        scratch_shapes=[pltpu.VMEM((hps, SEQ, 2 * LANES), BF16), pltpu.VMEM((hps, CTX_LEN, 2 * LANES), BF16)],
        compiler_params=_params(("arbitrary", "arbitrary"), vmem),
        name="mla_attn_ctx" if with_ctx else "mla_attn",
    )(*args)
    return res[:n_base], res[n_base:]


def _na_band_start(block):
    return min(max(block * NA_RQ - NA_WIN_R // 2, 0), GRID_ROWS - NA_RK)


def _na_bias_kind(block):
    return 0 if block == 0 else (2 if block == NA_BLOCKS - 1 else 1)


def _na_row_offsets():
    n_dr = 2 * NA_WIN_R - 1
    idx = np.full((3, NA_RQ, NA_RK), n_dr, np.int32)
    for kind, block in ((0, 0), (1, 2), (2, NA_BLOCKS - 1)):
        for qr in range(NA_RQ):
            r = block * NA_RQ + qr
            r0 = min(max(r - NA_WIN_R // 2, 0), GRID_ROWS - NA_WIN_R)
            for ki in range(NA_RK):
                kr = _na_band_start(block) + ki
                if r0 <= kr < r0 + NA_WIN_R:
                    idx[kind, qr, ki] = kr - r + NA_WIN_R - 1
    return idx


def _na_bias_kernel(w_ref, o_ref):
    shape = (GRID_W, LANES)
    c = lax.broadcasted_iota(jnp.int32, shape, 0)
    kc = lax.broadcasted_iota(jnp.int32, shape, 1)
    c0 = jnp.clip(c - NA_WIN_C // 2, 0, GRID_W - NA_WIN_C)
    col_ok = (kc >= c0) & (kc < c0 + NA_WIN_C)
    neg = jnp.full((GRID_W, GRID_W), -jnp.inf, F32)
    offsets = _na_row_offsets()
    n_dr = 2 * NA_WIN_R - 1
    tiles = {n_dr: neg}
    for d in sorted(set(int(v) for v in offsets.reshape(-1)) - {n_dr}):
        row = jnp.broadcast_to(w_ref[d:d + 1, :], shape)
        skew = pltpu.roll(row, LANES - (GRID_W - 1), 1, stride=1, stride_axis=0)
        tiles[d] = jnp.where(col_ok, skew, -jnp.inf)[:, :GRID_W]
    for kind in range(3):
        for qr in range(NA_RQ):
            for ki in range(NA_RK):
                o_ref[kind, qr * GRID_W:(qr + 1) * GRID_W, ki * GRID_W:(ki + 1) * GRID_W] = (
                    tiles[int(offsets[kind, qr, ki])])


def _na_bias(rpb):
    n_dr, n_dc = 2 * NA_WIN_R - 1, 2 * NA_WIN_C - 1
    lo = GRID_W - 1 - (NA_WIN_C - 1)
    w = jnp.pad(rpb.reshape(DEPTH * NA_HEADS, n_dr, n_dc) * LOG2E, ((0, 0), (0, 0), (lo, LANES - lo - n_dc)),
                constant_values=-jnp.inf)
    vmem = 2 * (_nbytes((2 * 8, LANES), F32) + _nbytes((3, NA_NQ, NA_NK + GRID_W), F32)) + 16 * _nbytes(
        (GRID_W, LANES), F32)
    return pl.pallas_call(
        _na_bias_kernel,
        out_shape=jax.ShapeDtypeStruct((w.shape[0], 3, NA_NQ, NA_NK), F32),
        grid=(w.shape[0],),
        in_specs=[pl.BlockSpec((None, n_dr, LANES), lambda g: (g, 0, 0))],
        out_specs=pl.BlockSpec((None, 3, NA_NQ, NA_NK), lambda g: (g, 0, 0, 0)),
        compiler_params=_params(("parallel",), vmem),
        name="na_bias",
    )(w)


def _na_kernel(*refs, with_ctx, n_riders):
    ins, rider_in, outs, rider_out, (vx_ref, vxc_ref) = _split_refs(
        refs, 7 if with_ctx else 6, n_riders, 2 if with_ctx else 1, n_riders, 2)
    q_ref, k_ref, v_ref, kc_ref, vc_ref, bias_ref = ins[:6]
    o_ref = outs[0]
    _cast_riders(rider_in, rider_out)
    d = NA_HEAD_DIM
    for hh in range(NA_HEADS_PER_STEP):
        cols = slice(hh * d, (hh + 1) * d)
        _fill_values_ones(v_ref.at[:, cols], vx_ref.at[hh])
        _fill_values_ones(vc_ref.at[:, cols], vxc_ref.at[hh])
        for block in range(NA_BLOCKS):
            rows = slice(block * NA_NQ, (block + 1) * NA_NQ)
            q = _scaled_q(q_ref[rows, cols], d)
            k0 = _na_band_start(block) * GRID_W
            s_loc = _dot_nt(q, k_ref[k0:k0 + NA_NK, cols]) + bias_ref[hh, _na_bias_kind(block)]
            s_ctx = _dot_nt(q, kc_ref[:, cols])
            o = _softmax_pv([s_loc, s_ctx], [vx_ref[hh, k0:k0 + NA_NK, :], vxc_ref[hh]])
            o_ref[rows, cols] = o.astype(BF16)
        if with_ctx:
            s = _dot_nt(_scaled_q(ins[6][:, cols], d), kc_ref[:, cols])
            outs[1][:, cols] = _softmax_pv([s], [vxc_ref[hh]]).astype(BF16)


def _na_attn(zl, zc, bias, layer, with_ctx, riders):
    hps = NA_HEADS_PER_STEP
    w = hps * NA_HEAD_DIM
    steps = NA_HEADS // hps
    col = lambda off: (lambda g, b: (b, off // w + g))
    in_specs = [
        pl.BlockSpec((SEQ, w), col(Z_NA_Q)), pl.BlockSpec((SEQ, w), col(Z_NA_K)), pl.BlockSpec((SEQ, w), col(Z_NA_V)),
        pl.BlockSpec((CTX_LEN, w), col(Z_NA_K)), pl.BlockSpec((CTX_LEN, w), col(Z_NA_V)),
        pl.BlockSpec((hps, 3, NA_NQ, NA_NK), lambda g, b: (layer * steps + g, 0, 0, 0)),
    ]
    args = [zl, zl, zl, zc, zc, bias]
    out_shape = [jax.ShapeDtypeStruct((BATCH * SEQ, NA_DIM), BF16)]
    out_specs = [pl.BlockSpec((SEQ, w), lambda g, b: (b, g))]
    if with_ctx:
        in_specs.append(pl.BlockSpec((CTX_LEN, w), col(Z_NA_Q)))
        args.append(zc)
        out_shape.append(jax.ShapeDtypeStruct((BATCH * CTX_LEN, NA_DIM), BF16))
        out_specs.append(pl.BlockSpec((CTX_LEN, w), lambda g, b: (b, g)))
    vmem = (2 * (5 * _nbytes((SEQ, w), BF16) + _nbytes((hps, 3, NA_NQ, NA_NK), F32))
            + 2 * hps * _nbytes((SEQ, 2 * LANES), BF16) + 4 * hps * _nbytes((NA_NQ, NA_NK + CTX_LEN), F32))
    n_base = len(out_shape)
    for wt, n_blocks, axis in riders:
        in_spec, out_spec, shape, block = _rider_specs(wt, layer, n_blocks, axis, (steps, BATCH))
        in_specs.append(in_spec)
        args.append(wt)
        out_specs.append(out_spec)
        out_shape.append(shape)
        vmem += 2 * (_nbytes(block, F32) + _nbytes(block, BF16))
    res = pl.pallas_call(
        functools.partial(_na_kernel, with_ctx=with_ctx, n_riders=len(riders)),
        out_shape=tuple(out_shape), grid=(steps, BATCH), in_specs=in_specs, out_specs=tuple(out_specs),
        scratch_shapes=[pltpu.VMEM((hps, SEQ, 2 * LANES), BF16), pltpu.VMEM((hps, CTX_LEN, 2 * LANES), BF16)],
        compiler_params=_params(("arbitrary", "arbitrary"), vmem),
        name="na_attn_ctx" if with_ctx else "na_attn",
    )(*args)
    return res[:n_base], res[n_base:]


def _out_proj_kernel(g_ref, a_ref, n_ref, x_ref, gb_ref, gc_ref, h_ref, gcp_ref, hp_ref, gcn_ref, hn_ref, cw_ref,
                     w_ref, lng_ref, lnb_ref, *rest, seq_len):
    if len(rest) == 2:
        o_ref, mix_ref = rest
    else:
        *ada_in, o_ref, ada_out, mix_ref = rest
        ada_out[...] = _ada_block(*ada_in)
    tm = o_ref.shape[0]
    u = gc_ref[...].astype(F32) * h_ref[...].astype(F32)
    u_before = gcp_ref[CONV_HALO - 1:, :].astype(F32) * hp_ref[CONV_HALO - 1:, :].astype(F32)
    u_after = gcn_ref[:1, :].astype(F32) * hn_ref[:1, :].astype(F32)
    row = lax.broadcasted_iota(jnp.int32, u.shape, 0)
    pos = (pl.program_id(0) * tm + row) % seq_len
    u_prev = jnp.where(row == 0, u_before, pltpu.roll(u, 1, 0))
    u_prev = jnp.where(pos == 0, 0.0, u_prev)
    u_next = jnp.where(row == tm - 1, u_after, pltpu.roll(u, tm - 1, 0))
    u_next = jnp.where(pos == seq_len - 1, 0.0, u_next)
    y = u_prev * cw_ref[0:1, :] + u * cw_ref[1:2, :] + u_next * cw_ref[2:3, :]
    a_w = MLA_HEADS * MLA_V
    mix_ref[:, :a_w] = a_ref[...]
    mix_ref[:, a_w:a_w + CONV_DIM] = (gb_ref[...].astype(F32) * y).astype(BF16)
    mix_ref[:, a_w + CONV_DIM:] = n_ref[...]

    for r0 in range(0, tm, TM_OUT_CHUNK):
        rows = slice(r0, r0 + TM_OUT_CHUNK)
        r = DEEPNORM_ALPHA * x_ref[rows, :] + g_ref[...] * _dot(mix_ref[rows, :], w_ref[...])
        o_ref[rows, :] = _layer_norm(r, lng_ref[...], lnb_ref[...])


def _out_proj(a, n, z, x, mods, conv_w, w_out, ln_g, ln_b, layer, row_of_tile, seq_len, next_ada=None):
    rows = x.shape[0]
    tm = TM_OUT
    steps = rows // tm
    assert tm % seq_len == 0 or seq_len % tm == 0
    row = lambda i: (i, 0)
    per_layer = lambda i: (layer, 0, 0)
    conv_blk = lambda off: pl.BlockSpec((tm, CONV_DIM), lambda i: (i, off // CONV_DIM))
    halo = tm // CONV_HALO
    before = lambda off: pl.BlockSpec((CONV_HALO, CONV_DIM), lambda i: (jnp.maximum(i * halo - 1, 0), off // CONV_DIM))
    after = lambda off: pl.BlockSpec(
        (CONV_HALO, CONV_DIM), lambda i: (jnp.minimum((i + 1) * halo, rows // CONV_HALO - 1), off // CONV_DIM))
    vmem = (2 * (2 * _nbytes((tm, D_MODEL), F32) + _nbytes((tm, D_MODEL), BF16) + _nbytes((tm, CONV_DIM), BF16))
            + _nbytes((tm, D_MODEL), BF16) + _nbytes(w_out.shape, BF16)
            + 3 * _nbytes((TM_OUT_CHUNK, D_MODEL), F32) + 4 * _nbytes((tm, CONV_DIM), F32))
    in_specs = [
        _mod_spec(2, row_of_tile),
        pl.BlockSpec((tm, a.shape[1]), row), pl.BlockSpec((tm, n.shape[1]), row), pl.BlockSpec((tm, D_MODEL), row),
        conv_blk(Z_CONV_B), conv_blk(Z_CONV_C), conv_blk(Z_CONV_H),
        before(Z_CONV_C), before(Z_CONV_H), after(Z_CONV_C), after(Z_CONV_H),
        pl.BlockSpec((None,) + conv_w.shape[1:], per_layer),
        pl.BlockSpec(w_out.shape, lambda i: (0, 0), pipeline_mode=pl.Buffered(1)),
        pl.BlockSpec((None, 1, D_MODEL), per_layer), pl.BlockSpec((None, 1, D_MODEL), per_layer),
    ]
    args = [mods, a, n, x, z, z, z, z, z, z, z, conv_w, w_out, ln_g, ln_b]
    out_shape = [jax.ShapeDtypeStruct((rows, D_MODEL), F32)]
    out_specs = [pl.BlockSpec((tm, D_MODEL), row)]
    if next_ada is not None:
        tn = N_MOD * D_MODEL // steps
        assert tn % LANES == 0 and tn * steps == N_MOD * D_MODEL
        in_specs += [
            pl.BlockSpec((MOD_ROWS, D_MODEL), lambda i: (0, 0)),
            pl.BlockSpec((None, D_MODEL, tn), lambda i: (layer + 1, 0, i)),
            pl.BlockSpec((None, 1, tn), lambda i: (layer + 1, 0, i)),
        ]
        args += list(next_ada)
        out_shape.append(jax.ShapeDtypeStruct((MOD_ROWS, N_MOD * D_MODEL), F32))
        out_specs.append(pl.BlockSpec((MOD_ROWS, tn), lambda i: (0, i)))
        vmem += 2 * _nbytes((D_MODEL, tn), F32) + _nbytes((D_MODEL, tn), BF16)
    res = pl.pallas_call(
        functools.partial(_out_proj_kernel, seq_len=seq_len),
        out_shape=tuple(out_shape), grid=(steps,), in_specs=in_specs, out_specs=tuple(out_specs),
        scratch_shapes=[pltpu.VMEM((tm, D_MODEL), BF16)],
        compiler_params=_params(("arbitrary",), vmem),
        name="out_proj",
    )(*args)
    return res if next_ada is not None else res[0]


def _ffn_kernel(sh_ref, sc_ref, g_ref, x_ref, wg_ref, wu_ref, wd_ref, lng_ref, lnb_ref, o_ref, xm_ref):
    f = pl.program_id(1)

    @pl.when(f == 0)
    def _():
        x = x_ref[...]
        xm_ref[...] = (x * (1.0 + sc_ref[...]) + sh_ref[...]).astype(BF16)
        o_ref[...] = DEEPNORM_ALPHA * x

    xm = xm_ref[...]
    gate = _dot(xm, wg_ref[...])
    up = _dot(xm, wu_ref[...])
    hidden = (gate * jax.nn.sigmoid(gate) * up).astype(BF16)
    o_ref[...] += g_ref[...] * _dot(hidden, wd_ref[...])

    @pl.when(f == pl.num_programs(1) - 1)
    def _():
        o_ref[...] = _layer_norm(o_ref[...], lng_ref[...], lnb_ref[...])


def _ffn(x, mods, w_gate, w_up, w_down, ln_g, ln_b, layer, row_of_tile):
    rows = x.shape[0]
    per_layer = lambda i, f: (layer, 0, 0)
    vmem = (4 * _nbytes((TM, D_MODEL), F32) + 2 * 3 * _nbytes((D_MODEL, TF), BF16)
            + _nbytes((TM, D_MODEL), BF16) + 3 * _nbytes((TM, TF), F32) + _nbytes((TM, D_MODEL), F32))
    return pl.pallas_call(
        _ffn_kernel,
        out_shape=jax.ShapeDtypeStruct((rows, D_MODEL), F32),
        grid=(rows // TM, D_FF // TF),
        in_specs=[
            _mod_spec(3, row_of_tile), _mod_spec(4, row_of_tile), _mod_spec(5, row_of_tile),
            pl.BlockSpec((TM, D_MODEL), lambda i, f: (i, 0)),
            pl.BlockSpec((D_MODEL, TF), lambda i, f: (0, f)),
            pl.BlockSpec((D_MODEL, TF), lambda i, f: (0, f)),
            pl.BlockSpec((TF, D_MODEL), lambda i, f: (f, 0)),
            pl.BlockSpec((None, 1, D_MODEL), per_layer), pl.BlockSpec((None, 1, D_MODEL), per_layer),
        ],
        out_specs=pl.BlockSpec((TM, D_MODEL), lambda i, f: (i, 0)),
        scratch_shapes=[pltpu.VMEM((TM, D_MODEL), BF16)],
        compiler_params=_params(("parallel", "arbitrary"), vmem),
        name="ffn",
    )(mods, mods, mods, x, w_gate, w_up, w_down, ln_g, ln_b)


def _rotate_half_cols(w):
    a1, a2, b1, b2 = jnp.split(w, 4, axis=-1)
    return jnp.concatenate([-a2, a1, -b2, b1], axis=-1)


def _w_in_prep_src(piece):
    rope = OFF_ROPE // PREP_PIECE
    tail = Z_ROPE // PREP_PIECE
    return jnp.where(piece < rope, piece, jnp.where(piece < tail, piece + 1, rope))


def _w_in_prep_kernel(*refs):
    *piece_refs, o_ref = refs
    pieces = [r[...] for r in piece_refs]
    n = len(pieces)
    is_last = pl.program_id(1) == pl.num_programs(1) - 1
    rot_piece = Z_DIM // PREP_PIECE - 1
    for p in range(n):
        piece = (Z_PAD // PREP_COLS - 1) * n + p
        if piece == rot_piece:
            a1, a2, b1, b2 = jnp.split(pieces[p], 4, axis=0)
            pieces[p] = jnp.where(is_last, jnp.concatenate([-a2, a1, -b2, b1], axis=0), pieces[p])
        elif piece > rot_piece:
            pieces[p] = jnp.where(is_last, 0.0, pieces[p])
    o_ref[...] = jnp.concatenate(pieces, axis=0).T.astype(BF16)


def _w_in_prep(w_in):
    w_t = jnp.swapaxes(w_in, 1, 2)
    n = PREP_COLS // PREP_PIECE
    piece_spec = lambda p: pl.BlockSpec((None, PREP_PIECE, D_MODEL), lambda l, g: (l, _w_in_prep_src(g * n + p), 0))
    vmem = 2 * (_nbytes((PREP_COLS, D_MODEL), F32) + _nbytes((D_MODEL, PREP_COLS), BF16)) + 3 * _nbytes(
        (PREP_COLS, D_MODEL), F32)
    return pl.pallas_call(
        _w_in_prep_kernel,
        out_shape=jax.ShapeDtypeStruct((DEPTH, D_MODEL, Z_PAD), BF16),
        grid=(DEPTH, Z_PAD // PREP_COLS),
        in_specs=[piece_spec(p) for p in range(n)],
        out_specs=pl.BlockSpec((None, D_MODEL, PREP_COLS), lambda l, g: (l, 0, g)),
        compiler_params=_params(("parallel", "parallel"), vmem),
        name="w_in_prep",
    )(*([w_t] * n))


def _relayout_wq(w):
    w = w.reshape(DEPTH, MLA_Q_RANK, MLA_HEADS, MLA_QK)
    rope = w[..., MLA_NOPE:]
    cols = jnp.concatenate([w, _rotate_half_cols(rope)], axis=-1)
    return cols.reshape(DEPTH, MLA_Q_RANK, MLA_HEADS * QK_PAD).astype(BF16)


def _rope_tables():
    t = jnp.arange(SEQ)
    row = (t // GRID_W).astype(F32)
    col = (t % GRID_W).astype(F32)
    n_freq = MLA_ROPE // 4
    inv = ROPE_THETA ** (-jnp.arange(n_freq, dtype=F32) / n_freq)
    ar = row[:, None] * inv
    ac = col[:, None] * inv
    ang = jnp.concatenate([ar, ar, ac, ac], axis=-1)
    lat = jnp.concatenate([jnp.cos(ang), jnp.sin(ang)], axis=1)
    ctx = jnp.concatenate([jnp.ones((TM_PROJ, MLA_ROPE), F32), jnp.zeros((TM_PROJ, MLA_ROPE), F32)], axis=1)
    return lat, ctx


def kernel(x, c, ctx, c_ctx, ada_w, ada_b, w_in, mla_q_norm, mla_wq_b, mla_kv_norm, mla_wkv_b, conv_w, na_rpb,
           w_out, ln1_g, ln1_b, ffn_w_gate, ffn_w_up, ffn_w_down, ln2_g, ln2_b):
    assert x.shape == (BATCH, SEQ, D_MODEL) and ctx.shape == (BATCH, CTX_LEN, D_MODEL)
    cc = jnp.concatenate([c, c_ctx[None, :], jnp.zeros((MOD_ROWS - BATCH - 1, D_MODEL), F32)], axis=0)
    ada_b = ada_b.reshape(DEPTH, 1, N_MOD * D_MODEL)
    as_table = lambda m: m.reshape(MOD_ROWS * N_MOD, 1, D_MODEL)
    mods = as_table(_ada(cc, ada_w, ada_b, 0))
    ta_lat, ta_ctx = _rope_tables()
    rows3d = lambda v: v.reshape(DEPTH, 1, -1)

    w_in_p = _w_in_prep(w_in)
    wq_all = _relayout_wq(mla_wq_b)
    wkv = mla_wkv_b.astype(BF16)
    g_q, g_kv = rows3d(mla_q_norm), rows3d(mla_kv_norm)
    bias = _na_bias(na_rpb)
    ln1 = (rows3d(ln1_g), rows3d(ln1_b))
    ln2 = (rows3d(ln2_g), rows3d(ln2_b))

    xl = x.reshape(BATCH * SEQ, D_MODEL)
    xc = ctx.reshape(BATCH * CTX_LEN, D_MODEL)
    lat_row = lambda tm: (lambda i: i // (SEQ // tm))
    ctx_row = lambda i: CTX_MOD_ROW
    for l in range(DEPTH):
        last = l == DEPTH - 1
        zl = _in_proj(xl, mods, w_in_p, l, lat_row(TM))
        zc = _in_proj(xc, mods, w_in_p, l, ctx_row)
        ql, kl, vl = _mla_proj(zl, g_q, g_kv, wq_all, wkv, ta_lat, l, lambda i: i % (SEQ // TM_PROJ))
        qc, kc, vc = _mla_proj(zc, g_q, g_kv, wq_all, wkv, ta_ctx, l, lambda i: 0)
        a, (w_g, w_u) = _mla_attn(ql, kl, vl, qc, kc, vc, not last, l,
                                  [(ffn_w_gate, FFN_CAST_BLOCKS, 1), (ffn_w_up, FFN_CAST_BLOCKS, 1)])
        n, (w_d, w_o) = _na_attn(zl, zc, bias, l, not last,
                                 [(ffn_w_down, FFN_CAST_BLOCKS, 0), (w_out, OUT_CAST_BLOCKS, 0)])
        if last:
            x1 = _out_proj(a[0], n[0], zl, xl, mods, conv_w, w_o, *ln1, l, lat_row(TM_OUT), SEQ)
        else:
            x1, next_mods = _out_proj(a[0], n[0], zl, xl, mods, conv_w, w_o, *ln1, l, lat_row(TM_OUT), SEQ,
                                      next_ada=(cc, ada_w, ada_b))
        xl = _ffn(x1, mods, w_g, w_u, w_d, *ln2, l, lat_row(TM))
        if not last:
            x1c = _out_proj(a[1], n[1], zc, xc, mods, conv_w, w_o, *ln1, l, ctx_row, CTX_LEN)
            xc = _ffn(x1c, mods, w_g, w_u, w_d, *ln2, l, ctx_row)
            mods = as_table(next_mods)
    return xl.reshape(BATCH, SEQ, D_MODEL)
```

```python
import functools

import numpy as np
import jax
import jax.numpy as jnp
from jax import lax
from jax.experimental import pallas as pl
from jax.experimental.pallas import tpu as pltpu

D_MODEL = 2048
BATCH = 4
SEQ = 2048
DEPTH = 2
CTX_LEN = 256
GRID_W = 64
GRID_ROWS = SEQ // GRID_W
MLA_HEADS = 6
MLA_Q_RANK = 512
MLA_KV_RANK = 512
MLA_NOPE = 128
MLA_ROPE = 64
MLA_V = 128
MLA_QK = MLA_NOPE + MLA_ROPE
CONV_DIM = 512
NA_HEADS = 6
NA_HEAD_DIM = 128
NA_DIM = NA_HEADS * NA_HEAD_DIM
NA_WIN_R = 8
NA_WIN_C = 16
OFF_MLA_Q = 0
OFF_MLA_KV = OFF_MLA_Q + MLA_Q_RANK
OFF_ROPE = OFF_MLA_KV + MLA_KV_RANK
OFF_CONV_B = OFF_ROPE + MLA_ROPE
OFF_CONV_C = OFF_CONV_B + CONV_DIM
OFF_CONV_H = OFF_CONV_C + CONV_DIM
OFF_NA_Q = OFF_CONV_H + CONV_DIM
OFF_NA_K = OFF_NA_Q + NA_DIM
OFF_NA_V = OFF_NA_K + NA_DIM
IN_DIM = OFF_NA_V + NA_DIM
D_FF = -(-8 * D_MODEL // (3 * 256)) * 256
ROPE_THETA = 10000.0
LN_EPS = 1e-6
RMS_EPS = 1e-6
DEEPNORM_ALPHA = (2 * DEPTH) ** 0.25
LOG2E = 1.4426950408889634
N_MOD = 6
MOD_ROWS = 8
CTX_MOD_ROW = BATCH

LANES = 128
V7X_VMEM_BYTES = 64 * 1024 * 1024
V7X_VMEM_BUDGET = V7X_VMEM_BYTES - 2 * 1024 * 1024

Z_Q = 0
Z_KV = Z_Q + MLA_Q_RANK
Z_CONV_B = Z_KV + MLA_KV_RANK
Z_CONV_C = Z_CONV_B + CONV_DIM
Z_CONV_H = Z_CONV_C + CONV_DIM
Z_NA_Q = Z_CONV_H + CONV_DIM
Z_NA_K = Z_NA_Q + NA_DIM
Z_NA_V = Z_NA_K + NA_DIM
Z_ROPE = Z_NA_V + NA_DIM
Z_DIM = Z_ROPE + 2 * MLA_ROPE
V7X_MXU_COLS = 256
Z_PAD = -(-Z_DIM // (5 * V7X_MXU_COLS)) * (5 * V7X_MXU_COLS)

QK_PAD = 2 * LANES

TM = 1024
TM_OUT = 512
TM_OUT_CHUNK = 256
CONV_HALO = 16
TM_PROJ = 1024
PREP_PIECE = MLA_ROPE
PREP_COLS = 5 * LANES
TN_IN = 5 * V7X_MXU_COLS
TN_ADA = 1024
TF = 512
TF_HALF = V7X_MXU_COLS
TQ = 256
Q_TILE_UNROLL = 8
NA_RQ = 4
NA_RK = NA_RQ + NA_WIN_R - 1
NA_NQ = NA_RQ * GRID_W
NA_NK = NA_RK * GRID_W
NA_BLOCKS = GRID_ROWS // NA_RQ
FFN_CAST_BLOCKS = 11
MLA_HEADS_PER_STEP = 2
OUT_CAST_BLOCKS = 8
NA_HEADS_PER_STEP = 2

F32 = jnp.float32
BF16 = jnp.bfloat16


def _params(semantics, vmem_bytes):
    assert vmem_bytes <= V7X_VMEM_BUDGET, vmem_bytes
    return pltpu.CompilerParams(dimension_semantics=semantics, vmem_limit_bytes=int(vmem_bytes))


def _nbytes(shape, dtype):
    return int(np.prod(shape)) * jnp.dtype(dtype).itemsize


def _dot(a, b):
    return jnp.dot(a, b, preferred_element_type=F32)


def _dot_nt(a, b):
    return lax.dot_general(a, b, (((1,), (1,)), ((), ())), preferred_element_type=F32)


def _mod_spec(chunk, row_of_tile):
    return pl.BlockSpec((None, 1, D_MODEL), lambda i, *_: (chunk + row_of_tile(i) * N_MOD, 0, 0))


def _layer_norm(r, g, b):
    mu = jnp.mean(r, axis=-1, keepdims=True)
    c = r - mu
    var = jnp.mean(c * c, axis=-1, keepdims=True)
    return c * lax.rsqrt(var + LN_EPS) * g + b


def _ada_block(cc_ref, w_ref, b_ref):
    cc = cc_ref[...]
    s = (cc * jax.nn.sigmoid(cc)).astype(BF16)
    return _dot(s, w_ref[...].astype(BF16)) + b_ref[...]


def _ada_kernel(cc_ref, w_ref, b_ref, o_ref):
    o_ref[...] = _ada_block(cc_ref, w_ref, b_ref)


def _ada(cc, ada_w, ada_b, layer):
    n = N_MOD * D_MODEL
    vmem = 2 * (_nbytes((D_MODEL, TN_ADA), F32) + _nbytes((MOD_ROWS, TN_ADA), F32) * 2
                + _nbytes((MOD_ROWS, D_MODEL), F32)) + _nbytes((D_MODEL, TN_ADA), BF16) * 2
    return pl.pallas_call(
        _ada_kernel,
        out_shape=jax.ShapeDtypeStruct((MOD_ROWS, n), F32),
        grid=(n // TN_ADA,),
        in_specs=[
            pl.BlockSpec((MOD_ROWS, D_MODEL), lambda j: (0, 0)),
            pl.BlockSpec((None, D_MODEL, TN_ADA), lambda j: (layer, 0, j)),
            pl.BlockSpec((None, 1, TN_ADA), lambda j: (layer, 0, j)),
        ],
        out_specs=pl.BlockSpec((MOD_ROWS, TN_ADA), lambda j: (0, j)),
        compiler_params=_params(("parallel",), vmem),
        name="ada_mod",
    )(cc, ada_w, ada_b)


def _in_proj_kernel(sh_ref, sc_ref, x_ref, w_ref, o_ref, xm_ref):
    @pl.when(pl.program_id(1) == 0)
    def _():
        xm_ref[...] = (x_ref[...] * (1.0 + sc_ref[...]) + sh_ref[...]).astype(BF16)

    o_ref[...] = _dot(xm_ref[...], w_ref[...]).astype(BF16)


def _in_proj(x, mods, w_in_p, layer, row_of_tile):
    rows = x.shape[0]
    vmem = (2 * (_nbytes((TM, D_MODEL), F32) + _nbytes((D_MODEL, TN_IN), BF16) + _nbytes((TM, TN_IN), BF16))
            + _nbytes((TM, D_MODEL), BF16) + _nbytes((TM, TN_IN), F32))
    return pl.pallas_call(
        _in_proj_kernel,
        out_shape=jax.ShapeDtypeStruct((rows, Z_PAD), BF16),
        grid=(rows // TM, Z_PAD // TN_IN),
        in_specs=[
            _mod_spec(0, row_of_tile),
            _mod_spec(1, row_of_tile),
            pl.BlockSpec((TM, D_MODEL), lambda i, j: (i, 0)),
            pl.BlockSpec((None, D_MODEL, TN_IN), lambda i, j: (layer, 0, j)),
        ],
        out_specs=pl.BlockSpec((TM, TN_IN), lambda i, j: (i, j)),
        scratch_shapes=[pltpu.VMEM((TM, D_MODEL), BF16)],
        compiler_params=_params(("parallel", "arbitrary"), vmem),
        name="in_proj",
    )(mods, mods, x, w_in_p)


def _rms_norm(x, g):
    return x * lax.rsqrt(jnp.mean(x * x, axis=-1, keepdims=True) + RMS_EPS) * g


def _rope_sum(pair, table, lane):
    y = pair * table
    return jnp.where(lane < MLA_ROPE, y + pltpu.roll(y, MLA_ROPE, 1), 0.0)


def _mla_proj_kernel(z_ref, slot_ref, gq_ref, gkv_ref, wq_ref, wkv_ref, ta_ref, q_ref, k_ref, v_ref):
    cq = z_ref[:, Z_Q:Z_Q + MLA_Q_RANK].astype(F32)
    ckv = z_ref[:, Z_KV:Z_KV + MLA_KV_RANK].astype(F32)
    qa = _dot(_rms_norm(cq, gq_ref[...]).astype(BF16), wq_ref[...]) * (MLA_QK ** -0.5 * LOG2E)
    kv = _dot(_rms_norm(ckv, gkv_ref[...]).astype(BF16), wkv_ref[...])
    ta = ta_ref[...]
    lane = lax.broadcasted_iota(jnp.int32, ta.shape, 1)
    kro = _rope_sum(slot_ref[...].astype(F32), ta, lane).astype(BF16)
    for h in range(MLA_HEADS):
        ob = h * QK_PAD
        q_ref[:, ob:ob + LANES] = qa[:, ob:ob + LANES].astype(BF16)
        q_ref[:, ob + LANES:ob + QK_PAD] = _rope_sum(qa[:, ob + LANES:ob + QK_PAD], ta, lane).astype(BF16)
        k_ref[:, ob:ob + LANES] = kv[:, ob:ob + LANES].astype(BF16)
        k_ref[:, ob + LANES:ob + QK_PAD] = kro
        v_ref[:, h * MLA_V:(h + 1) * MLA_V] = kv[:, ob + LANES:ob + QK_PAD].astype(BF16)


def _mla_proj(z, g_q, g_kv, wq_all, wkv, ta, layer, table_tile):
    rows = z.shape[0]
    per_layer = lambda i: (layer, 0, 0)
    zw = Z_KV + MLA_KV_RANK
    hq = MLA_HEADS * QK_PAD
    hv = MLA_HEADS * MLA_V
    tm = TM_PROJ
    vmem = (2 * (_nbytes((tm, zw), BF16) + _nbytes((tm, LANES), BF16) + _nbytes((tm, LANES), F32)
                 + _nbytes(wq_all.shape[1:], BF16) + _nbytes(wkv.shape[1:], BF16)
                 + 2 * _nbytes((tm, hq), BF16) + _nbytes((tm, hv), BF16))
            + _nbytes((tm, hq), F32) + _nbytes((tm, zw), F32))
    return pl.pallas_call(
        _mla_proj_kernel,
        out_shape=(jax.ShapeDtypeStruct((rows, hq), BF16), jax.ShapeDtypeStruct((rows, hq), BF16),
                   jax.ShapeDtypeStruct((rows, hv), BF16)),
        grid=(rows // tm,),
        in_specs=[
            pl.BlockSpec((tm, zw), lambda i: (i, 0)),
            pl.BlockSpec((tm, 2 * MLA_ROPE), lambda i: (i, Z_ROPE // (2 * MLA_ROPE))),
            pl.BlockSpec((None, 1, MLA_Q_RANK), per_layer),
            pl.BlockSpec((None, 1, MLA_KV_RANK), per_layer),
            pl.BlockSpec((None,) + wq_all.shape[1:], per_layer),
            pl.BlockSpec((None,) + wkv.shape[1:], per_layer),
            pl.BlockSpec((tm, LANES), lambda i: (table_tile(i), 0)),
        ],
        out_specs=(pl.BlockSpec((tm, hq), lambda i: (i, 0)), pl.BlockSpec((tm, hq), lambda i: (i, 0)),
                   pl.BlockSpec((tm, hv), lambda i: (i, 0))),
        compiler_params=_params(("parallel",), vmem),
        name="mla_proj",
    )(z, z, g_q, g_kv, wq_all, wkv, ta)


def _rider_specs(w, layer, n_blocks, axis, grid):
    rows, cols = w.shape[1:]
    block = (rows // n_blocks, cols) if axis == 0 else (rows, cols // n_blocks)
    assert block[0] * block[1] * n_blocks == rows * cols and n_blocks <= grid[0] * grid[1]

    def at(outer, inner):
        step = jnp.minimum(outer * grid[1] + inner, n_blocks - 1)
        return (step, 0) if axis == 0 else (0, step)

    in_spec = pl.BlockSpec((None,) + block, lambda outer, inner: (layer,) + at(outer, inner))
    return in_spec, pl.BlockSpec(block, at), jax.ShapeDtypeStruct((rows, cols), BF16), block


def _cast_riders(in_refs, out_refs):
    for i_ref, o_ref in zip(in_refs, out_refs):
        o_ref[...] = i_ref[...].astype(BF16)


def _split_refs(refs, *counts):
    out, pos = [], 0
    for c in counts:
        out.append(refs[pos:pos + c])
        pos += c
    assert pos == len(refs)
    return out


def _softmax_pv(scores, values_ones):
    m = functools.reduce(jnp.maximum, [jnp.max(s, axis=-1, keepdims=True) for s in scores])
    acc = functools.reduce(jnp.add, [_dot(jnp.exp2(s - m).astype(BF16), v) for s, v in zip(scores, values_ones)])
    return acc[:, :LANES] / acc[:, LANES:]


def _fill_values_ones(v_ref, vx_ref):
    vx_ref[:, :LANES] = v_ref[...]
    vx_ref[:, LANES:] = jnp.ones((vx_ref.shape[0], LANES), BF16)


def _scaled_q(q, head_dim):
    return (q.astype(F32) * (head_dim ** -0.5 * LOG2E)).astype(BF16)


def _mla_attn_kernel(*refs, with_ctx, n_riders):
    ins, rider_in, outs, rider_out, (vxl_ref, vxc_ref) = _split_refs(
        refs, 6 if with_ctx else 5, n_riders, 2 if with_ctx else 1, n_riders, 2)
    q_ref, kl_ref, vl_ref, kc_ref, vc_ref = ins[:5]
    o_ref = outs[0]
    _cast_riders(rider_in, rider_out)
    for hh in range(MLA_HEADS_PER_STEP):
        qk = slice(hh * QK_PAD, (hh + 1) * QK_PAD)
        vcols = slice(hh * MLA_V, (hh + 1) * MLA_V)
        _fill_values_ones(vl_ref.at[:, vcols], vxl_ref.at[hh])
        _fill_values_ones(vc_ref.at[:, vcols], vxc_ref.at[hh])

        def q_tile(t, carry, hh=hh, qk=qk, vcols=vcols):
            r0 = pl.multiple_of(t * TQ, TQ)
            q = q_ref[pl.ds(r0, TQ), qk]
            s_lat = _dot_nt(q, kl_ref[:, qk])
            s_ctx = _dot_nt(q, kc_ref[:, qk])
            o = _softmax_pv([s_lat, s_ctx], [vxl_ref[hh], vxc_ref[hh]])
            o_ref[pl.ds(r0, TQ), vcols] = o.astype(BF16)
            return carry

        lax.fori_loop(0, SEQ // TQ, q_tile, 0, unroll=Q_TILE_UNROLL)
        if with_ctx:
            s = _dot_nt(ins[5][:, qk], kc_ref[:, qk])
            outs[1][:, vcols] = _softmax_pv([s], [vxc_ref[hh]]).astype(BF16)


def _mla_attn(ql, kl, vl, qc, kc, vc, with_ctx, layer, riders):
    hps = MLA_HEADS_PER_STEP
    grid = (BATCH, MLA_HEADS // hps)
    head = lambda b, g: (b, g)
    wqk, wv = hps * QK_PAD, hps * MLA_V
    in_specs = [
        pl.BlockSpec((SEQ, wqk), head), pl.BlockSpec((SEQ, wqk), head), pl.BlockSpec((SEQ, wv), head),
        pl.BlockSpec((CTX_LEN, wqk), head), pl.BlockSpec((CTX_LEN, wv), head),
    ]
    args = [ql, kl, vl, kc, vc]
    out_shape = [jax.ShapeDtypeStruct((BATCH * SEQ, MLA_HEADS * MLA_V), BF16)]
    out_specs = [pl.BlockSpec((SEQ, wv), head)]
    if with_ctx:
        in_specs.append(pl.BlockSpec((CTX_LEN, wqk), head))
        args.append(qc)
        out_shape.append(jax.ShapeDtypeStruct((BATCH * CTX_LEN, MLA_HEADS * MLA_V), BF16))
        out_specs.append(pl.BlockSpec((CTX_LEN, wv), head))
    vmem = (2 * (3 * _nbytes((SEQ, wqk), BF16) + 2 * _nbytes((SEQ, wv), BF16))
            + 2 * hps * _nbytes((SEQ, 2 * LANES), BF16) + Q_TILE_UNROLL * _nbytes((TQ, SEQ + CTX_LEN), BF16))
    n_base = len(out_shape)
    for w, n_blocks, axis in riders:
        in_spec, out_spec, shape, block = _rider_specs(w, layer, n_blocks, axis, grid)
        in_specs.append(in_spec)
        args.append(w)
        out_specs.append(out_spec)
        out_shape.append(shape)
        vmem += 2 * (_nbytes(block, F32) + _nbytes(block, BF16))
    res = pl.pallas_call(
        functools.partial(_mla_attn_kernel, with_ctx=with_ctx, n_riders=len(riders)),
        out_shape=tuple(out_shape), grid=grid, in_specs=in_specs, out_specs=tuple(out_specs),
        scratch_shapes=[pltpu.VMEM((hps, SEQ, 2 * LANES), BF16), pltpu.VMEM((hps, CTX_LEN, 2 * LANES), BF16)],
        compiler_params=_params(("arbitrary", "arbitrary"), vmem),
        name="mla_attn_ctx" if with_ctx else "mla_attn",
    )(*args)
    return res[:n_base], res[n_base:]


def _na_band_start(block):
    return min(max(block * NA_RQ - NA_WIN_R // 2, 0), GRID_ROWS - NA_RK)


def _na_bias_kind(block):
    return 0 if block == 0 else (2 if block == NA_BLOCKS - 1 else 1)


def _na_row_offsets():
    n_dr = 2 * NA_WIN_R - 1
    idx = np.full((3, NA_RQ, NA_RK), n_dr, np.int32)
    for kind, block in ((0, 0), (1, 2), (2, NA_BLOCKS - 1)):
        for qr in range(NA_RQ):
            r = block * NA_RQ + qr
            r0 = min(max(r - NA_WIN_R // 2, 0), GRID_ROWS - NA_WIN_R)
            for ki in range(NA_RK):
                kr = _na_band_start(block) + ki
                if r0 <= kr < r0 + NA_WIN_R:
                    idx[kind, qr, ki] = kr - r + NA_WIN_R - 1
    return idx


def _na_bias_kernel(w_ref, o_ref):
    shape = (GRID_W, LANES)
    c = lax.broadcasted_iota(jnp.int32, shape, 0)
    kc = lax.broadcasted_iota(jnp.int32, shape, 1)
    c0 = jnp.clip(c - NA_WIN_C // 2, 0, GRID_W - NA_WIN_C)
    col_ok = (kc >= c0) & (kc < c0 + NA_WIN_C)
    neg = jnp.full((GRID_W, GRID_W), -jnp.inf, F32)
    offsets = _na_row_offsets()
    n_dr = 2 * NA_WIN_R - 1
    tiles = {n_dr: neg}
    for d in sorted(set(int(v) for v in offsets.reshape(-1)) - {n_dr}):
        row = jnp.broadcast_to(w_ref[d:d + 1, :], shape)
        skew = pltpu.roll(row, LANES - (GRID_W - 1), 1, stride=1, stride_axis=0)
        tiles[d] = jnp.where(col_ok, skew, -jnp.inf)[:, :GRID_W]
    for kind in range(3):
        for qr in range(NA_RQ):
            for ki in range(NA_RK):
                o_ref[kind, qr * GRID_W:(qr + 1) * GRID_W, ki * GRID_W:(ki + 1) * GRID_W] = (
                    tiles[int(offsets[kind, qr, ki])])


def _na_bias(rpb):
    n_dr, n_dc = 2 * NA_WIN_R - 1, 2 * NA_WIN_C - 1
    lo = GRID_W - 1 - (NA_WIN_C - 1)
    w = jnp.pad(rpb.reshape(DEPTH * NA_HEADS, n_dr, n_dc) * LOG2E, ((0, 0), (0, 0), (lo, LANES - lo - n_dc)),
                constant_values=-jnp.inf)
    vmem = 2 * (_nbytes((2 * 8, LANES), F32) + _nbytes((3, NA_NQ, NA_NK + GRID_W), F32)) + 16 * _nbytes(
        (GRID_W, LANES), F32)
    return pl.pallas_call(
        _na_bias_kernel,
        out_shape=jax.ShapeDtypeStruct((w.shape[0], 3, NA_NQ, NA_NK), F32),
        grid=(w.shape[0],),
        in_specs=[pl.BlockSpec((None, n_dr, LANES), lambda g: (g, 0, 0))],
        out_specs=pl.BlockSpec((None, 3, NA_NQ, NA_NK), lambda g: (g, 0, 0, 0)),
        compiler_params=_params(("parallel",), vmem),
        name="na_bias",
    )(w)


def _na_kernel(*refs, with_ctx, n_riders):
    ins, rider_in, outs, rider_out, (vx_ref, vxc_ref) = _split_refs(
        refs, 7 if with_ctx else 6, n_riders, 2 if with_ctx else 1, n_riders, 2)
    q_ref, k_ref, v_ref, kc_ref, vc_ref, bias_ref = ins[:6]
    o_ref = outs[0]
    _cast_riders(rider_in, rider_out)
    d = NA_HEAD_DIM
    for hh in range(NA_HEADS_PER_STEP):
        cols = slice(hh * d, (hh + 1) * d)
        _fill_values_ones(v_ref.at[:, cols], vx_ref.at[hh])
        _fill_values_ones(vc_ref.at[:, cols], vxc_ref.at[hh])
        for block in range(NA_BLOCKS):
            rows = slice(block * NA_NQ, (block + 1) * NA_NQ)
            q = _scaled_q(q_ref[rows, cols], d)
            k0 = _na_band_start(block) * GRID_W
            s_loc = _dot_nt(q, k_ref[k0:k0 + NA_NK, cols]) + bias_ref[hh, _na_bias_kind(block)]
            s_ctx = _dot_nt(q, kc_ref[:, cols])
            o = _softmax_pv([s_loc, s_ctx], [vx_ref[hh, k0:k0 + NA_NK, :], vxc_ref[hh]])
            o_ref[rows, cols] = o.astype(BF16)
        if with_ctx:
            s = _dot_nt(_scaled_q(ins[6][:, cols], d), kc_ref[:, cols])
            outs[1][:, cols] = _softmax_pv([s], [vxc_ref[hh]]).astype(BF16)


def _na_attn(zl, zc, bias, layer, with_ctx, riders):
    hps = NA_HEADS_PER_STEP
    w = hps * NA_HEAD_DIM
    steps = NA_HEADS // hps
    col = lambda off: (lambda g, b: (b, off // w + g))
    in_specs = [
        pl.BlockSpec((SEQ, w), col(Z_NA_Q)), pl.BlockSpec((SEQ, w), col(Z_NA_K)), pl.BlockSpec((SEQ, w), col(Z_NA_V)),
        pl.BlockSpec((CTX_LEN, w), col(Z_NA_K)), pl.BlockSpec((CTX_LEN, w), col(Z_NA_V)),
        pl.BlockSpec((hps, 3, NA_NQ, NA_NK), lambda g, b: (layer * steps + g, 0, 0, 0)),
    ]
    args = [zl, zl, zl, zc, zc, bias]
    out_shape = [jax.ShapeDtypeStruct((BATCH * SEQ, NA_DIM), BF16)]
    out_specs = [pl.BlockSpec((SEQ, w), lambda g, b: (b, g))]
    if with_ctx:
        in_specs.append(pl.BlockSpec((CTX_LEN, w), col(Z_NA_Q)))
        args.append(zc)
        out_shape.append(jax.ShapeDtypeStruct((BATCH * CTX_LEN, NA_DIM), BF16))
        out_specs.append(pl.BlockSpec((CTX_LEN, w), lambda g, b: (b, g)))
    vmem = (2 * (5 * _nbytes((SEQ, w), BF16) + _nbytes((hps, 3, NA_NQ, NA_NK), F32))
            + 2 * hps * _nbytes((SEQ, 2 * LANES), BF16) + 4 * hps * _nbytes((NA_NQ, NA_NK + CTX_LEN), F32))
    n_base = len(out_shape)
    for wt, n_blocks, axis in riders:
        in_spec, out_spec, shape, block = _rider_specs(wt, layer, n_blocks, axis, (steps, BATCH))
        in_specs.append(in_spec)
        args.append(wt)
        out_specs.append(out_spec)
        out_shape.append(shape)
        vmem += 2 * (_nbytes(block, F32) + _nbytes(block, BF16))
    res = pl.pallas_call(
        functools.partial(_na_kernel, with_ctx=with_ctx, n_riders=len(riders)),
        out_shape=tuple(out_shape), grid=(steps, BATCH), in_specs=in_specs, out_specs=tuple(out_specs),
        scratch_shapes=[pltpu.VMEM((hps, SEQ, 2 * LANES), BF16), pltpu.VMEM((hps, CTX_LEN, 2 * LANES), BF16)],
        compiler_params=_params(("arbitrary", "arbitrary"), vmem),
        name="na_attn_ctx" if with_ctx else "na_attn",
    )(*args)
    return res[:n_base], res[n_base:]


def _out_proj_kernel(g_ref, a_ref, n_ref, x_ref, gb_ref, gc_ref, h_ref, gcp_ref, hp_ref, gcn_ref, hn_ref, cw_ref,
                     w_ref, lng_ref, lnb_ref, *rest, seq_len):
    if len(rest) == 2:
        o_ref, mix_ref = rest
    else:
        *ada_in, o_ref, ada_out, mix_ref = rest
        ada_out[...] = _ada_block(*ada_in)
    tm = o_ref.shape[0]
    u = gc_ref[...].astype(F32) * h_ref[...].astype(F32)
    u_before = gcp_ref[CONV_HALO - 1:, :].astype(F32) * hp_ref[CONV_HALO - 1:, :].astype(F32)
    u_after = gcn_ref[:1, :].astype(F32) * hn_ref[:1, :].astype(F32)
    row = lax.broadcasted_iota(jnp.int32, u.shape, 0)
    pos = (pl.program_id(0) * tm + row) % seq_len
    u_prev = jnp.where(row == 0, u_before, pltpu.roll(u, 1, 0))
    u_prev = jnp.where(pos == 0, 0.0, u_prev)
    u_next = jnp.where(row == tm - 1, u_after, pltpu.roll(u, tm - 1, 0))
    u_next = jnp.where(pos == seq_len - 1, 0.0, u_next)
    y = u_prev * cw_ref[0:1, :] + u * cw_ref[1:2, :] + u_next * cw_ref[2:3, :]
    a_w = MLA_HEADS * MLA_V
    mix_ref[:, :a_w] = a_ref[...]
    mix_ref[:, a_w:a_w + CONV_DIM] = (gb_ref[...].astype(F32) * y).astype(BF16)
    mix_ref[:, a_w + CONV_DIM:] = n_ref[...]

    for r0 in range(0, tm, TM_OUT_CHUNK):
        rows = slice(r0, r0 + TM_OUT_CHUNK)
        r = DEEPNORM_ALPHA * x_ref[rows, :] + g_ref[...] * _dot(mix_ref[rows, :], w_ref[...])
        o_ref[rows, :] = _layer_norm(r, lng_ref[...], lnb_ref[...])


def _out_proj(a, n, z, x, mods, conv_w, w_out, ln_g, ln_b, layer, row_of_tile, seq_len, next_ada=None):
    rows = x.shape[0]
    tm = TM_OUT
    steps = rows // tm
    assert tm % seq_len == 0 or seq_len % tm == 0
    row = lambda i: (i, 0)
    per_layer = lambda i: (layer, 0, 0)
    conv_blk = lambda off: pl.BlockSpec((tm, CONV_DIM), lambda i: (i, off // CONV_DIM))
    halo = tm // CONV_HALO
    before = lambda off: pl.BlockSpec((CONV_HALO, CONV_DIM), lambda i: (jnp.maximum(i * halo - 1, 0), off // CONV_DIM))
    after = lambda off: pl.BlockSpec(
        (CONV_HALO, CONV_DIM), lambda i: (jnp.minimum((i + 1) * halo, rows // CONV_HALO - 1), off // CONV_DIM))
    vmem = (2 * (2 * _nbytes((tm, D_MODEL), F32) + _nbytes((tm, D_MODEL), BF16) + _nbytes((tm, CONV_DIM), BF16))
            + _nbytes((tm, D_MODEL), BF16) + _nbytes(w_out.shape, BF16)
            + 3 * _nbytes((TM_OUT_CHUNK, D_MODEL), F32) + 4 * _nbytes((tm, CONV_DIM), F32))
    in_specs = [
        _mod_spec(2, row_of_tile),
        pl.BlockSpec((tm, a.shape[1]), row), pl.BlockSpec((tm, n.shape[1]), row), pl.BlockSpec((tm, D_MODEL), row),
        conv_blk(Z_CONV_B), conv_blk(Z_CONV_C), conv_blk(Z_CONV_H),
        before(Z_CONV_C), before(Z_CONV_H), after(Z_CONV_C), after(Z_CONV_H),
        pl.BlockSpec((None,) + conv_w.shape[1:], per_layer),
        pl.BlockSpec(w_out.shape, lambda i: (0, 0), pipeline_mode=pl.Buffered(1)),
        pl.BlockSpec((None, 1, D_MODEL), per_layer), pl.BlockSpec((None, 1, D_MODEL), per_layer),
    ]
    args = [mods, a, n, x, z, z, z, z, z, z, z, conv_w, w_out, ln_g, ln_b]
    out_shape = [jax.ShapeDtypeStruct((rows, D_MODEL), F32)]
    out_specs = [pl.BlockSpec((tm, D_MODEL), row)]
    if next_ada is not None:
        tn = N_MOD * D_MODEL // steps
        assert tn % LANES == 0 and tn * steps == N_MOD * D_MODEL
        in_specs += [
            pl.BlockSpec((MOD_ROWS, D_MODEL), lambda i: (0, 0)),
            pl.BlockSpec((None, D_MODEL, tn), lambda i: (layer + 1, 0, i)),
            pl.BlockSpec((None, 1, tn), lambda i: (layer + 1, 0, i)),
        ]
        args += list(next_ada)
        out_shape.append(jax.ShapeDtypeStruct((MOD_ROWS, N_MOD * D_MODEL), F32))
        out_specs.append(pl.BlockSpec((MOD_ROWS, tn), lambda i: (0, i)))
        vmem += 2 * _nbytes((D_MODEL, tn), F32) + _nbytes((D_MODEL, tn), BF16)
    res = pl.pallas_call(
        functools.partial(_out_proj_kernel, seq_len=seq_len),
        out_shape=tuple(out_shape), grid=(steps,), in_specs=in_specs, out_specs=tuple(out_specs),
        scratch_shapes=[pltpu.VMEM((tm, D_MODEL), BF16)],
        compiler_params=_params(("arbitrary",), vmem),
        name="out_proj",
    )(*args)
    return res if next_ada is not None else res[0]


def _ffn_kernel(sh_ref, sc_ref, g_ref, x_ref, wg_ref, wu_ref, wd_ref, lng_ref, lnb_ref, o_ref, xm_ref):
    f = pl.program_id(1)

    @pl.when(f == 0)
    def _():
        x = x_ref[...]
        xm_ref[...] = (x * (1.0 + sc_ref[...]) + sh_ref[...]).astype(BF16)
        o_ref[...] = DEEPNORM_ALPHA * x

    xm = xm_ref[...]
    down = None
    for c0 in range(0, TF, TF_HALF):
        cols = slice(c0, c0 + TF_HALF)
        gate = _dot(xm, wg_ref[:, cols])
        up = _dot(xm, wu_ref[:, cols])
        hidden = (gate * jax.nn.sigmoid(gate) * up).astype(BF16)
        part = _dot(hidden, wd_ref[cols, :])
        down = part if down is None else down + part
    o_ref[...] += g_ref[...] * down

    @pl.when(f == pl.num_programs(1) - 1)
    def _():
        o_ref[...] = _layer_norm(o_ref[...], lng_ref[...], lnb_ref[...])


def _ffn(x, mods, w_gate, w_up, w_down, ln_g, ln_b, layer, row_of_tile):
    rows = x.shape[0]
    per_layer = lambda i, f: (layer, 0, 0)
    vmem = (4 * _nbytes((TM, D_MODEL), F32) + 2 * 3 * _nbytes((D_MODEL, TF), BF16)
            + _nbytes((TM, D_MODEL), BF16) + 3 * _nbytes((TM, TF), F32) + _nbytes((TM, D_MODEL), F32))
    return pl.pallas_call(
        _ffn_kernel,
        out_shape=jax.ShapeDtypeStruct((rows, D_MODEL), F32),
        grid=(rows // TM, D_FF // TF),
        in_specs=[
            _mod_spec(3, row_of_tile), _mod_spec(4, row_of_tile), _mod_spec(5, row_of_tile),
            pl.BlockSpec((TM, D_MODEL), lambda i, f: (i, 0)),
            pl.BlockSpec((D_MODEL, TF), lambda i, f: (0, f)),
            pl.BlockSpec((D_MODEL, TF), lambda i, f: (0, f)),
            pl.BlockSpec((TF, D_MODEL), lambda i, f: (f, 0)),
            pl.BlockSpec((None, 1, D_MODEL), per_layer), pl.BlockSpec((None, 1, D_MODEL), per_layer),
        ],
        out_specs=pl.BlockSpec((TM, D_MODEL), lambda i, f: (i, 0)),
        scratch_shapes=[pltpu.VMEM((TM, D_MODEL), BF16)],
        compiler_params=_params(("parallel", "arbitrary"), vmem),
        name="ffn",
    )(mods, mods, mods, x, w_gate, w_up, w_down, ln_g, ln_b)


def _rotate_half_cols(w):
    a1, a2, b1, b2 = jnp.split(w, 4, axis=-1)
    return jnp.concatenate([-a2, a1, -b2, b1], axis=-1)


def _w_in_prep_src(piece):
    rope = OFF_ROPE // PREP_PIECE
    tail = Z_ROPE // PREP_PIECE
    return jnp.where(piece < rope, piece, jnp.where(piece < tail, piece + 1, rope))


def _w_in_prep_kernel(*refs):
    *piece_refs, o_ref = refs
    pieces = [r[...] for r in piece_refs]
    n = len(pieces)
    is_last = pl.program_id(1) == pl.num_programs(1) - 1
    rot_piece = Z_DIM // PREP_PIECE - 1
    for p in range(n):
        piece = (Z_PAD // PREP_COLS - 1) * n + p
        if piece == rot_piece:
            a1, a2, b1, b2 = jnp.split(pieces[p], 4, axis=0)
            pieces[p] = jnp.where(is_last, jnp.concatenate([-a2, a1, -b2, b1], axis=0), pieces[p])
        elif piece > rot_piece:
            pieces[p] = jnp.where(is_last, 0.0, pieces[p])
    o_ref[...] = jnp.concatenate(pieces, axis=0).T.astype(BF16)


def _w_in_prep(w_in):
    w_t = jnp.swapaxes(w_in, 1, 2)
    n = PREP_COLS // PREP_PIECE
    piece_spec = lambda p: pl.BlockSpec((None, PREP_PIECE, D_MODEL), lambda l, g: (l, _w_in_prep_src(g * n + p), 0))
    vmem = 2 * (_nbytes((PREP_COLS, D_MODEL), F32) + _nbytes((D_MODEL, PREP_COLS), BF16)) + 3 * _nbytes(
        (PREP_COLS, D_MODEL), F32)
    return pl.pallas_call(
        _w_in_prep_kernel,
        out_shape=jax.ShapeDtypeStruct((DEPTH, D_MODEL, Z_PAD), BF16),
        grid=(DEPTH, Z_PAD // PREP_COLS),
        in_specs=[piece_spec(p) for p in range(n)],
        out_specs=pl.BlockSpec((None, D_MODEL, PREP_COLS), lambda l, g: (l, 0, g)),
        compiler_params=_params(("parallel", "parallel"), vmem),
        name="w_in_prep",
    )(*([w_t] * n))


def _relayout_wq(w):
    w = w.reshape(DEPTH, MLA_Q_RANK, MLA_HEADS, MLA_QK)
    rope = w[..., MLA_NOPE:]
    cols = jnp.concatenate([w, _rotate_half_cols(rope)], axis=-1)
    return cols.reshape(DEPTH, MLA_Q_RANK, MLA_HEADS * QK_PAD).astype(BF16)


def _rope_tables():
    t = jnp.arange(SEQ)
    row = (t // GRID_W).astype(F32)
    col = (t % GRID_W).astype(F32)
    n_freq = MLA_ROPE // 4
    inv = ROPE_THETA ** (-jnp.arange(n_freq, dtype=F32) / n_freq)
    ar = row[:, None] * inv
    ac = col[:, None] * inv
    ang = jnp.concatenate([ar, ar, ac, ac], axis=-1)
    lat = jnp.concatenate([jnp.cos(ang), jnp.sin(ang)], axis=1)
    ctx = jnp.concatenate([jnp.ones((TM_PROJ, MLA_ROPE), F32), jnp.zeros((TM_PROJ, MLA_ROPE), F32)], axis=1)
    return lat, ctx


def kernel(x, c, ctx, c_ctx, ada_w, ada_b, w_in, mla_q_norm, mla_wq_b, mla_kv_norm, mla_wkv_b, conv_w, na_rpb,
           w_out, ln1_g, ln1_b, ffn_w_gate, ffn_w_up, ffn_w_down, ln2_g, ln2_b):
    assert x.shape == (BATCH, SEQ, D_MODEL) and ctx.shape == (BATCH, CTX_LEN, D_MODEL)
    cc = jnp.concatenate([c, c_ctx[None, :], jnp.zeros((MOD_ROWS - BATCH - 1, D_MODEL), F32)], axis=0)
    ada_b = ada_b.reshape(DEPTH, 1, N_MOD * D_MODEL)
    as_table = lambda m: m.reshape(MOD_ROWS * N_MOD, 1, D_MODEL)
    mods = as_table(_ada(cc, ada_w, ada_b, 0))
    ta_lat, ta_ctx = _rope_tables()
    rows3d = lambda v: v.reshape(DEPTH, 1, -1)

    w_in_p = _w_in_prep(w_in)
    wq_all = _relayout_wq(mla_wq_b)
    wkv = mla_wkv_b.astype(BF16)
    g_q, g_kv = rows3d(mla_q_norm), rows3d(mla_kv_norm)
    bias = _na_bias(na_rpb)
    ln1 = (rows3d(ln1_g), rows3d(ln1_b))
    ln2 = (rows3d(ln2_g), rows3d(ln2_b))

    xl = x.reshape(BATCH * SEQ, D_MODEL)
    xc = ctx.reshape(BATCH * CTX_LEN, D_MODEL)
    lat_row = lambda tm: (lambda i: i // (SEQ // tm))
    ctx_row = lambda i: CTX_MOD_ROW
    for l in range(DEPTH):
        last = l == DEPTH - 1
        zl = _in_proj(xl, mods, w_in_p, l, lat_row(TM))
        zc = _in_proj(xc, mods, w_in_p, l, ctx_row)
        ql, kl, vl = _mla_proj(zl, g_q, g_kv, wq_all, wkv, ta_lat, l, lambda i: i % (SEQ // TM_PROJ))
        qc, kc, vc = _mla_proj(zc, g_q, g_kv, wq_all, wkv, ta_ctx, l, lambda i: 0)
        a, (w_g, w_u) = _mla_attn(ql, kl, vl, qc, kc, vc, not last, l,
                                  [(ffn_w_gate, FFN_CAST_BLOCKS, 1), (ffn_w_up, FFN_CAST_BLOCKS, 1)])
        n, (w_d, w_o) = _na_attn(zl, zc, bias, l, not last,
                                 [(ffn_w_down, FFN_CAST_BLOCKS, 0), (w_out, OUT_CAST_BLOCKS, 0)])
        if last:
            x1 = _out_proj(a[0], n[0], zl, xl, mods, conv_w, w_o, *ln1, l, lat_row(TM_OUT), SEQ)
        else:
            x1, next_mods = _out_proj(a[0], n[0], zl, xl, mods, conv_w, w_o, *ln1, l, lat_row(TM_OUT), SEQ,
                                      next_ada=(cc, ada_w, ada_b))
        xl = _ffn(x1, mods, w_g, w_u, w_d, *ln2, l, lat_row(TM))
        if not last:
            x1c = _out_proj(a[1], n[1], zc, xc, mods, conv_w, w_o, *ln1, l, ctx_row, CTX_LEN)
            xc = _ffn(x1c, mods, w_g, w_u, w_d, *ln2, l, ctx_row)
            mods = as_table(next_mods)
    return xl.reshape(BATCH, SEQ, D_MODEL)
```
